```python
import math
import jax, jax.numpy as jnp
from jax import lax
import numpy as np

D_MODEL = 1024
BATCH = 4
SEQ = 8192
DEPTH = 1
DEC_BATCH = 8
DEC_SEQ = 4096
PAST_LEN = 128

N_ATT_HEADS = 4
ATT_QK_DIM = 64
ATT_V_DIM = 2 * ATT_QK_DIM
ATT_QK_WIDTH = N_ATT_HEADS * 2 * ATT_QK_DIM
ATT_WIDTH = N_ATT_HEADS * ATT_V_DIM
Q_BLOCK = 128
REL_BUCKETS = 32
REL_MAX_DIST = 128
SSM_D_INNER = D_MODEL
SSM_HEAD_DIM = 64
SSM_HEADS = SSM_D_INNER // SSM_HEAD_DIM
SSM_GROUPS = 2
SSM_HEADS_PER_GROUP = SSM_HEADS // SSM_GROUPS
SSM_STATE = 64
SSM_CONV = 3
SSM_CHUNK = 128
CONV_CH = SSM_D_INNER + 2 * SSM_GROUPS * SSM_STATE
IN_WIDTH = 2 * ATT_QK_WIDTH + ATT_WIDTH + SSM_D_INNER + CONV_CH + 2 * SSM_HEADS
N_EXPERTS = 16
CAPACITY_FACTOR = 2
EXPERT_FF = D_MODEL
PLE_DIM = 256
EPS = 1e-6

kernel_name = 'hybrid_diffattn_ssd_ec_encoder'


def _rmsnorm(x, g):
    xf = x.astype(jnp.float32)
    y = xf * lax.rsqrt(jnp.mean(xf * xf, axis=-1, keepdims=True) + EPS)
    return (y * g.astype(jnp.float32)).astype(x.dtype)


def _rel_bucket(rel):
    half = REL_BUCKETS // 2
    max_exact = half // 2
    ret = jnp.where(rel > 0, half, 0)
    n = jnp.abs(rel)
    nf = jnp.maximum(n, 1).astype(jnp.float32)
    large = max_exact + (jnp.log(nf / max_exact) / math.log(REL_MAX_DIST / max_exact)
                         * (half - max_exact)).astype(jnp.int32)
    large = jnp.minimum(large, half - 1)
    return ret + jnp.where(n < max_exact, n, large)


def _diff_attention(q, k, v, rel_bias, lam, lam_init, g_subln):
    b, L = q.shape[0], q.shape[1]
    q = q.reshape(b, L, N_ATT_HEADS, 2, ATT_QK_DIM)
    k = k.reshape(b, L, N_ATT_HEADS, 2, ATT_QK_DIM)
    v = v.reshape(b, L, N_ATT_HEADS, ATT_V_DIM)
    nb = L // Q_BLOCK
    qb = q.reshape(b, nb, Q_BLOCK, N_ATT_HEADS, 2, ATT_QK_DIM).transpose(1, 0, 2, 3, 4, 5)
    starts = jnp.arange(nb, dtype=jnp.int32) * Q_BLOCK
    kpos = jnp.arange(L, dtype=jnp.int32)
    scale = ATT_QK_DIM ** -0.5
    table = rel_bias.astype(jnp.float32)

    def block(args):
        qi, s0 = args
        qpos = s0 + jnp.arange(Q_BLOCK, dtype=jnp.int32)
        rel = kpos[None, :] - qpos[:, None]
        bias = table[_rel_bucket(rel)]
        bias = bias.transpose(2, 0, 1).reshape(N_ATT_HEADS, 2, Q_BLOCK, L)
        s = jnp.einsum('bqhmd,bkhmd->bhmqk', qi, k).astype(jnp.float32) * scale + bias
        pr = jax.nn.softmax(s, axis=-1)
        a = pr[:, :, 0] - lam * pr[:, :, 1]
        return jnp.einsum('bhqk,bkhe->bqhe', a.astype(v.dtype), v)

    o = lax.map(block, (qb, starts))
    o = o.transpose(1, 0, 2, 3, 4).reshape(b, L, N_ATT_HEADS, ATT_V_DIM)
    o = _rmsnorm(o, g_subln) * (1.0 - lam_init)
    return o.reshape(b, L, ATT_WIDTH)


def _ssd(x, dt, A, B, C):
    b, L = x.shape[0], x.shape[1]
    nc = L // SSM_CHUNK
    G, R, P, N, Q = SSM_GROUPS, SSM_HEADS_PER_GROUP, SSM_HEAD_DIM, SSM_STATE, SSM_CHUNK
    xr = x.reshape(b, nc, Q, G, R, P)
    dtr = dt.reshape(b, nc, Q, G, R)
    Br = B.reshape(b, nc, Q, G, N)
    Cr = C.reshape(b, nc, Q, G, N)
    a = dtr * A.reshape(G, R)
    acum = jnp.cumsum(a, axis=2)
    xdt = xr * dtr[..., None]
    seg = acum[:, :, :, None] - acum[:, :, None, :]
    lower = jnp.tril(jnp.ones((Q, Q), dtype=bool))[:, :, None, None]
    Lm = jnp.exp(jnp.where(lower, seg, -jnp.inf))
    CB = jnp.einsum('bcign,bcjgn->bcijg', Cr, Br)
    y_diag = jnp.einsum('bcijgr,bcjgrp->bcigrp', CB[..., None] * Lm, xdt)
    decay_s = jnp.exp(acum[:, :, -1:] - acum)
    states = jnp.einsum('bcjgn,bcjgrp->bcgrpn', Br, xdt * decay_s[..., None])
    alast = acum[:, :, -1]
    S = jnp.cumsum(alast, axis=1)
    diff = (S - alast)[:, :, None] - S[:, None, :]
    strict = (jnp.arange(nc)[None, :] < jnp.arange(nc)[:, None])[:, :, None, None]
    Dm = jnp.exp(jnp.where(strict, diff, -jnp.inf))
    prev = jnp.einsum('bkcgr,bcgrpn->bkgrpn', Dm, states)
    y_off = jnp.einsum('bkign,bkgrpn->bkigrp', Cr, prev) * jnp.exp(acum)[..., None]
    return (y_diag + y_off).reshape(b, L, SSM_HEADS, P)


def _ssm_branch(z, xbc, dt_raw, conv_w, conv_b, dt_bias, a_log, d_skip, g_norm):
    pad = SSM_CONV // 2
    xbc = lax.conv_general_dilated(xbc, conv_w[:, None, :], window_strides=(1,),
                                   padding=[(pad, pad)],
                                   dimension_numbers=('NWC', 'WIO', 'NWC'),
                                   feature_group_count=CONV_CH) + conv_b
    xbc = jax.nn.silu(xbc)
    b, L = xbc.shape[0], xbc.shape[1]
    xs, Bm, Cm = jnp.split(xbc, [SSM_D_INNER, SSM_D_INNER + SSM_GROUPS * SSM_STATE], axis=-1)
    xh = xs.reshape(b, L, SSM_HEADS, SSM_HEAD_DIM).astype(jnp.float32)
    Bm = Bm.reshape(b, L, SSM_GROUPS, SSM_STATE).astype(jnp.float32)
    Cm = Cm.reshape(b, L, SSM_GROUPS, SSM_STATE).astype(jnp.float32)
    dtf = dt_raw.astype(jnp.float32)
    dtb = dt_bias.astype(jnp.float32)
    dt_f = jax.nn.softplus(dtf[..., :SSM_HEADS] + dtb[0])
    dt_b = jax.nn.softplus(dtf[..., SSM_HEADS:] + dtb[1])
    A = -jnp.exp(a_log.astype(jnp.float32))
    flip = lambda t: jnp.flip(t, axis=1)
    y_f = _ssd(xh, dt_f, A[0], Bm, Cm)
    y_b = flip(_ssd(flip(xh), flip(dt_b), A[1], flip(Bm), flip(Cm)))
    y = y_f + y_b + xh * d_skip.astype(jnp.float32)[:, None]
    y = y.reshape(b, L, SSM_D_INNER) * jax.nn.silu(z.astype(jnp.float32))
    return _rmsnorm(y, g_norm).astype(z.dtype)


def _expert_choice(u, w_router, w_gate, w_up, w_down):
    b, L, d = u.shape
    T = b * L
    tok = u.reshape(T, d)
    aff = jax.nn.softmax((tok @ w_router).astype(jnp.float32), axis=-1)
    cap = max(1, CAPACITY_FACTOR * T // N_EXPERTS)
    gate, idx = lax.top_k(aff.T, cap)
    xs = tok[idx]
    hmid = jax.nn.silu(jnp.einsum('ecd,edf->ecf', xs, w_gate)) * jnp.einsum('ecd,edf->ecf', xs, w_up)
    ye = jnp.einsum('ecf,efd->ecd', hmid, w_down) * gate[..., None].astype(u.dtype)
    out = jnp.zeros((T, d), dtype=u.dtype).at[idx.reshape(-1)].add(ye.reshape(-1, d))
    return out.reshape(b, L, d)


def _forward(x, p, w):
    h = x
    for i in range(DEPTH):
        u = _rmsnorm(h, w['g_mix'][i])
        proj = u @ w['w_in'][i]
        o1 = ATT_QK_WIDTH
        o2 = o1 + ATT_QK_WIDTH
        o3 = o2 + ATT_WIDTH
        o4 = o3 + SSM_D_INNER
        o5 = o4 + CONV_CH
        q, k, v, z, xbc, dt = jnp.split(proj, [o1, o2, o3, o4, o5], axis=-1)
        lam_init = 0.8 - 0.6 * math.exp(-0.3 * i)
        lp = w['lam_params'][i].astype(jnp.float32)
        lam = jnp.exp(jnp.sum(lp[0] * lp[1])) - jnp.exp(jnp.sum(lp[2] * lp[3])) + lam_init
        att = _diff_attention(q, k, v, w['rel_bias'], lam, lam_init, w['g_subln'][i])
        ssm = _ssm_branch(z, xbc, dt, w['conv_w'][i], w['conv_b'][i], w['dt_bias'][i],
                          w['a_log'][i], w['d_skip'][i], w['g_ssm_norm'][i])
        gates = jax.nn.sigmoid(u @ w['w_gate'][i] + w['b_gate'][i])
        g_a, g_s = jnp.split(gates, 2, axis=-1)
        merged = g_a * (att @ w['w_att_up'][i]) + g_s * (ssm @ w['w_ssm_up'][i])
        h = h + merged @ w['w_out'][i]
        u2 = _rmsnorm(h, w['g_ffn'][i])
        h = h + _expert_choice(u2, w['w_router'][i], w['w_exp_gate'][i], w['w_exp_up'][i], w['w_exp_down'][i])
        ple_gate = jax.nn.sigmoid(h @ w['w_ple_gate'][i] + w['b_ple_gate'][i])
        h = h + ple_gate * (p[i] @ w['w_ple_proj'][i])
    return _rmsnorm(h, w['g_final'])


def setup_inputs(seed: int = 0) -> dict:
    key = jax.random.key(seed)
    ks = jax.random.split(key, 32)
    f32 = jnp.float32
    def nrm(k, shape, scale):
        return jax.random.normal(k, shape, dtype=f32) * scale
    def gain(k, shape):
        return 1.0 + 0.05 * jax.random.normal(k, shape, dtype=f32)
    dt0 = jnp.exp(jax.random.uniform(ks[8], (DEPTH, 2, SSM_HEADS), dtype=f32)
                  * (math.log(0.1) - math.log(0.001)) + math.log(0.001))
    dt_bias = dt0 + jnp.log(-jnp.expm1(-dt0))
    a_log = jnp.log(jax.random.uniform(ks[9], (DEPTH, 2, SSM_HEADS), dtype=f32, minval=1.0, maxval=16.0))
    return {
        'x_prompt': nrm(ks[0], (BATCH, SEQ, D_MODEL), 1.0),
        'x_sample': nrm(ks[1], (DEC_BATCH, DEC_SEQ, D_MODEL), 1.0),
        'p_prompt': nrm(ks[2], (DEPTH, BATCH, SEQ, PLE_DIM), 1.0),
        'p_sample': nrm(ks[3], (DEPTH, DEC_BATCH, DEC_SEQ, PLE_DIM), 1.0),
        'rel_bias': nrm(ks[4], (REL_BUCKETS, 2 * N_ATT_HEADS), 0.5),
        'g_mix': gain(ks[5], (DEPTH, D_MODEL)),
        'w_in': nrm(ks[6], (DEPTH, D_MODEL, IN_WIDTH), D_MODEL ** -0.5),
        'conv_w': nrm(ks[7], (DEPTH, SSM_CONV, CONV_CH), SSM_CONV ** -0.5),
        'conv_b': nrm(ks[10], (DEPTH, CONV_CH), 0.02),
        'dt_bias': dt_bias,
        'a_log': a_log,
        'd_skip': gain(ks[11], (DEPTH, SSM_HEADS)),
        'g_ssm_norm': gain(ks[12], (DEPTH, SSM_D_INNER)),
        'lam_params': nrm(ks[13], (DEPTH, 4, ATT_QK_DIM), 0.1),
        'g_subln': gain(ks[14], (DEPTH, ATT_V_DIM)),
        'w_att_up': nrm(ks[15], (DEPTH, ATT_WIDTH, D_MODEL), ATT_WIDTH ** -0.5),
        'w_ssm_up': nrm(ks[16], (DEPTH, SSM_D_INNER, D_MODEL), SSM_D_INNER ** -0.5),
        'w_gate': nrm(ks[17], (DEPTH, D_MODEL, 2 * D_MODEL), D_MODEL ** -0.5),
        'b_gate': nrm(ks[18], (DEPTH, 2 * D_MODEL), 0.02),
        'w_out': nrm(ks[19], (DEPTH, D_MODEL, D_MODEL), D_MODEL ** -0.5),
        'g_ffn': gain(ks[20], (DEPTH, D_MODEL)),
        'w_router': nrm(ks[21], (DEPTH, D_MODEL, N_EXPERTS), D_MODEL ** -0.5),
        'w_exp_gate': nrm(ks[22], (DEPTH, N_EXPERTS, D_MODEL, EXPERT_FF), D_MODEL ** -0.5),
        'w_exp_up': nrm(ks[23], (DEPTH, N_EXPERTS, D_MODEL, EXPERT_FF), D_MODEL ** -0.5),
        'w_exp_down': nrm(ks[24], (DEPTH, N_EXPERTS, EXPERT_FF, D_MODEL), EXPERT_FF ** -0.5),
        'w_ple_gate': nrm(ks[25], (DEPTH, D_MODEL, D_MODEL), D_MODEL ** -0.5),
        'b_ple_gate': nrm(ks[26], (DEPTH, D_MODEL), 0.02),
        'w_ple_proj': nrm(ks[27], (DEPTH, PLE_DIM, D_MODEL), PLE_DIM ** -0.5),
        'g_final': gain(ks[28], (D_MODEL,)),
    }


def reference(x_prompt, x_sample, p_prompt, p_sample, rel_bias, g_mix, w_in, conv_w, conv_b,
              dt_bias, a_log, d_skip, g_ssm_norm, lam_params, g_subln, w_att_up, w_ssm_up,
              w_gate, b_gate, w_out, g_ffn, w_router, w_exp_gate, w_exp_up, w_exp_down,
              w_ple_gate, b_ple_gate, w_ple_proj, g_final):
    w = {
        'rel_bias': rel_bias, 'g_mix': g_mix, 'w_in': w_in, 'conv_w': conv_w, 'conv_b': conv_b,
        'dt_bias': dt_bias, 'a_log': a_log, 'd_skip': d_skip, 'g_ssm_norm': g_ssm_norm,
        'lam_params': lam_params, 'g_subln': g_subln, 'w_att_up': w_att_up, 'w_ssm_up': w_ssm_up,
        'w_gate': w_gate, 'b_gate': b_gate, 'w_out': w_out, 'g_ffn': g_ffn, 'w_router': w_router,
        'w_exp_gate': w_exp_gate, 'w_exp_up': w_exp_up, 'w_exp_down': w_exp_down,
        'w_ple_gate': w_ple_gate, 'b_ple_gate': b_ple_gate, 'w_ple_proj': w_ple_proj,
        'g_final': g_final,
    }
    y_prompt = _forward(x_prompt, p_prompt, w)
    y_sample = _forward(x_sample, p_sample, w)
    return (y_prompt, y_sample)
```

```python
import functools
import math

import jax
import jax.numpy as jnp
from jax import lax
from jax.experimental import pallas as pl
from jax.experimental.pallas import tpu as pltpu

F32 = jnp.float32
BF16 = jnp.bfloat16

D_MODEL = 1024
N_HEADS = 4
QK_DIM = 64
HEAD_W = 2 * QK_DIM
ATT_W = N_HEADS * HEAD_W
REL_BUCKETS = 32
REL_MAX_DIST = 128
SSM_INNER = 1024
SSM_P = 64
SSM_HEADS = 16
SSM_GROUPS = 2
SSM_N = 64
SSM_CHUNK = 128
CONV_CH = SSM_INNER + 2 * SSM_GROUPS * SSM_N
N_EXPERTS = 16
CAPACITY_FACTOR = 2
PLE_DIM = 256
EPS = 1e-6
LAM_INIT = 0.8 - 0.6 * math.exp(-0.3 * 0)

ATT_BLOCK = 256
VMEM_LIMIT = 56 * 1024 * 1024


def _params(sem, vmem=VMEM_LIMIT):
    return pltpu.CompilerParams(dimension_semantics=sem, vmem_limit_bytes=vmem)


def _const_spec(shape):
    nd = len(shape)
    return pl.BlockSpec(shape, lambda *_: (0,) * nd)


def _nt_dot(a, b, **kw):
    return lax.dot_general(a, b, (((1,), (1,)), ((), ())), preferred_element_type=F32, **kw)


def _inproj_kernel(x_ref, g_ref, wqkv_ref, wz_ref, wxbc_ref, wdt_ref, wdtt_ref, wgate_ref, bgate_ref,
                   qkv_ref, z_ref, xbc_ref, dt_ref, dtt_ref, gates_ref):
    x = x_ref[...]
    u = x * lax.rsqrt(jnp.mean(x * x, axis=-1, keepdims=True) + EPS) * g_ref[...]
    ub = u.astype(BF16)
    qkv_ref[...] = jnp.dot(ub, wqkv_ref[...], preferred_element_type=F32).astype(BF16)
    z_ref[...] = jnp.dot(ub, wz_ref[...], preferred_element_type=F32)
    xbc_ref[...] = jnp.dot(ub, wxbc_ref[...], preferred_element_type=F32)
    dt_ref[...] = jnp.dot(ub, wdt_ref[...], preferred_element_type=F32)
    dtt_ref[...] = _nt_dot(wdtt_ref[...], ub)
    gates_ref[...] = jax.nn.sigmoid(jnp.dot(ub, wgate_ref[...], preferred_element_type=F32) + bgate_ref[...])


def _inproj(x, g_mix, wqkv, wz, wxbc, wdt, wdtt, wgate, bgate, *, tm=256):
    B, L, D = x.shape
    grid = (B, L // tm)
    row = lambda w: pl.BlockSpec((None, tm, w), lambda b, i: (b, i, 0))
    ndt = 2 * SSM_HEADS
    return pl.pallas_call(
        _inproj_kernel,
        grid=grid,
        in_specs=[row(D), _const_spec((1, D)), _const_spec(wqkv.shape), _const_spec(wz.shape),
                  _const_spec(wxbc.shape), _const_spec(wdt.shape), _const_spec(wdtt.shape),
                  _const_spec(wgate.shape), _const_spec((1, 2 * D))],
        out_specs=[row(3 * ATT_W), row(SSM_INNER), row(CONV_CH), row(ndt),
                   pl.BlockSpec((None, ndt, tm), lambda b, i: (b, 0, i)), row(2 * D)],
        out_shape=[jax.ShapeDtypeStruct((B, L, 3 * ATT_W), BF16),
                   jax.ShapeDtypeStruct((B, L, SSM_INNER), F32),
                   jax.ShapeDtypeStruct((B, L, CONV_CH), F32),
                   jax.ShapeDtypeStruct((B, L, ndt), F32),
                   jax.ShapeDtypeStruct((B, ndt, L), F32),
                   jax.ShapeDtypeStruct((B, L, 2 * D), F32)],
        compiler_params=_params(("parallel", "parallel")),
        name="inproj",
    )(x, g_mix, wqkv, wz, wxbc, wdt, wdtt, wgate, bgate)


def _rel_bucket(rel):
    half = REL_BUCKETS // 2
    max_exact = half // 2
    ret = jnp.where(rel > 0, half, 0)
    n = jnp.abs(rel)
    nf = jnp.maximum(n, 1).astype(jnp.float32)
    large = max_exact + (jnp.log(nf / max_exact) / math.log(REL_MAX_DIST / max_exact)
                         * (half - max_exact)).astype(jnp.int32)
    large = jnp.minimum(large, half - 1)
    return ret + jnp.where(n < max_exact, n, large)


def _bias_kernel(table_ref, bucket_ref, out_ref, *, T):
    h = pl.program_id(0)
    out_ref[...] = jnp.zeros(out_ref.shape, F32)
    for d in range(5):
        for m in range(2):
            rows = pl.ds(m * T, T)

            def body(b, carry, d=d, m=m, rows=rows):
                val = table_ref[b, 2 * h + m]
                out_ref[d, rows, :] = jnp.where(bucket_ref[d] == b, val, out_ref[d, rows, :])
                return carry

            lax.fori_loop(0, REL_BUCKETS, body, 0)


def _bias_tiles(rel_bias, T):
    i = jnp.arange(T, dtype=jnp.int32)
    base = i[None, :] - i[:, None]
    bucket = jnp.stack([_rel_bucket(base + (d - 2) * T) for d in range(5)])
    return pl.pallas_call(
        functools.partial(_bias_kernel, T=T),
        grid=(N_HEADS,),
        in_specs=[pl.BlockSpec(memory_space=pltpu.SMEM), _const_spec((5, T, T))],
        out_specs=pl.BlockSpec((None, 5, 2 * T, T), lambda h: (h, 0, 0, 0)),
        out_shape=jax.ShapeDtypeStruct((N_HEADS, 5, 2 * T, T), F32),
        compiler_params=_params(("parallel",)),
        name="rel_bias_tiles",
    )(rel_bias.astype(F32), bucket)


def _attn_kernel(lam_ref, q_ref, k_ref, v_ref, bt_ref, g_ref, o_ref, m_sc, l_sc, acc_sc, *, T, nk):
    i = pl.program_id(2)
    q = q_ref[...]
    lane = lax.broadcasted_iota(jnp.int32, q.shape, 1)
    zero = jnp.zeros_like(q)
    q2 = jnp.concatenate([jnp.where(lane < QK_DIM, q, zero), jnp.where(lane >= QK_DIM, q, zero)], axis=0)
    m_sc[...] = jnp.full(m_sc.shape, -jnp.inf, F32)
    l_sc[...] = jnp.zeros(l_sc.shape, F32)
    acc_sc[...] = jnp.zeros(acc_sc.shape, F32)

    def body(j, carry):
        start = pl.multiple_of(j * T, T)
        kb = k_ref[pl.ds(start, T), :]
        vb = v_ref[pl.ds(start, T), :]
        s = _nt_dot(q2, kb) + bt_ref[jnp.clip(j - i, -2, 2) + 2]
        m_prev = m_sc[...]
        m_new = jnp.maximum(m_prev, jnp.max(s, axis=-1, keepdims=True))
        alpha = jnp.exp(m_prev - m_new)
        p = jnp.exp(s - m_new)
        l_sc[...] = alpha * l_sc[...] + jnp.sum(p, axis=-1, keepdims=True)
        acc_sc[...] = alpha * acc_sc[...] + jnp.dot(p.astype(BF16), vb, preferred_element_type=F32)
        m_sc[...] = m_new
        return carry

    lax.fori_loop(0, nk, body, 0)
    o = acc_sc[...] / l_sc[...]
    od = o[:T] - lam_ref[0] * o[T:]
    y = od * lax.rsqrt(jnp.mean(od * od, axis=-1, keepdims=True) + EPS) * g_ref[...]
    o_ref[...] = (y * (1.0 - LAM_INIT)).astype(o_ref.dtype)


def _attention(qkv, btiles, lam, g_subln, *, T=ATT_BLOCK):
    B, L, _ = qkv.shape
    nk = L // T
    grid = (B, N_HEADS, nk)
    return pl.pallas_call(
        functools.partial(_attn_kernel, T=T, nk=nk),
        grid=grid,
        in_specs=[pl.BlockSpec(memory_space=pltpu.SMEM),
                  pl.BlockSpec((None, T, HEAD_W), lambda b, h, i: (b, i, h)),
                  pl.BlockSpec((None, L, HEAD_W), lambda b, h, i: (b, 0, N_HEADS + h)),
                  pl.BlockSpec((None, L, HEAD_W), lambda b, h, i: (b, 0, 2 * N_HEADS + h)),
                  pl.BlockSpec((None, 5, 2 * T, T), lambda b, h, i: (h, 0, 0, 0)),
                  _const_spec((1, HEAD_W))],
        out_specs=pl.BlockSpec((None, T, HEAD_W), lambda b, h, i: (b, i, h)),
        out_shape=jax.ShapeDtypeStruct((B, L, ATT_W), BF16),
        scratch_shapes=[pltpu.VMEM((2 * T, 1), F32), pltpu.VMEM((2 * T, 1), F32),
                        pltpu.VMEM((2 * T, HEAD_W), F32)],
        compiler_params=_params(("parallel", "parallel", "parallel")),
        name="diff_attention",
    )(lam, qkv, qkv, qkv, btiles, g_subln)


def _conv_kernel(x_ref, prev_ref, next_ref, dt_ref, dtt_ref, w_ref, b_ref, dtb_ref, dtbt_ref,
                 xs_ref, bc_ref, dts_ref, dtst_ref, *, tl, nt):
    i = pl.program_id(1)
    x = x_ref[...]
    row = lax.broadcasted_iota(jnp.int32, x.shape, 0)
    prev_row = jnp.where(i > 0, prev_ref[7:8, :], 0.0)
    next_row = jnp.where(i < nt - 1, next_ref[0:1, :], 0.0)
    xm1 = jnp.where(row == 0, prev_row, pltpu.roll(x, 1, 0))
    xp1 = jnp.where(row == tl - 1, next_row, pltpu.roll(x, tl - 1, 0))
    y = w_ref[0:1, :] * xm1 + w_ref[1:2, :] * x + w_ref[2:3, :] * xp1 + b_ref[...]
    y = y * jax.nn.sigmoid(y)
    xs_ref[...] = y[:, :SSM_INNER].astype(BF16)
    bc_ref[...] = y[:, SSM_INNER:].astype(BF16)

    def softplus(v):
        return jnp.maximum(v, 0.0) + jnp.log1p(jnp.exp(-jnp.abs(v)))

    dts_ref[...] = softplus(dt_ref[...] + dtb_ref[...])
    dtst_ref[...] = softplus(dtt_ref[...] + dtbt_ref[...])


def _conv(xbc, dt, dtt, conv_w, conv_b, dtb, dtbt, *, tl=512):
    B, L, _ = xbc.shape
    nt = L // tl
    r8 = tl // 8
    ndt = 2 * SSM_HEADS
    row = lambda w: pl.BlockSpec((None, tl, w), lambda b, i: (b, i, 0))
    colspec = pl.BlockSpec((None, ndt, tl), lambda b, i: (b, 0, i))
    return pl.pallas_call(
        functools.partial(_conv_kernel, tl=tl, nt=nt),
        grid=(B, nt),
        in_specs=[row(CONV_CH),
                  pl.BlockSpec((None, 8, CONV_CH), lambda b, i: (b, jnp.maximum(i * r8 - 1, 0), 0)),
                  pl.BlockSpec((None, 8, CONV_CH), lambda b, i: (b, jnp.minimum((i + 1) * r8, L // 8 - 1), 0)),
                  row(ndt), colspec,
                  _const_spec((3, CONV_CH)), _const_spec((1, CONV_CH)), _const_spec((1, ndt)),
                  _const_spec((ndt, 1))],
        out_specs=[row(SSM_INNER), row(2 * SSM_GROUPS * SSM_N), row(ndt), colspec],
        out_shape=[jax.ShapeDtypeStruct((B, L, SSM_INNER), BF16),
                   jax.ShapeDtypeStruct((B, L, 2 * SSM_GROUPS * SSM_N), BF16),
                   jax.ShapeDtypeStruct((B, L, ndt), F32),
                   jax.ShapeDtypeStruct((B, ndt, L), F32)],
        compiler_params=_params(("parallel", "parallel")),
        name="ssm_conv",
    )(xbc, xbc, xbc, dt, dtt, conv_w, conv_b, dtb, dtbt)


def _ssd_chunk(xs_ref, bc_ref, dts_ref, dtst_ref, alog_ref, alogt_ref, st_ref, *, rev):
    Q = SSM_CHUNK
    HI = lax.Precision.HIGHEST
    off = SSM_HEADS if rev else 0
    ri = lax.broadcasted_iota(jnp.int32, (Q, Q), 0)
    ci = lax.broadcasted_iota(jnp.int32, (Q, Q), 1)
    incl = (ci >= ri) if rev else (ci <= ri)
    tri = incl.astype(F32)
    dt = dts_ref[:, off:off + SSM_HEADS]
    dtt = dtst_ref[off:off + SSM_HEADS, :]
    a = dt * (-jnp.exp(alog_ref[:, off:off + SSM_HEADS]))
    at = dtt * (-jnp.exp(alogt_ref[off:off + SSM_HEADS, :]))
    acum = jnp.dot(tri, a, preferred_element_type=F32, precision=HI)
    acumt = _nt_dot(at, tri, precision=HI)
    edge = 0 if rev else Q - 1
    bc = bc_ref[...]
    bpair = bc[:, :Q]
    cpair = bc[:, Q:]
    low = ci < SSM_N
    rlow = ri < SSM_N
    blockmask = low == rlow
    ys = []
    for g in range(SSM_GROUPS):
        gmask = low if g == 0 else jnp.logical_not(low)
        cg = jnp.where(gmask, cpair, jnp.zeros_like(cpair))
        cb = _nt_dot(cg, bpair)
        dupm = ((ri - g * SSM_N) == (ci % SSM_N)).astype(BF16)
        cdup = jnp.dot(cpair, dupm, preferred_element_type=F32)
        sel2 = ((ci - g * SSM_N) == (ri % SSM_N)).astype(BF16)
        bt2 = _nt_dot(sel2, bpair)
        for hp in range(SSM_HEADS // SSM_GROUPS // 2):
            h0 = g * (SSM_HEADS // SSM_GROUPS) + 2 * hp
            pair = h0 // 2
            xpair = xs_ref[:, h0 * SSM_P:(h0 + 2) * SSM_P]
            ms = []
            es = []
            ws = []
            for h in (h0, h0 + 1):
                e_col = jnp.broadcast_to(acum[:, h:h + 1], (Q, Q))
                r_row = jnp.broadcast_to(acumt[h:h + 1, :], (Q, Q))
                dt_row = jnp.broadcast_to(dtt[h:h + 1, :], (Q, Q))
                seg = jnp.where(incl, e_col - r_row, -jnp.inf)
                ms.append((cb * jnp.exp(seg) * dt_row).astype(BF16))
                es.append(jnp.exp(e_col))
                ws.append(dt_row * jnp.exp(acumt[h:h + 1, edge:edge + 1] - r_row))
            st = st_ref[pair]
            xblk = jnp.concatenate([jnp.where(low, xpair, jnp.zeros_like(xpair)),
                                    jnp.where(low, jnp.zeros_like(xpair), xpair)], axis=0)
            y = jnp.dot(jnp.concatenate(ms, axis=1), xblk, preferred_element_type=F32)
            ce = (cdup * jnp.where(low, es[0], es[1])).astype(BF16)
            y = y + jnp.dot(ce, st.astype(BF16), preferred_element_type=F32)
            ys.append(y)
            btw = (bt2 * jnp.where(rlow, ws[0], ws[1])).astype(BF16)
            snew = jnp.dot(btw, xpair, preferred_element_type=F32)
            dec = jnp.where(rlow, jnp.exp(acumt[h0:h0 + 1, edge:edge + 1]),
                            jnp.exp(acumt[h0 + 1:h0 + 2, edge:edge + 1]))
            st_ref[pair] = dec * st + jnp.where(blockmask, snew, 0.0)
    return jnp.concatenate(ys, axis=1)


def _ssd_fwd_kernel(xs_ref, bc_ref, dts_ref, dtst_ref, alog_ref, alogt_ref, y_ref, st_ref):
    @pl.when(pl.program_id(1) == 0)
    def _():
        st_ref[...] = jnp.zeros(st_ref.shape, F32)

    y_ref[...] = _ssd_chunk(xs_ref, bc_ref, dts_ref, dtst_ref, alog_ref, alogt_ref, st_ref, rev=False)


def _ssd_bwd_kernel(xs_ref, bc_ref, dts_ref, dtst_ref, alog_ref, alogt_ref, yf_ref, z_ref, dskip_ref, g_ref,
                    o_ref, st_ref):
    @pl.when(pl.program_id(1) == 0)
    def _():
        st_ref[...] = jnp.zeros(st_ref.shape, F32)

    yb = _ssd_chunk(xs_ref, bc_ref, dts_ref, dtst_ref, alog_ref, alogt_ref, st_ref, rev=True)
    z = z_ref[...]
    y = (yf_ref[...] + yb + xs_ref[...].astype(F32) * dskip_ref[...]) * (z * jax.nn.sigmoid(z))
    y = y * lax.rsqrt(jnp.mean(y * y, axis=-1, keepdims=True) + EPS) * g_ref[...]
    o_ref[...] = y.astype(o_ref.dtype)


def _ssd(xs, bc, dts, dtst, alog, alogt, z, dskip, g_norm):
    B, L, _ = xs.shape
    Q = SSM_CHUNK
    nc = L // Q
    ndt = 2 * SSM_HEADS
    npair = SSM_HEADS // 2

    def specs(cidx):
        row = lambda w: pl.BlockSpec((None, Q, w), lambda b, c: (b, cidx(c), 0))
        return row, [row(SSM_INNER), row(2 * SSM_GROUPS * SSM_N), row(ndt),
                     pl.BlockSpec((None, ndt, Q), lambda b, c: (b, 0, cidx(c))),
                     _const_spec((1, ndt)), _const_spec((ndt, 1))]

    scratch = [pltpu.VMEM((npair, 2 * SSM_N, 2 * SSM_P), F32)]
    row, in_specs = specs(lambda c: c)
    yf = pl.pallas_call(
        _ssd_fwd_kernel,
        grid=(B, nc),
        in_specs=in_specs,
        out_specs=row(SSM_INNER),
        out_shape=jax.ShapeDtypeStruct((B, L, SSM_INNER), F32),
        scratch_shapes=scratch,
        compiler_params=_params(("parallel", "arbitrary")),
        name="ssd_forward",
    )(xs, bc, dts, dtst, alog, alogt)
    row, in_specs = specs(lambda c: nc - 1 - c)
    return pl.pallas_call(
        _ssd_bwd_kernel,
        grid=(B, nc),
        in_specs=in_specs + [row(SSM_INNER), row(SSM_INNER), _const_spec((1, SSM_INNER)),
                             _const_spec((1, SSM_INNER))],
        out_specs=row(SSM_INNER),
        out_shape=jax.ShapeDtypeStruct((B, L, SSM_INNER), BF16),
        scratch_shapes=scratch,
        compiler_params=_params(("parallel", "arbitrary")),
        name="ssd_reverse_gate_norm",
    )(xs, bc, dts, dtst, alog, alogt, yf, z, dskip, g_norm)


def _merge_kernel(att_ref, ssm_ref, gates_ref, x_ref, wau_ref, wsu_ref, wout_ref, gffn_ref, wr_ref,
                  h_ref, u2_ref, aff_ref):
    D = D_MODEL
    ya = jnp.dot(att_ref[...], wau_ref[...], preferred_element_type=F32)
    ysm = jnp.dot(ssm_ref[...], wsu_ref[...], preferred_element_type=F32)
    merged = gates_ref[:, :D] * ya + gates_ref[:, D:] * ysm
    h = x_ref[...] + jnp.dot(merged.astype(BF16), wout_ref[...], preferred_element_type=F32)
    h_ref[...] = h
    u2 = (h * lax.rsqrt(jnp.mean(h * h, axis=-1, keepdims=True) + EPS) * gffn_ref[...]).astype(BF16)
    u2_ref[...] = u2
    logits = jnp.dot(u2, wr_ref[...], preferred_element_type=F32)
    e = jnp.exp(logits - jnp.max(logits, axis=-1, keepdims=True))
    aff_ref[...] = e / jnp.sum(e, axis=-1, keepdims=True)


def _merge(att, ssm, gates, x, wau, wsu, wout, g_ffn, wr, *, tm=256):
    B, L, D = x.shape
    row = lambda w: pl.BlockSpec((None, tm, w), lambda b, i: (b, i, 0))
    return pl.pallas_call(
        _merge_kernel,
        grid=(B, L // tm),
        in_specs=[row(ATT_W), row(SSM_INNER), row(2 * D), row(D), _const_spec(wau.shape),
                  _const_spec(wsu.shape), _const_spec(wout.shape), _const_spec((1, D)), _const_spec(wr.shape)],
        out_specs=[row(D), row(D), row(N_EXPERTS)],
        out_shape=[jax.ShapeDtypeStruct((B, L, D), F32), jax.ShapeDtypeStruct((B, L, D), BF16),
                   jax.ShapeDtypeStruct((B, L, N_EXPERTS), F32)],
        compiler_params=_params(("parallel", "parallel")),
        name="merge_router",
    )(att, ssm, gates, x, wau, wsu, wout, g_ffn, wr)


def _expert_kernel(xs_ref, gate_ref, wg_ref, wu_ref, wd_ref, o_ref):
    xs = xs_ref[...]
    hg = jnp.dot(xs, wg_ref[...], preferred_element_type=F32)
    hu = jnp.dot(xs, wu_ref[...], preferred_element_type=F32)
    hmid = (hg * jax.nn.sigmoid(hg) * hu).astype(BF16)
    o_ref[...] = jnp.dot(hmid, wd_ref[...], preferred_element_type=F32) * gate_ref[...]


def _experts(xs, gate, wg, wu, wd, *, tc=512):
    E, C, D = xs.shape
    tc = min(tc, C)
    wspec = pl.BlockSpec((None, D, D), lambda e, i: (e, 0, 0))
    return pl.pallas_call(
        _expert_kernel,
        grid=(E, C // tc),
        in_specs=[pl.BlockSpec((None, tc, D), lambda e, i: (e, i, 0)),
                  pl.BlockSpec((None, tc, 1), lambda e, i: (e, i, 0)), wspec, wspec, wspec],
        out_specs=pl.BlockSpec((None, tc, D), lambda e, i: (e, i, 0)),
        out_shape=jax.ShapeDtypeStruct((E, C, D), F32),
        compiler_params=_params(("parallel", "parallel")),
        name="experts",
    )(xs, gate, wg, wu, wd)


def _ple_kernel(h_ref, moe_ref, p_ref, wpg_ref, bpg_ref, wpp_ref, gfin_ref, o_ref):
    h = h_ref[...] + moe_ref[...]
    gate = jax.nn.sigmoid(jnp.dot(h.astype(BF16), wpg_ref[...], preferred_element_type=F32) + bpg_ref[...])
    h = h + gate * jnp.dot(p_ref[...].astype(BF16), wpp_ref[...], preferred_element_type=F32)
    o_ref[...] = h * lax.rsqrt(jnp.mean(h * h, axis=-1, keepdims=True) + EPS) * gfin_ref[...]


def _ple(h, moe, p, wpg, bpg, wpp, g_final, *, tm=256):
    B, L, D = h.shape
    row = lambda w: pl.BlockSpec((None, tm, w), lambda b, i: (b, i, 0))
    return pl.pallas_call(
        _ple_kernel,
        grid=(B, L // tm),
        in_specs=[row(D), row(D), row(PLE_DIM), _const_spec(wpg.shape), _const_spec((1, D)),
                  _const_spec(wpp.shape), _const_spec((1, D))],
        out_specs=row(D),
        out_shape=jax.ShapeDtypeStruct((B, L, D), F32),
        compiler_params=_params(("parallel", "parallel")),
        name="ple_final_norm",
    )(h, moe, p, wpg, bpg, wpp, g_final)


def _forward(x, p, w, btiles):
    B, L, D = x.shape
    T = B * L
    qkv, z, xbc, dt, dtt, gates = _inproj(x, w['g_mix'], w['wqkv'], w['wz'], w['wxbc'], w['wdt'], w['wdtt'],
                                          w['wgate'], w['bgate'])
    att = _attention(qkv, btiles, w['lam'], w['g_subln'])
    xs, bc, dts, dtst = _conv(xbc, dt, dtt, w['conv_w'], w['conv_b'], w['dtb'], w['dtbt'])
    ssm = _ssd(xs, bc, dts, dtst, w['alog'], w['alogt'], z, w['dskip'], w['g_ssm_norm'])
    h, u2, aff = _merge(att, ssm, gates, x, w['wau'], w['wsu'], w['wout'], w['g_ffn'], w['wr'])
    cap = max(1, CAPACITY_FACTOR * T // N_EXPERTS)
    gate, idx = lax.top_k(aff.reshape(T, N_EXPERTS).T, cap)
    xsel = u2.reshape(T, D)[idx]
    ye = _experts(xsel, gate[..., None], w['weg'], w['weu'], w['wed'])
    moe = jnp.zeros((T, D), F32).at[idx.reshape(-1)].add(ye.reshape(-1, D)).reshape(B, L, D)
    return _ple(h, moe, p, w['wpg'], w['bpg'], w['wpp'], w['g_final'])


def kernel(x_prompt, x_sample, p_prompt, p_sample, rel_bias, g_mix, w_in, conv_w, conv_b, dt_bias, a_log,
           d_skip, g_ssm_norm, lam_params, g_subln, w_att_up, w_ssm_up, w_gate, b_gate, w_out, g_ffn,
           w_router, w_exp_gate, w_exp_up, w_exp_down, w_ple_gate, b_ple_gate, w_ple_proj, g_final):
    i = 0
    D = D_MODEL
    win = w_in[i]
    o1 = ATT_W
    o3 = 3 * ATT_W
    o4 = o3 + SSM_INNER
    o5 = o4 + CONV_CH
    wqkv = jnp.concatenate([win[:, :o1] * (QK_DIM ** -0.5), win[:, o1:o3]], axis=1).astype(BF16)
    wdt = win[:, o5:].astype(BF16)
    lp = lam_params[i].astype(F32)
    lam = jnp.exp(jnp.sum(lp[0] * lp[1])) - jnp.exp(jnp.sum(lp[2] * lp[3])) + LAM_INIT
    ndt = 2 * SSM_HEADS
    w = {
        'g_mix': g_mix[i].reshape(1, D), 'wqkv': wqkv, 'wz': win[:, o3:o4].astype(BF16),
        'wxbc': win[:, o4:o5].astype(BF16), 'wdt': wdt, 'wdtt': wdt.T,
        'wgate': w_gate[i].astype(BF16), 'bgate': b_gate[i].reshape(1, 2 * D),
        'lam': lam.reshape(1), 'g_subln': g_subln[i].reshape(1, HEAD_W),
        'conv_w': conv_w[i], 'conv_b': conv_b[i].reshape(1, CONV_CH),
        'dtb': dt_bias[i].reshape(1, ndt), 'dtbt': dt_bias[i].reshape(ndt, 1),
        'alog': a_log[i].reshape(1, ndt), 'alogt': a_log[i].reshape(ndt, 1),
        'dskip': jnp.repeat(d_skip[i], SSM_P).reshape(1, SSM_INNER),
        'g_ssm_norm': g_ssm_norm[i].reshape(1, SSM_INNER),
        'wau': w_att_up[i].astype(BF16), 'wsu': w_ssm_up[i].astype(BF16), 'wout': w_out[i].astype(BF16),
        'g_ffn': g_ffn[i].reshape(1, D), 'wr': w_router[i].astype(BF16),
        'weg': w_exp_gate[i].astype(BF16), 'weu': w_exp_up[i].astype(BF16), 'wed': w_exp_down[i].astype(BF16),
        'wpg': w_ple_gate[i].astype(BF16), 'bpg': b_ple_gate[i].reshape(1, D),
        'wpp': w_ple_proj[i].astype(BF16), 'g_final': g_final.reshape(1, D),
    }
    btiles = _bias_tiles(rel_bias, ATT_BLOCK)
    return (_forward(x_prompt, p_prompt[i], w, btiles), _forward(x_sample, p_sample[i], w, btiles))
```

```python
import functools
import math

import jax
import jax.numpy as jnp
from jax import lax
from jax.experimental import pallas as pl
from jax.experimental.pallas import tpu as pltpu

F32 = jnp.float32
BF16 = jnp.bfloat16

D_MODEL = 1024
N_HEADS = 4
QK_DIM = 64
HEAD_W = 2 * QK_DIM
ATT_W = N_HEADS * HEAD_W
REL_BUCKETS = 32
REL_MAX_DIST = 128
SSM_INNER = 1024
SSM_P = 64
SSM_HEADS = 16
SSM_GROUPS = 2
SSM_N = 64
SSM_CHUNK = 128
CONV_CH = SSM_INNER + 2 * SSM_GROUPS * SSM_N
N_EXPERTS = 16
CAPACITY_FACTOR = 2
PLE_DIM = 256
EPS = 1e-6
LAM_INIT = 0.8 - 0.6 * math.exp(-0.3 * 0)
LOG2E = math.log2(math.e)

ATT_BLOCK = 256
VMEM_LIMIT = 56 * 1024 * 1024


def _params(sem, vmem=VMEM_LIMIT):
    return pltpu.CompilerParams(dimension_semantics=sem, vmem_limit_bytes=vmem)


def _const_spec(shape):
    nd = len(shape)
    return pl.BlockSpec(shape, lambda *_: (0,) * nd)


def _nt_dot(a, b, **kw):
    return lax.dot_general(a, b, (((1,), (1,)), ((), ())), preferred_element_type=F32, **kw)


def _inproj_kernel(x_ref, g_ref, wqt_ref, wk_ref, wvt_ref, wz_ref, wxbc_ref, wdt_ref, wdtt_ref, wgate_ref,
                   bgate_ref, qt_ref, k_ref, vt_ref, z_ref, xbc_ref, dt_ref, dtt_ref, gates_ref):
    x = x_ref[...]
    u = x * lax.rsqrt(jnp.mean(x * x, axis=-1, keepdims=True) + EPS) * g_ref[...]
    ub = u.astype(BF16)
    qt_ref[...] = _nt_dot(wqt_ref[...], ub).astype(BF16)
    k_ref[...] = jnp.dot(ub, wk_ref[...], preferred_element_type=F32).astype(BF16)
    vt_ref[...] = _nt_dot(wvt_ref[...], ub).astype(BF16)
    z_ref[...] = jnp.dot(ub, wz_ref[...], preferred_element_type=F32)
    xbc_ref[...] = jnp.dot(ub, wxbc_ref[...], preferred_element_type=F32)
    dt_ref[...] = jnp.dot(ub, wdt_ref[...], preferred_element_type=F32)
    dtt_ref[...] = _nt_dot(wdtt_ref[...], ub)
    gates_ref[...] = jax.nn.sigmoid(jnp.dot(ub, wgate_ref[...], preferred_element_type=F32) + bgate_ref[...])


def _inproj(x, g_mix, wqt, wk, wvt, wz, wxbc, wdt, wdtt, wgate, bgate, *, tm=256):
    B, L, D = x.shape
    grid = (B, L // tm)
    row = lambda w: pl.BlockSpec((None, tm, w), lambda b, i: (b, i, 0))
    col = lambda w: pl.BlockSpec((None, w, tm), lambda b, i: (b, 0, i))
    ndt = 2 * SSM_HEADS
    return pl.pallas_call(
        _inproj_kernel,
        grid=grid,
        in_specs=[row(D), _const_spec((1, D)), _const_spec(wqt.shape), _const_spec(wk.shape),
                  _const_spec(wvt.shape), _const_spec(wz.shape),
                  _const_spec(wxbc.shape), _const_spec(wdt.shape), _const_spec(wdtt.shape),
                  _const_spec(wgate.shape), _const_spec((1, 2 * D))],
        out_specs=[col(ATT_W), row(ATT_W), col(ATT_W), row(SSM_INNER), row(CONV_CH), row(ndt),
                   col(ndt), row(2 * D)],
        out_shape=[jax.ShapeDtypeStruct((B, ATT_W, L), BF16),
                   jax.ShapeDtypeStruct((B, L, ATT_W), BF16),
                   jax.ShapeDtypeStruct((B, ATT_W, L), BF16),
                   jax.ShapeDtypeStruct((B, L, SSM_INNER), F32),
                   jax.ShapeDtypeStruct((B, L, CONV_CH), F32),
                   jax.ShapeDtypeStruct((B, L, ndt), F32),
                   jax.ShapeDtypeStruct((B, ndt, L), F32),
                   jax.ShapeDtypeStruct((B, L, 2 * D), F32)],
        compiler_params=_params(("parallel", "parallel")),
        name="inproj",
    )(x, g_mix, wqt, wk, wvt, wz, wxbc, wdt, wdtt, wgate, bgate)


def _rel_bucket(rel):
    half = REL_BUCKETS // 2
    max_exact = half // 2
    ret = jnp.where(rel > 0, half, 0)
    n = jnp.abs(rel)
    nf = jnp.maximum(n, 1).astype(jnp.float32)
    large = max_exact + (jnp.log(nf / max_exact) / math.log(REL_MAX_DIST / max_exact)
                         * (half - max_exact)).astype(jnp.int32)
    large = jnp.minimum(large, half - 1)
    return ret + jnp.where(n < max_exact, n, large)


def _bias_kernel(table_ref, bucket_ref, out_ref, *, T):
    h = pl.program_id(0)
    out_ref[...] = jnp.zeros(out_ref.shape, F32)
    for d in range(5):
        for m in range(2):
            cols = pl.ds(m * T, T)

            def body(b, carry, d=d, m=m, cols=cols):
                val = table_ref[b, 2 * h + m] * LOG2E
                out_ref[d, :, cols] = jnp.where(bucket_ref[d] == b, val, out_ref[d, :, cols])
                return carry

            lax.fori_loop(0, REL_BUCKETS, body, 0)


def _bias_tiles(rel_bias, T):
    i = jnp.arange(T, dtype=jnp.int32)
    base = i[:, None] - i[None, :]
    bucket = jnp.stack([_rel_bucket(base + (d - 2) * T) for d in range(5)])
    return pl.pallas_call(
        functools.partial(_bias_kernel, T=T),
        grid=(N_HEADS,),
        in_specs=[pl.BlockSpec(memory_space=pltpu.SMEM), _const_spec((5, T, T))],
        out_specs=pl.BlockSpec((None, 5, T, 2 * T), lambda h: (h, 0, 0, 0)),
        out_shape=jax.ShapeDtypeStruct((N_HEADS, 5, T, 2 * T), F32),
        compiler_params=_params(("parallel",)),
        name="rel_bias_tiles",
    )(rel_bias.astype(F32), bucket)


def _attn_kernel(lam_ref, qt_ref, k_ref, vt_ref, bt_ref, g_ref, o_ref, acc_sc, s_sc, p_sc, *, T, nk):
    i = pl.program_id(2)
    qt = qt_ref[...]
    sub = lax.broadcasted_iota(jnp.int32, qt.shape, 0)
    zero = jnp.zeros_like(qt)
    q2t = jnp.concatenate([jnp.where(sub < QK_DIM, qt, zero), jnp.where(sub >= QK_DIM, qt, zero)], axis=1)
    acc_sc[...] = jnp.zeros(acc_sc.shape, F32)

    def scores(j, slot):
        kb = k_ref[pl.ds(pl.multiple_of(j * T, T), T), :]
        s_sc[slot] = jnp.dot(kb, q2t, preferred_element_type=F32) + bt_ref[jnp.clip(j - i, -2, 2) + 2]

    def softmax(slot, m, l):
        s = s_sc[slot]
        m_new = jnp.maximum(m, jnp.max(s, axis=0, keepdims=True))
        alpha = jnp.exp2(m - m_new)
        p = jnp.exp2(s - m_new)
        p_sc[slot] = p.astype(BF16)
        return m_new, alpha * l + jnp.sum(p, axis=0, keepdims=True), alpha

    def accumulate(j, slot, alpha):
        vb = vt_ref[:, pl.ds(pl.multiple_of(j * T, T), T)]
        acc_sc[...] = alpha * acc_sc[...] + jnp.dot(vb, p_sc[slot], preferred_element_type=F32)

    m = jnp.full((1, 2 * T), -jnp.inf, F32)
    l = jnp.zeros((1, 2 * T), F32)
    scores(0, 0)
    scores(1, 1)
    m, l, alpha = softmax(0, m, l)

    def pair(u, carry):
        m, l, alpha = carry
        t = 2 * u
        scores(t, 0)
        m, l, alpha1 = softmax(1, m, l)
        accumulate(t - 2, 0, alpha)
        scores(t + 1, 1)
        m, l, alpha0 = softmax(0, m, l)
        accumulate(t - 1, 1, alpha1)
        return m, l, alpha0

    m, l, alpha = lax.fori_loop(1, nk // 2, pair, (m, l, alpha))
    m, l, alpha1 = softmax(1, m, l)
    accumulate(nk - 2, 0, alpha)
    accumulate(nk - 1, 1, alpha1)

    o = acc_sc[...] / l
    od = o[:, :T] - lam_ref[0] * o[:, T:]
    y = od * lax.rsqrt(jnp.mean(od * od, axis=0, keepdims=True) + EPS) * g_ref[...]
    o_ref[...] = (y * (1.0 - LAM_INIT)).T.astype(o_ref.dtype)


def _attention(qt, k, vt, btiles, lam, g_col, *, T=ATT_BLOCK):
    B, L, _ = k.shape
    nk = L // T
    assert nk % 2 == 0 and nk >= 4
    return pl.pallas_call(
        functools.partial(_attn_kernel, T=T, nk=nk),
        grid=(B, N_HEADS, nk),
        in_specs=[pl.BlockSpec(memory_space=pltpu.SMEM),
                  pl.BlockSpec((None, HEAD_W, T), lambda b, h, i: (b, h, i)),
                  pl.BlockSpec((None, L, HEAD_W), lambda b, h, i: (b, 0, h)),
                  pl.BlockSpec((None, HEAD_W, L), lambda b, h, i: (b, h, 0)),
                  pl.BlockSpec((None, 5, T, 2 * T), lambda b, h, i: (h, 0, 0, 0)),
                  _const_spec((HEAD_W, 1))],
        out_specs=pl.BlockSpec((None, T, HEAD_W), lambda b, h, i: (b, i, h)),
        out_shape=jax.ShapeDtypeStruct((B, L, ATT_W), BF16),
        scratch_shapes=[pltpu.VMEM((HEAD_W, 2 * T), F32), pltpu.VMEM((2, T, 2 * T), F32),
                        pltpu.VMEM((2, T, 2 * T), BF16)],
        compiler_params=_params(("parallel", "parallel", "parallel")),
        name="diff_attention",
    )(lam, qt, k, vt, btiles, g_col)


def _conv_kernel(x_ref, prev_ref, next_ref, dt_ref, dtt_ref, w_ref, b_ref, dtb_ref, dtbt_ref,
                 xs_ref, bc_ref, dts_ref, dtst_ref, *, tl, nt):
    i = pl.program_id(1)
    x = x_ref[...]
    row = lax.broadcasted_iota(jnp.int32, x.shape, 0)
    prev_row = jnp.where(i > 0, prev_ref[7:8, :], 0.0)
    next_row = jnp.where(i < nt - 1, next_ref[0:1, :], 0.0)
    xm1 = jnp.where(row == 0, prev_row, pltpu.roll(x, 1, 0))
    xp1 = jnp.where(row == tl - 1, next_row, pltpu.roll(x, tl - 1, 0))
    y = w_ref[0:1, :] * xm1 + w_ref[1:2, :] * x + w_ref[2:3, :] * xp1 + b_ref[...]
    y = y * jax.nn.sigmoid(y)
    xs_ref[...] = y[:, :SSM_INNER].astype(BF16)
    bc_ref[...] = y[:, SSM_INNER:].astype(BF16)

    def softplus(v):
        return jnp.maximum(v, 0.0) + jnp.log1p(jnp.exp(-jnp.abs(v)))

    dts_ref[...] = softplus(dt_ref[...] + dtb_ref[...])
    dtst_ref[...] = softplus(dtt_ref[...] + dtbt_ref[...])


def _conv(xbc, dt, dtt, conv_w, conv_b, dtb, dtbt, *, tl=512):
    B, L, _ = xbc.shape
    nt = L // tl
    r8 = tl // 8
    ndt = 2 * SSM_HEADS
    row = lambda w: pl.BlockSpec((None, tl, w), lambda b, i: (b, i, 0))
    colspec = pl.BlockSpec((None, ndt, tl), lambda b, i: (b, 0, i))
    return pl.pallas_call(
        functools.partial(_conv_kernel, tl=tl, nt=nt),
        grid=(B, nt),
        in_specs=[row(CONV_CH),
                  pl.BlockSpec((None, 8, CONV_CH), lambda b, i: (b, jnp.maximum(i * r8 - 1, 0), 0)),
                  pl.BlockSpec((None, 8, CONV_CH), lambda b, i: (b, jnp.minimum((i + 1) * r8, L // 8 - 1), 0)),
                  row(ndt), colspec,
                  _const_spec((3, CONV_CH)), _const_spec((1, CONV_CH)), _const_spec((1, ndt)),
                  _const_spec((ndt, 1))],
        out_specs=[row(SSM_INNER), row(2 * SSM_GROUPS * SSM_N), row(ndt), colspec],
        out_shape=[jax.ShapeDtypeStruct((B, L, SSM_INNER), BF16),
                   jax.ShapeDtypeStruct((B, L, 2 * SSM_GROUPS * SSM_N), BF16),
                   jax.ShapeDtypeStruct((B, L, ndt), F32),
                   jax.ShapeDtypeStruct((B, ndt, L), F32)],
        compiler_params=_params(("parallel", "parallel")),
        name="ssm_conv",
    )(xbc, xbc, xbc, dt, dtt, conv_w, conv_b, dtb, dtbt)


def _ssd_chunk(xs_ref, bc_ref, dts_ref, dtst_ref, alog_ref, alogt_ref, st_ref, *, rev):
    Q = SSM_CHUNK
    HI = lax.Precision.HIGHEST
    off = SSM_HEADS if rev else 0
    ri = lax.broadcasted_iota(jnp.int32, (Q, Q), 0)
    ci = lax.broadcasted_iota(jnp.int32, (Q, Q), 1)
    incl = (ci >= ri) if rev else (ci <= ri)
    tri = incl.astype(F32)
    dt = dts_ref[:, off:off + SSM_HEADS]
    dtt = dtst_ref[off:off + SSM_HEADS, :]
    a = dt * (-jnp.exp(alog_ref[:, off:off + SSM_HEADS]))
    at = dtt * (-jnp.exp(alogt_ref[off:off + SSM_HEADS, :]))
    acum = jnp.dot(tri, a, preferred_element_type=F32, precision=HI)
    acumt = _nt_dot(at, tri, precision=HI)
    edge = 0 if rev else Q - 1
    bc = bc_ref[...]
    bpair = bc[:, :Q]
    cpair = bc[:, Q:]
    low = ci < SSM_N
    rlow = ri < SSM_N
    blockmask = low == rlow
    ys = []
    for g in range(SSM_GROUPS):
        gmask = low if g == 0 else jnp.logical_not(low)
        cg = jnp.where(gmask, cpair, jnp.zeros_like(cpair))
        cb = _nt_dot(cg, bpair)
        dupm = ((ri - g * SSM_N) == (ci % SSM_N)).astype(BF16)
        cdup = jnp.dot(cpair, dupm, preferred_element_type=F32)
        sel2 = ((ci - g * SSM_N) == (ri % SSM_N)).astype(BF16)
        bt2 = _nt_dot(sel2, bpair)
        for hp in range(SSM_HEADS // SSM_GROUPS // 2):
            h0 = g * (SSM_HEADS // SSM_GROUPS) + 2 * hp
            pair = h0 // 2
            xpair = xs_ref[:, h0 * SSM_P:(h0 + 2) * SSM_P]
            ms = []
            es = []
            ws = []
            for h in (h0, h0 + 1):
                e_col = jnp.broadcast_to(acum[:, h:h + 1], (Q, Q))
                r_row = jnp.broadcast_to(acumt[h:h + 1, :], (Q, Q))
                dt_row = jnp.broadcast_to(dtt[h:h + 1, :], (Q, Q))
                seg = jnp.where(incl, e_col - r_row, -jnp.inf)
                ms.append((cb * jnp.exp(seg) * dt_row).astype(BF16))
                es.append(jnp.exp(e_col))
                ws.append(dt_row * jnp.exp(acumt[h:h + 1, edge:edge + 1] - r_row))
            st = st_ref[pair]
            xblk = jnp.concatenate([jnp.where(low, xpair, jnp.zeros_like(xpair)),
                                    jnp.where(low, jnp.zeros_like(xpair), xpair)], axis=0)
            y = jnp.dot(jnp.concatenate(ms, axis=1), xblk, preferred_element_type=F32)
            ce = (cdup * jnp.where(low, es[0], es[1])).astype(BF16)
            y = y + jnp.dot(ce, st.astype(BF16), preferred_element_type=F32)
            ys.append(y)
            btw = (bt2 * jnp.where(rlow, ws[0], ws[1])).astype(BF16)
            snew = jnp.dot(btw, xpair, preferred_element_type=F32)
            dec = jnp.where(rlow, jnp.exp(acumt[h0:h0 + 1, edge:edge + 1]),
                            jnp.exp(acumt[h0 + 1:h0 + 2, edge:edge + 1]))
            st_ref[pair] = dec * st + jnp.where(blockmask, snew, 0.0)
    return jnp.concatenate(ys, axis=1)


def _ssd_fwd_kernel(xs_ref, bc_ref, dts_ref, dtst_ref, alog_ref, alogt_ref, y_ref, st_ref):
    @pl.when(pl.program_id(1) == 0)
    def _():
        st_ref[...] = jnp.zeros(st_ref.shape, F32)

    y_ref[...] = _ssd_chunk(xs_ref, bc_ref, dts_ref, dtst_ref, alog_ref, alogt_ref, st_ref, rev=False)


def _ssd_bwd_kernel(xs_ref, bc_ref, dts_ref, dtst_ref, alog_ref, alogt_ref, yf_ref, z_ref, dskip_ref, g_ref,
                    o_ref, st_ref):
    @pl.when(pl.program_id(1) == 0)
    def _():
        st_ref[...] = jnp.zeros(st_ref.shape, F32)

    yb = _ssd_chunk(xs_ref, bc_ref, dts_ref, dtst_ref, alog_ref, alogt_ref, st_ref, rev=True)
    z = z_ref[...]
    y = (yf_ref[...] + yb + xs_ref[...].astype(F32) * dskip_ref[...]) * (z * jax.nn.sigmoid(z))
    y = y * lax.rsqrt(jnp.mean(y * y, axis=-1, keepdims=True) + EPS) * g_ref[...]
    o_ref[...] = y.astype(o_ref.dtype)


def _ssd(xs, bc, dts, dtst, alog, alogt, z, dskip, g_norm):
    B, L, _ = xs.shape
    Q = SSM_CHUNK
    nc = L // Q
    ndt = 2 * SSM_HEADS
    npair = SSM_HEADS // 2

    def specs(cidx):
        row = lambda w: pl.BlockSpec((None, Q, w), lambda b, c: (b, cidx(c), 0))
        return row, [row(SSM_INNER), row(2 * SSM_GROUPS * SSM_N), row(ndt),
                     pl.BlockSpec((None, ndt, Q), lambda b, c: (b, 0, cidx(c))),
                     _const_spec((1, ndt)), _const_spec((ndt, 1))]

    scratch = [pltpu.VMEM((npair, 2 * SSM_N, 2 * SSM_P), F32)]
    row, in_specs = specs(lambda c: c)
    yf = pl.pallas_call(
        _ssd_fwd_kernel,
        grid=(B, nc),
        in_specs=in_specs,
        out_specs=row(SSM_INNER),
        out_shape=jax.ShapeDtypeStruct((B, L, SSM_INNER), F32),
        scratch_shapes=scratch,
        compiler_params=_params(("parallel", "arbitrary")),
        name="ssd_forward",
    )(xs, bc, dts, dtst, alog, alogt)
    row, in_specs = specs(lambda c: nc - 1 - c)
    return pl.pallas_call(
        _ssd_bwd_kernel,
        grid=(B, nc),
        in_specs=in_specs + [row(SSM_INNER), row(SSM_INNER), _const_spec((1, SSM_INNER)),
                             _const_spec((1, SSM_INNER))],
        out_specs=row(SSM_INNER),
        out_shape=jax.ShapeDtypeStruct((B, L, SSM_INNER), BF16),
        scratch_shapes=scratch,
        compiler_params=_params(("parallel", "arbitrary")),
        name="ssd_reverse_gate_norm",
    )(xs, bc, dts, dtst, alog, alogt, yf, z, dskip, g_norm)


def _merge_kernel(att_ref, ssm_ref, gates_ref, x_ref, wau_ref, wsu_ref, wout_ref, gffn_ref, wr_ref,
                  h_ref, u2_ref, aff_ref):
    D = D_MODEL
    ya = jnp.dot(att_ref[...], wau_ref[...], preferred_element_type=F32)
    ysm = jnp.dot(ssm_ref[...], wsu_ref[...], preferred_element_type=F32)
    merged = gates_ref[:, :D] * ya + gates_ref[:, D:] * ysm
    h = x_ref[...] + jnp.dot(merged.astype(BF16), wout_ref[...], preferred_element_type=F32)
    h_ref[...] = h
    u2 = (h * lax.rsqrt(jnp.mean(h * h, axis=-1, keepdims=True) + EPS) * gffn_ref[...]).astype(BF16)
    u2_ref[...] = u2
    logits = jnp.dot(u2, wr_ref[...], preferred_element_type=F32)
    e = jnp.exp(logits - jnp.max(logits, axis=-1, keepdims=True))
    aff_ref[...] = e / jnp.sum(e, axis=-1, keepdims=True)


def _merge(att, ssm, gates, x, wau, wsu, wout, g_ffn, wr, *, tm=256):
    B, L, D = x.shape
    row = lambda w: pl.BlockSpec((None, tm, w), lambda b, i: (b, i, 0))
    return pl.pallas_call(
        _merge_kernel,
        grid=(B, L // tm),
        in_specs=[row(ATT_W), row(SSM_INNER), row(2 * D), row(D), _const_spec(wau.shape),
                  _const_spec(wsu.shape), _const_spec(wout.shape), _const_spec((1, D)), _const_spec(wr.shape)],
        out_specs=[row(D), row(D), row(N_EXPERTS)],
        out_shape=[jax.ShapeDtypeStruct((B, L, D), F32), jax.ShapeDtypeStruct((B, L, D), BF16),
                   jax.ShapeDtypeStruct((B, L, N_EXPERTS), F32)],
        compiler_params=_params(("parallel", "parallel")),
        name="merge_router",
    )(att, ssm, gates, x, wau, wsu, wout, g_ffn, wr)


def _expert_kernel(xs_ref, gate_ref, wg_ref, wu_ref, wd_ref, o_ref):
    xs = xs_ref[...]
    hg = jnp.dot(xs, wg_ref[...], preferred_element_type=F32)
    hu = jnp.dot(xs, wu_ref[...], preferred_element_type=F32)
    hmid = (hg * jax.nn.sigmoid(hg) * hu).astype(BF16)
    o_ref[...] = jnp.dot(hmid, wd_ref[...], preferred_element_type=F32) * gate_ref[...]


def _experts(xs, gate, wg, wu, wd, *, tc=512):
    E, C, D = xs.shape
    tc = min(tc, C)
    wspec = pl.BlockSpec((None, D, D), lambda e, i: (e, 0, 0))
    return pl.pallas_call(
        _expert_kernel,
        grid=(E, C // tc),
        in_specs=[pl.BlockSpec((None, tc, D), lambda e, i: (e, i, 0)),
                  pl.BlockSpec((None, tc, 1), lambda e, i: (e, i, 0)), wspec, wspec, wspec],
        out_specs=pl.BlockSpec((None, tc, D), lambda e, i: (e, i, 0)),
        out_shape=jax.ShapeDtypeStruct((E, C, D), F32),
        compiler_params=_params(("parallel", "parallel")),
        name="experts",
    )(xs, gate, wg, wu, wd)


def _ple_kernel(h_ref, moe_ref, p_ref, wpg_ref, bpg_ref, wpp_ref, gfin_ref, o_ref):
    h = h_ref[...] + moe_ref[...]
    gate = jax.nn.sigmoid(jnp.dot(h.astype(BF16), wpg_ref[...], preferred_element_type=F32) + bpg_ref[...])
    h = h + gate * jnp.dot(p_ref[...].astype(BF16), wpp_ref[...], preferred_element_type=F32)
    o_ref[...] = h * lax.rsqrt(jnp.mean(h * h, axis=-1, keepdims=True) + EPS) * gfin_ref[...]


def _ple(h, moe, p, wpg, bpg, wpp, g_final, *, tm=256):
    B, L, D = h.shape
    row = lambda w: pl.BlockSpec((None, tm, w), lambda b, i: (b, i, 0))
    return pl.pallas_call(
        _ple_kernel,
        grid=(B, L // tm),
        in_specs=[row(D), row(D), row(PLE_DIM), _const_spec(wpg.shape), _const_spec((1, D)),
                  _const_spec(wpp.shape), _const_spec((1, D))],
        out_specs=row(D),
        out_shape=jax.ShapeDtypeStruct((B, L, D), F32),
        compiler_params=_params(("parallel", "parallel")),
        name="ple_final_norm",
    )(h, moe, p, wpg, bpg, wpp, g_final)


def _forward(x, p, w, btiles):
    B, L, D = x.shape
    T = B * L
    qt, k, vt, z, xbc, dt, dtt, gates = _inproj(x, w['g_mix'], w['wqt'], w['wk'], w['wvt'], w['wz'], w['wxbc'],
                                                w['wdt'], w['wdtt'], w['wgate'], w['bgate'])
    att = _attention(qt, k, vt, btiles, w['lam'], w['g_subln'])
    xs, bc, dts, dtst = _conv(xbc, dt, dtt, w['conv_w'], w['conv_b'], w['dtb'], w['dtbt'])
    ssm = _ssd(xs, bc, dts, dtst, w['alog'], w['alogt'], z, w['dskip'], w['g_ssm_norm'])
    h, u2, aff = _merge(att, ssm, gates, x, w['wau'], w['wsu'], w['wout'], w['g_ffn'], w['wr'])
    cap = max(1, CAPACITY_FACTOR * T // N_EXPERTS)
    gate, idx = lax.top_k(aff.reshape(T, N_EXPERTS).T, cap)
    xsel = u2.reshape(T, D)[idx]
    ye = _experts(xsel, gate[..., None], w['weg'], w['weu'], w['wed'])
    moe = jnp.zeros((T, D), F32).at[idx.reshape(-1)].add(ye.reshape(-1, D)).reshape(B, L, D)
    return _ple(h, moe, p, w['wpg'], w['bpg'], w['wpp'], w['g_final'])


def kernel(x_prompt, x_sample, p_prompt, p_sample, rel_bias, g_mix, w_in, conv_w, conv_b, dt_bias, a_log,
           d_skip, g_ssm_norm, lam_params, g_subln, w_att_up, w_ssm_up, w_gate, b_gate, w_out, g_ffn,
           w_router, w_exp_gate, w_exp_up, w_exp_down, w_ple_gate, b_ple_gate, w_ple_proj, g_final):
    i = 0
    D = D_MODEL
    win = w_in[i]
    o1 = ATT_W
    o3 = 3 * ATT_W
    o4 = o3 + SSM_INNER
    o5 = o4 + CONV_CH
    wqt = (win[:, :o1] * (QK_DIM ** -0.5 * LOG2E)).T.astype(BF16)
    wk = win[:, o1:2 * o1].astype(BF16)
    wvt = win[:, 2 * o1:o3].T.astype(BF16)
    wdt = win[:, o5:].astype(BF16)
    lp = lam_params[i].astype(F32)
    lam = jnp.exp(jnp.sum(lp[0] * lp[1])) - jnp.exp(jnp.sum(lp[2] * lp[3])) + LAM_INIT
    ndt = 2 * SSM_HEADS
    w = {
        'g_mix': g_mix[i].reshape(1, D), 'wqt': wqt, 'wk': wk, 'wvt': wvt, 'wz': win[:, o3:o4].astype(BF16),
        'wxbc': win[:, o4:o5].astype(BF16), 'wdt': wdt, 'wdtt': wdt.T,
        'wgate': w_gate[i].astype(BF16), 'bgate': b_gate[i].reshape(1, 2 * D),
        'lam': lam.reshape(1), 'g_subln': g_subln[i].reshape(HEAD_W, 1),
        'conv_w': conv_w[i], 'conv_b': conv_b[i].reshape(1, CONV_CH),
        'dtb': dt_bias[i].reshape(1, ndt), 'dtbt': dt_bias[i].reshape(ndt, 1),
        'alog': a_log[i].reshape(1, ndt), 'alogt': a_log[i].reshape(ndt, 1),
        'dskip': jnp.repeat(d_skip[i], SSM_P).reshape(1, SSM_INNER),
        'g_ssm_norm': g_ssm_norm[i].reshape(1, SSM_INNER),
        'wau': w_att_up[i].astype(BF16), 'wsu': w_ssm_up[i].astype(BF16), 'wout': w_out[i].astype(BF16),
        'g_ffn': g_ffn[i].reshape(1, D), 'wr': w_router[i].astype(BF16),
        'weg': w_exp_gate[i].astype(BF16), 'weu': w_exp_up[i].astype(BF16), 'wed': w_exp_down[i].astype(BF16),
        'wpg': w_ple_gate[i].astype(BF16), 'bpg': b_ple_gate[i].reshape(1, D),
        'wpp': w_ple_proj[i].astype(BF16), 'g_final': g_final.reshape(1, D),
    }
    btiles = _bias_tiles(rel_bias, ATT_BLOCK)
    return (_forward(x_prompt, p_prompt[i], w, btiles), _forward(x_sample, p_sample[i], w, btiles))
```

```python
import functools
import math

import jax
import jax.numpy as jnp
from jax import lax
from jax.experimental import pallas as pl
from jax.experimental.pallas import tpu as pltpu

F32 = jnp.float32
BF16 = jnp.bfloat16

D_MODEL = 1024
N_HEADS = 4
QK_DIM = 64
HEAD_W = 2 * QK_DIM
ATT_W = N_HEADS * HEAD_W
REL_BUCKETS = 32
REL_MAX_DIST = 128
SSM_INNER = 1024
SSM_P = 64
SSM_HEADS = 16
SSM_GROUPS = 2
SSM_N = 64
SSM_CHUNK = 128
CONV_CH = SSM_INNER + 2 * SSM_GROUPS * SSM_N
N_EXPERTS = 16
CAPACITY_FACTOR = 2
PLE_DIM = 256
EPS = 1e-6
LAM_INIT = 0.8 - 0.6 * math.exp(-0.3 * 0)
LOG2E = math.log2(math.e)

GATE_EXT = 128
ROUTE_BLK = 256
GATHER_BLK = 512
SLOT_TILE = 256
COMBINE_WIN = 72
BSTART_W = 256

ATT_BLOCK = 256
VMEM_LIMIT = 56 * 1024 * 1024


def _params(sem, vmem=VMEM_LIMIT):
    return pltpu.CompilerParams(dimension_semantics=sem, vmem_limit_bytes=vmem)


def _const_spec(shape):
    nd = len(shape)
    return pl.BlockSpec(shape, lambda *_: (0,) * nd)


def _nt_dot(a, b, **kw):
    return lax.dot_general(a, b, (((1,), (1,)), ((), ())), preferred_element_type=F32, **kw)


def _inproj_kernel(x_ref, g_ref, wqt_ref, wk_ref, wvt_ref, wz_ref, wxbc_ref, wdt_ref, wdtt_ref, wgate_ref,
                   bgate_ref, qt_ref, k_ref, vt_ref, z_ref, xbc_ref, dt_ref, dtt_ref, gates_ref):
    x = x_ref[...]
    u = x * lax.rsqrt(jnp.mean(x * x, axis=-1, keepdims=True) + EPS) * g_ref[...]
    ub = u.astype(BF16)
    qt_ref[...] = _nt_dot(wqt_ref[...], ub).astype(BF16)
    k_ref[...] = jnp.dot(ub, wk_ref[...], preferred_element_type=F32).astype(BF16)
    vt_ref[...] = _nt_dot(wvt_ref[...], ub).astype(BF16)
    z_ref[...] = jnp.dot(ub, wz_ref[...], preferred_element_type=F32)
    xbc_ref[...] = jnp.dot(ub, wxbc_ref[...], preferred_element_type=F32)
    dt_ref[...] = jnp.dot(ub, wdt_ref[...], preferred_element_type=F32)
    dtt_ref[...] = _nt_dot(wdtt_ref[...], ub)
    gates_ref[...] = jax.nn.sigmoid(jnp.dot(ub, wgate_ref[...], preferred_element_type=F32) + bgate_ref[...])


def _inproj(x, g_mix, wqt, wk, wvt, wz, wxbc, wdt, wdtt, wgate, bgate, *, tm=256):
    B, L, D = x.shape
    grid = (B, L // tm)
    row = lambda w: pl.BlockSpec((None, tm, w), lambda b, i: (b, i, 0))
    col = lambda w: pl.BlockSpec((None, w, tm), lambda b, i: (b, 0, i))
    ndt = 2 * SSM_HEADS
    return pl.pallas_call(
        _inproj_kernel,
        grid=grid,
        in_specs=[row(D), _const_spec((1, D)), _const_spec(wqt.shape), _const_spec(wk.shape),
                  _const_spec(wvt.shape), _const_spec(wz.shape),
                  _const_spec(wxbc.shape), _const_spec(wdt.shape), _const_spec(wdtt.shape),
                  _const_spec(wgate.shape), _const_spec((1, 2 * D))],
        out_specs=[col(ATT_W), row(ATT_W), col(ATT_W), row(SSM_INNER), row(CONV_CH), row(ndt),
                   col(ndt), row(2 * D)],
        out_shape=[jax.ShapeDtypeStruct((B, ATT_W, L), BF16),
                   jax.ShapeDtypeStruct((B, L, ATT_W), BF16),
                   jax.ShapeDtypeStruct((B, ATT_W, L), BF16),
                   jax.ShapeDtypeStruct((B, L, SSM_INNER), F32),
                   jax.ShapeDtypeStruct((B, L, CONV_CH), F32),
                   jax.ShapeDtypeStruct((B, L, ndt), F32),
                   jax.ShapeDtypeStruct((B, ndt, L), F32),
                   jax.ShapeDtypeStruct((B, L, 2 * D), F32)],
        compiler_params=_params(("parallel", "parallel")),
        name="inproj",
    )(x, g_mix, wqt, wk, wvt, wz, wxbc, wdt, wdtt, wgate, bgate)


def _rel_bucket(rel):
    half = REL_BUCKETS // 2
    max_exact = half // 2
    ret = jnp.where(rel > 0, half, 0)
    n = jnp.abs(rel)
    nf = jnp.maximum(n, 1).astype(jnp.float32)
    large = max_exact + (jnp.log(nf / max_exact) / math.log(REL_MAX_DIST / max_exact)
                         * (half - max_exact)).astype(jnp.int32)
    large = jnp.minimum(large, half - 1)
    return ret + jnp.where(n < max_exact, n, large)


def _bias_kernel(table_ref, bucket_ref, out_ref, *, T):
    h = pl.program_id(0)
    out_ref[...] = jnp.zeros(out_ref.shape, F32)
    for d in range(5):
        for m in range(2):
            cols = pl.ds(m * T, T)

            def body(b, carry, d=d, m=m, cols=cols):
                val = table_ref[b, 2 * h + m] * LOG2E
                out_ref[d, :, cols] = jnp.where(bucket_ref[d] == b, val, out_ref[d, :, cols])
                return carry

            lax.fori_loop(0, REL_BUCKETS, body, 0)


def _bias_tiles(rel_bias, T):
    i = jnp.arange(T, dtype=jnp.int32)
    base = i[:, None] - i[None, :]
    bucket = jnp.stack([_rel_bucket(base + (d - 2) * T) for d in range(5)])
    return pl.pallas_call(
        functools.partial(_bias_kernel, T=T),
        grid=(N_HEADS,),
        in_specs=[pl.BlockSpec(memory_space=pltpu.SMEM), _const_spec((5, T, T))],
        out_specs=pl.BlockSpec((None, 5, T, 2 * T), lambda h: (h, 0, 0, 0)),
        out_shape=jax.ShapeDtypeStruct((N_HEADS, 5, T, 2 * T), F32),
        compiler_params=_params(("parallel",)),
        name="rel_bias_tiles",
    )(rel_bias.astype(F32), bucket)


def _attn_kernel(lam_ref, qt_ref, k_ref, vt_ref, bt_ref, g_ref, o_ref, acc_sc, s_sc, p_sc, *, T, nk):
    i = pl.program_id(2)
    qt = qt_ref[...]
    sub = lax.broadcasted_iota(jnp.int32, qt.shape, 0)
    zero = jnp.zeros_like(qt)
    q2t = jnp.concatenate([jnp.where(sub < QK_DIM, qt, zero), jnp.where(sub >= QK_DIM, qt, zero)], axis=1)
    acc_sc[...] = jnp.zeros(acc_sc.shape, F32)

    def scores(j, slot):
        kb = k_ref[pl.ds(pl.multiple_of(j * T, T), T), :]
        s_sc[slot] = jnp.dot(kb, q2t, preferred_element_type=F32) + bt_ref[jnp.clip(j - i, -2, 2) + 2]

    def softmax(slot, m, l):
        s = s_sc[slot]
        m_new = jnp.maximum(m, jnp.max(s, axis=0, keepdims=True))
        alpha = jnp.exp2(m - m_new)
        p = jnp.exp2(s - m_new)
        p_sc[slot] = p.astype(BF16)
        return m_new, alpha * l + jnp.sum(p, axis=0, keepdims=True), alpha

    def accumulate(j, slot, alpha):
        vb = vt_ref[:, pl.ds(pl.multiple_of(j * T, T), T)]
        acc_sc[...] = alpha * acc_sc[...] + jnp.dot(vb, p_sc[slot], preferred_element_type=F32)

    m = jnp.full((1, 2 * T), -jnp.inf, F32)
    l = jnp.zeros((1, 2 * T), F32)
    scores(0, 0)
    scores(1, 1)
    m, l, alpha = softmax(0, m, l)

    def pair(u, carry):
        m, l, alpha = carry
        t = 2 * u
        scores(t, 0)
        m, l, alpha1 = softmax(1, m, l)
        accumulate(t - 2, 0, alpha)
        scores(t + 1, 1)
        m, l, alpha0 = softmax(0, m, l)
        accumulate(t - 1, 1, alpha1)
        return m, l, alpha0

    m, l, alpha = lax.fori_loop(1, nk // 2, pair, (m, l, alpha))
    m, l, alpha1 = softmax(1, m, l)
    accumulate(nk - 2, 0, alpha)
    accumulate(nk - 1, 1, alpha1)

    o = acc_sc[...] / l
    od = o[:, :T] - lam_ref[0] * o[:, T:]
    y = od * lax.rsqrt(jnp.mean(od * od, axis=0, keepdims=True) + EPS) * g_ref[...]
    o_ref[...] = (y * (1.0 - LAM_INIT)).T.astype(o_ref.dtype)


def _attention(qt, k, vt, btiles, lam, g_col, *, T=ATT_BLOCK):
    B, L, _ = k.shape
    nk = L // T
    assert nk % 2 == 0 and nk >= 4
    return pl.pallas_call(
        functools.partial(_attn_kernel, T=T, nk=nk),
        grid=(B, N_HEADS, nk),
        in_specs=[pl.BlockSpec(memory_space=pltpu.SMEM),
                  pl.BlockSpec((None, HEAD_W, T), lambda b, h, i: (b, h, i)),
                  pl.BlockSpec((None, L, HEAD_W), lambda b, h, i: (b, 0, h)),
                  pl.BlockSpec((None, HEAD_W, L), lambda b, h, i: (b, h, 0)),
                  pl.BlockSpec((None, 5, T, 2 * T), lambda b, h, i: (h, 0, 0, 0)),
                  _const_spec((HEAD_W, 1))],
        out_specs=pl.BlockSpec((None, T, HEAD_W), lambda b, h, i: (b, i, h)),
        out_shape=jax.ShapeDtypeStruct((B, L, ATT_W), BF16),
        scratch_shapes=[pltpu.VMEM((HEAD_W, 2 * T), F32), pltpu.VMEM((2, T, 2 * T), F32),
                        pltpu.VMEM((2, T, 2 * T), BF16)],
        compiler_params=_params(("parallel", "parallel", "parallel")),
        name="diff_attention",
    )(lam, qt, k, vt, btiles, g_col)


def _conv_kernel(x_ref, prev_ref, next_ref, dt_ref, dtt_ref, w_ref, b_ref, dtb_ref, dtbt_ref,
                 xs_ref, bc_ref, dts_ref, dtst_ref, *, tl, nt):
    i = pl.program_id(1)
    x = x_ref[...]
    row = lax.broadcasted_iota(jnp.int32, x.shape, 0)
    prev_row = jnp.where(i > 0, prev_ref[7:8, :], 0.0)
    next_row = jnp.where(i < nt - 1, next_ref[0:1, :], 0.0)
    xm1 = jnp.where(row == 0, prev_row, pltpu.roll(x, 1, 0))
    xp1 = jnp.where(row == tl - 1, next_row, pltpu.roll(x, tl - 1, 0))
    y = w_ref[0:1, :] * xm1 + w_ref[1:2, :] * x + w_ref[2:3, :] * xp1 + b_ref[...]
    y = y * jax.nn.sigmoid(y)
    xs_ref[...] = y[:, :SSM_INNER].astype(BF16)
    bc_ref[...] = y[:, SSM_INNER:].astype(BF16)

    def softplus(v):
        return jnp.maximum(v, 0.0) + jnp.log1p(jnp.exp(-jnp.abs(v)))

    dts_ref[...] = softplus(dt_ref[...] + dtb_ref[...])
    dtst_ref[...] = softplus(dtt_ref[...] + dtbt_ref[...])


def _conv(xbc, dt, dtt, conv_w, conv_b, dtb, dtbt, *, tl=512):
    B, L, _ = xbc.shape
    nt = L // tl
    r8 = tl // 8
    ndt = 2 * SSM_HEADS
    row = lambda w: pl.BlockSpec((None, tl, w), lambda b, i: (b, i, 0))
    colspec = pl.BlockSpec((None, ndt, tl), lambda b, i: (b, 0, i))
    return pl.pallas_call(
        functools.partial(_conv_kernel, tl=tl, nt=nt),
        grid=(B, nt),
        in_specs=[row(CONV_CH),
                  pl.BlockSpec((None, 8, CONV_CH), lambda b, i: (b, jnp.maximum(i * r8 - 1, 0), 0)),
                  pl.BlockSpec((None, 8, CONV_CH), lambda b, i: (b, jnp.minimum((i + 1) * r8, L // 8 - 1), 0)),
                  row(ndt), colspec,
                  _const_spec((3, CONV_CH)), _const_spec((1, CONV_CH)), _const_spec((1, ndt)),
                  _const_spec((ndt, 1))],
        out_specs=[row(SSM_INNER), row(2 * SSM_GROUPS * SSM_N), row(ndt), colspec],
        out_shape=[jax.ShapeDtypeStruct((B, L, SSM_INNER), BF16),
                   jax.ShapeDtypeStruct((B, L, 2 * SSM_GROUPS * SSM_N), BF16),
                   jax.ShapeDtypeStruct((B, L, ndt), F32),
                   jax.ShapeDtypeStruct((B, ndt, L), F32)],
        compiler_params=_params(("parallel", "parallel")),
        name="ssm_conv",
    )(xbc, xbc, xbc, dt, dtt, conv_w, conv_b, dtb, dtbt)


def _ssd_chunk(xs_ref, bc_ref, dts_ref, dtst_ref, alog_ref, alogt_ref, st_ref, *, rev):
    Q = SSM_CHUNK
    HI = lax.Precision.HIGHEST
    off = SSM_HEADS if rev else 0
    ri = lax.broadcasted_iota(jnp.int32, (Q, Q), 0)
    ci = lax.broadcasted_iota(jnp.int32, (Q, Q), 1)
    incl = (ci >= ri) if rev else (ci <= ri)
    tri = incl.astype(F32)
    dt = dts_ref[:, off:off + SSM_HEADS]
    dtt = dtst_ref[off:off + SSM_HEADS, :]
    a = dt * (-jnp.exp(alog_ref[:, off:off + SSM_HEADS]))
    at = dtt * (-jnp.exp(alogt_ref[off:off + SSM_HEADS, :]))
    acum = jnp.dot(tri, a, preferred_element_type=F32, precision=HI)
    acumt = _nt_dot(at, tri, precision=HI)
    edge = 0 if rev else Q - 1
    bc = bc_ref[...]
    bpair = bc[:, :Q]
    cpair = bc[:, Q:]
    low = ci < SSM_N
    rlow = ri < SSM_N
    blockmask = low == rlow
    ys = []
    for g in range(SSM_GROUPS):
        gmask = low if g == 0 else jnp.logical_not(low)
        cg = jnp.where(gmask, cpair, jnp.zeros_like(cpair))
        cb = _nt_dot(cg, bpair)
        dupm = ((ri - g * SSM_N) == (ci % SSM_N)).astype(BF16)
        cdup = jnp.dot(cpair, dupm, preferred_element_type=F32)
        sel2 = ((ci - g * SSM_N) == (ri % SSM_N)).astype(BF16)
        bt2 = _nt_dot(sel2, bpair)
        for hp in range(SSM_HEADS // SSM_GROUPS // 2):
            h0 = g * (SSM_HEADS // SSM_GROUPS) + 2 * hp
            pair = h0 // 2
            xpair = xs_ref[:, h0 * SSM_P:(h0 + 2) * SSM_P]
            ms = []
            es = []
            ws = []
            for h in (h0, h0 + 1):
                e_col = jnp.broadcast_to(acum[:, h:h + 1], (Q, Q))
                r_row = jnp.broadcast_to(acumt[h:h + 1, :], (Q, Q))
                dt_row = jnp.broadcast_to(dtt[h:h + 1, :], (Q, Q))
                seg = jnp.where(incl, e_col - r_row, -jnp.inf)
                ms.append((cb * jnp.exp(seg) * dt_row).astype(BF16))
                es.append(jnp.exp(e_col))
                ws.append(dt_row * jnp.exp(acumt[h:h + 1, edge:edge + 1] - r_row))
            st = st_ref[pair]
            xblk = jnp.concatenate([jnp.where(low, xpair, jnp.zeros_like(xpair)),
                                    jnp.where(low, jnp.zeros_like(xpair), xpair)], axis=0)
            y = jnp.dot(jnp.concatenate(ms, axis=1), xblk, preferred_element_type=F32)
            ce = (cdup * jnp.where(low, es[0], es[1])).astype(BF16)
            y = y + jnp.dot(ce, st.astype(BF16), preferred_element_type=F32)
            ys.append(y)
            btw = (bt2 * jnp.where(rlow, ws[0], ws[1])).astype(BF16)
            snew = jnp.dot(btw, xpair, preferred_element_type=F32)
            dec = jnp.where(rlow, jnp.exp(acumt[h0:h0 + 1, edge:edge + 1]),
                            jnp.exp(acumt[h0 + 1:h0 + 2, edge:edge + 1]))
            st_ref[pair] = dec * st + jnp.where(blockmask, snew, 0.0)
    return jnp.concatenate(ys, axis=1)


def _ssd_fwd_kernel(xs_ref, bc_ref, dts_ref, dtst_ref, alog_ref, alogt_ref, y_ref, st_ref):
    @pl.when(pl.program_id(1) == 0)
    def _():
        st_ref[...] = jnp.zeros(st_ref.shape, F32)

    y_ref[...] = _ssd_chunk(xs_ref, bc_ref, dts_ref, dtst_ref, alog_ref, alogt_ref, st_ref, rev=False)


def _ssd_bwd_kernel(xs_ref, bc_ref, dts_ref, dtst_ref, alog_ref, alogt_ref, yf_ref, z_ref, dskip_ref, g_ref,
                    o_ref, st_ref):
    @pl.when(pl.program_id(1) == 0)
    def _():
        st_ref[...] = jnp.zeros(st_ref.shape, F32)

    yb = _ssd_chunk(xs_ref, bc_ref, dts_ref, dtst_ref, alog_ref, alogt_ref, st_ref, rev=True)
    z = z_ref[...]
    y = (yf_ref[...] + yb + xs_ref[...].astype(F32) * dskip_ref[...]) * (z * jax.nn.sigmoid(z))
    y = y * lax.rsqrt(jnp.mean(y * y, axis=-1, keepdims=True) + EPS) * g_ref[...]
    o_ref[...] = y.astype(o_ref.dtype)


def _ssd(xs, bc, dts, dtst, alog, alogt, z, dskip, g_norm):
    B, L, _ = xs.shape
    Q = SSM_CHUNK
    nc = L // Q
    ndt = 2 * SSM_HEADS
    npair = SSM_HEADS // 2

    def specs(cidx):
        row = lambda w: pl.BlockSpec((None, Q, w), lambda b, c: (b, cidx(c), 0))
        return row, [row(SSM_INNER), row(2 * SSM_GROUPS * SSM_N), row(ndt),
                     pl.BlockSpec((None, ndt, Q), lambda b, c: (b, 0, cidx(c))),
                     _const_spec((1, ndt)), _const_spec((ndt, 1))]

    scratch = [pltpu.VMEM((npair, 2 * SSM_N, 2 * SSM_P), F32)]
    row, in_specs = specs(lambda c: c)
    yf = pl.pallas_call(
        _ssd_fwd_kernel,
        grid=(B, nc),
        in_specs=in_specs,
        out_specs=row(SSM_INNER),
        out_shape=jax.ShapeDtypeStruct((B, L, SSM_INNER), F32),
        scratch_shapes=scratch,
        compiler_params=_params(("parallel", "arbitrary")),
        name="ssd_forward",
    )(xs, bc, dts, dtst, alog, alogt)
    row, in_specs = specs(lambda c: nc - 1 - c)
    return pl.pallas_call(
        _ssd_bwd_kernel,
        grid=(B, nc),
        in_specs=in_specs + [row(SSM_INNER), row(SSM_INNER), _const_spec((1, SSM_INNER)),
                             _const_spec((1, SSM_INNER))],
        out_specs=row(SSM_INNER),
        out_shape=jax.ShapeDtypeStruct((B, L, SSM_INNER), BF16),
        scratch_shapes=scratch,
        compiler_params=_params(("parallel", "arbitrary")),
        name="ssd_reverse_gate_norm",
    )(xs, bc, dts, dtst, alog, alogt, yf, z, dskip, g_norm)


def _merge_kernel(att_ref, ssm_ref, gates_ref, x_ref, wau_ref, wsu_ref, wout_ref, gffn_ref, wr_ref, wrt_ref,
                  place_ref, h_ref, u2_ref, afft_ref):
    D = D_MODEL
    E = N_EXPERTS
    ya = jnp.dot(att_ref[...], wau_ref[...], preferred_element_type=F32)
    ysm = jnp.dot(ssm_ref[...], wsu_ref[...], preferred_element_type=F32)
    merged = gates_ref[:, :D] * ya + gates_ref[:, D:] * ysm
    h = x_ref[...] + jnp.dot(merged.astype(BF16), wout_ref[...], preferred_element_type=F32)
    h_ref[...] = h
    u2 = (h * lax.rsqrt(jnp.mean(h * h, axis=-1, keepdims=True) + EPS) * gffn_ref[...]).astype(BF16)
    u2_ref[:, :D] = u2
    logits = jnp.dot(u2, wr_ref[...], preferred_element_type=F32)
    e = jnp.exp(logits - jnp.max(logits, axis=-1, keepdims=True))
    aff = e / jnp.sum(e, axis=-1, keepdims=True)
    hi = aff.astype(BF16)
    r1 = aff - hi.astype(F32)
    mid = r1.astype(BF16)
    lo = (r1 - mid.astype(F32)).astype(BF16)
    ext = (jnp.dot(hi, place_ref[0:E, :], preferred_element_type=F32)
           + jnp.dot(mid, place_ref[E:2 * E, :], preferred_element_type=F32)
           + jnp.dot(lo, place_ref[2 * E:3 * E, :], preferred_element_type=F32))
    u2_ref[:, D:] = ext.astype(BF16)
    lt = _nt_dot(wrt_ref[...], u2)
    et = jnp.exp(lt - jnp.max(lt, axis=0, keepdims=True))
    afft_ref[...] = et / jnp.sum(et, axis=0, keepdims=True)


def _merge(att, ssm, gates, x, wau, wsu, wout, g_ffn, wr, *, tm=256):
    B, L, D = x.shape
    E = N_EXPERTS
    nl = L // tm
    row = lambda w: pl.BlockSpec((None, tm, w), lambda b, i: (b, i, 0))
    place = jnp.eye(3 * E, GATE_EXT, dtype=BF16)
    return pl.pallas_call(
        _merge_kernel,
        grid=(B, nl),
        in_specs=[row(ATT_W), row(SSM_INNER), row(2 * D), row(D), _const_spec(wau.shape),
                  _const_spec(wsu.shape), _const_spec(wout.shape), _const_spec((1, D)), _const_spec(wr.shape),
                  _const_spec((E, D)), _const_spec((3 * E, GATE_EXT))],
        out_specs=[row(D), row(D + GATE_EXT), pl.BlockSpec((E, tm), lambda b, i: (0, b * nl + i))],
        out_shape=[jax.ShapeDtypeStruct((B, L, D), F32), jax.ShapeDtypeStruct((B, L, D + GATE_EXT), BF16),
                   jax.ShapeDtypeStruct((E, B * L), F32)],
        compiler_params=_params(("parallel", "parallel")),
        name="merge_router",
    )(att, ssm, gates, x, wau, wsu, wout, g_ffn, wr, wr.T, place)


def _route_kernel(afft_ref, posm_ref, bstart_ref, *, cap, nb):
    E = N_EXPERTS
    W = ROUTE_BLK

    def bit_body(k, thr):
        cand = thr | lax.shift_left(jnp.int32(1), 30 - k)
        x = lax.bitcast_convert_type(afft_ref[...], jnp.int32)
        cnt = jnp.sum(jnp.where(x >= cand, 1.0, 0.0), axis=1, keepdims=True)
        return jnp.where(cnt >= cap, cand, thr)

    thr = lax.fori_loop(0, 31, bit_body, jnp.zeros((E, 1), jnp.int32))
    x = lax.bitcast_convert_type(afft_ref[...], jnp.int32)
    n_gt = jnp.sum(jnp.where(x > thr, 1.0, 0.0), axis=1, keepdims=True)
    need = cap - n_gt
    ri = lax.broadcasted_iota(jnp.int32, (W, W), 0)
    ci = lax.broadcasted_iota(jnp.int32, (W, W), 1)
    upper = jnp.where(ri <= ci, 1.0, 0.0).astype(BF16)
    bstart_ref[...] = jnp.zeros(bstart_ref.shape, jnp.int32)
    c_eq = jnp.zeros((E, 1), F32)
    c_sel = jnp.zeros((E, 1), F32)
    for b in range(nb):
        xb = lax.bitcast_convert_type(afft_ref[:, b * W:(b + 1) * W], jnp.int32)
        eq = xb == thr
        eqc = jnp.dot(jnp.where(eq, 1.0, 0.0).astype(BF16), upper, preferred_element_type=F32) + c_eq
        sel = (xb > thr) | (eq & (eqc <= need))
        selc = jnp.dot(jnp.where(sel, 1.0, 0.0).astype(BF16), upper, preferred_element_type=F32) + c_sel
        posm_ref[:, b * W:(b + 1) * W] = jnp.where(sel, selc, 0.0).astype(jnp.int32)
        bstart_ref[:, b:b + 1] = c_sel.astype(jnp.int32)
        c_eq = eqc[:, W - 1:W]
        c_sel = selc[:, W - 1:W]
    bstart_ref[:, nb:nb + 1] = c_sel.astype(jnp.int32)


def _route(afft, cap):
    E, T = afft.shape
    nb = T // ROUTE_BLK
    assert nb < BSTART_W and cap % SLOT_TILE == 0
    return pl.pallas_call(
        functools.partial(_route_kernel, cap=float(cap), nb=nb),
        out_shape=[jax.ShapeDtypeStruct((E, T), jnp.int32), jax.ShapeDtypeStruct((E, BSTART_W), jnp.int32)],
        compiler_params=_params(None),
        name="route_select",
    )(afft)


def _pair_schedule(bstart, cap, ngb):
    step = GATHER_BLK // ROUTE_BLK
    st = bstart[:, 0:ngb * step + 1:step]
    lo, hi = st[:, :-1], st[:, 1:]
    ntile = cap // SLOT_TILE
    t_lo = jnp.arange(ntile, dtype=jnp.int32) * SLOT_TILE
    t_hi = t_lo + SLOT_TILE
    b_first = jnp.sum(hi[:, None, :] <= t_lo[None, :, None], axis=2).astype(jnp.int32)
    b_last = jnp.sum(lo[:, None, :] < t_hi[None, :, None], axis=2).astype(jnp.int32) - 1
    cnt = b_last - b_first + 1
    off_in = jnp.cumsum(cnt, axis=1)
    off_ex = off_in - cnt
    S = ngb + ntile
    k = jnp.arange(S, dtype=jnp.int32)
    tile = jnp.minimum(jnp.sum(off_in[:, None, :] <= k[None, :, None], axis=2), ntile - 1).astype(jnp.int32)
    take = lambda a: jnp.take_along_axis(a, tile, axis=1)
    blk = jnp.minimum(take(b_first) + k[None, :] - take(off_ex), take(b_last))
    real = k[None, :] < off_in[:, -1:]
    first = real & (k[None, :] == take(off_ex))
    last = real & (k[None, :] == take(off_in) - 1)
    flags = first.astype(jnp.int32) + 2 * last.astype(jnp.int32) + 4 * real.astype(jnp.int32)
    return tile, blk.astype(jnp.int32), flags


def _expert_kernel(tile_ref, blk_ref, flag_ref, u_ref, posm_ref, wg_ref, wu_ref, wd_ref, o_ref, x_sc):
    D = D_MODEL
    e = pl.program_id(0)
    k = pl.program_id(1)
    flags = flag_ref[e, k]

    @pl.when((flags & 1) != 0)
    def _():
        x_sc[...] = jnp.zeros(x_sc.shape, F32)

    @pl.when((flags & 4) != 0)
    def _():
        slot = lax.broadcasted_iota(jnp.int32, (SLOT_TILE, GATHER_BLK), 0) + (tile_ref[e, k] * SLOT_TILE + 1)
        onehot = jnp.where(posm_ref[...] == slot, 1.0, 0.0).astype(BF16)
        x_sc[...] += jnp.dot(onehot, u_ref[...], preferred_element_type=F32)

    @pl.when((flags & 2) != 0)
    def _():
        xs = x_sc[:, :D].astype(BF16)
        ext = x_sc[:, D:]
        lane = lax.broadcasted_iota(jnp.int32, ext.shape, 1)
        mine = (lane == e) | (lane == e + N_EXPERTS) | (lane == e + 2 * N_EXPERTS)
        gate = jnp.sum(jnp.where(mine, ext, 0.0), axis=1, keepdims=True)
        hg = jnp.dot(xs, wg_ref[...], preferred_element_type=F32)
        hu = jnp.dot(xs, wu_ref[...], preferred_element_type=F32)
        hmid = (hg * jax.nn.sigmoid(hg) * hu).astype(BF16)
        o_ref[...] = jnp.dot(hmid, wd_ref[...], preferred_element_type=F32) * gate


def _experts(u2ext, posm, tile, blk, flags, wg, wu, wd, cap):
    T, DX = u2ext.shape
    E, D = N_EXPERTS, D_MODEL
    S = tile.shape[1]
    wspec = pl.BlockSpec((None, D, D), lambda e, k, tj, tb, tf: (e, 0, 0))
    return pl.pallas_call(
        _expert_kernel,
        grid_spec=pltpu.PrefetchScalarGridSpec(
            num_scalar_prefetch=3,
            grid=(E, S),
            in_specs=[pl.BlockSpec((GATHER_BLK, DX), lambda e, k, tj, tb, tf: (tb[e, k], 0)),
                      pl.BlockSpec((None, 1, GATHER_BLK), lambda e, k, tj, tb, tf: (e, 0, tb[e, k])),
                      wspec, wspec, wspec],
            out_specs=pl.BlockSpec((None, SLOT_TILE, D), lambda e, k, tj, tb, tf: (e, tj[e, k], 0)),
            scratch_shapes=[pltpu.VMEM((SLOT_TILE, DX), F32)]),
        out_shape=jax.ShapeDtypeStruct((E, cap, D), F32),
        compiler_params=_params(("parallel", "arbitrary")),
        name="experts_gather_mlp",
    )(tile, blk, flags, u2ext, posm.reshape(E, 1, T), wg, wu, wd)


def _combine_kernel(bstart_ref, posm_ref, ye_ref, o_ref, buf, sem, *, cap):
    E = N_EXPERTS
    WN = COMBINE_WIN
    b = pl.program_id(0)
    top = cap - WN

    def window(e, base):
        return pltpu.make_async_copy(ye_ref.at[e, pl.ds(pl.multiple_of(base, 8), WN)],
                                     buf.at[pl.ds(e * WN, WN)], sem.at[e])

    lows = []
    for e in range(E):
        low = (bstart_ref[e, b] // 8) * 8
        lows.append(low)
        window(e, jnp.minimum(low, top)).start()

    w_iota = lax.broadcasted_iota(jnp.int32, (WN, ROUTE_BLK), 0)

    def onehot_rows(slot0, low, base):
        ok = (slot0 >= low) & (slot0 < low + WN)
        return jnp.where(ok & (slot0 - base == w_iota), 1.0, 0.0).astype(BF16)

    def scatter(onehot, rows):
        hi = rows.astype(BF16)
        lo = (rows - hi.astype(F32)).astype(BF16)
        tn = lambda a, c: lax.dot_general(a, c, (((0,), (0,)), ((), ())), preferred_element_type=F32)
        return tn(onehot, hi) + tn(onehot, lo)

    hots = []
    for e in range(E):
        slot0 = posm_ref[e:e + 1, :] - 1
        hots.append(onehot_rows(slot0, lows[e], jnp.minimum(lows[e], top)))
    for e in range(E):
        window(e, jnp.minimum(lows[e], top)).wait()
    o_ref[...] = scatter(jnp.concatenate(hots, axis=0), buf[...])

    def per_expert(e, carry):
        low0 = (bstart_ref[e, b] // 8) * 8
        nwin = (bstart_ref[e, b + 1] - low0 + WN - 1) // WN

        def per_window(c, carry):
            low = low0 + c * WN
            base = jnp.minimum(low, top)
            cp = pltpu.make_async_copy(ye_ref.at[e, pl.ds(pl.multiple_of(base, 8), WN)],
                                       buf.at[pl.ds(0, WN)], sem.at[0])
            cp.start()
            cp.wait()
            mine = lax.broadcasted_iota(jnp.int32, (E, ROUTE_BLK), 0) == e
            slot0 = jnp.sum(jnp.where(mine, posm_ref[...], 0), axis=0, keepdims=True) - 1
            o_ref[...] += scatter(onehot_rows(slot0, low, base), buf[0:WN, :])
            return carry

        return lax.fori_loop(1, nwin, per_window, carry)

    lax.fori_loop(0, E, per_expert, 0)


def _combine(ye, posm, bstart, cap):
    E, T = posm.shape
    D = D_MODEL
    nb = T // ROUTE_BLK
    assert cap >= COMBINE_WIN
    return pl.pallas_call(
        functools.partial(_combine_kernel, cap=cap),
        grid_spec=pltpu.PrefetchScalarGridSpec(
            num_scalar_prefetch=1,
            grid=(nb,),
            in_specs=[pl.BlockSpec((E, ROUTE_BLK), lambda b, bs: (0, b)),
                      pl.BlockSpec(memory_space=pl.ANY)],
            out_specs=pl.BlockSpec((ROUTE_BLK, D), lambda b, bs: (b, 0)),
            scratch_shapes=[pltpu.VMEM((E * COMBINE_WIN, D), F32), pltpu.SemaphoreType.DMA((E,))]),
        out_shape=jax.ShapeDtypeStruct((T, D), F32),
        compiler_params=_params(("arbitrary",)),
        name="experts_combine",
    )(bstart, posm, ye)


def _ple_kernel(h_ref, moe_ref, p_ref, wpg_ref, bpg_ref, wpp_ref, gfin_ref, o_ref):
    h = h_ref[...] + moe_ref[...]
    gate = jax.nn.sigmoid(jnp.dot(h.astype(BF16), wpg_ref[...], preferred_element_type=F32) + bpg_ref[...])
    h = h + gate * jnp.dot(p_ref[...].astype(BF16), wpp_ref[...], preferred_element_type=F32)
    o_ref[...] = h * lax.rsqrt(jnp.mean(h * h, axis=-1, keepdims=True) + EPS) * gfin_ref[...]


def _ple(h, moe, p, wpg, bpg, wpp, g_final, *, tm=256):
    B, L, D = h.shape
    row = lambda w: pl.BlockSpec((None, tm, w), lambda b, i: (b, i, 0))
    return pl.pallas_call(
        _ple_kernel,
        grid=(B, L // tm),
        in_specs=[row(D), row(D), row(PLE_DIM), _const_spec(wpg.shape), _const_spec((1, D)),
                  _const_spec(wpp.shape), _const_spec((1, D))],
        out_specs=row(D),
        out_shape=jax.ShapeDtypeStruct((B, L, D), F32),
        compiler_params=_params(("parallel", "parallel")),
        name="ple_final_norm",
    )(h, moe, p, wpg, bpg, wpp, g_final)


def _forward(x, p, w, btiles):
    B, L, D = x.shape
    T = B * L
    qt, k, vt, z, xbc, dt, dtt, gates = _inproj(x, w['g_mix'], w['wqt'], w['wk'], w['wvt'], w['wz'], w['wxbc'],
                                                w['wdt'], w['wdtt'], w['wgate'], w['bgate'])
    att = _attention(qt, k, vt, btiles, w['lam'], w['g_subln'])
    xs, bc, dts, dtst = _conv(xbc, dt, dtt, w['conv_w'], w['conv_b'], w['dtb'], w['dtbt'])
    ssm = _ssd(xs, bc, dts, dtst, w['alog'], w['alogt'], z, w['dskip'], w['g_ssm_norm'])
    h, u2ext, afft = _merge(att, ssm, gates, x, w['wau'], w['wsu'], w['wout'], w['g_ffn'], w['wr'])
    cap = max(1, CAPACITY_FACTOR * T // N_EXPERTS)
    posm, bstart = _route(afft, cap)
    tile, blk, flags = _pair_schedule(bstart, cap, T // GATHER_BLK)
    ye = _experts(u2ext.reshape(T, D + GATE_EXT), posm, tile, blk, flags, w['weg'], w['weu'], w['wed'], cap)
    moe = _combine(ye, posm, bstart, cap).reshape(B, L, D)
    return _ple(h, moe, p, w['wpg'], w['bpg'], w['wpp'], w['g_final'])


def kernel(x_prompt, x_sample, p_prompt, p_sample, rel_bias, g_mix, w_in, conv_w, conv_b, dt_bias, a_log,
           d_skip, g_ssm_norm, lam_params, g_subln, w_att_up, w_ssm_up, w_gate, b_gate, w_out, g_ffn,
           w_router, w_exp_gate, w_exp_up, w_exp_down, w_ple_gate, b_ple_gate, w_ple_proj, g_final):
    i = 0
    D = D_MODEL
    win = w_in[i]
    o1 = ATT_W
    o3 = 3 * ATT_W
    o4 = o3 + SSM_INNER
    o5 = o4 + CONV_CH
    wqt = (win[:, :o1] * (QK_DIM ** -0.5 * LOG2E)).T.astype(BF16)
    wk = win[:, o1:2 * o1].astype(BF16)
    wvt = win[:, 2 * o1:o3].T.astype(BF16)
    wdt = win[:, o5:].astype(BF16)
    lp = lam_params[i].astype(F32)
    lam = jnp.exp(jnp.sum(lp[0] * lp[1])) - jnp.exp(jnp.sum(lp[2] * lp[3])) + LAM_INIT
    ndt = 2 * SSM_HEADS
    w = {
        'g_mix': g_mix[i].reshape(1, D), 'wqt': wqt, 'wk': wk, 'wvt': wvt, 'wz': win[:, o3:o4].astype(BF16),
        'wxbc': win[:, o4:o5].astype(BF16), 'wdt': wdt, 'wdtt': wdt.T,
        'wgate': w_gate[i].astype(BF16), 'bgate': b_gate[i].reshape(1, 2 * D),
        'lam': lam.reshape(1), 'g_subln': g_subln[i].reshape(HEAD_W, 1),
        'conv_w': conv_w[i], 'conv_b': conv_b[i].reshape(1, CONV_CH),
        'dtb': dt_bias[i].reshape(1, ndt), 'dtbt': dt_bias[i].reshape(ndt, 1),
        'alog': a_log[i].reshape(1, ndt), 'alogt': a_log[i].reshape(ndt, 1),
        'dskip': jnp.repeat(d_skip[i], SSM_P).reshape(1, SSM_INNER),
        'g_ssm_norm': g_ssm_norm[i].reshape(1, SSM_INNER),
        'wau': w_att_up[i].astype(BF16), 'wsu': w_ssm_up[i].astype(BF16), 'wout': w_out[i].astype(BF16),
        'g_ffn': g_ffn[i].reshape(1, D), 'wr': w_router[i].astype(BF16),
        'weg': w_exp_gate[i].astype(BF16), 'weu': w_exp_up[i].astype(BF16), 'wed': w_exp_down[i].astype(BF16),
        'wpg': w_ple_gate[i].astype(BF16), 'bpg': b_ple_gate[i].reshape(1, D),
        'wpp': w_ple_proj[i].astype(BF16), 'g_final': g_final.reshape(1, D),
    }
    btiles = _bias_tiles(rel_bias, ATT_BLOCK)
    return (_forward(x_prompt, p_prompt[i], w, btiles), _forward(x_sample, p_sample[i], w, btiles))
```

```python
import functools
import math

import jax
import jax.numpy as jnp
from jax import lax
from jax.experimental import pallas as pl
from jax.experimental.pallas import tpu as pltpu

F32 = jnp.float32
BF16 = jnp.bfloat16

D_MODEL = 1024
N_HEADS = 4
QK_DIM = 64
HEAD_W = 2 * QK_DIM
ATT_W = N_HEADS * HEAD_W
REL_BUCKETS = 32
REL_MAX_DIST = 128
SSM_INNER = 1024
SSM_P = 64
SSM_HEADS = 16
SSM_GROUPS = 2
SSM_N = 64
SSM_CHUNK = 128
CONV_CH = SSM_INNER + 2 * SSM_GROUPS * SSM_N
N_EXPERTS = 16
CAPACITY_FACTOR = 2
PLE_DIM = 256
EPS = 1e-6
LAM_INIT = 0.8 - 0.6 * math.exp(-0.3 * 0)
LOG2E = math.log2(math.e)

GATE_EXT = 128
ROUTE_BLK = 256
GATHER_BLK = 512
SLOT_TILE = 256
COMBINE_WIN = 72
BSTART_W = 256

ATT_BLOCK = 256
ATT_QK_RATIO = 2
ATT_BIAS_TILES = ATT_QK_RATIO + 4
ONES_ROWS = 16
VMEM_LIMIT = 56 * 1024 * 1024


def _params(sem, vmem=VMEM_LIMIT):
    return pltpu.CompilerParams(dimension_semantics=sem, vmem_limit_bytes=vmem)


def _const_spec(shape):
    nd = len(shape)
    return pl.BlockSpec(shape, lambda *_: (0,) * nd)


def _nt_dot(a, b, **kw):
    return lax.dot_general(a, b, (((1,), (1,)), ((), ())), preferred_element_type=F32, **kw)


def _inproj_kernel(x_ref, g_ref, wqt_ref, wk_ref, wvt_ref, wz_ref, wxbc_ref, wdt_ref, wdtt_ref, wgate_ref,
                   bgate_ref, qt_ref, k_ref, vt_ref, z_ref, xbc_ref, dt_ref, dtt_ref, gates_ref):
    x = x_ref[...]
    u = x * lax.rsqrt(jnp.mean(x * x, axis=-1, keepdims=True) + EPS) * g_ref[...]
    ub = u.astype(BF16)
    qt_ref[...] = _nt_dot(wqt_ref[...], ub).astype(BF16)
    k_ref[...] = jnp.dot(ub, wk_ref[...], preferred_element_type=F32).astype(BF16)
    vt_ref[...] = _nt_dot(wvt_ref[...], ub).astype(BF16)
    z_ref[...] = jnp.dot(ub, wz_ref[...], preferred_element_type=F32)
    xbc_ref[...] = jnp.dot(ub, wxbc_ref[...], preferred_element_type=F32)
    dt_ref[...] = jnp.dot(ub, wdt_ref[...], preferred_element_type=F32)
    dtt_ref[...] = _nt_dot(wdtt_ref[...], ub)
    gates_ref[...] = jax.nn.sigmoid(jnp.dot(ub, wgate_ref[...], preferred_element_type=F32) + bgate_ref[...])


def _inproj(x, g_mix, wqt, wk, wvt, wz, wxbc, wdt, wdtt, wgate, bgate, *, tm=256):
    B, L, D = x.shape
    grid = (B, L // tm)
    row = lambda w: pl.BlockSpec((None, tm, w), lambda b, i: (b, i, 0))
    col = lambda w: pl.BlockSpec((None, w, tm), lambda b, i: (b, 0, i))
    ndt = 2 * SSM_HEADS
    return pl.pallas_call(
        _inproj_kernel,
        grid=grid,
        in_specs=[row(D), _const_spec((1, D)), _const_spec(wqt.shape), _const_spec(wk.shape),
                  _const_spec(wvt.shape), _const_spec(wz.shape),
                  _const_spec(wxbc.shape), _const_spec(wdt.shape), _const_spec(wdtt.shape),
                  _const_spec(wgate.shape), _const_spec((1, 2 * D))],
        out_specs=[col(ATT_W), row(ATT_W), col(ATT_W), row(SSM_INNER), row(CONV_CH), row(ndt),
                   col(ndt), row(2 * D)],
        out_shape=[jax.ShapeDtypeStruct((B, ATT_W, L), BF16),
                   jax.ShapeDtypeStruct((B, L, ATT_W), BF16),
                   jax.ShapeDtypeStruct((B, ATT_W, L), BF16),
                   jax.ShapeDtypeStruct((B, L, SSM_INNER), F32),
                   jax.ShapeDtypeStruct((B, L, CONV_CH), F32),
                   jax.ShapeDtypeStruct((B, L, ndt), F32),
                   jax.ShapeDtypeStruct((B, ndt, L), F32),
                   jax.ShapeDtypeStruct((B, L, 2 * D), F32)],
        compiler_params=_params(("parallel", "parallel")),
        name="inproj",
    )(x, g_mix, wqt, wk, wvt, wz, wxbc, wdt, wdtt, wgate, bgate)


def _rel_bucket(rel):
    half = REL_BUCKETS // 2
    max_exact = half // 2
    ret = jnp.where(rel > 0, half, 0)
    n = jnp.abs(rel)
    nf = jnp.maximum(n, 1).astype(jnp.float32)
    large = max_exact + (jnp.log(nf / max_exact) / math.log(REL_MAX_DIST / max_exact)
                         * (half - max_exact)).astype(jnp.int32)
    large = jnp.minimum(large, half - 1)
    return ret + jnp.where(n < max_exact, n, large)


def _bias_kernel(table_ref, bucket_ref, out_ref, *, TQ):
    h = pl.program_id(0)
    out_ref[...] = jnp.zeros(out_ref.shape, F32)
    for d in range(ATT_BIAS_TILES):
        for m in range(2):
            cols = pl.ds(m * TQ, TQ)

            def body(b, carry, d=d, m=m, cols=cols):
                val = table_ref[b, 2 * h + m] * LOG2E
                out_ref[d, :, cols] = jnp.where(bucket_ref[d] == b, val, out_ref[d, :, cols])
                return carry

            lax.fori_loop(0, REL_BUCKETS, body, 0)


def _bias_tiles(rel_bias, TQ, TK):
    assert TK > REL_MAX_DIST
    base = jnp.arange(TK, dtype=jnp.int32)[:, None] - jnp.arange(TQ, dtype=jnp.int32)[None, :]
    bucket = jnp.stack([_rel_bucket(base + (d - 2) * TK) for d in range(ATT_BIAS_TILES)])
    return pl.pallas_call(
        functools.partial(_bias_kernel, TQ=TQ),
        grid=(N_HEADS,),
        in_specs=[pl.BlockSpec(memory_space=pltpu.SMEM), _const_spec((ATT_BIAS_TILES, TK, TQ))],
        out_specs=pl.BlockSpec((None, ATT_BIAS_TILES, TK, 2 * TQ), lambda h: (h, 0, 0, 0)),
        out_shape=jax.ShapeDtypeStruct((N_HEADS, ATT_BIAS_TILES, TK, 2 * TQ), F32),
        compiler_params=_params(("parallel",)),
        name="rel_bias_tiles",
    )(rel_bias.astype(F32), bucket)


def _attn_kernel(lam_ref, qt_ref, k_ref, vt_ref, bt_ref, g_ref, o_ref, acc_sc, s_sc, p_sc, *, TQ, T, nk):
    i = pl.program_id(2)
    qt = qt_ref[...]
    sub = lax.broadcasted_iota(jnp.int32, qt.shape, 0)
    zero = jnp.zeros_like(qt)
    q2t = jnp.concatenate([jnp.where(sub < QK_DIM, qt, zero), jnp.where(sub >= QK_DIM, qt, zero)], axis=1)
    acc_sc[...] = jnp.zeros(acc_sc.shape, F32)

    def scores(j, slot):
        kb = k_ref[pl.ds(pl.multiple_of(j * T, T), T), :]
        tile = jnp.clip(j - ATT_QK_RATIO * i, -2, ATT_QK_RATIO + 1) + 2
        s_sc[slot] = jnp.dot(kb, q2t, preferred_element_type=F32) + bt_ref[tile]

    def softmax(slot, m):
        s = s_sc[slot]
        m_new = jnp.maximum(m, jnp.max(s, axis=0, keepdims=True))
        p_sc[slot] = jnp.exp2(s - m_new).astype(BF16)
        return m_new, jnp.exp2(m - m_new)

    ones = jnp.ones((ONES_ROWS, T), BF16)

    def accumulate(j, slot, alpha):
        vb = jnp.concatenate([vt_ref[:, pl.ds(pl.multiple_of(j * T, T), T)], ones], axis=0)
        acc_sc[...] = alpha * acc_sc[...] + jnp.dot(vb, p_sc[slot], preferred_element_type=F32)

    m = jnp.full((1, 2 * TQ), -jnp.inf, F32)
    scores(0, 0)
    scores(1, 1)
    m, alpha = softmax(0, m)

    def pair(u, carry):
        m, alpha = carry
        t = 2 * u
        scores(t, 0)
        m, alpha1 = softmax(1, m)
        accumulate(t - 2, 0, alpha)
        scores(t + 1, 1)
        m, alpha0 = softmax(0, m)
        accumulate(t - 1, 1, alpha1)
        return m, alpha0

    m, alpha = lax.fori_loop(1, nk // 2, pair, (m, alpha))
    m, alpha1 = softmax(1, m)
    accumulate(nk - 2, 0, alpha)
    accumulate(nk - 1, 1, alpha1)

    o = acc_sc[0:HEAD_W, :] / acc_sc[HEAD_W:HEAD_W + 1, :]
    od = o[:, :TQ] - lam_ref[0] * o[:, TQ:]
    y = od * lax.rsqrt(jnp.mean(od * od, axis=0, keepdims=True) + EPS) * g_ref[...]
    o_ref[...] = (y * (1.0 - LAM_INIT)).T.astype(o_ref.dtype)


def _attention(qt, k, vt, btiles, lam, g_col, *, T=ATT_BLOCK):
    B, L, _ = k.shape
    TQ = ATT_QK_RATIO * T
    nk = L // T
    assert nk % 2 == 0 and nk >= 4 and L % TQ == 0
    return pl.pallas_call(
        functools.partial(_attn_kernel, TQ=TQ, T=T, nk=nk),
        grid=(B, N_HEADS, L // TQ),
        in_specs=[pl.BlockSpec(memory_space=pltpu.SMEM),
                  pl.BlockSpec((None, HEAD_W, TQ), lambda b, h, i: (b, h, i)),
                  pl.BlockSpec((None, L, HEAD_W), lambda b, h, i: (b, 0, h)),
                  pl.BlockSpec((None, HEAD_W, L), lambda b, h, i: (b, h, 0)),
                  pl.BlockSpec((None, ATT_BIAS_TILES, T, 2 * TQ), lambda b, h, i: (h, 0, 0, 0)),
                  _const_spec((HEAD_W, 1))],
        out_specs=pl.BlockSpec((None, TQ, HEAD_W), lambda b, h, i: (b, i, h)),
        out_shape=jax.ShapeDtypeStruct((B, L, ATT_W), BF16),
        scratch_shapes=[pltpu.VMEM((HEAD_W + ONES_ROWS, 2 * TQ), F32), pltpu.VMEM((2, T, 2 * TQ), F32),
                        pltpu.VMEM((2, T, 2 * TQ), BF16)],
        compiler_params=_params(("parallel", "parallel", "parallel")),
        name="diff_attention",
    )(lam, qt, k, vt, btiles, g_col)


def _conv_kernel(x_ref, prev_ref, next_ref, dt_ref, dtt_ref, w_ref, b_ref, dtb_ref, dtbt_ref,
                 xs_ref, bc_ref, dts_ref, dtst_ref, *, tl, nt):
    i = pl.program_id(1)
    x = x_ref[...]
    row = lax.broadcasted_iota(jnp.int32, x.shape, 0)
    prev_row = jnp.where(i > 0, prev_ref[7:8, :], 0.0)
    next_row = jnp.where(i < nt - 1, next_ref[0:1, :], 0.0)
    xm1 = jnp.where(row == 0, prev_row, pltpu.roll(x, 1, 0))
    xp1 = jnp.where(row == tl - 1, next_row, pltpu.roll(x, tl - 1, 0))
    y = w_ref[0:1, :] * xm1 + w_ref[1:2, :] * x + w_ref[2:3, :] * xp1 + b_ref[...]
    y = y * jax.nn.sigmoid(y)
    xs_ref[...] = y[:, :SSM_INNER].astype(BF16)
    bc_ref[...] = y[:, SSM_INNER:].astype(BF16)

    def softplus(v):
        return jnp.maximum(v, 0.0) + jnp.log1p(jnp.exp(-jnp.abs(v)))

    dts_ref[...] = softplus(dt_ref[...] + dtb_ref[...])
    dtst_ref[...] = softplus(dtt_ref[...] + dtbt_ref[...])


def _conv(xbc, dt, dtt, conv_w, conv_b, dtb, dtbt, *, tl=512):
    B, L, _ = xbc.shape
    nt = L // tl
    r8 = tl // 8
    ndt = 2 * SSM_HEADS
    row = lambda w: pl.BlockSpec((None, tl, w), lambda b, i: (b, i, 0))
    colspec = pl.BlockSpec((None, ndt, tl), lambda b, i: (b, 0, i))
    return pl.pallas_call(
        functools.partial(_conv_kernel, tl=tl, nt=nt),
        grid=(B, nt),
        in_specs=[row(CONV_CH),
                  pl.BlockSpec((None, 8, CONV_CH), lambda b, i: (b, jnp.maximum(i * r8 - 1, 0), 0)),
                  pl.BlockSpec((None, 8, CONV_CH), lambda b, i: (b, jnp.minimum((i + 1) * r8, L // 8 - 1), 0)),
                  row(ndt), colspec,
                  _const_spec((3, CONV_CH)), _const_spec((1, CONV_CH)), _const_spec((1, ndt)),
                  _const_spec((ndt, 1))],
        out_specs=[row(SSM_INNER), row(2 * SSM_GROUPS * SSM_N), row(ndt), colspec],
        out_shape=[jax.ShapeDtypeStruct((B, L, SSM_INNER), BF16),
                   jax.ShapeDtypeStruct((B, L, 2 * SSM_GROUPS * SSM_N), BF16),
                   jax.ShapeDtypeStruct((B, L, ndt), F32),
                   jax.ShapeDtypeStruct((B, ndt, L), F32)],
        compiler_params=_params(("parallel", "parallel")),
        name="ssm_conv",
    )(xbc, xbc, xbc, dt, dtt, conv_w, conv_b, dtb, dtbt)


def _ssd_chunk(xs_ref, bc_ref, dts_ref, dtst_ref, alog_ref, alogt_ref, st_ref, *, rev):
    Q = SSM_CHUNK
    HI = lax.Precision.HIGHEST
    off = SSM_HEADS if rev else 0
    ri = lax.broadcasted_iota(jnp.int32, (Q, Q), 0)
    ci = lax.broadcasted_iota(jnp.int32, (Q, Q), 1)
    incl = (ci >= ri) if rev else (ci <= ri)
    tri = incl.astype(F32)
    dt = dts_ref[:, off:off + SSM_HEADS]
    dtt = dtst_ref[off:off + SSM_HEADS, :]
    a = dt * (-jnp.exp(alog_ref[:, off:off + SSM_HEADS]))
    at = dtt * (-jnp.exp(alogt_ref[off:off + SSM_HEADS, :]))
    acum = jnp.dot(tri, a, preferred_element_type=F32, precision=HI)
    acumt = _nt_dot(at, tri, precision=HI)
    edge = 0 if rev else Q - 1
    bc = bc_ref[...]
    bpair = bc[:, :Q]
    cpair = bc[:, Q:]
    low = ci < SSM_N
    rlow = ri < SSM_N
    blockmask = low == rlow
    ys = []
    for g in range(SSM_GROUPS):
        gmask = low if g == 0 else jnp.logical_not(low)
        cg = jnp.where(gmask, cpair, jnp.zeros_like(cpair))
        cb = _nt_dot(cg, bpair)
        dupm = ((ri - g * SSM_N) == (ci % SSM_N)).astype(BF16)
        cdup = jnp.dot(cpair, dupm, preferred_element_type=F32)
        sel2 = ((ci - g * SSM_N) == (ri % SSM_N)).astype(BF16)
        bt2 = _nt_dot(sel2, bpair)
        for hp in range(SSM_HEADS // SSM_GROUPS // 2):
            h0 = g * (SSM_HEADS // SSM_GROUPS) + 2 * hp
            pair = h0 // 2
            xpair = xs_ref[:, h0 * SSM_P:(h0 + 2) * SSM_P]
            ms = []
            es = []
            ws = []
            for h in (h0, h0 + 1):
                e_col = jnp.broadcast_to(acum[:, h:h + 1], (Q, Q))
                r_row = jnp.broadcast_to(acumt[h:h + 1, :], (Q, Q))
                dt_row = jnp.broadcast_to(dtt[h:h + 1, :], (Q, Q))
                seg = jnp.where(incl, e_col - r_row, -jnp.inf)
                ms.append((cb * jnp.exp(seg) * dt_row).astype(BF16))
                es.append(jnp.exp(e_col))
                ws.append(dt_row * jnp.exp(acumt[h:h + 1, edge:edge + 1] - r_row))
            st = st_ref[pair]
            xblk = jnp.concatenate([jnp.where(low, xpair, jnp.zeros_like(xpair)),
                                    jnp.where(low, jnp.zeros_like(xpair), xpair)], axis=0)
            y = jnp.dot(jnp.concatenate(ms, axis=1), xblk, preferred_element_type=F32)
            ce = (cdup * jnp.where(low, es[0], es[1])).astype(BF16)
            y = y + jnp.dot(ce, st.astype(BF16), preferred_element_type=F32)
            ys.append(y)
            btw = (bt2 * jnp.where(rlow, ws[0], ws[1])).astype(BF16)
            snew = jnp.dot(btw, xpair, preferred_element_type=F32)
            dec = jnp.where(rlow, jnp.exp(acumt[h0:h0 + 1, edge:edge + 1]),
                            jnp.exp(acumt[h0 + 1:h0 + 2, edge:edge + 1]))
            st_ref[pair] = dec * st + jnp.where(blockmask, snew, 0.0)
    return jnp.concatenate(ys, axis=1)


def _ssd_fwd_kernel(xs_ref, bc_ref, dts_ref, dtst_ref, alog_ref, alogt_ref, y_ref, st_ref):
    @pl.when(pl.program_id(1) == 0)
    def _():
        st_ref[...] = jnp.zeros(st_ref.shape, F32)

    y_ref[...] = _ssd_chunk(xs_ref, bc_ref, dts_ref, dtst_ref, alog_ref, alogt_ref, st_ref, rev=False)


def _ssd_bwd_kernel(xs_ref, bc_ref, dts_ref, dtst_ref, alog_ref, alogt_ref, yf_ref, z_ref, dskip_ref, g_ref,
                    o_ref, st_ref):
    @pl.when(pl.program_id(1) == 0)
    def _():
        st_ref[...] = jnp.zeros(st_ref.shape, F32)

    yb = _ssd_chunk(xs_ref, bc_ref, dts_ref, dtst_ref, alog_ref, alogt_ref, st_ref, rev=True)
    z = z_ref[...]
    y = (yf_ref[...] + yb + xs_ref[...].astype(F32) * dskip_ref[...]) * (z * jax.nn.sigmoid(z))
    y = y * lax.rsqrt(jnp.mean(y * y, axis=-1, keepdims=True) + EPS) * g_ref[...]
    o_ref[...] = y.astype(o_ref.dtype)


def _ssd(xs, bc, dts, dtst, alog, alogt, z, dskip, g_norm):
    B, L, _ = xs.shape
    Q = SSM_CHUNK
    nc = L // Q
    ndt = 2 * SSM_HEADS
    npair = SSM_HEADS // 2

    def specs(cidx):
        row = lambda w: pl.BlockSpec((None, Q, w), lambda b, c: (b, cidx(c), 0))
        return row, [row(SSM_INNER), row(2 * SSM_GROUPS * SSM_N), row(ndt),
                     pl.BlockSpec((None, ndt, Q), lambda b, c: (b, 0, cidx(c))),
                     _const_spec((1, ndt)), _const_spec((ndt, 1))]

    scratch = [pltpu.VMEM((npair, 2 * SSM_N, 2 * SSM_P), F32)]
    row, in_specs = specs(lambda c: c)
    yf = pl.pallas_call(
        _ssd_fwd_kernel,
        grid=(B, nc),
        in_specs=in_specs,
        out_specs=row(SSM_INNER),
        out_shape=jax.ShapeDtypeStruct((B, L, SSM_INNER), F32),
        scratch_shapes=scratch,
        compiler_params=_params(("parallel", "arbitrary")),
        name="ssd_forward",
    )(xs, bc, dts, dtst, alog, alogt)
    row, in_specs = specs(lambda c: nc - 1 - c)
    return pl.pallas_call(
        _ssd_bwd_kernel,
        grid=(B, nc),
        in_specs=in_specs + [row(SSM_INNER), row(SSM_INNER), _const_spec((1, SSM_INNER)),
                             _const_spec((1, SSM_INNER))],
        out_specs=row(SSM_INNER),
        out_shape=jax.ShapeDtypeStruct((B, L, SSM_INNER), BF16),
        scratch_shapes=scratch,
        compiler_params=_params(("parallel", "arbitrary")),
        name="ssd_reverse_gate_norm",
    )(xs, bc, dts, dtst, alog, alogt, yf, z, dskip, g_norm)


def _merge_kernel(att_ref, ssm_ref, gates_ref, x_ref, wau_ref, wsu_ref, wout_ref, gffn_ref, wr_ref, wrt_ref,
                  place_ref, h_ref, u2_ref, afft_ref):
    D = D_MODEL
    E = N_EXPERTS
    ya = jnp.dot(att_ref[...], wau_ref[...], preferred_element_type=F32)
    ysm = jnp.dot(ssm_ref[...], wsu_ref[...], preferred_element_type=F32)
    merged = gates_ref[:, :D] * ya + gates_ref[:, D:] * ysm
    h = x_ref[...] + jnp.dot(merged.astype(BF16), wout_ref[...], preferred_element_type=F32)
    h_ref[...] = h
    u2 = (h * lax.rsqrt(jnp.mean(h * h, axis=-1, keepdims=True) + EPS) * gffn_ref[...]).astype(BF16)
    u2_ref[:, :D] = u2
    logits = jnp.dot(u2, wr_ref[...], preferred_element_type=F32)
    e = jnp.exp(logits - jnp.max(logits, axis=-1, keepdims=True))
    aff = e / jnp.sum(e, axis=-1, keepdims=True)
    hi = aff.astype(BF16)
    r1 = aff - hi.astype(F32)
    mid = r1.astype(BF16)
    lo = (r1 - mid.astype(F32)).astype(BF16)
    ext = (jnp.dot(hi, place_ref[0:E, :], preferred_element_type=F32)
           + jnp.dot(mid, place_ref[E:2 * E, :], preferred_element_type=F32)
           + jnp.dot(lo, place_ref[2 * E:3 * E, :], preferred_element_type=F32))
    u2_ref[:, D:] = ext.astype(BF16)
    lt = _nt_dot(wrt_ref[...], u2)
    et = jnp.exp(lt - jnp.max(lt, axis=0, keepdims=True))
    afft_ref[...] = et / jnp.sum(et, axis=0, keepdims=True)


def _merge(att, ssm, gates, x, wau, wsu, wout, g_ffn, wr, *, tm=256):
    B, L, D = x.shape
    E = N_EXPERTS
    nl = L // tm
    row = lambda w: pl.BlockSpec((None, tm, w), lambda b, i: (b, i, 0))
    place = jnp.eye(3 * E, GATE_EXT, dtype=BF16)
    return pl.pallas_call(
        _merge_kernel,
        grid=(B, nl),
        in_specs=[row(ATT_W), row(SSM_INNER), row(2 * D), row(D), _const_spec(wau.shape),
                  _const_spec(wsu.shape), _const_spec(wout.shape), _const_spec((1, D)), _const_spec(wr.shape),
                  _const_spec((E, D)), _const_spec((3 * E, GATE_EXT))],
        out_specs=[row(D), row(D + GATE_EXT), pl.BlockSpec((E, tm), lambda b, i: (0, b * nl + i))],
        out_shape=[jax.ShapeDtypeStruct((B, L, D), F32), jax.ShapeDtypeStruct((B, L, D + GATE_EXT), BF16),
                   jax.ShapeDtypeStruct((E, B * L), F32)],
        compiler_params=_params(("parallel", "parallel")),
        name="merge_router",
    )(att, ssm, gates, x, wau, wsu, wout, g_ffn, wr, wr.T, place)


def _route_kernel(afft_ref, posm_ref, bstart_ref, *, cap, nb):
    E = N_EXPERTS
    W = ROUTE_BLK

    def bit_body(k, thr):
        cand = thr | lax.shift_left(jnp.int32(1), 30 - k)
        x = lax.bitcast_convert_type(afft_ref[...], jnp.int32)
        cnt = jnp.sum(jnp.where(x >= cand, 1.0, 0.0), axis=1, keepdims=True)
        return jnp.where(cnt >= cap, cand, thr)

    thr = lax.fori_loop(0, 31, bit_body, jnp.zeros((E, 1), jnp.int32))
    x = lax.bitcast_convert_type(afft_ref[...], jnp.int32)
    n_gt = jnp.sum(jnp.where(x > thr, 1.0, 0.0), axis=1, keepdims=True)
    need = cap - n_gt
    ri = lax.broadcasted_iota(jnp.int32, (W, W), 0)
    ci = lax.broadcasted_iota(jnp.int32, (W, W), 1)
    upper = jnp.where(ri <= ci, 1.0, 0.0).astype(BF16)
    bstart_ref[...] = jnp.zeros(bstart_ref.shape, jnp.int32)
    c_eq = jnp.zeros((E, 1), F32)
    c_sel = jnp.zeros((E, 1), F32)
    for b in range(nb):
        xb = lax.bitcast_convert_type(afft_ref[:, b * W:(b + 1) * W], jnp.int32)
        eq = xb == thr
        eqc = jnp.dot(jnp.where(eq, 1.0, 0.0).astype(BF16), upper, preferred_element_type=F32) + c_eq
        sel = (xb > thr) | (eq & (eqc <= need))
        selc = jnp.dot(jnp.where(sel, 1.0, 0.0).astype(BF16), upper, preferred_element_type=F32) + c_sel
        posm_ref[:, b * W:(b + 1) * W] = jnp.where(sel, selc, 0.0).astype(jnp.int32)
        bstart_ref[:, b:b + 1] = c_sel.astype(jnp.int32)
        c_eq = eqc[:, W - 1:W]
        c_sel = selc[:, W - 1:W]
    bstart_ref[:, nb:nb + 1] = c_sel.astype(jnp.int32)


def _route(afft, cap):
    E, T = afft.shape
    nb = T // ROUTE_BLK
    assert nb < BSTART_W and cap % SLOT_TILE == 0
    return pl.pallas_call(
        functools.partial(_route_kernel, cap=float(cap), nb=nb),
        out_shape=[jax.ShapeDtypeStruct((E, T), jnp.int32), jax.ShapeDtypeStruct((E, BSTART_W), jnp.int32)],
        compiler_params=_params(None),
        name="route_select",
    )(afft)


def _pair_schedule(bstart, cap, ngb):
    step = GATHER_BLK // ROUTE_BLK
    st = bstart[:, 0:ngb * step + 1:step]
    lo, hi = st[:, :-1], st[:, 1:]
    ntile = cap // SLOT_TILE
    t_lo = jnp.arange(ntile, dtype=jnp.int32) * SLOT_TILE
    t_hi = t_lo + SLOT_TILE
    b_first = jnp.sum(hi[:, None, :] <= t_lo[None, :, None], axis=2).astype(jnp.int32)
    b_last = jnp.sum(lo[:, None, :] < t_hi[None, :, None], axis=2).astype(jnp.int32) - 1
    return b_first, b_last - b_first + 1


def _expert_kernel(first_ref, cnt_ref, u_hbm, posm_ref, wg_ref, wu_ref, wd_ref, o_ref, x_sc, ubuf, sem):
    D = D_MODEL
    e = pl.program_id(0)
    j = pl.program_id(1)
    b0 = first_ref[e, j]
    n = cnt_ref[e, j]

    def block_copy(b, slot):
        return pltpu.make_async_copy(u_hbm.at[pl.ds(pl.multiple_of(b * GATHER_BLK, GATHER_BLK), GATHER_BLK)],
                                     ubuf.at[slot], sem.at[slot])

    block_copy(b0, 0).start()
    x_sc[...] = jnp.zeros(x_sc.shape, F32)
    slot_ids = lax.broadcasted_iota(jnp.int32, (SLOT_TILE, GATHER_BLK), 0) + (j * SLOT_TILE + 1)

    def body(c, carry):
        slot = c % 2
        block_copy(b0 + c, slot).wait()

        @pl.when(c + 1 < n)
        def _():
            block_copy(b0 + c + 1, 1 - slot).start()

        posrow = posm_ref[:, pl.ds(pl.multiple_of((b0 + c) * GATHER_BLK, GATHER_BLK), GATHER_BLK)]
        onehot = jnp.where(posrow == slot_ids, 1.0, 0.0).astype(BF16)
        x_sc[...] += jnp.dot(onehot, ubuf[slot], preferred_element_type=F32)
        return carry

    lax.fori_loop(0, n, body, 0)
    xs = x_sc[:, :D].astype(BF16)
    ext = x_sc[:, D:]
    lane = lax.broadcasted_iota(jnp.int32, ext.shape, 1)
    mine = (lane == e) | (lane == e + N_EXPERTS) | (lane == e + 2 * N_EXPERTS)
    gate = jnp.sum(jnp.where(mine, ext, 0.0), axis=1, keepdims=True)
    hg = jnp.dot(xs, wg_ref[...], preferred_element_type=F32)
    hu = jnp.dot(xs, wu_ref[...], preferred_element_type=F32)
    hmid = (hg * jax.nn.sigmoid(hg) * hu).astype(BF16)
    o_ref[...] = jnp.dot(hmid, wd_ref[...], preferred_element_type=F32) * gate


def _experts(u2ext, posm, b_first, b_cnt, wg, wu, wd, cap):
    T, DX = u2ext.shape
    E, D = N_EXPERTS, D_MODEL
    wspec = pl.BlockSpec((None, D, D), lambda e, j, bf, bc: (e, 0, 0))
    return pl.pallas_call(
        _expert_kernel,
        grid_spec=pltpu.PrefetchScalarGridSpec(
            num_scalar_prefetch=2,
            grid=(E, cap // SLOT_TILE),
            in_specs=[pl.BlockSpec(memory_space=pl.ANY),
                      pl.BlockSpec((None, 1, T), lambda e, j, bf, bc: (e, 0, 0)),
                      wspec, wspec, wspec],
            out_specs=pl.BlockSpec((None, SLOT_TILE, D), lambda e, j, bf, bc: (e, j, 0)),
            scratch_shapes=[pltpu.VMEM((SLOT_TILE, DX), F32), pltpu.VMEM((2, GATHER_BLK, DX), BF16),
                            pltpu.SemaphoreType.DMA((2,))]),
        out_shape=jax.ShapeDtypeStruct((E, cap, D), F32),
        compiler_params=_params(("parallel", "parallel")),
        name="experts_gather_mlp",
    )(b_first, b_cnt, u2ext, posm.reshape(E, 1, T), wg, wu, wd)


def _combine_kernel(bstart_ref, posm_ref, ye_ref, o_ref, buf, sem, *, cap):
    E = N_EXPERTS
    WN = COMBINE_WIN
    b = pl.program_id(0)
    top = cap - WN

    def window(e, base):
        return pltpu.make_async_copy(ye_ref.at[e, pl.ds(pl.multiple_of(base, 8), WN)],
                                     buf.at[pl.ds(e * WN, WN)], sem.at[e])

    lows = []
    for e in range(E):
        low = (bstart_ref[e, b] // 8) * 8
        lows.append(low)
        window(e, jnp.minimum(low, top)).start()

    w_iota = lax.broadcasted_iota(jnp.int32, (WN, ROUTE_BLK), 0)

    def onehot_rows(slot0, low, base):
        ok = (slot0 >= low) & (slot0 < low + WN)
        return jnp.where(ok & (slot0 - base == w_iota), 1.0, 0.0).astype(BF16)

    def scatter(onehot, rows):
        hi = rows.astype(BF16)
        lo = (rows - hi.astype(F32)).astype(BF16)
        tn = lambda a, c: lax.dot_general(a, c, (((0,), (0,)), ((), ())), preferred_element_type=F32)
        return tn(onehot, hi) + tn(onehot, lo)

    hots = []
    for e in range(E):
        slot0 = posm_ref[e:e + 1, :] - 1
        hots.append(onehot_rows(slot0, lows[e], jnp.minimum(lows[e], top)))
    for e in range(E):
        window(e, jnp.minimum(lows[e], top)).wait()
    o_ref[...] = scatter(jnp.concatenate(hots, axis=0), buf[...])

    def per_expert(e, carry):
        low0 = (bstart_ref[e, b] // 8) * 8
        nwin = (bstart_ref[e, b + 1] - low0 + WN - 1) // WN

        def per_window(c, carry):
            low = low0 + c * WN
            base = jnp.minimum(low, top)
            cp = pltpu.make_async_copy(ye_ref.at[e, pl.ds(pl.multiple_of(base, 8), WN)],
                                       buf.at[pl.ds(0, WN)], sem.at[0])
            cp.start()
            cp.wait()
            mine = lax.broadcasted_iota(jnp.int32, (E, ROUTE_BLK), 0) == e
            slot0 = jnp.sum(jnp.where(mine, posm_ref[...], 0), axis=0, keepdims=True) - 1
            o_ref[...] += scatter(onehot_rows(slot0, low, base), buf[0:WN, :])
            return carry

        return lax.fori_loop(1, nwin, per_window, carry)

    lax.fori_loop(0, E, per_expert, 0)


def _combine(ye, posm, bstart, cap):
    E, T = posm.shape
    D = D_MODEL
    nb = T // ROUTE_BLK
    assert cap >= COMBINE_WIN
    return pl.pallas_call(
        functools.partial(_combine_kernel, cap=cap),
        grid_spec=pltpu.PrefetchScalarGridSpec(
            num_scalar_prefetch=1,
            grid=(nb,),
            in_specs=[pl.BlockSpec((E, ROUTE_BLK), lambda b, bs: (0, b)),
                      pl.BlockSpec(memory_space=pl.ANY)],
            out_specs=pl.BlockSpec((ROUTE_BLK, D), lambda b, bs: (b, 0)),
            scratch_shapes=[pltpu.VMEM((E * COMBINE_WIN, D), F32), pltpu.SemaphoreType.DMA((E,))]),
        out_shape=jax.ShapeDtypeStruct((T, D), F32),
        compiler_params=_params(("arbitrary",)),
        name="experts_combine",
    )(bstart, posm, ye)


def _ple_kernel(h_ref, moe_ref, p_ref, wpg_ref, bpg_ref, wpp_ref, gfin_ref, o_ref):
    h = h_ref[...] + moe_ref[...]
    gate = jax.nn.sigmoid(jnp.dot(h.astype(BF16), wpg_ref[...], preferred_element_type=F32) + bpg_ref[...])
    h = h + gate * jnp.dot(p_ref[...].astype(BF16), wpp_ref[...], preferred_element_type=F32)
    o_ref[...] = h * lax.rsqrt(jnp.mean(h * h, axis=-1, keepdims=True) + EPS) * gfin_ref[...]


def _ple(h, moe, p, wpg, bpg, wpp, g_final, *, tm=256):
    B, L, D = h.shape
    row = lambda w: pl.BlockSpec((None, tm, w), lambda b, i: (b, i, 0))
    return pl.pallas_call(
        _ple_kernel,
        grid=(B, L // tm),
        in_specs=[row(D), row(D), row(PLE_DIM), _const_spec(wpg.shape), _const_spec((1, D)),
                  _const_spec(wpp.shape), _const_spec((1, D))],
        out_specs=row(D),
        out_shape=jax.ShapeDtypeStruct((B, L, D), F32),
        compiler_params=_params(("parallel", "parallel")),
        name="ple_final_norm",
    )(h, moe, p, wpg, bpg, wpp, g_final)


def _forward(x, p, w, btiles):
    B, L, D = x.shape
    T = B * L
    qt, k, vt, z, xbc, dt, dtt, gates = _inproj(x, w['g_mix'], w['wqt'], w['wk'], w['wvt'], w['wz'], w['wxbc'],
                                                w['wdt'], w['wdtt'], w['wgate'], w['bgate'])
    att = _attention(qt, k, vt, btiles, w['lam'], w['g_subln'])
    xs, bc, dts, dtst = _conv(xbc, dt, dtt, w['conv_w'], w['conv_b'], w['dtb'], w['dtbt'])
    ssm = _ssd(xs, bc, dts, dtst, w['alog'], w['alogt'], z, w['dskip'], w['g_ssm_norm'])
    h, u2ext, afft = _merge(att, ssm, gates, x, w['wau'], w['wsu'], w['wout'], w['g_ffn'], w['wr'])
    cap = max(1, CAPACITY_FACTOR * T // N_EXPERTS)
    posm, bstart = _route(afft, cap)
    b_first, b_cnt = _pair_schedule(bstart, cap, T // GATHER_BLK)
    ye = _experts(u2ext.reshape(T, D + GATE_EXT), posm, b_first, b_cnt, w['weg'], w['weu'], w['wed'], cap)
    moe = _combine(ye, posm, bstart, cap).reshape(B, L, D)
    return _ple(h, moe, p, w['wpg'], w['bpg'], w['wpp'], w['g_final'])


def kernel(x_prompt, x_sample, p_prompt, p_sample, rel_bias, g_mix, w_in, conv_w, conv_b, dt_bias, a_log,
           d_skip, g_ssm_norm, lam_params, g_subln, w_att_up, w_ssm_up, w_gate, b_gate, w_out, g_ffn,
           w_router, w_exp_gate, w_exp_up, w_exp_down, w_ple_gate, b_ple_gate, w_ple_proj, g_final):
    i = 0
    D = D_MODEL
    win = w_in[i]
    o1 = ATT_W
    o3 = 3 * ATT_W
    o4 = o3 + SSM_INNER
    o5 = o4 + CONV_CH
    wqt = (win[:, :o1] * (QK_DIM ** -0.5 * LOG2E)).T.astype(BF16)
    wk = win[:, o1:2 * o1].astype(BF16)
    wvt = win[:, 2 * o1:o3].T.astype(BF16)
    wdt = win[:, o5:].astype(BF16)
    lp = lam_params[i].astype(F32)
    lam = jnp.exp(jnp.sum(lp[0] * lp[1])) - jnp.exp(jnp.sum(lp[2] * lp[3])) + LAM_INIT
    ndt = 2 * SSM_HEADS
    w = {
        'g_mix': g_mix[i].reshape(1, D), 'wqt': wqt, 'wk': wk, 'wvt': wvt, 'wz': win[:, o3:o4].astype(BF16),
        'wxbc': win[:, o4:o5].astype(BF16), 'wdt': wdt, 'wdtt': wdt.T,
        'wgate': w_gate[i].astype(BF16), 'bgate': b_gate[i].reshape(1, 2 * D),
        'lam': lam.reshape(1), 'g_subln': g_subln[i].reshape(HEAD_W, 1),
        'conv_w': conv_w[i], 'conv_b': conv_b[i].reshape(1, CONV_CH),
        'dtb': dt_bias[i].reshape(1, ndt), 'dtbt': dt_bias[i].reshape(ndt, 1),
        'alog': a_log[i].reshape(1, ndt), 'alogt': a_log[i].reshape(ndt, 1),
        'dskip': jnp.repeat(d_skip[i], SSM_P).reshape(1, SSM_INNER),
        'g_ssm_norm': g_ssm_norm[i].reshape(1, SSM_INNER),
        'wau': w_att_up[i].astype(BF16), 'wsu': w_ssm_up[i].astype(BF16), 'wout': w_out[i].astype(BF16),
        'g_ffn': g_ffn[i].reshape(1, D), 'wr': w_router[i].astype(BF16),
        'weg': w_exp_gate[i].astype(BF16), 'weu': w_exp_up[i].astype(BF16), 'wed': w_exp_down[i].astype(BF16),
        'wpg': w_ple_gate[i].astype(BF16), 'bpg': b_ple_gate[i].reshape(1, D),
        'wpp': w_ple_proj[i].astype(BF16), 'g_final': g_final.reshape(1, D),
    }
    btiles = _bias_tiles(rel_bias, ATT_QK_RATIO * ATT_BLOCK, ATT_BLOCK)
    return (_forward(x_prompt, p_prompt[i], w, btiles), _forward(x_sample, p_sample[i], w, btiles))
```

```python
import functools
import math

import jax
import jax.numpy as jnp
from jax import lax
from jax.experimental import pallas as pl
from jax.experimental.pallas import tpu as pltpu

F32 = jnp.float32
BF16 = jnp.bfloat16

D_MODEL = 1024
N_HEADS = 4
QK_DIM = 64
HEAD_W = 2 * QK_DIM
ATT_W = N_HEADS * HEAD_W
REL_BUCKETS = 32
REL_MAX_DIST = 128
SSM_INNER = 1024
SSM_P = 64
SSM_HEADS = 16
SSM_GROUPS = 2
SSM_N = 64
SSM_CHUNK = 128
CONV_CH = SSM_INNER + 2 * SSM_GROUPS * SSM_N
N_EXPERTS = 16
CAPACITY_FACTOR = 2
PLE_DIM = 256
EPS = 1e-6
LAM_INIT = 0.8 - 0.6 * math.exp(-0.3 * 0)
LOG2E = math.log2(math.e)

GATE_EXT = 128
ROUTE_BLK = 256
GATHER_BLK = 512
GATHER_BUFS = 3
SLOT_TILE = 256
COMBINE_WIN = 72
BSTART_W = 256

ATT_BLOCK = 256
ATT_QK_RATIO = 2
ATT_BIAS_TILES = ATT_QK_RATIO + 4
ONES_ROWS = 16
VMEM_LIMIT = 56 * 1024 * 1024


def _params(sem, vmem=VMEM_LIMIT):
    return pltpu.CompilerParams(dimension_semantics=sem, vmem_limit_bytes=vmem)


def _const_spec(shape):
    nd = len(shape)
    return pl.BlockSpec(shape, lambda *_: (0,) * nd)


def _nt_dot(a, b, **kw):
    return lax.dot_general(a, b, (((1,), (1,)), ((), ())), preferred_element_type=F32, **kw)


def _inproj_kernel(x_ref, g_ref, wqt_ref, wk_ref, wvt_ref, wz_ref, wxbc_ref, wdt_ref, wdtt_ref, wgate_ref,
                   bgate_ref, qt_ref, k_ref, vt_ref, z_ref, xbc_ref, dt_ref, dtt_ref, gates_ref):
    x = x_ref[...]
    u = x * lax.rsqrt(jnp.mean(x * x, axis=-1, keepdims=True) + EPS) * g_ref[...]
    ub = u.astype(BF16)
    qt_ref[...] = _nt_dot(wqt_ref[...], ub).astype(BF16)
    k_ref[...] = jnp.dot(ub, wk_ref[...], preferred_element_type=F32).astype(BF16)
    vt_ref[...] = _nt_dot(wvt_ref[...], ub).astype(BF16)
    z_ref[...] = jnp.dot(ub, wz_ref[...], preferred_element_type=F32)
    xbc_ref[...] = jnp.dot(ub, wxbc_ref[...], preferred_element_type=F32)
    dt_ref[...] = jnp.dot(ub, wdt_ref[...], preferred_element_type=F32)
    dtt_ref[...] = _nt_dot(wdtt_ref[...], ub)
    gates_ref[...] = jax.nn.sigmoid(jnp.dot(ub, wgate_ref[...], preferred_element_type=F32) + bgate_ref[...])


def _inproj(x, g_mix, wqt, wk, wvt, wz, wxbc, wdt, wdtt, wgate, bgate, *, tm=256):
    B, L, D = x.shape
    grid = (B, L // tm)
    row = lambda w: pl.BlockSpec((None, tm, w), lambda b, i: (b, i, 0))
    col = lambda w: pl.BlockSpec((None, w, tm), lambda b, i: (b, 0, i))
    ndt = 2 * SSM_HEADS
    return pl.pallas_call(
        _inproj_kernel,
        grid=grid,
        in_specs=[row(D), _const_spec((1, D)), _const_spec(wqt.shape), _const_spec(wk.shape),
                  _const_spec(wvt.shape), _const_spec(wz.shape),
                  _const_spec(wxbc.shape), _const_spec(wdt.shape), _const_spec(wdtt.shape),
                  _const_spec(wgate.shape), _const_spec((1, 2 * D))],
        out_specs=[col(ATT_W), row(ATT_W), col(ATT_W), row(SSM_INNER), row(CONV_CH), row(ndt),
                   col(ndt), row(2 * D)],
        out_shape=[jax.ShapeDtypeStruct((B, ATT_W, L), BF16),
                   jax.ShapeDtypeStruct((B, L, ATT_W), BF16),
                   jax.ShapeDtypeStruct((B, ATT_W, L), BF16),
                   jax.ShapeDtypeStruct((B, L, SSM_INNER), F32),
                   jax.ShapeDtypeStruct((B, L, CONV_CH), F32),
                   jax.ShapeDtypeStruct((B, L, ndt), F32),
                   jax.ShapeDtypeStruct((B, ndt, L), F32),
                   jax.ShapeDtypeStruct((B, L, 2 * D), F32)],
        compiler_params=_params(("parallel", "parallel")),
        name="inproj",
    )(x, g_mix, wqt, wk, wvt, wz, wxbc, wdt, wdtt, wgate, bgate)


def _rel_bucket(rel):
    half = REL_BUCKETS // 2
    max_exact = half // 2
    ret = jnp.where(rel > 0, half, 0)
    n = jnp.abs(rel)
    nf = jnp.maximum(n, 1).astype(jnp.float32)
    large = max_exact + (jnp.log(nf / max_exact) / math.log(REL_MAX_DIST / max_exact)
                         * (half - max_exact)).astype(jnp.int32)
    large = jnp.minimum(large, half - 1)
    return ret + jnp.where(n < max_exact, n, large)


def _bias_kernel(table_ref, bucket_ref, out_ref, *, TQ):
    h = pl.program_id(0)
    out_ref[...] = jnp.zeros(out_ref.shape, F32)
    for d in range(ATT_BIAS_TILES):
        for m in range(2):
            cols = pl.ds(m * TQ, TQ)

            def body(b, carry, d=d, m=m, cols=cols):
                val = table_ref[b, 2 * h + m] * LOG2E
                out_ref[d, :, cols] = jnp.where(bucket_ref[d] == b, val, out_ref[d, :, cols])
                return carry

            lax.fori_loop(0, REL_BUCKETS, body, 0)


def _bias_tiles(rel_bias, TQ, TK):
    assert TK > REL_MAX_DIST
    base = jnp.arange(TK, dtype=jnp.int32)[:, None] - jnp.arange(TQ, dtype=jnp.int32)[None, :]
    bucket = jnp.stack([_rel_bucket(base + (d - 2) * TK) for d in range(ATT_BIAS_TILES)])
    return pl.pallas_call(
        functools.partial(_bias_kernel, TQ=TQ),
        grid=(N_HEADS,),
        in_specs=[pl.BlockSpec(memory_space=pltpu.SMEM), _const_spec((ATT_BIAS_TILES, TK, TQ))],
        out_specs=pl.BlockSpec((None, ATT_BIAS_TILES, TK, 2 * TQ), lambda h: (h, 0, 0, 0)),
        out_shape=jax.ShapeDtypeStruct((N_HEADS, ATT_BIAS_TILES, TK, 2 * TQ), F32),
        compiler_params=_params(("parallel",)),
        name="rel_bias_tiles",
    )(rel_bias.astype(F32), bucket)


def _attn_kernel(lam_ref, qt_ref, k_ref, vt_ref, bt_ref, g_ref, o_ref, acc_sc, s_sc, p_sc, *, TQ, T, nk):
    i = pl.program_id(2)
    qt = qt_ref[...]
    sub = lax.broadcasted_iota(jnp.int32, qt.shape, 0)
    zero = jnp.zeros_like(qt)
    q2t = jnp.concatenate([jnp.where(sub < QK_DIM, qt, zero), jnp.where(sub >= QK_DIM, qt, zero)], axis=1)
    acc_sc[...] = jnp.zeros(acc_sc.shape, F32)

    def scores(j, slot):
        kb = k_ref[pl.ds(pl.multiple_of(j * T, T), T), :]
        tile = jnp.clip(j - ATT_QK_RATIO * i, -2, ATT_QK_RATIO + 1) + 2
        s_sc[slot] = jnp.dot(kb, q2t, preferred_element_type=F32) + bt_ref[tile]

    def softmax(slot, m):
        s = s_sc[slot]
        m_new = jnp.maximum(m, jnp.max(s, axis=0, keepdims=True))
        p_sc[slot] = jnp.exp2(s - m_new).astype(BF16)
        return m_new, jnp.exp2(m - m_new)

    ones = jnp.ones((ONES_ROWS, T), BF16)

    def accumulate(j, slot, alpha):
        vb = jnp.concatenate([vt_ref[:, pl.ds(pl.multiple_of(j * T, T), T)], ones], axis=0)
        acc_sc[...] = alpha * acc_sc[...] + jnp.dot(vb, p_sc[slot], preferred_element_type=F32)

    m = jnp.full((1, 2 * TQ), -jnp.inf, F32)
    scores(0, 0)
    scores(1, 1)
    m, alpha = softmax(0, m)

    def pair(u, carry):
        m, alpha = carry
        t = 2 * u
        scores(t, 0)
        m, alpha1 = softmax(1, m)
        accumulate(t - 2, 0, alpha)
        scores(t + 1, 1)
        m, alpha0 = softmax(0, m)
        accumulate(t - 1, 1, alpha1)
        return m, alpha0

    m, alpha = lax.fori_loop(1, nk // 2, pair, (m, alpha))
    m, alpha1 = softmax(1, m)
    accumulate(nk - 2, 0, alpha)
    accumulate(nk - 1, 1, alpha1)

    o = acc_sc[0:HEAD_W, :] / acc_sc[HEAD_W:HEAD_W + 1, :]
    od = o[:, :TQ] - lam_ref[0] * o[:, TQ:]
    y = od * lax.rsqrt(jnp.mean(od * od, axis=0, keepdims=True) + EPS) * g_ref[...]
    o_ref[...] = (y * (1.0 - LAM_INIT)).T.astype(o_ref.dtype)


def _attention(qt, k, vt, btiles, lam, g_col, *, T=ATT_BLOCK):
    B, L, _ = k.shape
    TQ = ATT_QK_RATIO * T
    nk = L // T
    assert nk % 2 == 0 and nk >= 4 and L % TQ == 0
    return pl.pallas_call(
        functools.partial(_attn_kernel, TQ=TQ, T=T, nk=nk),
        grid=(B, N_HEADS, L // TQ),
        in_specs=[pl.BlockSpec(memory_space=pltpu.SMEM),
                  pl.BlockSpec((None, HEAD_W, TQ), lambda b, h, i: (b, h, i)),
                  pl.BlockSpec((None, L, HEAD_W), lambda b, h, i: (b, 0, h)),
                  pl.BlockSpec((None, HEAD_W, L), lambda b, h, i: (b, h, 0)),
                  pl.BlockSpec((None, ATT_BIAS_TILES, T, 2 * TQ), lambda b, h, i: (h, 0, 0, 0)),
                  _const_spec((HEAD_W, 1))],
        out_specs=pl.BlockSpec((None, TQ, HEAD_W), lambda b, h, i: (b, i, h)),
        out_shape=jax.ShapeDtypeStruct((B, L, ATT_W), BF16),
        scratch_shapes=[pltpu.VMEM((HEAD_W + ONES_ROWS, 2 * TQ), F32), pltpu.VMEM((2, T, 2 * TQ), F32),
                        pltpu.VMEM((2, T, 2 * TQ), BF16)],
        compiler_params=_params(("parallel", "parallel", "parallel")),
        name="diff_attention",
    )(lam, qt, k, vt, btiles, g_col)


def _conv_kernel(x_ref, prev_ref, next_ref, dt_ref, dtt_ref, w_ref, b_ref, dtb_ref, dtbt_ref,
                 xs_ref, bc_ref, dts_ref, dtst_ref, *, tl, nt):
    i = pl.program_id(1)
    x = x_ref[...]
    row = lax.broadcasted_iota(jnp.int32, x.shape, 0)
    prev_row = jnp.where(i > 0, prev_ref[7:8, :], 0.0)
    next_row = jnp.where(i < nt - 1, next_ref[0:1, :], 0.0)
    xm1 = jnp.where(row == 0, prev_row, pltpu.roll(x, 1, 0))
    xp1 = jnp.where(row == tl - 1, next_row, pltpu.roll(x, tl - 1, 0))
    y = w_ref[0:1, :] * xm1 + w_ref[1:2, :] * x + w_ref[2:3, :] * xp1 + b_ref[...]
    y = y * jax.nn.sigmoid(y)
    xs_ref[...] = y[:, :SSM_INNER].astype(BF16)
    bc_ref[...] = y[:, SSM_INNER:].astype(BF16)

    def softplus(v):
        return jnp.maximum(v, 0.0) + jnp.log1p(jnp.exp(-jnp.abs(v)))

    dts_ref[...] = softplus(dt_ref[...] + dtb_ref[...])
    dtst_ref[...] = softplus(dtt_ref[...] + dtbt_ref[...])


def _conv(xbc, dt, dtt, conv_w, conv_b, dtb, dtbt, *, tl=512):
    B, L, _ = xbc.shape
    nt = L // tl
    r8 = tl // 8
    ndt = 2 * SSM_HEADS
    row = lambda w: pl.BlockSpec((None, tl, w), lambda b, i: (b, i, 0))
    colspec = pl.BlockSpec((None, ndt, tl), lambda b, i: (b, 0, i))
    return pl.pallas_call(
        functools.partial(_conv_kernel, tl=tl, nt=nt),
        grid=(B, nt),
        in_specs=[row(CONV_CH),
                  pl.BlockSpec((None, 8, CONV_CH), lambda b, i: (b, jnp.maximum(i * r8 - 1, 0), 0)),
                  pl.BlockSpec((None, 8, CONV_CH), lambda b, i: (b, jnp.minimum((i + 1) * r8, L // 8 - 1), 0)),
                  row(ndt), colspec,
                  _const_spec((3, CONV_CH)), _const_spec((1, CONV_CH)), _const_spec((1, ndt)),
                  _const_spec((ndt, 1))],
        out_specs=[row(SSM_INNER), row(2 * SSM_GROUPS * SSM_N), row(ndt), colspec],
        out_shape=[jax.ShapeDtypeStruct((B, L, SSM_INNER), BF16),
                   jax.ShapeDtypeStruct((B, L, 2 * SSM_GROUPS * SSM_N), BF16),
                   jax.ShapeDtypeStruct((B, L, ndt), F32),
                   jax.ShapeDtypeStruct((B, ndt, L), F32)],
        compiler_params=_params(("parallel", "parallel")),
        name="ssm_conv",
    )(xbc, xbc, xbc, dt, dtt, conv_w, conv_b, dtb, dtbt)


def _ssd_chunk(xs_ref, bc_ref, dts_ref, dtst_ref, alog_ref, alogt_ref, st_ref, *, rev):
    Q = SSM_CHUNK
    HI = lax.Precision.HIGHEST
    off = SSM_HEADS if rev else 0
    ri = lax.broadcasted_iota(jnp.int32, (Q, Q), 0)
    ci = lax.broadcasted_iota(jnp.int32, (Q, Q), 1)
    incl = (ci >= ri) if rev else (ci <= ri)
    tri = incl.astype(F32)
    dt = dts_ref[:, off:off + SSM_HEADS]
    dtt = dtst_ref[off:off + SSM_HEADS, :]
    a = dt * (-jnp.exp(alog_ref[:, off:off + SSM_HEADS]))
    at = dtt * (-jnp.exp(alogt_ref[off:off + SSM_HEADS, :]))
    acum = jnp.dot(tri, a, preferred_element_type=F32, precision=HI)
    acumt = _nt_dot(at, tri, precision=HI)
    edge = 0 if rev else Q - 1
    bc = bc_ref[...]
    bpair = bc[:, :Q]
    cpair = bc[:, Q:]
    low = ci < SSM_N
    rlow = ri < SSM_N
    blockmask = low == rlow
    ys = []
    for g in range(SSM_GROUPS):
        gmask = low if g == 0 else jnp.logical_not(low)
        cg = jnp.where(gmask, cpair, jnp.zeros_like(cpair))
        cb = _nt_dot(cg, bpair)
        dupm = ((ri - g * SSM_N) == (ci % SSM_N)).astype(BF16)
        cdup = jnp.dot(cpair, dupm, preferred_element_type=F32)
        sel2 = ((ci - g * SSM_N) == (ri % SSM_N)).astype(BF16)
        bt2 = _nt_dot(sel2, bpair)
        for hp in range(SSM_HEADS // SSM_GROUPS // 2):
            h0 = g * (SSM_HEADS // SSM_GROUPS) + 2 * hp
            pair = h0 // 2
            xpair = xs_ref[:, h0 * SSM_P:(h0 + 2) * SSM_P]
            ms = []
            es = []
            ws = []
            for h in (h0, h0 + 1):
                e_col = jnp.broadcast_to(acum[:, h:h + 1], (Q, Q))
                r_row = jnp.broadcast_to(acumt[h:h + 1, :], (Q, Q))
                dt_row = jnp.broadcast_to(dtt[h:h + 1, :], (Q, Q))
                seg = jnp.where(incl, e_col - r_row, -jnp.inf)
                ms.append((cb * jnp.exp(seg) * dt_row).astype(BF16))
                es.append(jnp.exp(e_col))
                ws.append(dt_row * jnp.exp(acumt[h:h + 1, edge:edge + 1] - r_row))
            st = st_ref[pair]
            xblk = jnp.concatenate([jnp.where(low, xpair, jnp.zeros_like(xpair)),
                                    jnp.where(low, jnp.zeros_like(xpair), xpair)], axis=0)
            y = jnp.dot(jnp.concatenate(ms, axis=1), xblk, preferred_element_type=F32)
            ce = (cdup * jnp.where(low, es[0], es[1])).astype(BF16)
            y = y + jnp.dot(ce, st.astype(BF16), preferred_element_type=F32)
            ys.append(y)
            btw = (bt2 * jnp.where(rlow, ws[0], ws[1])).astype(BF16)
            snew = jnp.dot(btw, xpair, preferred_element_type=F32)
            dec = jnp.where(rlow, jnp.exp(acumt[h0:h0 + 1, edge:edge + 1]),
                            jnp.exp(acumt[h0 + 1:h0 + 2, edge:edge + 1]))
            st_ref[pair] = dec * st + jnp.where(blockmask, snew, 0.0)
    return jnp.concatenate(ys, axis=1)


def _ssd_fwd_kernel(xs_ref, bc_ref, dts_ref, dtst_ref, alog_ref, alogt_ref, y_ref, st_ref):
    @pl.when(pl.program_id(1) == 0)
    def _():
        st_ref[...] = jnp.zeros(st_ref.shape, F32)

    y_ref[...] = _ssd_chunk(xs_ref, bc_ref, dts_ref, dtst_ref, alog_ref, alogt_ref, st_ref, rev=False)


def _ssd_bwd_kernel(xs_ref, bc_ref, dts_ref, dtst_ref, alog_ref, alogt_ref, yf_ref, z_ref, dskip_ref, g_ref,
                    o_ref, st_ref):
    @pl.when(pl.program_id(1) == 0)
    def _():
        st_ref[...] = jnp.zeros(st_ref.shape, F32)

    yb = _ssd_chunk(xs_ref, bc_ref, dts_ref, dtst_ref, alog_ref, alogt_ref, st_ref, rev=True)
    z = z_ref[...]
    y = (yf_ref[...] + yb + xs_ref[...].astype(F32) * dskip_ref[...]) * (z * jax.nn.sigmoid(z))
    y = y * lax.rsqrt(jnp.mean(y * y, axis=-1, keepdims=True) + EPS) * g_ref[...]
    o_ref[...] = y.astype(o_ref.dtype)


def _ssd(xs, bc, dts, dtst, alog, alogt, z, dskip, g_norm):
    B, L, _ = xs.shape
    Q = SSM_CHUNK
    nc = L // Q
    ndt = 2 * SSM_HEADS
    npair = SSM_HEADS // 2

    def specs(cidx):
        row = lambda w: pl.BlockSpec((None, Q, w), lambda b, c: (b, cidx(c), 0))
        return row, [row(SSM_INNER), row(2 * SSM_GROUPS * SSM_N), row(ndt),
                     pl.BlockSpec((None, ndt, Q), lambda b, c: (b, 0, cidx(c))),
                     _const_spec((1, ndt)), _const_spec((ndt, 1))]

    scratch = [pltpu.VMEM((npair, 2 * SSM_N, 2 * SSM_P), F32)]
    row, in_specs = specs(lambda c: c)
    yf = pl.pallas_call(
        _ssd_fwd_kernel,
        grid=(B, nc),
        in_specs=in_specs,
        out_specs=row(SSM_INNER),
        out_shape=jax.ShapeDtypeStruct((B, L, SSM_INNER), F32),
        scratch_shapes=scratch,
        compiler_params=_params(("parallel", "arbitrary")),
        name="ssd_forward",
    )(xs, bc, dts, dtst, alog, alogt)
    row, in_specs = specs(lambda c: nc - 1 - c)
    return pl.pallas_call(
        _ssd_bwd_kernel,
        grid=(B, nc),
        in_specs=in_specs + [row(SSM_INNER), row(SSM_INNER), _const_spec((1, SSM_INNER)),
                             _const_spec((1, SSM_INNER))],
        out_specs=row(SSM_INNER),
        out_shape=jax.ShapeDtypeStruct((B, L, SSM_INNER), BF16),
        scratch_shapes=scratch,
        compiler_params=_params(("parallel", "arbitrary")),
        name="ssd_reverse_gate_norm",
    )(xs, bc, dts, dtst, alog, alogt, yf, z, dskip, g_norm)


def _merge_kernel(att_ref, ssm_ref, gates_ref, x_ref, wau_ref, wsu_ref, wout_ref, gffn_ref, wr_ref, wrt_ref,
                  place_ref, h_ref, u2_ref, afft_ref):
    D = D_MODEL
    E = N_EXPERTS
    ya = jnp.dot(att_ref[...], wau_ref[...], preferred_element_type=F32)
    ysm = jnp.dot(ssm_ref[...], wsu_ref[...], preferred_element_type=F32)
    merged = gates_ref[:, :D] * ya + gates_ref[:, D:] * ysm
    h = x_ref[...] + jnp.dot(merged.astype(BF16), wout_ref[...], preferred_element_type=F32)
    h_ref[...] = h
    u2 = (h * lax.rsqrt(jnp.mean(h * h, axis=-1, keepdims=True) + EPS) * gffn_ref[...]).astype(BF16)
    u2_ref[:, :D] = u2
    logits = jnp.dot(u2, wr_ref[...], preferred_element_type=F32)
    e = jnp.exp(logits - jnp.max(logits, axis=-1, keepdims=True))
    aff = e / jnp.sum(e, axis=-1, keepdims=True)
    hi = aff.astype(BF16)
    r1 = aff - hi.astype(F32)
    mid = r1.astype(BF16)
    lo = (r1 - mid.astype(F32)).astype(BF16)
    ext = (jnp.dot(hi, place_ref[0:E, :], preferred_element_type=F32)
           + jnp.dot(mid, place_ref[E:2 * E, :], preferred_element_type=F32)
           + jnp.dot(lo, place_ref[2 * E:3 * E, :], preferred_element_type=F32))
    u2_ref[:, D:] = ext.astype(BF16)
    lt = _nt_dot(wrt_ref[...], u2)
    et = jnp.exp(lt - jnp.max(lt, axis=0, keepdims=True))
    afft_ref[...] = et / jnp.sum(et, axis=0, keepdims=True)


def _merge(att, ssm, gates, x, wau, wsu, wout, g_ffn, wr, *, tm=256):
    B, L, D = x.shape
    E = N_EXPERTS
    nl = L // tm
    row = lambda w: pl.BlockSpec((None, tm, w), lambda b, i: (b, i, 0))
    place = jnp.eye(3 * E, GATE_EXT, dtype=BF16)
    return pl.pallas_call(
        _merge_kernel,
        grid=(B, nl),
        in_specs=[row(ATT_W), row(SSM_INNER), row(2 * D), row(D), _const_spec(wau.shape),
                  _const_spec(wsu.shape), _const_spec(wout.shape), _const_spec((1, D)), _const_spec(wr.shape),
                  _const_spec((E, D)), _const_spec((3 * E, GATE_EXT))],
        out_specs=[row(D), row(D + GATE_EXT), pl.BlockSpec((E, tm), lambda b, i: (0, b * nl + i))],
        out_shape=[jax.ShapeDtypeStruct((B, L, D), F32), jax.ShapeDtypeStruct((B, L, D + GATE_EXT), BF16),
                   jax.ShapeDtypeStruct((E, B * L), F32)],
        compiler_params=_params(("parallel", "parallel")),
        name="merge_router",
    )(att, ssm, gates, x, wau, wsu, wout, g_ffn, wr, wr.T, place)


def _route_kernel(afft_ref, posm_ref, bstart_ref, *, cap, nb):
    E = N_EXPERTS
    W = ROUTE_BLK

    def bit_body(k, thr):
        cand = thr | lax.shift_left(jnp.int32(1), 30 - k)
        x = lax.bitcast_convert_type(afft_ref[...], jnp.int32)
        cnt = jnp.sum(jnp.where(x >= cand, 1.0, 0.0), axis=1, keepdims=True)
        return jnp.where(cnt >= cap, cand, thr)

    thr = lax.fori_loop(0, 31, bit_body, jnp.zeros((E, 1), jnp.int32))
    x = lax.bitcast_convert_type(afft_ref[...], jnp.int32)
    n_gt = jnp.sum(jnp.where(x > thr, 1.0, 0.0), axis=1, keepdims=True)
    need = cap - n_gt
    ri = lax.broadcasted_iota(jnp.int32, (W, W), 0)
    ci = lax.broadcasted_iota(jnp.int32, (W, W), 1)
    upper = jnp.where(ri <= ci, 1.0, 0.0).astype(BF16)
    bstart_ref[...] = jnp.zeros(bstart_ref.shape, jnp.int32)
    c_eq = jnp.zeros((E, 1), F32)
    c_sel = jnp.zeros((E, 1), F32)
    for b in range(nb):
        xb = lax.bitcast_convert_type(afft_ref[:, b * W:(b + 1) * W], jnp.int32)
        eq = xb == thr
        eqc = jnp.dot(jnp.where(eq, 1.0, 0.0).astype(BF16), upper, preferred_element_type=F32) + c_eq
        sel = (xb > thr) | (eq & (eqc <= need))
        selc = jnp.dot(jnp.where(sel, 1.0, 0.0).astype(BF16), upper, preferred_element_type=F32) + c_sel
        posm_ref[:, b * W:(b + 1) * W] = jnp.where(sel, selc, 0.0).astype(jnp.int32)
        bstart_ref[:, b:b + 1] = c_sel.astype(jnp.int32)
        c_eq = eqc[:, W - 1:W]
        c_sel = selc[:, W - 1:W]
    bstart_ref[:, nb:nb + 1] = c_sel.astype(jnp.int32)


def _route(afft, cap):
    E, T = afft.shape
    nb = T // ROUTE_BLK
    assert nb < BSTART_W and cap % SLOT_TILE == 0
    return pl.pallas_call(
        functools.partial(_route_kernel, cap=float(cap), nb=nb),
        out_shape=[jax.ShapeDtypeStruct((E, T), jnp.int32), jax.ShapeDtypeStruct((E, BSTART_W), jnp.int32)],
        compiler_params=_params(None),
        name="route_select",
    )(afft)


def _pair_schedule(bstart, cap, ngb):
    step = GATHER_BLK // ROUTE_BLK
    st = bstart[:, 0:ngb * step + 1:step]
    lo, hi = st[:, :-1], st[:, 1:]
    ntile = cap // SLOT_TILE
    t_lo = jnp.arange(ntile, dtype=jnp.int32) * SLOT_TILE
    t_hi = t_lo + SLOT_TILE
    b_first = jnp.sum(hi[:, None, :] <= t_lo[None, :, None], axis=2).astype(jnp.int32)
    b_last = jnp.sum(lo[:, None, :] < t_hi[None, :, None], axis=2).astype(jnp.int32) - 1
    return b_first, b_last - b_first + 1


def _expert_kernel(first_ref, cnt_ref, u_hbm, posm_ref, wg_ref, wu_ref, wd_ref, o_ref, x_sc, ubuf, sem):
    D = D_MODEL
    e = pl.program_id(0)
    j = pl.program_id(1)
    b0 = first_ref[e, j]
    n = cnt_ref[e, j]

    half = GATHER_BLK // 2

    def half_copies(b, slot):
        return [pltpu.make_async_copy(
            u_hbm.at[pl.ds(pl.multiple_of(b * GATHER_BLK + r * half, half), half)],
            ubuf.at[slot, pl.ds(r * half, half)], sem.at[slot, r]) for r in range(2)]

    def start(b, slot):
        for cp in half_copies(b, slot):
            cp.start()

    start(b0, 0)

    @pl.when(n > 1)
    def _():
        start(b0 + 1, 1)

    x_sc[...] = jnp.zeros(x_sc.shape, F32)
    slot_ids = lax.broadcasted_iota(jnp.int32, (SLOT_TILE, GATHER_BLK), 0) + (j * SLOT_TILE + 1)

    def body(c, carry):
        slot = c % GATHER_BUFS
        for cp in half_copies(b0 + c, slot):
            cp.wait()

        @pl.when(c + 2 < n)
        def _():
            start(b0 + c + 2, (c + 2) % GATHER_BUFS)

        posrow = posm_ref[:, pl.ds(pl.multiple_of((b0 + c) * GATHER_BLK, GATHER_BLK), GATHER_BLK)]
        onehot = jnp.where(posrow == slot_ids, 1.0, 0.0).astype(BF16)
        x_sc[...] += jnp.dot(onehot, ubuf[slot], preferred_element_type=F32)
        return carry

    lax.fori_loop(0, n, body, 0)
    xs = x_sc[:, :D].astype(BF16)
    ext = x_sc[:, D:]
    lane = lax.broadcasted_iota(jnp.int32, ext.shape, 1)
    mine = (lane == e) | (lane == e + N_EXPERTS) | (lane == e + 2 * N_EXPERTS)
    gate = jnp.sum(jnp.where(mine, ext, 0.0), axis=1, keepdims=True)
    hg = jnp.dot(xs, wg_ref[...], preferred_element_type=F32)
    hu = jnp.dot(xs, wu_ref[...], preferred_element_type=F32)
    hmid = (hg * jax.nn.sigmoid(hg) * hu).astype(BF16)
    o_ref[...] = jnp.dot(hmid, wd_ref[...], preferred_element_type=F32) * gate


def _experts(u2ext, posm, b_first, b_cnt, wg, wu, wd, cap):
    T, DX = u2ext.shape
    E, D = N_EXPERTS, D_MODEL
    wspec = pl.BlockSpec((None, D, D), lambda e, j, bf, bc: (e, 0, 0))
    return pl.pallas_call(
        _expert_kernel,
        grid_spec=pltpu.PrefetchScalarGridSpec(
            num_scalar_prefetch=2,
            grid=(E, cap // SLOT_TILE),
            in_specs=[pl.BlockSpec(memory_space=pl.ANY),
                      pl.BlockSpec((None, 1, T), lambda e, j, bf, bc: (e, 0, 0)),
                      wspec, wspec, wspec],
            out_specs=pl.BlockSpec((None, SLOT_TILE, D), lambda e, j, bf, bc: (e, j, 0)),
            scratch_shapes=[pltpu.VMEM((SLOT_TILE, DX), F32), pltpu.VMEM((GATHER_BUFS, GATHER_BLK, DX), BF16),
                            pltpu.SemaphoreType.DMA((GATHER_BUFS, 2))]),
        out_shape=jax.ShapeDtypeStruct((E, cap, D), F32),
        compiler_params=_params(("parallel", "parallel")),
        name="experts_gather_mlp",
    )(b_first, b_cnt, u2ext, posm.reshape(E, 1, T), wg, wu, wd)


def _combine_kernel(bstart_ref, posm_ref, ye_ref, o_ref, buf, xbuf, sem, xsem, *, cap, nb):
    E = N_EXPERTS
    WN = COMBINE_WIN
    b = pl.program_id(0)
    top = cap - WN
    cur = b % 2

    def window(blk, slot, e):
        base = jnp.minimum((bstart_ref[e, blk] // 8) * 8, top)
        return pltpu.make_async_copy(ye_ref.at[e, pl.ds(pl.multiple_of(base, 8), WN)],
                                     buf.at[slot, pl.ds(e * WN, WN)], sem.at[slot, e])

    @pl.when(b == 0)
    def _():
        for e in range(E):
            window(0, 0, e).start()

    @pl.when(b + 1 < nb)
    def _():
        for e in range(E):
            window(b + 1, 1 - cur, e).start()

    lows = [(bstart_ref[e, b] // 8) * 8 for e in range(E)]
    w_iota = lax.broadcasted_iota(jnp.int32, (WN, ROUTE_BLK), 0)

    def onehot_rows(slot0, low, base):
        ok = (slot0 >= low) & (slot0 < low + WN)
        return jnp.where(ok & (slot0 - base == w_iota), 1.0, 0.0).astype(BF16)

    def scatter(onehot, rows):
        hi = rows.astype(BF16)
        lo = (rows - hi.astype(F32)).astype(BF16)
        tn = lambda a, c: lax.dot_general(a, c, (((0,), (0,)), ((), ())), preferred_element_type=F32)
        return tn(onehot, hi) + tn(onehot, lo)

    hots = []
    for e in range(E):
        slot0 = posm_ref[e:e + 1, :] - 1
        hots.append(onehot_rows(slot0, lows[e], jnp.minimum(lows[e], top)))
    for e in range(E):
        window(b, cur, e).wait()
    o_ref[...] = scatter(jnp.concatenate(hots, axis=0), buf[cur])

    def per_expert(e, carry):
        low0 = (bstart_ref[e, b] // 8) * 8
        nwin = (bstart_ref[e, b + 1] - low0 + WN - 1) // WN

        def per_window(c, carry):
            low = low0 + c * WN
            base = jnp.minimum(low, top)
            cp = pltpu.make_async_copy(ye_ref.at[e, pl.ds(pl.multiple_of(base, 8), WN)], xbuf, xsem.at[0])
            cp.start()
            cp.wait()
            mine = lax.broadcasted_iota(jnp.int32, (E, ROUTE_BLK), 0) == e
            slot0 = jnp.sum(jnp.where(mine, posm_ref[...], 0), axis=0, keepdims=True) - 1
            o_ref[...] += scatter(onehot_rows(slot0, low, base), xbuf[...])
            return carry

        return lax.fori_loop(1, nwin, per_window, carry)

    lax.fori_loop(0, E, per_expert, 0)


def _combine(ye, posm, bstart, cap):
    E, T = posm.shape
    D = D_MODEL
    nb = T // ROUTE_BLK
    assert cap >= COMBINE_WIN
    return pl.pallas_call(
        functools.partial(_combine_kernel, cap=cap, nb=nb),
        grid_spec=pltpu.PrefetchScalarGridSpec(
            num_scalar_prefetch=1,
            grid=(nb,),
            in_specs=[pl.BlockSpec((E, ROUTE_BLK), lambda b, bs: (0, b)),
                      pl.BlockSpec(memory_space=pl.ANY)],
            out_specs=pl.BlockSpec((ROUTE_BLK, D), lambda b, bs: (b, 0)),
            scratch_shapes=[pltpu.VMEM((2, E * COMBINE_WIN, D), F32), pltpu.VMEM((COMBINE_WIN, D), F32),
                            pltpu.SemaphoreType.DMA((2, E)), pltpu.SemaphoreType.DMA((1,))]),
        out_shape=jax.ShapeDtypeStruct((T, D), F32),
        compiler_params=_params(("arbitrary",)),
        name="experts_combine",
    )(bstart, posm, ye)


def _ple_kernel(h_ref, moe_ref, p_ref, wpg_ref, bpg_ref, wpp_ref, gfin_ref, o_ref):
    h = h_ref[...] + moe_ref[...]
    gate = jax.nn.sigmoid(jnp.dot(h.astype(BF16), wpg_ref[...], preferred_element_type=F32) + bpg_ref[...])
    h = h + gate * jnp.dot(p_ref[...].astype(BF16), wpp_ref[...], preferred_element_type=F32)
    o_ref[...] = h * lax.rsqrt(jnp.mean(h * h, axis=-1, keepdims=True) + EPS) * gfin_ref[...]


def _ple(h, moe, p, wpg, bpg, wpp, g_final, *, tm=256):
    B, L, D = h.shape
    row = lambda w: pl.BlockSpec((None, tm, w), lambda b, i: (b, i, 0))
    return pl.pallas_call(
        _ple_kernel,
        grid=(B, L // tm),
        in_specs=[row(D), row(D), row(PLE_DIM), _const_spec(wpg.shape), _const_spec((1, D)),
                  _const_spec(wpp.shape), _const_spec((1, D))],
        out_specs=row(D),
        out_shape=jax.ShapeDtypeStruct((B, L, D), F32),
        compiler_params=_params(("parallel", "parallel")),
        name="ple_final_norm",
    )(h, moe, p, wpg, bpg, wpp, g_final)


def _forward(x, p, w, btiles):
    B, L, D = x.shape
    T = B * L
    qt, k, vt, z, xbc, dt, dtt, gates = _inproj(x, w['g_mix'], w['wqt'], w['wk'], w['wvt'], w['wz'], w['wxbc'],
                                                w['wdt'], w['wdtt'], w['wgate'], w['bgate'])
    att = _attention(qt, k, vt, btiles, w['lam'], w['g_subln'])
    xs, bc, dts, dtst = _conv(xbc, dt, dtt, w['conv_w'], w['conv_b'], w['dtb'], w['dtbt'])
    ssm = _ssd(xs, bc, dts, dtst, w['alog'], w['alogt'], z, w['dskip'], w['g_ssm_norm'])
    h, u2ext, afft = _merge(att, ssm, gates, x, w['wau'], w['wsu'], w['wout'], w['g_ffn'], w['wr'])
    cap = max(1, CAPACITY_FACTOR * T // N_EXPERTS)
    posm, bstart = _route(afft, cap)
    b_first, b_cnt = _pair_schedule(bstart, cap, T // GATHER_BLK)
    ye = _experts(u2ext.reshape(T, D + GATE_EXT), posm, b_first, b_cnt, w['weg'], w['weu'], w['wed'], cap)
    moe = _combine(ye, posm, bstart, cap).reshape(B, L, D)
    return _ple(h, moe, p, w['wpg'], w['bpg'], w['wpp'], w['g_final'])


def kernel(x_prompt, x_sample, p_prompt, p_sample, rel_bias, g_mix, w_in, conv_w, conv_b, dt_bias, a_log,
           d_skip, g_ssm_norm, lam_params, g_subln, w_att_up, w_ssm_up, w_gate, b_gate, w_out, g_ffn,
           w_router, w_exp_gate, w_exp_up, w_exp_down, w_ple_gate, b_ple_gate, w_ple_proj, g_final):
    i = 0
    D = D_MODEL
    win = w_in[i]
    o1 = ATT_W
    o3 = 3 * ATT_W
    o4 = o3 + SSM_INNER
    o5 = o4 + CONV_CH
    wqt = (win[:, :o1] * (QK_DIM ** -0.5 * LOG2E)).T.astype(BF16)
    wk = win[:, o1:2 * o1].astype(BF16)
    wvt = win[:, 2 * o1:o3].T.astype(BF16)
    wdt = win[:, o5:].astype(BF16)
    lp = lam_params[i].astype(F32)
    lam = jnp.exp(jnp.sum(lp[0] * lp[1])) - jnp.exp(jnp.sum(lp[2] * lp[3])) + LAM_INIT
    ndt = 2 * SSM_HEADS
    w = {
        'g_mix': g_mix[i].reshape(1, D), 'wqt': wqt, 'wk': wk, 'wvt': wvt, 'wz': win[:, o3:o4].astype(BF16),
        'wxbc': win[:, o4:o5].astype(BF16), 'wdt': wdt, 'wdtt': wdt.T,
        'wgate': w_gate[i].astype(BF16), 'bgate': b_gate[i].reshape(1, 2 * D),
        'lam': lam.reshape(1), 'g_subln': g_subln[i].reshape(HEAD_W, 1),
        'conv_w': conv_w[i], 'conv_b': conv_b[i].reshape(1, CONV_CH),
        'dtb': dt_bias[i].reshape(1, ndt), 'dtbt': dt_bias[i].reshape(ndt, 1),
        'alog': a_log[i].reshape(1, ndt), 'alogt': a_log[i].reshape(ndt, 1),
        'dskip': jnp.repeat(d_skip[i], SSM_P).reshape(1, SSM_INNER),
        'g_ssm_norm': g_ssm_norm[i].reshape(1, SSM_INNER),
        'wau': w_att_up[i].astype(BF16), 'wsu': w_ssm_up[i].astype(BF16), 'wout': w_out[i].astype(BF16),
        'g_ffn': g_ffn[i].reshape(1, D), 'wr': w_router[i].astype(BF16),
        'weg': w_exp_gate[i].astype(BF16), 'weu': w_exp_up[i].astype(BF16), 'wed': w_exp_down[i].astype(BF16),
        'wpg': w_ple_gate[i].astype(BF16), 'bpg': b_ple_gate[i].reshape(1, D),
        'wpp': w_ple_proj[i].astype(BF16), 'g_final': g_final.reshape(1, D),
    }
    btiles = _bias_tiles(rel_bias, ATT_QK_RATIO * ATT_BLOCK, ATT_BLOCK)
    return (_forward(x_prompt, p_prompt[i], w, btiles), _forward(x_sample, p_sample[i], w, btiles))
```

```python
import functools
import math

import jax
import jax.numpy as jnp
from jax import lax
from jax.experimental import pallas as pl
from jax.experimental.pallas import tpu as pltpu

F32 = jnp.float32
BF16 = jnp.bfloat16

D_MODEL = 1024
N_HEADS = 4
QK_DIM = 64
HEAD_W = 2 * QK_DIM
ATT_W = N_HEADS * HEAD_W
REL_BUCKETS = 32
REL_MAX_DIST = 128
SSM_INNER = 1024
SSM_P = 64
SSM_HEADS = 16
SSM_GROUPS = 2
SSM_N = 64
SSM_CHUNK = 128
CONV_CH = SSM_INNER + 2 * SSM_GROUPS * SSM_N
N_EXPERTS = 16
CAPACITY_FACTOR = 2
PLE_DIM = 256
EPS = 1e-6
LAM_INIT = 0.8 - 0.6 * math.exp(-0.3 * 0)
LOG2E = math.log2(math.e)

GATE_EXT = 128
ROUTE_BLK = 256
GATHER_BLK = 512
GATHER_BUFS = 3
SLOT_TILE = 256
COMBINE_WIN = 72
BSTART_W = 256

ATT_BLOCK = 256
ATT_QK_RATIO = 2
ATT_BIAS_TILES = ATT_QK_RATIO + 4
ONES_ROWS = 16
VMEM_LIMIT = 56 * 1024 * 1024


def _params(sem, vmem=VMEM_LIMIT):
    return pltpu.CompilerParams(dimension_semantics=sem, vmem_limit_bytes=vmem)


def _const_spec(shape):
    nd = len(shape)
    return pl.BlockSpec(shape, lambda *_: (0,) * nd)


def _nt_dot(a, b, **kw):
    return lax.dot_general(a, b, (((1,), (1,)), ((), ())), preferred_element_type=F32, **kw)


def _inproj_kernel(x_ref, g_ref, wqt_ref, wk_ref, wvt_ref, wz_ref, wxbc_ref, wdt_ref, wdtt_ref, wgate_ref,
                   bgate_ref, qt_ref, k_ref, vt_ref, z_ref, xbc_ref, dt_ref, dtt_ref, gates_ref):
    x = x_ref[...]
    u = x * lax.rsqrt(jnp.mean(x * x, axis=-1, keepdims=True) + EPS) * g_ref[...]
    ub = u.astype(BF16)
    qt_ref[...] = _nt_dot(wqt_ref[...], ub).astype(BF16)
    k_ref[...] = jnp.dot(ub, wk_ref[...], preferred_element_type=F32).astype(BF16)
    vt_ref[...] = _nt_dot(wvt_ref[...], ub).astype(BF16)
    z_ref[...] = jnp.dot(ub, wz_ref[...], preferred_element_type=F32)
    xbc_ref[...] = jnp.dot(ub, wxbc_ref[...], preferred_element_type=F32)
    dt_ref[...] = jnp.dot(ub, wdt_ref[...], preferred_element_type=F32)
    dtt_ref[...] = _nt_dot(wdtt_ref[...], ub)
    gates_ref[...] = jax.nn.sigmoid(jnp.dot(ub, wgate_ref[...], preferred_element_type=F32) + bgate_ref[...])


def _inproj(x, g_mix, wqt, wk, wvt, wz, wxbc, wdt, wdtt, wgate, bgate, *, tm=256):
    B, L, D = x.shape
    grid = (B, L // tm)
    row = lambda w: pl.BlockSpec((None, tm, w), lambda b, i: (b, i, 0))
    col = lambda w: pl.BlockSpec((None, w, tm), lambda b, i: (b, 0, i))
    ndt = 2 * SSM_HEADS
    return pl.pallas_call(
        _inproj_kernel,
        grid=grid,
        in_specs=[row(D), _const_spec((1, D)), _const_spec(wqt.shape), _const_spec(wk.shape),
                  _const_spec(wvt.shape), _const_spec(wz.shape),
                  _const_spec(wxbc.shape), _const_spec(wdt.shape), _const_spec(wdtt.shape),
                  _const_spec(wgate.shape), _const_spec((1, 2 * D))],
        out_specs=[col(ATT_W), row(ATT_W), col(ATT_W), row(SSM_INNER), row(CONV_CH), row(ndt),
                   col(ndt), row(2 * D)],
        out_shape=[jax.ShapeDtypeStruct((B, ATT_W, L), BF16),
                   jax.ShapeDtypeStruct((B, L, ATT_W), BF16),
                   jax.ShapeDtypeStruct((B, ATT_W, L), BF16),
                   jax.ShapeDtypeStruct((B, L, SSM_INNER), F32),
                   jax.ShapeDtypeStruct((B, L, CONV_CH), F32),
                   jax.ShapeDtypeStruct((B, L, ndt), F32),
                   jax.ShapeDtypeStruct((B, ndt, L), F32),
                   jax.ShapeDtypeStruct((B, L, 2 * D), F32)],
        compiler_params=_params(("parallel", "parallel")),
        name="inproj",
    )(x, g_mix, wqt, wk, wvt, wz, wxbc, wdt, wdtt, wgate, bgate)


def _rel_bucket(rel):
    half = REL_BUCKETS // 2
    max_exact = half // 2
    ret = jnp.where(rel > 0, half, 0)
    n = jnp.abs(rel)
    nf = jnp.maximum(n, 1).astype(jnp.float32)
    large = max_exact + (jnp.log(nf / max_exact) / math.log(REL_MAX_DIST / max_exact)
                         * (half - max_exact)).astype(jnp.int32)
    large = jnp.minimum(large, half - 1)
    return ret + jnp.where(n < max_exact, n, large)


def _bias_kernel(table_ref, bucket_ref, out_ref, *, TQ):
    h = pl.program_id(0)
    out_ref[...] = jnp.zeros(out_ref.shape, F32)
    for d in range(ATT_BIAS_TILES):
        for m in range(2):
            cols = pl.ds(m * TQ, TQ)

            def body(b, carry, d=d, m=m, cols=cols):
                val = table_ref[b, 2 * h + m] * LOG2E
                out_ref[d, :, cols] = jnp.where(bucket_ref[d] == b, val, out_ref[d, :, cols])
                return carry

            lax.fori_loop(0, REL_BUCKETS, body, 0)


def _bias_tiles(rel_bias, TQ, TK):
    assert TK > REL_MAX_DIST
    base = jnp.arange(TK, dtype=jnp.int32)[:, None] - jnp.arange(TQ, dtype=jnp.int32)[None, :]
    bucket = jnp.stack([_rel_bucket(base + (d - 2) * TK) for d in range(ATT_BIAS_TILES)])
    return pl.pallas_call(
        functools.partial(_bias_kernel, TQ=TQ),
        grid=(N_HEADS,),
        in_specs=[pl.BlockSpec(memory_space=pltpu.SMEM), _const_spec((ATT_BIAS_TILES, TK, TQ))],
        out_specs=pl.BlockSpec((None, ATT_BIAS_TILES, TK, 2 * TQ), lambda h: (h, 0, 0, 0)),
        out_shape=jax.ShapeDtypeStruct((N_HEADS, ATT_BIAS_TILES, TK, 2 * TQ), F32),
        compiler_params=_params(("parallel",)),
        name="rel_bias_tiles",
    )(rel_bias.astype(F32), bucket)


def _attn_kernel(lam_ref, qt_ref, k_ref, vt_ref, bt_ref, g_ref, o_ref, acc_sc, p_sc, *, TQ, T, nk):
    i = pl.program_id(2)
    qt = qt_ref[...]
    sub = lax.broadcasted_iota(jnp.int32, qt.shape, 0)
    zero = jnp.zeros_like(qt)
    q2t = jnp.concatenate([jnp.where(sub < QK_DIM, qt, zero), jnp.where(sub >= QK_DIM, qt, zero)], axis=1)
    acc_sc[...] = jnp.zeros(acc_sc.shape, F32)

    def scores_softmax(j, slot, m):
        kb = k_ref[pl.ds(pl.multiple_of(j * T, T), T), :]
        tile = jnp.clip(j - ATT_QK_RATIO * i, -2, ATT_QK_RATIO + 1) + 2
        s = jnp.dot(kb, q2t, preferred_element_type=F32) + bt_ref[tile]
        m_new = jnp.maximum(m, jnp.max(s, axis=0, keepdims=True))
        p_sc[slot] = jnp.exp2(s - m_new).astype(BF16)
        return m_new, jnp.exp2(m - m_new)

    ones = jnp.ones((ONES_ROWS, T), BF16)

    def accumulate(j, slot, alpha):
        vb = jnp.concatenate([vt_ref[:, pl.ds(pl.multiple_of(j * T, T), T)], ones], axis=0)
        acc_sc[...] = alpha * acc_sc[...] + jnp.dot(vb, p_sc[slot], preferred_element_type=F32)

    m = jnp.full((1, 2 * TQ), -jnp.inf, F32)
    m, alpha = scores_softmax(0, 0, m)

    def pair(u, carry):
        m, alpha = carry
        t = 2 * u
        m, alpha1 = scores_softmax(t - 1, 1, m)
        accumulate(t - 2, 0, alpha)
        m, alpha0 = scores_softmax(t, 0, m)
        accumulate(t - 1, 1, alpha1)
        return m, alpha0

    def quad(u, carry):
        return pair(2 * u + 1, pair(2 * u, carry))

    m, alpha = lax.fori_loop(1, nk // 4, quad, pair(1, (m, alpha)))
    m, alpha1 = scores_softmax(nk - 1, 1, m)
    accumulate(nk - 2, 0, alpha)
    accumulate(nk - 1, 1, alpha1)

    o = acc_sc[0:HEAD_W, :] / acc_sc[HEAD_W:HEAD_W + 1, :]
    od = o[:, :TQ] - lam_ref[0] * o[:, TQ:]
    y = od * lax.rsqrt(jnp.mean(od * od, axis=0, keepdims=True) + EPS) * g_ref[...]
    o_ref[...] = (y * (1.0 - LAM_INIT)).T.astype(o_ref.dtype)


def _attention(qt, k, vt, btiles, lam, g_col, *, T=ATT_BLOCK):
    B, L, _ = k.shape
    TQ = ATT_QK_RATIO * T
    nk = L // T
    assert nk % 4 == 0 and L % TQ == 0
    return pl.pallas_call(
        functools.partial(_attn_kernel, TQ=TQ, T=T, nk=nk),
        grid=(B, N_HEADS, L // TQ),
        in_specs=[pl.BlockSpec(memory_space=pltpu.SMEM),
                  pl.BlockSpec((None, HEAD_W, TQ), lambda b, h, i: (b, h, i)),
                  pl.BlockSpec((None, L, HEAD_W), lambda b, h, i: (b, 0, h)),
                  pl.BlockSpec((None, HEAD_W, L), lambda b, h, i: (b, h, 0)),
                  pl.BlockSpec((None, ATT_BIAS_TILES, T, 2 * TQ), lambda b, h, i: (h, 0, 0, 0)),
                  _const_spec((HEAD_W, 1))],
        out_specs=pl.BlockSpec((None, TQ, HEAD_W), lambda b, h, i: (b, i, h)),
        out_shape=jax.ShapeDtypeStruct((B, L, ATT_W), BF16),
        scratch_shapes=[pltpu.VMEM((HEAD_W + ONES_ROWS, 2 * TQ), F32), pltpu.VMEM((2, T, 2 * TQ), BF16)],
        compiler_params=_params(("parallel", "parallel", "parallel")),
        name="diff_attention",
    )(lam, qt, k, vt, btiles, g_col)


def _conv_kernel(x_ref, prev_ref, next_ref, dt_ref, dtt_ref, w_ref, b_ref, dtb_ref, dtbt_ref,
                 xs_ref, bc_ref, dts_ref, dtst_ref, *, tl, nt):
    i = pl.program_id(1)
    x = x_ref[...]
    row = lax.broadcasted_iota(jnp.int32, x.shape, 0)
    prev_row = jnp.where(i > 0, prev_ref[7:8, :], 0.0)
    next_row = jnp.where(i < nt - 1, next_ref[0:1, :], 0.0)
    xm1 = jnp.where(row == 0, prev_row, pltpu.roll(x, 1, 0))
    xp1 = jnp.where(row == tl - 1, next_row, pltpu.roll(x, tl - 1, 0))
    y = w_ref[0:1, :] * xm1 + w_ref[1:2, :] * x + w_ref[2:3, :] * xp1 + b_ref[...]
    y = y * jax.nn.sigmoid(y)
    xs_ref[...] = y[:, :SSM_INNER].astype(BF16)
    bc_ref[...] = y[:, SSM_INNER:].astype(BF16)

    def softplus(v):
        return jnp.maximum(v, 0.0) + jnp.log1p(jnp.exp(-jnp.abs(v)))

    dts_ref[...] = softplus(dt_ref[...] + dtb_ref[...])
    dtst_ref[...] = softplus(dtt_ref[...] + dtbt_ref[...])


def _conv(xbc, dt, dtt, conv_w, conv_b, dtb, dtbt, *, tl=512):
    B, L, _ = xbc.shape
    nt = L // tl
    r8 = tl // 8
    ndt = 2 * SSM_HEADS
    row = lambda w: pl.BlockSpec((None, tl, w), lambda b, i: (b, i, 0))
    colspec = pl.BlockSpec((None, ndt, tl), lambda b, i: (b, 0, i))
    return pl.pallas_call(
        functools.partial(_conv_kernel, tl=tl, nt=nt),
        grid=(B, nt),
        in_specs=[row(CONV_CH),
                  pl.BlockSpec((None, 8, CONV_CH), lambda b, i: (b, jnp.maximum(i * r8 - 1, 0), 0)),
                  pl.BlockSpec((None, 8, CONV_CH), lambda b, i: (b, jnp.minimum((i + 1) * r8, L // 8 - 1), 0)),
                  row(ndt), colspec,
                  _const_spec((3, CONV_CH)), _const_spec((1, CONV_CH)), _const_spec((1, ndt)),
                  _const_spec((ndt, 1))],
        out_specs=[row(SSM_INNER), row(2 * SSM_GROUPS * SSM_N), row(ndt), colspec],
        out_shape=[jax.ShapeDtypeStruct((B, L, SSM_INNER), BF16),
                   jax.ShapeDtypeStruct((B, L, 2 * SSM_GROUPS * SSM_N), BF16),
                   jax.ShapeDtypeStruct((B, L, ndt), F32),
                   jax.ShapeDtypeStruct((B, ndt, L), F32)],
        compiler_params=_params(("parallel", "parallel")),
        name="ssm_conv",
    )(xbc, xbc, xbc, dt, dtt, conv_w, conv_b, dtb, dtbt)


def _ssd_chunk(xs_ref, bc_ref, dts_ref, dtst_ref, alog_ref, alogt_ref, st_ref, *, rev):
    Q = SSM_CHUNK
    HI = lax.Precision.HIGHEST
    off = SSM_HEADS if rev else 0
    ri = lax.broadcasted_iota(jnp.int32, (Q, Q), 0)
    ci = lax.broadcasted_iota(jnp.int32, (Q, Q), 1)
    incl = (ci >= ri) if rev else (ci <= ri)
    tri = incl.astype(F32)
    dt = dts_ref[:, off:off + SSM_HEADS]
    dtt = dtst_ref[off:off + SSM_HEADS, :]
    a = dt * (-jnp.exp(alog_ref[:, off:off + SSM_HEADS]))
    at = dtt * (-jnp.exp(alogt_ref[off:off + SSM_HEADS, :]))
    acum = jnp.dot(tri, a, preferred_element_type=F32, precision=HI)
    acumt = _nt_dot(at, tri, precision=HI)
    edge = 0 if rev else Q - 1
    bc = bc_ref[...]
    bpair = bc[:, :Q]
    cpair = bc[:, Q:]
    low = ci < SSM_N
    rlow = ri < SSM_N
    blockmask = low == rlow
    ys = []
    for g in range(SSM_GROUPS):
        gmask = low if g == 0 else jnp.logical_not(low)
        cg = jnp.where(gmask, cpair, jnp.zeros_like(cpair))
        cb = _nt_dot(cg, bpair)
        dupm = ((ri - g * SSM_N) == (ci % SSM_N)).astype(BF16)
        cdup = jnp.dot(cpair, dupm, preferred_element_type=F32)
        sel2 = ((ci - g * SSM_N) == (ri % SSM_N)).astype(BF16)
        bt2 = _nt_dot(sel2, bpair)
        for hp in range(SSM_HEADS // SSM_GROUPS // 2):
            h0 = g * (SSM_HEADS // SSM_GROUPS) + 2 * hp
            pair = h0 // 2
            xpair = xs_ref[:, h0 * SSM_P:(h0 + 2) * SSM_P]
            ms = []
            es = []
            ws = []
            for h in (h0, h0 + 1):
                e_col = jnp.broadcast_to(acum[:, h:h + 1], (Q, Q))
                r_row = jnp.broadcast_to(acumt[h:h + 1, :], (Q, Q))
                dt_row = jnp.broadcast_to(dtt[h:h + 1, :], (Q, Q))
                seg = jnp.where(incl, e_col - r_row, -jnp.inf)
                ms.append((cb * jnp.exp(seg) * dt_row).astype(BF16))
                es.append(jnp.exp(e_col))
                ws.append(dt_row * jnp.exp(acumt[h:h + 1, edge:edge + 1] - r_row))
            st = st_ref[pair]
            xblk = jnp.concatenate([jnp.where(low, xpair, jnp.zeros_like(xpair)),
                                    jnp.where(low, jnp.zeros_like(xpair), xpair)], axis=0)
            y = jnp.dot(jnp.concatenate(ms, axis=1), xblk, preferred_element_type=F32)
            ce = (cdup * jnp.where(low, es[0], es[1])).astype(BF16)
            y = y + jnp.dot(ce, st.astype(BF16), preferred_element_type=F32)
            ys.append(y)
            btw = (bt2 * jnp.where(rlow, ws[0], ws[1])).astype(BF16)
            snew = jnp.dot(btw, xpair, preferred_element_type=F32)
            dec = jnp.where(rlow, jnp.exp(acumt[h0:h0 + 1, edge:edge + 1]),
                            jnp.exp(acumt[h0 + 1:h0 + 2, edge:edge + 1]))
            st_ref[pair] = dec * st + jnp.where(blockmask, snew, 0.0)
    return jnp.concatenate(ys, axis=1)


def _ssd_fwd_kernel(xs_ref, bc_ref, dts_ref, dtst_ref, alog_ref, alogt_ref, y_ref, st_ref):
    @pl.when(pl.program_id(1) == 0)
    def _():
        st_ref[...] = jnp.zeros(st_ref.shape, F32)

    y_ref[...] = _ssd_chunk(xs_ref, bc_ref, dts_ref, dtst_ref, alog_ref, alogt_ref, st_ref, rev=False)


def _ssd_bwd_kernel(xs_ref, bc_ref, dts_ref, dtst_ref, alog_ref, alogt_ref, yf_ref, z_ref, dskip_ref, g_ref,
                    o_ref, st_ref):
    @pl.when(pl.program_id(1) == 0)
    def _():
        st_ref[...] = jnp.zeros(st_ref.shape, F32)

    yb = _ssd_chunk(xs_ref, bc_ref, dts_ref, dtst_ref, alog_ref, alogt_ref, st_ref, rev=True)
    z = z_ref[...]
    y = (yf_ref[...] + yb + xs_ref[...].astype(F32) * dskip_ref[...]) * (z * jax.nn.sigmoid(z))
    y = y * lax.rsqrt(jnp.mean(y * y, axis=-1, keepdims=True) + EPS) * g_ref[...]
    o_ref[...] = y.astype(o_ref.dtype)


def _ssd(xs, bc, dts, dtst, alog, alogt, z, dskip, g_norm):
    B, L, _ = xs.shape
    Q = SSM_CHUNK
    nc = L // Q
    ndt = 2 * SSM_HEADS
    npair = SSM_HEADS // 2

    def specs(cidx):
        row = lambda w: pl.BlockSpec((None, Q, w), lambda b, c: (b, cidx(c), 0))
        return row, [row(SSM_INNER), row(2 * SSM_GROUPS * SSM_N), row(ndt),
                     pl.BlockSpec((None, ndt, Q), lambda b, c: (b, 0, cidx(c))),
                     _const_spec((1, ndt)), _const_spec((ndt, 1))]

    scratch = [pltpu.VMEM((npair, 2 * SSM_N, 2 * SSM_P), F32)]
    row, in_specs = specs(lambda c: c)
    yf = pl.pallas_call(
        _ssd_fwd_kernel,
        grid=(B, nc),
        in_specs=in_specs,
        out_specs=row(SSM_INNER),
        out_shape=jax.ShapeDtypeStruct((B, L, SSM_INNER), F32),
        scratch_shapes=scratch,
        compiler_params=_params(("parallel", "arbitrary")),
        name="ssd_forward",
    )(xs, bc, dts, dtst, alog, alogt)
    row, in_specs = specs(lambda c: nc - 1 - c)
    return pl.pallas_call(
        _ssd_bwd_kernel,
        grid=(B, nc),
        in_specs=in_specs + [row(SSM_INNER), row(SSM_INNER), _const_spec((1, SSM_INNER)),
                             _const_spec((1, SSM_INNER))],
        out_specs=row(SSM_INNER),
        out_shape=jax.ShapeDtypeStruct((B, L, SSM_INNER), BF16),
        scratch_shapes=scratch,
        compiler_params=_params(("parallel", "arbitrary")),
        name="ssd_reverse_gate_norm",
    )(xs, bc, dts, dtst, alog, alogt, yf, z, dskip, g_norm)


def _merge_kernel(att_ref, ssm_ref, gates_ref, x_ref, wau_ref, wsu_ref, wout_ref, gffn_ref, wr_ref, wrt_ref,
                  place_ref, h_ref, u2_ref, afft_ref):
    D = D_MODEL
    E = N_EXPERTS
    ya = jnp.dot(att_ref[...], wau_ref[...], preferred_element_type=F32)
    ysm = jnp.dot(ssm_ref[...], wsu_ref[...], preferred_element_type=F32)
    merged = gates_ref[:, :D] * ya + gates_ref[:, D:] * ysm
    h = x_ref[...] + jnp.dot(merged.astype(BF16), wout_ref[...], preferred_element_type=F32)
    h_ref[...] = h
    u2 = (h * lax.rsqrt(jnp.mean(h * h, axis=-1, keepdims=True) + EPS) * gffn_ref[...]).astype(BF16)
    u2_ref[:, :D] = u2
    logits = jnp.dot(u2, wr_ref[...], preferred_element_type=F32)
    e = jnp.exp(logits - jnp.max(logits, axis=-1, keepdims=True))
    aff = e / jnp.sum(e, axis=-1, keepdims=True)
    hi = aff.astype(BF16)
    r1 = aff - hi.astype(F32)
    mid = r1.astype(BF16)
    lo = (r1 - mid.astype(F32)).astype(BF16)
    ext = (jnp.dot(hi, place_ref[0:E, :], preferred_element_type=F32)
           + jnp.dot(mid, place_ref[E:2 * E, :], preferred_element_type=F32)
           + jnp.dot(lo, place_ref[2 * E:3 * E, :], preferred_element_type=F32))
    u2_ref[:, D:] = ext.astype(BF16)
    lt = _nt_dot(wrt_ref[...], u2)
    et = jnp.exp(lt - jnp.max(lt, axis=0, keepdims=True))
    afft_ref[...] = et / jnp.sum(et, axis=0, keepdims=True)


def _merge(att, ssm, gates, x, wau, wsu, wout, g_ffn, wr, *, tm=256):
    B, L, D = x.shape
    E = N_EXPERTS
    nl = L // tm
    row = lambda w: pl.BlockSpec((None, tm, w), lambda b, i: (b, i, 0))
    place = jnp.eye(3 * E, GATE_EXT, dtype=BF16)
    return pl.pallas_call(
        _merge_kernel,
        grid=(B, nl),
        in_specs=[row(ATT_W), row(SSM_INNER), row(2 * D), row(D), _const_spec(wau.shape),
                  _const_spec(wsu.shape), _const_spec(wout.shape), _const_spec((1, D)), _const_spec(wr.shape),
                  _const_spec((E, D)), _const_spec((3 * E, GATE_EXT))],
        out_specs=[row(D), row(D + GATE_EXT), pl.BlockSpec((E, tm), lambda b, i: (0, b * nl + i))],
        out_shape=[jax.ShapeDtypeStruct((B, L, D), F32), jax.ShapeDtypeStruct((B, L, D + GATE_EXT), BF16),
                   jax.ShapeDtypeStruct((E, B * L), F32)],
        compiler_params=_params(("parallel", "parallel")),
        name="merge_router",
    )(att, ssm, gates, x, wau, wsu, wout, g_ffn, wr, wr.T, place)


def _route_kernel(afft_ref, posm_ref, bstart_ref, *, cap, nb):
    E = N_EXPERTS
    W = ROUTE_BLK

    def bit_body(k, thr):
        cand = thr | lax.shift_left(jnp.int32(1), 30 - k)
        x = lax.bitcast_convert_type(afft_ref[...], jnp.int32)
        cnt = jnp.sum(jnp.where(x >= cand, 1.0, 0.0), axis=1, keepdims=True)
        return jnp.where(cnt >= cap, cand, thr)

    thr = lax.fori_loop(0, 31, bit_body, jnp.zeros((E, 1), jnp.int32))
    x = lax.bitcast_convert_type(afft_ref[...], jnp.int32)
    n_gt = jnp.sum(jnp.where(x > thr, 1.0, 0.0), axis=1, keepdims=True)
    need = cap - n_gt
    ri = lax.broadcasted_iota(jnp.int32, (W, W), 0)
    ci = lax.broadcasted_iota(jnp.int32, (W, W), 1)
    upper = jnp.where(ri <= ci, 1.0, 0.0).astype(BF16)
    bstart_ref[...] = jnp.zeros(bstart_ref.shape, jnp.int32)
    c_eq = jnp.zeros((E, 1), F32)
    c_sel = jnp.zeros((E, 1), F32)
    for b in range(nb):
        xb = lax.bitcast_convert_type(afft_ref[:, b * W:(b + 1) * W], jnp.int32)
        eq = xb == thr
        eqc = jnp.dot(jnp.where(eq, 1.0, 0.0).astype(BF16), upper, preferred_element_type=F32) + c_eq
        sel = (xb > thr) | (eq & (eqc <= need))
        selc = jnp.dot(jnp.where(sel, 1.0, 0.0).astype(BF16), upper, preferred_element_type=F32) + c_sel
        posm_ref[:, b * W:(b + 1) * W] = jnp.where(sel, selc, 0.0).astype(jnp.int32)
        bstart_ref[:, b:b + 1] = c_sel.astype(jnp.int32)
        c_eq = eqc[:, W - 1:W]
        c_sel = selc[:, W - 1:W]
    bstart_ref[:, nb:nb + 1] = c_sel.astype(jnp.int32)


def _route(afft, cap):
    E, T = afft.shape
    nb = T // ROUTE_BLK
    assert nb < BSTART_W and cap % SLOT_TILE == 0
    return pl.pallas_call(
        functools.partial(_route_kernel, cap=float(cap), nb=nb),
        out_shape=[jax.ShapeDtypeStruct((E, T), jnp.int32), jax.ShapeDtypeStruct((E, BSTART_W), jnp.int32)],
        compiler_params=_params(None),
        name="route_select",
    )(afft)


def _pair_schedule(bstart, cap, ngb):
    step = GATHER_BLK // ROUTE_BLK
    st = bstart[:, 0:ngb * step + 1:step]
    lo, hi = st[:, :-1], st[:, 1:]
    ntile = cap // SLOT_TILE
    t_lo = jnp.arange(ntile, dtype=jnp.int32) * SLOT_TILE
    t_hi = t_lo + SLOT_TILE
    b_first = jnp.sum(hi[:, None, :] <= t_lo[None, :, None], axis=2).astype(jnp.int32)
    b_last = jnp.sum(lo[:, None, :] < t_hi[None, :, None], axis=2).astype(jnp.int32) - 1
    return b_first, b_last - b_first + 1


def _expert_kernel(first_ref, cnt_ref, u_hbm, posm_ref, wg_ref, wu_ref, wd_ref, o_ref, x_sc, ubuf, sem):
    D = D_MODEL
    e = pl.program_id(0)
    j = pl.program_id(1)
    g = e * pl.num_programs(1) + j
    b0 = first_ref[g]
    n = cnt_ref[g]

    half = GATHER_BLK // 2

    def half_copies(b, slot):
        return [pltpu.make_async_copy(
            u_hbm.at[pl.ds(pl.multiple_of(b * GATHER_BLK + r * half, half), half)],
            ubuf.at[slot, pl.ds(r * half, half)], sem.at[slot, r]) for r in range(2)]

    def start(b, slot):
        for cp in half_copies(b, slot):
            cp.start()

    def start_tile(t):
        start(first_ref[t], 0)

        @pl.when(cnt_ref[t] > 1)
        def _():
            start(first_ref[t] + 1, 1)

    @pl.when(g == 0)
    def _():
        start_tile(0)

    x_sc[...] = jnp.zeros(x_sc.shape, F32)
    slot_ids = lax.broadcasted_iota(jnp.int32, (SLOT_TILE, GATHER_BLK), 0) + (j * SLOT_TILE + 1)

    def body(c, carry):
        slot = c % GATHER_BUFS
        for cp in half_copies(b0 + c, slot):
            cp.wait()

        @pl.when(c + 2 < n)
        def _():
            start(b0 + c + 2, (c + 2) % GATHER_BUFS)

        posrow = posm_ref[:, pl.ds(pl.multiple_of((b0 + c) * GATHER_BLK, GATHER_BLK), GATHER_BLK)]
        onehot = jnp.where(posrow == slot_ids, 1.0, 0.0).astype(BF16)
        x_sc[...] += jnp.dot(onehot, ubuf[slot], preferred_element_type=F32)
        return carry

    lax.fori_loop(0, n, body, 0)

    @pl.when(g + 1 < pl.num_programs(0) * pl.num_programs(1))
    def _():
        start_tile(g + 1)

    xs = x_sc[:, :D].astype(BF16)
    ext = x_sc[:, D:]
    lane = lax.broadcasted_iota(jnp.int32, ext.shape, 1)
    mine = (lane == e) | (lane == e + N_EXPERTS) | (lane == e + 2 * N_EXPERTS)
    gate = jnp.sum(jnp.where(mine, ext, 0.0), axis=1, keepdims=True)
    hg = jnp.dot(xs, wg_ref[...], preferred_element_type=F32)
    hu = jnp.dot(xs, wu_ref[...], preferred_element_type=F32)
    hmid = (hg * jax.nn.sigmoid(hg) * hu).astype(BF16)
    o_ref[...] = jnp.dot(hmid, wd_ref[...], preferred_element_type=F32) * gate


def _experts(u2ext, posm, b_first, b_cnt, wg, wu, wd, cap):
    T, DX = u2ext.shape
    E, D = N_EXPERTS, D_MODEL
    wspec = pl.BlockSpec((None, D, D), lambda e, j, bf, bc: (e, 0, 0))
    return pl.pallas_call(
        _expert_kernel,
        grid_spec=pltpu.PrefetchScalarGridSpec(
            num_scalar_prefetch=2,
            grid=(E, cap // SLOT_TILE),
            in_specs=[pl.BlockSpec(memory_space=pl.ANY),
                      pl.BlockSpec((None, 1, T), lambda e, j, bf, bc: (e, 0, 0)),
                      wspec, wspec, wspec],
            out_specs=pl.BlockSpec((None, SLOT_TILE, D), lambda e, j, bf, bc: (e, j, 0)),
            scratch_shapes=[pltpu.VMEM((SLOT_TILE, DX), F32), pltpu.VMEM((GATHER_BUFS, GATHER_BLK, DX), BF16),
                            pltpu.SemaphoreType.DMA((GATHER_BUFS, 2))]),
        out_shape=jax.ShapeDtypeStruct((E, cap, D), F32),
        compiler_params=_params(("arbitrary", "arbitrary")),
        name="experts_gather_mlp",
    )(b_first.reshape(-1), b_cnt.reshape(-1), u2ext, posm.reshape(E, 1, T), wg, wu, wd)


def _combine_kernel(bstart_ref, posm_ref, ye_ref, o_ref, buf, xbuf, sem, xsem, *, cap, nb):
    E = N_EXPERTS
    WN = COMBINE_WIN
    b = pl.program_id(0)
    top = cap - WN
    cur = b % 2

    def window(blk, slot, e):
        base = jnp.minimum((bstart_ref[e, blk] // 8) * 8, top)
        return pltpu.make_async_copy(ye_ref.at[e, pl.ds(pl.multiple_of(base, 8), WN)],
                                     buf.at[slot, pl.ds(e * WN, WN)], sem.at[slot, e])

    @pl.when(b == 0)
    def _():
        for e in range(E):
            window(0, 0, e).start()

    @pl.when(b + 1 < nb)
    def _():
        for e in range(E):
            window(b + 1, 1 - cur, e).start()

    lows = [(bstart_ref[e, b] // 8) * 8 for e in range(E)]
    w_iota = lax.broadcasted_iota(jnp.int32, (WN, ROUTE_BLK), 0)

    def onehot_rows(slot0, low, base):
        ok = (slot0 >= low) & (slot0 < low + WN)
        return jnp.where(ok & (slot0 - base == w_iota), 1.0, 0.0).astype(BF16)

    def scatter(onehot, rows):
        hi = rows.astype(BF16)
        lo = (rows - hi.astype(F32)).astype(BF16)
        tn = lambda a, c: lax.dot_general(a, c, (((0,), (0,)), ((), ())), preferred_element_type=F32)
        return tn(onehot, hi) + tn(onehot, lo)

    hots = []
    for e in range(E):
        slot0 = posm_ref[e:e + 1, :] - 1
        hots.append(onehot_rows(slot0, lows[e], jnp.minimum(lows[e], top)))
    for e in range(E):
        window(b, cur, e).wait()
    o_ref[...] = scatter(jnp.concatenate(hots, axis=0), buf[cur])

    def per_expert(e, carry):
        low0 = (bstart_ref[e, b] // 8) * 8
        nwin = (bstart_ref[e, b + 1] - low0 + WN - 1) // WN

        def per_window(c, carry):
            low = low0 + c * WN
            base = jnp.minimum(low, top)
            cp = pltpu.make_async_copy(ye_ref.at[e, pl.ds(pl.multiple_of(base, 8), WN)], xbuf, xsem.at[0])
            cp.start()
            cp.wait()
            mine = lax.broadcasted_iota(jnp.int32, (E, ROUTE_BLK), 0) == e
            slot0 = jnp.sum(jnp.where(mine, posm_ref[...], 0), axis=0, keepdims=True) - 1
            o_ref[...] += scatter(onehot_rows(slot0, low, base), xbuf[...])
            return carry

        return lax.fori_loop(1, nwin, per_window, carry)

    lax.fori_loop(0, E, per_expert, 0)


def _combine(ye, posm, bstart, cap):
    E, T = posm.shape
    D = D_MODEL
    nb = T // ROUTE_BLK
    assert cap >= COMBINE_WIN
    return pl.pallas_call(
        functools.partial(_combine_kernel, cap=cap, nb=nb),
        grid_spec=pltpu.PrefetchScalarGridSpec(
            num_scalar_prefetch=1,
            grid=(nb,),
            in_specs=[pl.BlockSpec((E, ROUTE_BLK), lambda b, bs: (0, b)),
                      pl.BlockSpec(memory_space=pl.ANY)],
            out_specs=pl.BlockSpec((ROUTE_BLK, D), lambda b, bs: (b, 0)),
            scratch_shapes=[pltpu.VMEM((2, E * COMBINE_WIN, D), F32), pltpu.VMEM((COMBINE_WIN, D), F32),
                            pltpu.SemaphoreType.DMA((2, E)), pltpu.SemaphoreType.DMA((1,))]),
        out_shape=jax.ShapeDtypeStruct((T, D), F32),
        compiler_params=_params(("arbitrary",)),
        name="experts_combine",
    )(bstart, posm, ye)


def _ple_kernel(h_ref, moe_ref, p_ref, wpg_ref, bpg_ref, wpp_ref, gfin_ref, o_ref):
    h = h_ref[...] + moe_ref[...]
    gate = jax.nn.sigmoid(jnp.dot(h.astype(BF16), wpg_ref[...], preferred_element_type=F32) + bpg_ref[...])
    h = h + gate * jnp.dot(p_ref[...].astype(BF16), wpp_ref[...], preferred_element_type=F32)
    o_ref[...] = h * lax.rsqrt(jnp.mean(h * h, axis=-1, keepdims=True) + EPS) * gfin_ref[...]


def _ple(h, moe, p, wpg, bpg, wpp, g_final, *, tm=256):
    B, L, D = h.shape
    row = lambda w: pl.BlockSpec((None, tm, w), lambda b, i: (b, i, 0))
    return pl.pallas_call(
        _ple_kernel,
        grid=(B, L // tm),
        in_specs=[row(D), row(D), row(PLE_DIM), _const_spec(wpg.shape), _const_spec((1, D)),
                  _const_spec(wpp.shape), _const_spec((1, D))],
        out_specs=row(D),
        out_shape=jax.ShapeDtypeStruct((B, L, D), F32),
        compiler_params=_params(("parallel", "parallel")),
        name="ple_final_norm",
    )(h, moe, p, wpg, bpg, wpp, g_final)


def _forward(x, p, w, btiles):
    B, L, D = x.shape
    T = B * L
    qt, k, vt, z, xbc, dt, dtt, gates = _inproj(x, w['g_mix'], w['wqt'], w['wk'], w['wvt'], w['wz'], w['wxbc'],
                                                w['wdt'], w['wdtt'], w['wgate'], w['bgate'])
    att = _attention(qt, k, vt, btiles, w['lam'], w['g_subln'])
    xs, bc, dts, dtst = _conv(xbc, dt, dtt, w['conv_w'], w['conv_b'], w['dtb'], w['dtbt'])
    ssm = _ssd(xs, bc, dts, dtst, w['alog'], w['alogt'], z, w['dskip'], w['g_ssm_norm'])
    h, u2ext, afft = _merge(att, ssm, gates, x, w['wau'], w['wsu'], w['wout'], w['g_ffn'], w['wr'])
    cap = max(1, CAPACITY_FACTOR * T // N_EXPERTS)
    posm, bstart = _route(afft, cap)
    b_first, b_cnt = _pair_schedule(bstart, cap, T // GATHER_BLK)
    ye = _experts(u2ext.reshape(T, D + GATE_EXT), posm, b_first, b_cnt, w['weg'], w['weu'], w['wed'], cap)
    moe = _combine(ye, posm, bstart, cap).reshape(B, L, D)
    return _ple(h, moe, p, w['wpg'], w['bpg'], w['wpp'], w['g_final'])


def kernel(x_prompt, x_sample, p_prompt, p_sample, rel_bias, g_mix, w_in, conv_w, conv_b, dt_bias, a_log,
           d_skip, g_ssm_norm, lam_params, g_subln, w_att_up, w_ssm_up, w_gate, b_gate, w_out, g_ffn,
           w_router, w_exp_gate, w_exp_up, w_exp_down, w_ple_gate, b_ple_gate, w_ple_proj, g_final):
    i = 0
    D = D_MODEL
    win = w_in[i]
    o1 = ATT_W
    o3 = 3 * ATT_W
    o4 = o3 + SSM_INNER
    o5 = o4 + CONV_CH
    wqt = (win[:, :o1] * (QK_DIM ** -0.5 * LOG2E)).T.astype(BF16)
    wk = win[:, o1:2 * o1].astype(BF16)
    wvt = win[:, 2 * o1:o3].T.astype(BF16)
    wdt = win[:, o5:].astype(BF16)
    lp = lam_params[i].astype(F32)
    lam = jnp.exp(jnp.sum(lp[0] * lp[1])) - jnp.exp(jnp.sum(lp[2] * lp[3])) + LAM_INIT
    ndt = 2 * SSM_HEADS
    w = {
        'g_mix': g_mix[i].reshape(1, D), 'wqt': wqt, 'wk': wk, 'wvt': wvt, 'wz': win[:, o3:o4].astype(BF16),
        'wxbc': win[:, o4:o5].astype(BF16), 'wdt': wdt, 'wdtt': wdt.T,
        'wgate': w_gate[i].astype(BF16), 'bgate': b_gate[i].reshape(1, 2 * D),
        'lam': lam.reshape(1), 'g_subln': g_subln[i].reshape(HEAD_W, 1),
        'conv_w': conv_w[i], 'conv_b': conv_b[i].reshape(1, CONV_CH),
        'dtb': dt_bias[i].reshape(1, ndt), 'dtbt': dt_bias[i].reshape(ndt, 1),
        'alog': a_log[i].reshape(1, ndt), 'alogt': a_log[i].reshape(ndt, 1),
        'dskip': jnp.repeat(d_skip[i], SSM_P).reshape(1, SSM_INNER),
        'g_ssm_norm': g_ssm_norm[i].reshape(1, SSM_INNER),
        'wau': w_att_up[i].astype(BF16), 'wsu': w_ssm_up[i].astype(BF16), 'wout': w_out[i].astype(BF16),
        'g_ffn': g_ffn[i].reshape(1, D), 'wr': w_router[i].astype(BF16),
        'weg': w_exp_gate[i].astype(BF16), 'weu': w_exp_up[i].astype(BF16), 'wed': w_exp_down[i].astype(BF16),
        'wpg': w_ple_gate[i].astype(BF16), 'bpg': b_ple_gate[i].reshape(1, D),
        'wpp': w_ple_proj[i].astype(BF16), 'g_final': g_final.reshape(1, D),
    }
    btiles = _bias_tiles(rel_bias, ATT_QK_RATIO * ATT_BLOCK, ATT_BLOCK)
    return (_forward(x_prompt, p_prompt[i], w, btiles), _forward(x_sample, p_sample[i], w, btiles))
```

```python
import functools
import math

import jax
import jax.numpy as jnp
from jax import lax
from jax.experimental import pallas as pl
from jax.experimental.pallas import tpu as pltpu

F32 = jnp.float32
BF16 = jnp.bfloat16

D_MODEL = 1024
N_HEADS = 4
QK_DIM = 64
HEAD_W = 2 * QK_DIM
ATT_W = N_HEADS * HEAD_W
REL_BUCKETS = 32
REL_MAX_DIST = 128
SSM_INNER = 1024
SSM_P = 64
SSM_HEADS = 16
SSM_GROUPS = 2
SSM_N = 64
SSM_CHUNK = 128
CONV_CH = SSM_INNER + 2 * SSM_GROUPS * SSM_N
N_EXPERTS = 16
CAPACITY_FACTOR = 2
PLE_DIM = 256
EPS = 1e-6
LAM_INIT = 0.8 - 0.6 * math.exp(-0.3 * 0)
LOG2E = math.log2(math.e)

GATE_EXT = 128
ROUTE_BLK = 256
GATHER_BLK = 512
GATHER_BUFS = 5
SLOT_TILE = 256
COMBINE_WIN = 72
BSTART_W = 256

ATT_BLOCK = 256
ATT_QK_RATIO = 2
ATT_BIAS_TILES = ATT_QK_RATIO + 4
ONES_ROWS = 16
VMEM_LIMIT = 56 * 1024 * 1024


def _params(sem, vmem=VMEM_LIMIT):
    return pltpu.CompilerParams(dimension_semantics=sem, vmem_limit_bytes=vmem)


def _const_spec(shape):
    nd = len(shape)
    return pl.BlockSpec(shape, lambda *_: (0,) * nd)


def _nt_dot(a, b, **kw):
    return lax.dot_general(a, b, (((1,), (1,)), ((), ())), preferred_element_type=F32, **kw)


def _inproj_kernel(x_ref, g_ref, wqt_ref, wk_ref, wvt_ref, wz_ref, wxbc_ref, wdt_ref, wdtt_ref, wgate_ref,
                   bgate_ref, qt_ref, k_ref, vt_ref, z_ref, xbc_ref, dt_ref, dtt_ref, gates_ref):
    x = x_ref[...]
    u = x * lax.rsqrt(jnp.mean(x * x, axis=-1, keepdims=True) + EPS) * g_ref[...]
    ub = u.astype(BF16)
    qt_ref[...] = _nt_dot(wqt_ref[...], ub).astype(BF16)
    k_ref[...] = jnp.dot(ub, wk_ref[...], preferred_element_type=F32).astype(BF16)
    vt_ref[...] = _nt_dot(wvt_ref[...], ub).astype(BF16)
    z_ref[...] = jnp.dot(ub, wz_ref[...], preferred_element_type=F32).astype(z_ref.dtype)
    xbc_ref[...] = jnp.dot(ub, wxbc_ref[...], preferred_element_type=F32)
    dt_ref[...] = jnp.dot(ub, wdt_ref[...], preferred_element_type=F32)
    dtt_ref[...] = _nt_dot(wdtt_ref[...], ub)
    gates = jax.nn.sigmoid(jnp.dot(ub, wgate_ref[...], preferred_element_type=F32) + bgate_ref[...])
    gates_ref[...] = gates.astype(gates_ref.dtype)


def _inproj(x, g_mix, wqt, wk, wvt, wz, wxbc, wdt, wdtt, wgate, bgate, *, tm=256):
    B, L, D = x.shape
    grid = (B, L // tm)
    row = lambda w: pl.BlockSpec((None, tm, w), lambda b, i: (b, i, 0))
    col = lambda w: pl.BlockSpec((None, w, tm), lambda b, i: (b, 0, i))
    ndt = 2 * SSM_HEADS
    return pl.pallas_call(
        _inproj_kernel,
        grid=grid,
        in_specs=[row(D), _const_spec((1, D)), _const_spec(wqt.shape), _const_spec(wk.shape),
                  _const_spec(wvt.shape), _const_spec(wz.shape),
                  _const_spec(wxbc.shape), _const_spec(wdt.shape), _const_spec(wdtt.shape),
                  _const_spec(wgate.shape), _const_spec((1, 2 * D))],
        out_specs=[col(ATT_W), row(ATT_W), col(ATT_W), row(SSM_INNER), row(CONV_CH), row(ndt),
                   col(ndt), row(2 * D)],
        out_shape=[jax.ShapeDtypeStruct((B, ATT_W, L), BF16),
                   jax.ShapeDtypeStruct((B, L, ATT_W), BF16),
                   jax.ShapeDtypeStruct((B, ATT_W, L), BF16),
                   jax.ShapeDtypeStruct((B, L, SSM_INNER), BF16),
                   jax.ShapeDtypeStruct((B, L, CONV_CH), F32),
                   jax.ShapeDtypeStruct((B, L, ndt), F32),
                   jax.ShapeDtypeStruct((B, ndt, L), F32),
                   jax.ShapeDtypeStruct((B, L, 2 * D), BF16)],
        compiler_params=_params(("parallel", "parallel")),
        name="inproj",
    )(x, g_mix, wqt, wk, wvt, wz, wxbc, wdt, wdtt, wgate, bgate)


def _rel_bucket(rel):
    half = REL_BUCKETS // 2
    max_exact = half // 2
    ret = jnp.where(rel > 0, half, 0)
    n = jnp.abs(rel)
    nf = jnp.maximum(n, 1).astype(jnp.float32)
    large = max_exact + (jnp.log(nf / max_exact) / math.log(REL_MAX_DIST / max_exact)
                         * (half - max_exact)).astype(jnp.int32)
    large = jnp.minimum(large, half - 1)
    return ret + jnp.where(n < max_exact, n, large)


def _bias_kernel(table_ref, bucket_ref, out_ref, *, TQ):
    h = pl.program_id(0)
    out_ref[...] = jnp.zeros(out_ref.shape, F32)
    for d in range(ATT_BIAS_TILES):
        for m in range(2):
            cols = pl.ds(m * TQ, TQ)

            def body(b, carry, d=d, m=m, cols=cols):
                val = table_ref[b, 2 * h + m] * LOG2E
                out_ref[d, :, cols] = jnp.where(bucket_ref[d] == b, val, out_ref[d, :, cols])
                return carry

            lax.fori_loop(0, REL_BUCKETS, body, 0)


def _bias_tiles(rel_bias, TQ, TK):
    assert TK > REL_MAX_DIST
    base = jnp.arange(TK, dtype=jnp.int32)[:, None] - jnp.arange(TQ, dtype=jnp.int32)[None, :]
    bucket = jnp.stack([_rel_bucket(base + (d - 2) * TK) for d in range(ATT_BIAS_TILES)])
    return pl.pallas_call(
        functools.partial(_bias_kernel, TQ=TQ),
        grid=(N_HEADS,),
        in_specs=[pl.BlockSpec(memory_space=pltpu.SMEM), _const_spec((ATT_BIAS_TILES, TK, TQ))],
        out_specs=pl.BlockSpec((None, ATT_BIAS_TILES, TK, 2 * TQ), lambda h: (h, 0, 0, 0)),
        out_shape=jax.ShapeDtypeStruct((N_HEADS, ATT_BIAS_TILES, TK, 2 * TQ), F32),
        compiler_params=_params(("parallel",)),
        name="rel_bias_tiles",
    )(rel_bias.astype(F32), bucket)


def _attn_kernel(lam_ref, qt_ref, k_ref, vt_ref, bt_ref, g_ref, o_ref, acc_sc, p_sc, *, TQ, T, nk):
    i = pl.program_id(2)
    qt = qt_ref[...]
    sub = lax.broadcasted_iota(jnp.int32, qt.shape, 0)
    zero = jnp.zeros_like(qt)
    q2t = jnp.concatenate([jnp.where(sub < QK_DIM, qt, zero), jnp.where(sub >= QK_DIM, qt, zero)], axis=1)
    acc_sc[...] = jnp.zeros(acc_sc.shape, F32)

    def scores_softmax(j, slot, m):
        kb = k_ref[pl.ds(pl.multiple_of(j * T, T), T), :]
        tile = jnp.clip(j - ATT_QK_RATIO * i, -2, ATT_QK_RATIO + 1) + 2
        s = jnp.dot(kb, q2t, preferred_element_type=F32) + bt_ref[tile]
        m_new = jnp.maximum(m, jnp.max(s, axis=0, keepdims=True))
        p_sc[slot] = jnp.exp2(s - m_new).astype(BF16)
        return m_new, jnp.exp2(m - m_new)

    ones = jnp.ones((ONES_ROWS, T), BF16)

    def accumulate(j, slot, alpha):
        vb = jnp.concatenate([vt_ref[:, pl.ds(pl.multiple_of(j * T, T), T)], ones], axis=0)
        acc_sc[...] = alpha * acc_sc[...] + jnp.dot(vb, p_sc[slot], preferred_element_type=F32)

    m = jnp.full((1, 2 * TQ), -jnp.inf, F32)
    m, alpha = scores_softmax(0, 0, m)

    def pair(u, carry):
        m, alpha = carry
        t = 2 * u
        m, alpha1 = scores_softmax(t - 1, 1, m)
        accumulate(t - 2, 0, alpha)
        m, alpha0 = scores_softmax(t, 0, m)
        accumulate(t - 1, 1, alpha1)
        return m, alpha0

    def quad(u, carry):
        return pair(2 * u + 1, pair(2 * u, carry))

    m, alpha = lax.fori_loop(1, nk // 4, quad, pair(1, (m, alpha)))
    m, alpha1 = scores_softmax(nk - 1, 1, m)
    accumulate(nk - 2, 0, alpha)
    accumulate(nk - 1, 1, alpha1)

    o = acc_sc[0:HEAD_W, :] / acc_sc[HEAD_W:HEAD_W + 1, :]
    od = o[:, :TQ] - lam_ref[0] * o[:, TQ:]
    y = od * lax.rsqrt(jnp.mean(od * od, axis=0, keepdims=True) + EPS) * g_ref[...]
    o_ref[...] = (y * (1.0 - LAM_INIT)).T.astype(o_ref.dtype)


def _attention(qt, k, vt, btiles, lam, g_col, *, T=ATT_BLOCK):
    B, L, _ = k.shape
    TQ = ATT_QK_RATIO * T
    nk = L // T
    assert nk % 4 == 0 and L % TQ == 0
    return pl.pallas_call(
        functools.partial(_attn_kernel, TQ=TQ, T=T, nk=nk),
        grid=(B, N_HEADS, L // TQ),
        in_specs=[pl.BlockSpec(memory_space=pltpu.SMEM),
                  pl.BlockSpec((None, HEAD_W, TQ), lambda b, h, i: (b, h, i)),
                  pl.BlockSpec((None, L, HEAD_W), lambda b, h, i: (b, 0, h)),
                  pl.BlockSpec((None, HEAD_W, L), lambda b, h, i: (b, h, 0)),
                  pl.BlockSpec((None, ATT_BIAS_TILES, T, 2 * TQ), lambda b, h, i: (h, 0, 0, 0)),
                  _const_spec((HEAD_W, 1))],
        out_specs=pl.BlockSpec((None, TQ, HEAD_W), lambda b, h, i: (b, i, h)),
        out_shape=jax.ShapeDtypeStruct((B, L, ATT_W), BF16),
        scratch_shapes=[pltpu.VMEM((HEAD_W + ONES_ROWS, 2 * TQ), F32), pltpu.VMEM((2, T, 2 * TQ), BF16)],
        compiler_params=_params(("parallel", "parallel", "parallel")),
        name="diff_attention",
    )(lam, qt, k, vt, btiles, g_col)


def _conv_kernel(x_ref, prev_ref, next_ref, dt_ref, dtt_ref, w_ref, b_ref, dtb_ref, dtbt_ref,
                 xs_ref, bc_ref, dts_ref, dtst_ref, *, tl, nt):
    i = pl.program_id(1)
    x = x_ref[...]
    row = lax.broadcasted_iota(jnp.int32, x.shape, 0)
    prev_row = jnp.where(i > 0, prev_ref[7:8, :], 0.0)
    next_row = jnp.where(i < nt - 1, next_ref[0:1, :], 0.0)
    xm1 = jnp.where(row == 0, prev_row, pltpu.roll(x, 1, 0))
    xp1 = jnp.where(row == tl - 1, next_row, pltpu.roll(x, tl - 1, 0))
    y = w_ref[0:1, :] * xm1 + w_ref[1:2, :] * x + w_ref[2:3, :] * xp1 + b_ref[...]
    y = y * jax.nn.sigmoid(y)
    xs_ref[...] = y[:, :SSM_INNER].astype(BF16)
    bc_ref[...] = y[:, SSM_INNER:].astype(BF16)

    def softplus(v):
        return jnp.maximum(v, 0.0) + jnp.log1p(jnp.exp(-jnp.abs(v)))

    dts_ref[...] = softplus(dt_ref[...] + dtb_ref[...])
    dtst_ref[...] = softplus(dtt_ref[...] + dtbt_ref[...])


def _conv(xbc, dt, dtt, conv_w, conv_b, dtb, dtbt, *, tl=512):
    B, L, _ = xbc.shape
    nt = L // tl
    r8 = tl // 8
    ndt = 2 * SSM_HEADS
    row = lambda w: pl.BlockSpec((None, tl, w), lambda b, i: (b, i, 0))
    colspec = pl.BlockSpec((None, ndt, tl), lambda b, i: (b, 0, i))
    return pl.pallas_call(
        functools.partial(_conv_kernel, tl=tl, nt=nt),
        grid=(B, nt),
        in_specs=[row(CONV_CH),
                  pl.BlockSpec((None, 8, CONV_CH), lambda b, i: (b, jnp.maximum(i * r8 - 1, 0), 0)),
                  pl.BlockSpec((None, 8, CONV_CH), lambda b, i: (b, jnp.minimum((i + 1) * r8, L // 8 - 1), 0)),
                  row(ndt), colspec,
                  _const_spec((3, CONV_CH)), _const_spec((1, CONV_CH)), _const_spec((1, ndt)),
                  _const_spec((ndt, 1))],
        out_specs=[row(SSM_INNER), row(2 * SSM_GROUPS * SSM_N), row(ndt), colspec],
        out_shape=[jax.ShapeDtypeStruct((B, L, SSM_INNER), BF16),
                   jax.ShapeDtypeStruct((B, L, 2 * SSM_GROUPS * SSM_N), BF16),
                   jax.ShapeDtypeStruct((B, L, ndt), F32),
                   jax.ShapeDtypeStruct((B, ndt, L), F32)],
        compiler_params=_params(("parallel", "parallel")),
        name="ssm_conv",
    )(xbc, xbc, xbc, dt, dtt, conv_w, conv_b, dtb, dtbt)


def _ssd_chunk(xs_ref, bc_ref, dts_ref, dtst_ref, alog_ref, alogt_ref, st_ref, *, rev):
    Q = SSM_CHUNK
    HI = lax.Precision.HIGHEST
    off = SSM_HEADS if rev else 0
    ri = lax.broadcasted_iota(jnp.int32, (Q, Q), 0)
    ci = lax.broadcasted_iota(jnp.int32, (Q, Q), 1)
    incl = (ci >= ri) if rev else (ci <= ri)
    tri = incl.astype(F32)
    dt = dts_ref[:, off:off + SSM_HEADS]
    dtt = dtst_ref[off:off + SSM_HEADS, :]
    a = dt * (-jnp.exp(alog_ref[:, off:off + SSM_HEADS]))
    at = dtt * (-jnp.exp(alogt_ref[off:off + SSM_HEADS, :]))
    acum = jnp.dot(tri, a, preferred_element_type=F32, precision=HI)
    acumt = _nt_dot(at, tri, precision=HI)
    edge = 0 if rev else Q - 1
    bc = bc_ref[...]
    bpair = bc[:, :Q]
    cpair = bc[:, Q:]
    low = ci < SSM_N
    rlow = ri < SSM_N
    blockmask = low == rlow
    total = acumt[:, edge:edge + 1]
    decay = jnp.exp(total)
    w_rows = dtt * jnp.exp(total - acumt)
    ys = []
    for g in range(SSM_GROUPS):
        gmask = low if g == 0 else jnp.logical_not(low)
        cg = jnp.where(gmask, cpair, jnp.zeros_like(cpair))
        cb = _nt_dot(cg, bpair)
        dupm = ((ri - g * SSM_N) == (ci % SSM_N)).astype(BF16)
        cdup = jnp.dot(cpair, dupm, preferred_element_type=F32)
        sel2 = ((ci - g * SSM_N) == (ri % SSM_N)).astype(BF16)
        bt2 = _nt_dot(sel2, bpair)
        for hp in range(SSM_HEADS // SSM_GROUPS // 2):
            h0 = g * (SSM_HEADS // SSM_GROUPS) + 2 * hp
            pair = h0 // 2
            xpair = xs_ref[:, h0 * SSM_P:(h0 + 2) * SSM_P]
            ms = []
            e_cols = []
            for h in (h0, h0 + 1):
                e_col = jnp.broadcast_to(acum[:, h:h + 1], (Q, Q))
                r_row = jnp.broadcast_to(acumt[h:h + 1, :], (Q, Q))
                seg = jnp.where(incl, e_col - r_row, -jnp.inf)
                ms.append((cb * jnp.exp(seg) * jnp.broadcast_to(dtt[h:h + 1, :], (Q, Q))).astype(BF16))
                e_cols.append(e_col)
            st = st_ref[pair]
            xblk = jnp.concatenate([jnp.where(low, xpair, jnp.zeros_like(xpair)),
                                    jnp.where(low, jnp.zeros_like(xpair), xpair)], axis=0)
            y = jnp.dot(jnp.concatenate(ms, axis=1), xblk, preferred_element_type=F32)
            ce = (cdup * jnp.exp(jnp.where(low, e_cols[0], e_cols[1]))).astype(BF16)
            y = y + jnp.dot(ce, st.astype(BF16), preferred_element_type=F32)
            ys.append(y)
            w_pair = jnp.where(rlow, jnp.broadcast_to(w_rows[h0:h0 + 1, :], (Q, Q)),
                               jnp.broadcast_to(w_rows[h0 + 1:h0 + 2, :], (Q, Q)))
            snew = jnp.dot((bt2 * w_pair).astype(BF16), xpair, preferred_element_type=F32)
            dec = jnp.where(rlow, decay[h0:h0 + 1, :], decay[h0 + 1:h0 + 2, :])
            st_ref[pair] = dec * st + jnp.where(blockmask, snew, 0.0)
    return jnp.concatenate(ys, axis=1)


def _ssd_fwd_kernel(xs_ref, bc_ref, dts_ref, dtst_ref, alog_ref, alogt_ref, y_ref, st_ref):
    @pl.when(pl.program_id(1) == 0)
    def _():
        st_ref[...] = jnp.zeros(st_ref.shape, F32)

    y = _ssd_chunk(xs_ref, bc_ref, dts_ref, dtst_ref, alog_ref, alogt_ref, st_ref, rev=False)
    y_ref[...] = y.astype(y_ref.dtype)


def _ssd_bwd_kernel(xs_ref, bc_ref, dts_ref, dtst_ref, alog_ref, alogt_ref, yf_ref, z_ref, dskip_ref, g_ref,
                    o_ref, st_ref):
    @pl.when(pl.program_id(1) == 0)
    def _():
        st_ref[...] = jnp.zeros(st_ref.shape, F32)

    yb = _ssd_chunk(xs_ref, bc_ref, dts_ref, dtst_ref, alog_ref, alogt_ref, st_ref, rev=True)
    z = z_ref[...].astype(F32)
    y = (yf_ref[...].astype(F32) + yb + xs_ref[...].astype(F32) * dskip_ref[...]) * (z * jax.nn.sigmoid(z))
    y = y * lax.rsqrt(jnp.mean(y * y, axis=-1, keepdims=True) + EPS) * g_ref[...]
    o_ref[...] = y.astype(o_ref.dtype)


def _ssd(xs, bc, dts, dtst, alog, alogt, z, dskip, g_norm):
    B, L, _ = xs.shape
    Q = SSM_CHUNK
    nc = L // Q
    ndt = 2 * SSM_HEADS
    npair = SSM_HEADS // 2

    def specs(cidx):
        row = lambda w: pl.BlockSpec((None, Q, w), lambda b, c: (b, cidx(c), 0))
        return row, [row(SSM_INNER), row(2 * SSM_GROUPS * SSM_N), row(ndt),
                     pl.BlockSpec((None, ndt, Q), lambda b, c: (b, 0, cidx(c))),
                     _const_spec((1, ndt)), _const_spec((ndt, 1))]

    scratch = [pltpu.VMEM((npair, 2 * SSM_N, 2 * SSM_P), F32)]
    row, in_specs = specs(lambda c: c)
    yf = pl.pallas_call(
        _ssd_fwd_kernel,
        grid=(B, nc),
        in_specs=in_specs,
        out_specs=row(SSM_INNER),
        out_shape=jax.ShapeDtypeStruct((B, L, SSM_INNER), BF16),
        scratch_shapes=scratch,
        compiler_params=_params(("parallel", "arbitrary")),
        name="ssd_forward",
    )(xs, bc, dts, dtst, alog, alogt)
    row, in_specs = specs(lambda c: nc - 1 - c)
    return pl.pallas_call(
        _ssd_bwd_kernel,
        grid=(B, nc),
        in_specs=in_specs + [row(SSM_INNER), row(SSM_INNER), _const_spec((1, SSM_INNER)),
                             _const_spec((1, SSM_INNER))],
        out_specs=row(SSM_INNER),
        out_shape=jax.ShapeDtypeStruct((B, L, SSM_INNER), BF16),
        scratch_shapes=scratch,
        compiler_params=_params(("parallel", "arbitrary")),
        name="ssd_reverse_gate_norm",
    )(xs, bc, dts, dtst, alog, alogt, yf, z, dskip, g_norm)


def _merge_kernel(att_ref, ssm_ref, gates_ref, x_ref, wau_ref, wsu_ref, wout_ref, gffn_ref, wr_ref, wrt_ref,
                  place_ref, h_ref, u2_ref, afft_ref):
    D = D_MODEL
    E = N_EXPERTS
    ya = jnp.dot(att_ref[...], wau_ref[...], preferred_element_type=F32)
    ysm = jnp.dot(ssm_ref[...], wsu_ref[...], preferred_element_type=F32)
    merged = gates_ref[:, :D] * ya + gates_ref[:, D:] * ysm
    h = x_ref[...] + jnp.dot(merged.astype(BF16), wout_ref[...], preferred_element_type=F32)
    h_ref[...] = h
    u2 = (h * lax.rsqrt(jnp.mean(h * h, axis=-1, keepdims=True) + EPS) * gffn_ref[...]).astype(BF16)
    u2_ref[:, :D] = u2
    logits = jnp.dot(u2, wr_ref[...], preferred_element_type=F32)
    e = jnp.exp(logits - jnp.max(logits, axis=-1, keepdims=True))
    aff = e / jnp.sum(e, axis=-1, keepdims=True)
    hi = aff.astype(BF16)
    r1 = aff - hi.astype(F32)
    mid = r1.astype(BF16)
    lo = (r1 - mid.astype(F32)).astype(BF16)
    ext = (jnp.dot(hi, place_ref[0:E, :], preferred_element_type=F32)
           + jnp.dot(mid, place_ref[E:2 * E, :], preferred_element_type=F32)
           + jnp.dot(lo, place_ref[2 * E:3 * E, :], preferred_element_type=F32))
    u2_ref[:, D:] = ext.astype(BF16)
    lt = _nt_dot(wrt_ref[...], u2)
    et = jnp.exp(lt - jnp.max(lt, axis=0, keepdims=True))
    afft_ref[...] = et / jnp.sum(et, axis=0, keepdims=True)


def _merge(att, ssm, gates, x, wau, wsu, wout, g_ffn, wr, *, tm=256):
    B, L, D = x.shape
    E = N_EXPERTS
    nl = L // tm
    row = lambda w: pl.BlockSpec((None, tm, w), lambda b, i: (b, i, 0))
    place = jnp.eye(3 * E, GATE_EXT, dtype=BF16)
    return pl.pallas_call(
        _merge_kernel,
        grid=(B, nl),
        in_specs=[row(ATT_W), row(SSM_INNER), row(2 * D), row(D), _const_spec(wau.shape),
                  _const_spec(wsu.shape), _const_spec(wout.shape), _const_spec((1, D)), _const_spec(wr.shape),
                  _const_spec((E, D)), _const_spec((3 * E, GATE_EXT))],
        out_specs=[row(D), row(D + GATE_EXT), pl.BlockSpec((E, tm), lambda b, i: (0, b * nl + i))],
        out_shape=[jax.ShapeDtypeStruct((B, L, D), F32), jax.ShapeDtypeStruct((B, L, D + GATE_EXT), BF16),
                   jax.ShapeDtypeStruct((E, B * L), F32)],
        compiler_params=_params(("parallel", "parallel")),
        name="merge_router",
    )(att, ssm, gates, x, wau, wsu, wout, g_ffn, wr, wr.T, place)


def _route_kernel(afft_ref, posm_ref, bstart_ref, *, cap, nb):
    E = N_EXPERTS
    W = ROUTE_BLK

    def bit_body(k, thr):
        cand = thr | lax.shift_left(jnp.int32(1), 30 - k)
        x = lax.bitcast_convert_type(afft_ref[...], jnp.int32)
        cnt = jnp.sum(jnp.where(x >= cand, 1.0, 0.0), axis=1, keepdims=True)
        return jnp.where(cnt >= cap, cand, thr)

    thr = lax.fori_loop(0, 31, bit_body, jnp.zeros((E, 1), jnp.int32))
    x = lax.bitcast_convert_type(afft_ref[...], jnp.int32)
    n_gt = jnp.sum(jnp.where(x > thr, 1.0, 0.0), axis=1, keepdims=True)
    need = cap - n_gt
    ri = lax.broadcasted_iota(jnp.int32, (W, W), 0)
    ci = lax.broadcasted_iota(jnp.int32, (W, W), 1)
    upper = jnp.where(ri <= ci, 1.0, 0.0).astype(BF16)
    bstart_ref[...] = jnp.zeros(bstart_ref.shape, jnp.int32)
    c_eq = jnp.zeros((E, 1), F32)
    c_sel = jnp.zeros((E, 1), F32)
    for b in range(nb):
        xb = lax.bitcast_convert_type(afft_ref[:, b * W:(b + 1) * W], jnp.int32)
        eq = xb == thr
        eqc = jnp.dot(jnp.where(eq, 1.0, 0.0).astype(BF16), upper, preferred_element_type=F32) + c_eq
        sel = (xb > thr) | (eq & (eqc <= need))
        selc = jnp.dot(jnp.where(sel, 1.0, 0.0).astype(BF16), upper, preferred_element_type=F32) + c_sel
        posm_ref[:, b * W:(b + 1) * W] = jnp.where(sel, selc, 0.0).astype(jnp.int32)
        bstart_ref[:, b:b + 1] = c_sel.astype(jnp.int32)
        c_eq = eqc[:, W - 1:W]
        c_sel = selc[:, W - 1:W]
    bstart_ref[:, nb:nb + 1] = c_sel.astype(jnp.int32)


def _route(afft, cap):
    E, T = afft.shape
    nb = T // ROUTE_BLK
    assert nb < BSTART_W and cap % SLOT_TILE == 0
    return pl.pallas_call(
        functools.partial(_route_kernel, cap=float(cap), nb=nb),
        out_shape=[jax.ShapeDtypeStruct((E, T), jnp.int32), jax.ShapeDtypeStruct((E, BSTART_W), jnp.int32)],
        compiler_params=_params(None),
        name="route_select",
    )(afft)


def _pair_schedule(bstart, cap, ngb):
    step = GATHER_BLK // ROUTE_BLK
    st = bstart[:, 0:ngb * step + 1:step]
    lo, hi = st[:, :-1], st[:, 1:]
    ntile = cap // SLOT_TILE
    t_lo = jnp.arange(ntile, dtype=jnp.int32) * SLOT_TILE
    t_hi = t_lo + SLOT_TILE
    b_first = jnp.sum(hi[:, None, :] <= t_lo[None, :, None], axis=2).astype(jnp.int32)
    b_last = jnp.sum(lo[:, None, :] < t_hi[None, :, None], axis=2).astype(jnp.int32) - 1
    return b_first, b_last - b_first + 1


def _expert_kernel(first_ref, cnt_ref, u_hbm, posm_ref, wg_ref, wu_ref, wd_ref, o_ref, x_sc, ubuf, sem):
    D = D_MODEL
    e = pl.program_id(0)
    j = pl.program_id(1)
    g = e * pl.num_programs(1) + j
    b0 = first_ref[g]
    n = cnt_ref[g]

    half = GATHER_BLK // 2

    def half_copies(b, slot):
        return [pltpu.make_async_copy(
            u_hbm.at[pl.ds(pl.multiple_of(b * GATHER_BLK + r * half, half), half)],
            ubuf.at[slot, pl.ds(r * half, half)], sem.at[slot, r]) for r in range(2)]

    def start(b, slot):
        for cp in half_copies(b, slot):
            cp.start()

    ahead = GATHER_BUFS - 1

    def start_tile(t):
        start(first_ref[t], 0)
        for r in range(1, ahead):
            @pl.when(cnt_ref[t] > r)
            def _(r=r):
                start(first_ref[t] + r, r)

    @pl.when(g == 0)
    def _():
        start_tile(0)

    x_sc[...] = jnp.zeros(x_sc.shape, F32)
    slot_ids = lax.broadcasted_iota(jnp.int32, (SLOT_TILE, GATHER_BLK), 0) + (j * SLOT_TILE + 1)

    def body(c, carry):
        slot = c % GATHER_BUFS
        for cp in half_copies(b0 + c, slot):
            cp.wait()

        @pl.when(c + ahead < n)
        def _():
            start(b0 + c + ahead, (c + ahead) % GATHER_BUFS)

        posrow = posm_ref[:, pl.ds(pl.multiple_of((b0 + c) * GATHER_BLK, GATHER_BLK), GATHER_BLK)]
        onehot = jnp.where(posrow == slot_ids, 1.0, 0.0).astype(BF16)
        x_sc[...] += jnp.dot(onehot, ubuf[slot], preferred_element_type=F32)
        return carry

    lax.fori_loop(0, n, body, 0)

    @pl.when(g + 1 < pl.num_programs(0) * pl.num_programs(1))
    def _():
        start_tile(g + 1)

    xs = x_sc[:, :D].astype(BF16)
    ext = x_sc[:, D:]
    lane = lax.broadcasted_iota(jnp.int32, ext.shape, 1)
    mine = (lane == e) | (lane == e + N_EXPERTS) | (lane == e + 2 * N_EXPERTS)
    gate = jnp.sum(jnp.where(mine, ext, 0.0), axis=1, keepdims=True)
    hg = jnp.dot(xs, wg_ref[...], preferred_element_type=F32)
    hu = jnp.dot(xs, wu_ref[...], preferred_element_type=F32)
    hmid = (hg * jax.nn.sigmoid(hg) * hu).astype(BF16)
    o_ref[...] = jnp.dot(hmid, wd_ref[...], preferred_element_type=F32) * gate


def _experts(u2ext, posm, b_first, b_cnt, wg, wu, wd, cap):
    T, DX = u2ext.shape
    E, D = N_EXPERTS, D_MODEL
    wspec = pl.BlockSpec((None, D, D), lambda e, j, bf, bc: (e, 0, 0))
    return pl.pallas_call(
        _expert_kernel,
        grid_spec=pltpu.PrefetchScalarGridSpec(
            num_scalar_prefetch=2,
            grid=(E, cap // SLOT_TILE),
            in_specs=[pl.BlockSpec(memory_space=pl.ANY),
                      pl.BlockSpec((None, 1, T), lambda e, j, bf, bc: (e, 0, 0)),
                      wspec, wspec, wspec],
            out_specs=pl.BlockSpec((None, SLOT_TILE, D), lambda e, j, bf, bc: (e, j, 0)),
            scratch_shapes=[pltpu.VMEM((SLOT_TILE, DX), F32), pltpu.VMEM((GATHER_BUFS, GATHER_BLK, DX), BF16),
                            pltpu.SemaphoreType.DMA((GATHER_BUFS, 2))]),
        out_shape=jax.ShapeDtypeStruct((E, cap, D), F32),
        compiler_params=_params(("arbitrary", "arbitrary")),
        name="experts_gather_mlp",
    )(b_first.reshape(-1), b_cnt.reshape(-1), u2ext, posm.reshape(E, 1, T), wg, wu, wd)


def _combine_kernel(bstart_ref, posm_ref, ye_ref, o_ref, buf, xbuf, sem, xsem, *, cap, nb):
    E = N_EXPERTS
    WN = COMBINE_WIN
    b = pl.program_id(0)
    top = cap - WN
    cur = b % 2

    def window(blk, slot, e):
        base = jnp.minimum((bstart_ref[e, blk] // 8) * 8, top)
        return pltpu.make_async_copy(ye_ref.at[e, pl.ds(pl.multiple_of(base, 8), WN)],
                                     buf.at[slot, pl.ds(e * WN, WN)], sem.at[slot, e])

    @pl.when(b == 0)
    def _():
        for e in range(E):
            window(0, 0, e).start()

    @pl.when(b + 1 < nb)
    def _():
        for e in range(E):
            window(b + 1, 1 - cur, e).start()

    lows = [(bstart_ref[e, b] // 8) * 8 for e in range(E)]
    w_iota = lax.broadcasted_iota(jnp.int32, (WN, ROUTE_BLK), 0)

    def onehot_rows(slot0, low, base):
        ok = (slot0 >= low) & (slot0 < low + WN)
        return jnp.where(ok & (slot0 - base == w_iota), 1.0, 0.0).astype(BF16)

    def scatter(onehot, rows):
        hi = rows.astype(BF16)
        lo = (rows - hi.astype(F32)).astype(BF16)
        tn = lambda a, c: lax.dot_general(a, c, (((0,), (0,)), ((), ())), preferred_element_type=F32)
        return tn(onehot, hi) + tn(onehot, lo)

    hots = []
    for e in range(E):
        slot0 = posm_ref[e:e + 1, :] - 1
        hots.append(onehot_rows(slot0, lows[e], jnp.minimum(lows[e], top)))
    for e in range(E):
        window(b, cur, e).wait()
    o_ref[...] = scatter(jnp.concatenate(hots, axis=0), buf[cur])

    def per_expert(e, carry):
        low0 = (bstart_ref[e, b] // 8) * 8
        nwin = (bstart_ref[e, b + 1] - low0 + WN - 1) // WN

        def per_window(c, carry):
            low = low0 + c * WN
            base = jnp.minimum(low, top)
            cp = pltpu.make_async_copy(ye_ref.at[e, pl.ds(pl.multiple_of(base, 8), WN)], xbuf, xsem.at[0])
            cp.start()
            cp.wait()
            mine = lax.broadcasted_iota(jnp.int32, (E, ROUTE_BLK), 0) == e
            slot0 = jnp.sum(jnp.where(mine, posm_ref[...], 0), axis=0, keepdims=True) - 1
            o_ref[...] += scatter(onehot_rows(slot0, low, base), xbuf[...])
            return carry

        return lax.fori_loop(1, nwin, per_window, carry)

    lax.fori_loop(0, E, per_expert, 0)


def _combine(ye, posm, bstart, cap):
    E, T = posm.shape
    D = D_MODEL
    nb = T // ROUTE_BLK
    assert cap >= COMBINE_WIN
    return pl.pallas_call(
        functools.partial(_combine_kernel, cap=cap, nb=nb),
        grid_spec=pltpu.PrefetchScalarGridSpec(
            num_scalar_prefetch=1,
            grid=(nb,),
            in_specs=[pl.BlockSpec((E, ROUTE_BLK), lambda b, bs: (0, b)),
                      pl.BlockSpec(memory_space=pl.ANY)],
            out_specs=pl.BlockSpec((ROUTE_BLK, D), lambda b, bs: (b, 0)),
            scratch_shapes=[pltpu.VMEM((2, E * COMBINE_WIN, D), F32), pltpu.VMEM((COMBINE_WIN, D), F32),
                            pltpu.SemaphoreType.DMA((2, E)), pltpu.SemaphoreType.DMA((1,))]),
        out_shape=jax.ShapeDtypeStruct((T, D), F32),
        compiler_params=_params(("arbitrary",)),
        name="experts_combine",
    )(bstart, posm, ye)


def _ple_kernel(h_ref, moe_ref, p_ref, wpg_ref, bpg_ref, wpp_ref, gfin_ref, o_ref):
    h = h_ref[...] + moe_ref[...]
    gate = jax.nn.sigmoid(jnp.dot(h.astype(BF16), wpg_ref[...], preferred_element_type=F32) + bpg_ref[...])
    h = h + gate * jnp.dot(p_ref[...].astype(BF16), wpp_ref[...], preferred_element_type=F32)
    o_ref[...] = h * lax.rsqrt(jnp.mean(h * h, axis=-1, keepdims=True) + EPS) * gfin_ref[...]


def _ple(h, moe, p, wpg, bpg, wpp, g_final, *, tm=256):
    B, L, D = h.shape
    row = lambda w: pl.BlockSpec((None, tm, w), lambda b, i: (b, i, 0))
    return pl.pallas_call(
        _ple_kernel,
        grid=(B, L // tm),
        in_specs=[row(D), row(D), row(PLE_DIM), _const_spec(wpg.shape), _const_spec((1, D)),
                  _const_spec(wpp.shape), _const_spec((1, D))],
        out_specs=row(D),
        out_shape=jax.ShapeDtypeStruct((B, L, D), F32),
        compiler_params=_params(("parallel", "parallel")),
        name="ple_final_norm",
    )(h, moe, p, wpg, bpg, wpp, g_final)


def _forward(x, p, w, btiles):
    B, L, D = x.shape
    T = B * L
    qt, k, vt, z, xbc, dt, dtt, gates = _inproj(x, w['g_mix'], w['wqt'], w['wk'], w['wvt'], w['wz'], w['wxbc'],
                                                w['wdt'], w['wdtt'], w['wgate'], w['bgate'])
    att = _attention(qt, k, vt, btiles, w['lam'], w['g_subln'])
    xs, bc, dts, dtst = _conv(xbc, dt, dtt, w['conv_w'], w['conv_b'], w['dtb'], w['dtbt'])
    ssm = _ssd(xs, bc, dts, dtst, w['alog'], w['alogt'], z, w['dskip'], w['g_ssm_norm'])
    h, u2ext, afft = _merge(att, ssm, gates, x, w['wau'], w['wsu'], w['wout'], w['g_ffn'], w['wr'])
    cap = max(1, CAPACITY_FACTOR * T // N_EXPERTS)
    posm, bstart = _route(afft, cap)
    b_first, b_cnt = _pair_schedule(bstart, cap, T // GATHER_BLK)
    ye = _experts(u2ext.reshape(T, D + GATE_EXT), posm, b_first, b_cnt, w['weg'], w['weu'], w['wed'], cap)
    moe = _combine(ye, posm, bstart, cap).reshape(B, L, D)
    return _ple(h, moe, p, w['wpg'], w['bpg'], w['wpp'], w['g_final'])


def kernel(x_prompt, x_sample, p_prompt, p_sample, rel_bias, g_mix, w_in, conv_w, conv_b, dt_bias, a_log,
           d_skip, g_ssm_norm, lam_params, g_subln, w_att_up, w_ssm_up, w_gate, b_gate, w_out, g_ffn,
           w_router, w_exp_gate, w_exp_up, w_exp_down, w_ple_gate, b_ple_gate, w_ple_proj, g_final):
    i = 0
    D = D_MODEL
    win = w_in[i]
    o1 = ATT_W
    o3 = 3 * ATT_W
    o4 = o3 + SSM_INNER
    o5 = o4 + CONV_CH
    wqt = (win[:, :o1] * (QK_DIM ** -0.5 * LOG2E)).T.astype(BF16)
    wk = win[:, o1:2 * o1].astype(BF16)
    wvt = win[:, 2 * o1:o3].T.astype(BF16)
    wdt = win[:, o5:].astype(BF16)
    lp = lam_params[i].astype(F32)
    lam = jnp.exp(jnp.sum(lp[0] * lp[1])) - jnp.exp(jnp.sum(lp[2] * lp[3])) + LAM_INIT
    ndt = 2 * SSM_HEADS
    w = {
        'g_mix': g_mix[i].reshape(1, D), 'wqt': wqt, 'wk': wk, 'wvt': wvt, 'wz': win[:, o3:o4].astype(BF16),
        'wxbc': win[:, o4:o5].astype(BF16), 'wdt': wdt, 'wdtt': wdt.T,
        'wgate': w_gate[i].astype(BF16), 'bgate': b_gate[i].reshape(1, 2 * D),
        'lam': lam.reshape(1), 'g_subln': g_subln[i].reshape(HEAD_W, 1),
        'conv_w': conv_w[i], 'conv_b': conv_b[i].reshape(1, CONV_CH),
        'dtb': dt_bias[i].reshape(1, ndt), 'dtbt': dt_bias[i].reshape(ndt, 1),
        'alog': a_log[i].reshape(1, ndt), 'alogt': a_log[i].reshape(ndt, 1),
        'dskip': jnp.repeat(d_skip[i], SSM_P).reshape(1, SSM_INNER),
        'g_ssm_norm': g_ssm_norm[i].reshape(1, SSM_INNER),
        'wau': w_att_up[i].astype(BF16), 'wsu': w_ssm_up[i].astype(BF16), 'wout': w_out[i].astype(BF16),
        'g_ffn': g_ffn[i].reshape(1, D), 'wr': w_router[i].astype(BF16),
        'weg': w_exp_gate[i].astype(BF16), 'weu': w_exp_up[i].astype(BF16), 'wed': w_exp_down[i].astype(BF16),
        'wpg': w_ple_gate[i].astype(BF16), 'bpg': b_ple_gate[i].reshape(1, D),
        'wpp': w_ple_proj[i].astype(BF16), 'g_final': g_final.reshape(1, D),
    }
    btiles = _bias_tiles(rel_bias, ATT_QK_RATIO * ATT_BLOCK, ATT_BLOCK)
    return (_forward(x_prompt, p_prompt[i], w, btiles), _forward(x_sample, p_sample[i], w, btiles))
```

```python
import functools
import math

import jax
import jax.numpy as jnp
from jax import lax
from jax.experimental import pallas as pl
from jax.experimental.pallas import tpu as pltpu

F32 = jnp.float32
BF16 = jnp.bfloat16

D_MODEL = 1024
N_HEADS = 4
QK_DIM = 64
HEAD_W = 2 * QK_DIM
ATT_W = N_HEADS * HEAD_W
REL_BUCKETS = 32
REL_MAX_DIST = 128
SSM_INNER = 1024
SSM_P = 64
SSM_HEADS = 16
SSM_GROUPS = 2
SSM_N = 64
SSM_CHUNK = 128
CONV_CH = SSM_INNER + 2 * SSM_GROUPS * SSM_N
N_EXPERTS = 16
CAPACITY_FACTOR = 2
PLE_DIM = 256
EPS = 1e-6
LAM_INIT = 0.8 - 0.6 * math.exp(-0.3 * 0)
LOG2E = math.log2(math.e)

GATE_EXT = 128
ROUTE_BLK = 256
GATHER_BLK = 512
GATHER_BUFS = 5
SLOT_TILE = 256
COMBINE_WIN = 72
BSTART_W = 256

ATT_BLOCK = 256
ATT_QK_RATIO = 4
ATT_BIAS_TILES = ATT_QK_RATIO + 4
ONES_ROWS = 16
VMEM_LIMIT = 56 * 1024 * 1024


def _params(sem, vmem=VMEM_LIMIT):
    return pltpu.CompilerParams(dimension_semantics=sem, vmem_limit_bytes=vmem)


def _const_spec(shape):
    nd = len(shape)
    return pl.BlockSpec(shape, lambda *_: (0,) * nd)


def _nt_dot(a, b, **kw):
    return lax.dot_general(a, b, (((1,), (1,)), ((), ())), preferred_element_type=F32, **kw)


def _inproj_kernel(x_ref, g_ref, wqt_ref, wk_ref, wvt_ref, wz_ref, wxbc_ref, wdt_ref, wdtt_ref, wgate_ref,
                   bgate_ref, qt_ref, k_ref, vt_ref, z_ref, xbc_ref, dt_ref, dtt_ref, gates_ref):
    x = x_ref[...]
    u = x * lax.rsqrt(jnp.mean(x * x, axis=-1, keepdims=True) + EPS) * g_ref[...]
    ub = u.astype(BF16)
    qt_ref[...] = _nt_dot(wqt_ref[...], ub).astype(BF16)
    k_ref[...] = jnp.dot(ub, wk_ref[...], preferred_element_type=F32).astype(BF16)
    vt_ref[...] = _nt_dot(wvt_ref[...], ub).astype(BF16)
    z_ref[...] = jnp.dot(ub, wz_ref[...], preferred_element_type=F32).astype(z_ref.dtype)
    xbc_ref[...] = jnp.dot(ub, wxbc_ref[...], preferred_element_type=F32)
    dt_ref[...] = jnp.dot(ub, wdt_ref[...], preferred_element_type=F32)
    dtt_ref[...] = _nt_dot(wdtt_ref[...], ub)
    gates = jax.nn.sigmoid(jnp.dot(ub, wgate_ref[...], preferred_element_type=F32) + bgate_ref[...])
    gates_ref[...] = gates.astype(gates_ref.dtype)


def _inproj(x, g_mix, wqt, wk, wvt, wz, wxbc, wdt, wdtt, wgate, bgate, *, tm=256):
    B, L, D = x.shape
    grid = (B, L // tm)
    row = lambda w: pl.BlockSpec((None, tm, w), lambda b, i: (b, i, 0))
    col = lambda w: pl.BlockSpec((None, w, tm), lambda b, i: (b, 0, i))
    ndt = 2 * SSM_HEADS
    return pl.pallas_call(
        _inproj_kernel,
        grid=grid,
        in_specs=[row(D), _const_spec((1, D)), _const_spec(wqt.shape), _const_spec(wk.shape),
                  _const_spec(wvt.shape), _const_spec(wz.shape),
                  _const_spec(wxbc.shape), _const_spec(wdt.shape), _const_spec(wdtt.shape),
                  _const_spec(wgate.shape), _const_spec((1, 2 * D))],
        out_specs=[col(ATT_W), row(ATT_W), col(ATT_W), row(SSM_INNER), row(CONV_CH), row(ndt),
                   col(ndt), row(2 * D)],
        out_shape=[jax.ShapeDtypeStruct((B, ATT_W, L), BF16),
                   jax.ShapeDtypeStruct((B, L, ATT_W), BF16),
                   jax.ShapeDtypeStruct((B, ATT_W, L), BF16),
                   jax.ShapeDtypeStruct((B, L, SSM_INNER), BF16),
                   jax.ShapeDtypeStruct((B, L, CONV_CH), F32),
                   jax.ShapeDtypeStruct((B, L, ndt), F32),
                   jax.ShapeDtypeStruct((B, ndt, L), F32),
                   jax.ShapeDtypeStruct((B, L, 2 * D), BF16)],
        compiler_params=_params(("parallel", "parallel")),
        name="inproj",
    )(x, g_mix, wqt, wk, wvt, wz, wxbc, wdt, wdtt, wgate, bgate)


def _rel_bucket(rel):
    half = REL_BUCKETS // 2
    max_exact = half // 2
    ret = jnp.where(rel > 0, half, 0)
    n = jnp.abs(rel)
    nf = jnp.maximum(n, 1).astype(jnp.float32)
    large = max_exact + (jnp.log(nf / max_exact) / math.log(REL_MAX_DIST / max_exact)
                         * (half - max_exact)).astype(jnp.int32)
    large = jnp.minimum(large, half - 1)
    return ret + jnp.where(n < max_exact, n, large)


def _bias_kernel(table_ref, bucket_ref, out_ref, *, TQ):
    h = pl.program_id(0)
    out_ref[...] = jnp.zeros(out_ref.shape, F32)
    for d in range(ATT_BIAS_TILES):
        for m in range(2):
            cols = pl.ds(m * TQ, TQ)

            def body(b, carry, d=d, m=m, cols=cols):
                val = table_ref[b, 2 * h + m] * LOG2E
                out_ref[d, :, cols] = jnp.where(bucket_ref[d] == b, val, out_ref[d, :, cols])
                return carry

            lax.fori_loop(0, REL_BUCKETS, body, 0)


def _bias_tiles(rel_bias, TQ, TK):
    assert TK > REL_MAX_DIST
    base = jnp.arange(TK, dtype=jnp.int32)[:, None] - jnp.arange(TQ, dtype=jnp.int32)[None, :]
    bucket = jnp.stack([_rel_bucket(base + (d - 2) * TK) for d in range(ATT_BIAS_TILES)])
    return pl.pallas_call(
        functools.partial(_bias_kernel, TQ=TQ),
        grid=(N_HEADS,),
        in_specs=[pl.BlockSpec(memory_space=pltpu.SMEM), _const_spec((ATT_BIAS_TILES, TK, TQ))],
        out_specs=pl.BlockSpec((None, ATT_BIAS_TILES, TK, 2 * TQ), lambda h: (h, 0, 0, 0)),
        out_shape=jax.ShapeDtypeStruct((N_HEADS, ATT_BIAS_TILES, TK, 2 * TQ), F32),
        compiler_params=_params(("parallel",)),
        name="rel_bias_tiles",
    )(rel_bias.astype(F32), bucket)


def _attn_kernel(lam_ref, qt_ref, k_ref, vt_ref, bt_ref, g_ref, o_ref, acc_sc, p_sc, *, TQ, T, nk):
    i = pl.program_id(2)
    qt = qt_ref[...]
    sub = lax.broadcasted_iota(jnp.int32, qt.shape, 0)
    zero = jnp.zeros_like(qt)
    q2t = jnp.concatenate([jnp.where(sub < QK_DIM, qt, zero), jnp.where(sub >= QK_DIM, qt, zero)], axis=1)
    acc_sc[...] = jnp.zeros(acc_sc.shape, F32)

    def scores_softmax(j, slot, m):
        kb = k_ref[pl.ds(pl.multiple_of(j * T, T), T), :]
        tile = jnp.clip(j - ATT_QK_RATIO * i, -2, ATT_QK_RATIO + 1) + 2
        s = jnp.dot(kb, q2t, preferred_element_type=F32) + bt_ref[tile]
        m_new = jnp.maximum(m, jnp.max(s, axis=0, keepdims=True))
        p_sc[slot] = jnp.exp2(s - m_new).astype(BF16)
        return m_new, jnp.exp2(m - m_new)

    ones = jnp.ones((ONES_ROWS, T), BF16)

    def accumulate(j, slot, alpha):
        vb = jnp.concatenate([vt_ref[:, pl.ds(pl.multiple_of(j * T, T), T)], ones], axis=0)
        acc_sc[...] = alpha * acc_sc[...] + jnp.dot(vb, p_sc[slot], preferred_element_type=F32)

    m = jnp.full((1, 2 * TQ), -jnp.inf, F32)
    m, alpha = scores_softmax(0, 0, m)

    def pair(u, carry):
        m, alpha = carry
        t = 2 * u
        m, alpha1 = scores_softmax(t - 1, 1, m)
        accumulate(t - 2, 0, alpha)
        m, alpha0 = scores_softmax(t, 0, m)
        accumulate(t - 1, 1, alpha1)
        return m, alpha0

    def quad(u, carry):
        return pair(2 * u + 1, pair(2 * u, carry))

    m, alpha = lax.fori_loop(1, nk // 4, quad, pair(1, (m, alpha)))
    m, alpha1 = scores_softmax(nk - 1, 1, m)
    accumulate(nk - 2, 0, alpha)
    accumulate(nk - 1, 1, alpha1)

    o = acc_sc[0:HEAD_W, :] / acc_sc[HEAD_W:HEAD_W + 1, :]
    od = o[:, :TQ] - lam_ref[0] * o[:, TQ:]
    y = od * lax.rsqrt(jnp.mean(od * od, axis=0, keepdims=True) + EPS) * g_ref[...]
    o_ref[...] = (y * (1.0 - LAM_INIT)).T.astype(o_ref.dtype)


def _attention(qt, k, vt, btiles, lam, g_col, *, T=ATT_BLOCK):
    B, L, _ = k.shape
    TQ = ATT_QK_RATIO * T
    nk = L // T
    assert nk % 4 == 0 and L % TQ == 0
    return pl.pallas_call(
        functools.partial(_attn_kernel, TQ=TQ, T=T, nk=nk),
        grid=(B, N_HEADS, L // TQ),
        in_specs=[pl.BlockSpec(memory_space=pltpu.SMEM),
                  pl.BlockSpec((None, HEAD_W, TQ), lambda b, h, i: (b, h, i)),
                  pl.BlockSpec((None, L, HEAD_W), lambda b, h, i: (b, 0, h)),
                  pl.BlockSpec((None, HEAD_W, L), lambda b, h, i: (b, h, 0)),
                  pl.BlockSpec((None, ATT_BIAS_TILES, T, 2 * TQ), lambda b, h, i: (h, 0, 0, 0)),
                  _const_spec((HEAD_W, 1))],
        out_specs=pl.BlockSpec((None, TQ, HEAD_W), lambda b, h, i: (b, i, h)),
        out_shape=jax.ShapeDtypeStruct((B, L, ATT_W), BF16),
        scratch_shapes=[pltpu.VMEM((HEAD_W + ONES_ROWS, 2 * TQ), F32), pltpu.VMEM((2, T, 2 * TQ), BF16)],
        compiler_params=_params(("parallel", "parallel", "parallel")),
        name="diff_attention",
    )(lam, qt, k, vt, btiles, g_col)


def _conv_kernel(x_ref, prev_ref, next_ref, dt_ref, dtt_ref, w_ref, b_ref, dtb_ref, dtbt_ref,
                 xs_ref, bc_ref, dts_ref, dtst_ref, *, tl, nt):
    i = pl.program_id(1)
    x = x_ref[...]
    row = lax.broadcasted_iota(jnp.int32, x.shape, 0)
    prev_row = jnp.where(i > 0, prev_ref[7:8, :], 0.0)
    next_row = jnp.where(i < nt - 1, next_ref[0:1, :], 0.0)
    xm1 = jnp.where(row == 0, prev_row, pltpu.roll(x, 1, 0))
    xp1 = jnp.where(row == tl - 1, next_row, pltpu.roll(x, tl - 1, 0))
    y = w_ref[0:1, :] * xm1 + w_ref[1:2, :] * x + w_ref[2:3, :] * xp1 + b_ref[...]
    y = y * jax.nn.sigmoid(y)
    xs_ref[...] = y[:, :SSM_INNER].astype(BF16)
    bc_ref[...] = y[:, SSM_INNER:].astype(BF16)

    def softplus(v):
        return jnp.maximum(v, 0.0) + jnp.log1p(jnp.exp(-jnp.abs(v)))

    dts_ref[...] = softplus(dt_ref[...] + dtb_ref[...])
    dtst_ref[...] = softplus(dtt_ref[...] + dtbt_ref[...])


def _conv(xbc, dt, dtt, conv_w, conv_b, dtb, dtbt, *, tl=512):
    B, L, _ = xbc.shape
    nt = L // tl
    r8 = tl // 8
    ndt = 2 * SSM_HEADS
    row = lambda w: pl.BlockSpec((None, tl, w), lambda b, i: (b, i, 0))
    colspec = pl.BlockSpec((None, ndt, tl), lambda b, i: (b, 0, i))
    return pl.pallas_call(
        functools.partial(_conv_kernel, tl=tl, nt=nt),
        grid=(B, nt),
        in_specs=[row(CONV_CH),
                  pl.BlockSpec((None, 8, CONV_CH), lambda b, i: (b, jnp.maximum(i * r8 - 1, 0), 0)),
                  pl.BlockSpec((None, 8, CONV_CH), lambda b, i: (b, jnp.minimum((i + 1) * r8, L // 8 - 1), 0)),
                  row(ndt), colspec,
                  _const_spec((3, CONV_CH)), _const_spec((1, CONV_CH)), _const_spec((1, ndt)),
                  _const_spec((ndt, 1))],
        out_specs=[row(SSM_INNER), row(2 * SSM_GROUPS * SSM_N), row(ndt), colspec],
        out_shape=[jax.ShapeDtypeStruct((B, L, SSM_INNER), BF16),
                   jax.ShapeDtypeStruct((B, L, 2 * SSM_GROUPS * SSM_N), BF16),
                   jax.ShapeDtypeStruct((B, L, ndt), F32),
                   jax.ShapeDtypeStruct((B, ndt, L), F32)],
        compiler_params=_params(("parallel", "parallel")),
        name="ssm_conv",
    )(xbc, xbc, xbc, dt, dtt, conv_w, conv_b, dtb, dtbt)


def _ssd_chunk(xs_ref, bc_ref, dts_ref, dtst_ref, alog_ref, alogt_ref, st_ref, *, rev):
    Q = SSM_CHUNK
    HI = lax.Precision.HIGHEST
    off = SSM_HEADS if rev else 0
    ri = lax.broadcasted_iota(jnp.int32, (Q, Q), 0)
    ci = lax.broadcasted_iota(jnp.int32, (Q, Q), 1)
    incl = (ci >= ri) if rev else (ci <= ri)
    tri = incl.astype(F32)
    dt = dts_ref[:, off:off + SSM_HEADS]
    dtt = dtst_ref[off:off + SSM_HEADS, :]
    a = dt * (-jnp.exp(alog_ref[:, off:off + SSM_HEADS]))
    at = dtt * (-jnp.exp(alogt_ref[off:off + SSM_HEADS, :]))
    acum = jnp.dot(tri, a, preferred_element_type=F32, precision=HI)
    acumt = _nt_dot(at, tri, precision=HI)
    edge = 0 if rev else Q - 1
    bc = bc_ref[...]
    bpair = bc[:, :Q]
    cpair = bc[:, Q:]
    low = ci < SSM_N
    rlow = ri < SSM_N
    blockmask = low == rlow
    total = acumt[:, edge:edge + 1]
    decay = jnp.exp(total)
    w_rows = dtt * jnp.exp(total - acumt)
    ys = []
    for g in range(SSM_GROUPS):
        gmask = low if g == 0 else jnp.logical_not(low)
        cg = jnp.where(gmask, cpair, jnp.zeros_like(cpair))
        cb = _nt_dot(cg, bpair)
        dupm = ((ri - g * SSM_N) == (ci % SSM_N)).astype(BF16)
        cdup = jnp.dot(cpair, dupm, preferred_element_type=F32)
        sel2 = ((ci - g * SSM_N) == (ri % SSM_N)).astype(BF16)
        bt2 = _nt_dot(sel2, bpair)
        for hp in range(SSM_HEADS // SSM_GROUPS // 2):
            h0 = g * (SSM_HEADS // SSM_GROUPS) + 2 * hp
            pair = h0 // 2
            xpair = xs_ref[:, h0 * SSM_P:(h0 + 2) * SSM_P]
            ms = []
            e_cols = []
            for h in (h0, h0 + 1):
                e_col = jnp.broadcast_to(acum[:, h:h + 1], (Q, Q))
                r_row = jnp.broadcast_to(acumt[h:h + 1, :], (Q, Q))
                seg = jnp.where(incl, e_col - r_row, -jnp.inf)
                ms.append((cb * jnp.exp(seg) * jnp.broadcast_to(dtt[h:h + 1, :], (Q, Q))).astype(BF16))
                e_cols.append(e_col)
            st = st_ref[pair]
            xblk = jnp.concatenate([jnp.where(low, xpair, jnp.zeros_like(xpair)),
                                    jnp.where(low, jnp.zeros_like(xpair), xpair)], axis=0)
            y = jnp.dot(jnp.concatenate(ms, axis=1), xblk, preferred_element_type=F32)
            ce = (cdup * jnp.exp(jnp.where(low, e_cols[0], e_cols[1]))).astype(BF16)
            y = y + jnp.dot(ce, st.astype(BF16), preferred_element_type=F32)
            ys.append(y)
            w_pair = jnp.where(rlow, jnp.broadcast_to(w_rows[h0:h0 + 1, :], (Q, Q)),
                               jnp.broadcast_to(w_rows[h0 + 1:h0 + 2, :], (Q, Q)))
            snew = jnp.dot((bt2 * w_pair).astype(BF16), xpair, preferred_element_type=F32)
            dec = jnp.where(rlow, decay[h0:h0 + 1, :], decay[h0 + 1:h0 + 2, :])
            st_ref[pair] = dec * st + jnp.where(blockmask, snew, 0.0)
    return jnp.concatenate(ys, axis=1)


def _ssd_fwd_kernel(xs_ref, bc_ref, dts_ref, dtst_ref, alog_ref, alogt_ref, y_ref, st_ref):
    @pl.when(pl.program_id(1) == 0)
    def _():
        st_ref[...] = jnp.zeros(st_ref.shape, F32)

    y = _ssd_chunk(xs_ref, bc_ref, dts_ref, dtst_ref, alog_ref, alogt_ref, st_ref, rev=False)
    y_ref[...] = y.astype(y_ref.dtype)


def _ssd_bwd_kernel(xs_ref, bc_ref, dts_ref, dtst_ref, alog_ref, alogt_ref, yf_ref, z_ref, dskip_ref, g_ref,
                    o_ref, st_ref):
    @pl.when(pl.program_id(1) == 0)
    def _():
        st_ref[...] = jnp.zeros(st_ref.shape, F32)

    yb = _ssd_chunk(xs_ref, bc_ref, dts_ref, dtst_ref, alog_ref, alogt_ref, st_ref, rev=True)
    z = z_ref[...].astype(F32)
    y = (yf_ref[...].astype(F32) + yb + xs_ref[...].astype(F32) * dskip_ref[...]) * (z * jax.nn.sigmoid(z))
    y = y * lax.rsqrt(jnp.mean(y * y, axis=-1, keepdims=True) + EPS) * g_ref[...]
    o_ref[...] = y.astype(o_ref.dtype)


def _ssd(xs, bc, dts, dtst, alog, alogt, z, dskip, g_norm):
    B, L, _ = xs.shape
    Q = SSM_CHUNK
    nc = L // Q
    ndt = 2 * SSM_HEADS
    npair = SSM_HEADS // 2

    def specs(cidx):
        row = lambda w: pl.BlockSpec((None, Q, w), lambda b, c: (b, cidx(c), 0))
        return row, [row(SSM_INNER), row(2 * SSM_GROUPS * SSM_N), row(ndt),
                     pl.BlockSpec((None, ndt, Q), lambda b, c: (b, 0, cidx(c))),
                     _const_spec((1, ndt)), _const_spec((ndt, 1))]

    scratch = [pltpu.VMEM((npair, 2 * SSM_N, 2 * SSM_P), F32)]
    row, in_specs = specs(lambda c: c)
    yf = pl.pallas_call(
        _ssd_fwd_kernel,
        grid=(B, nc),
        in_specs=in_specs,
        out_specs=row(SSM_INNER),
        out_shape=jax.ShapeDtypeStruct((B, L, SSM_INNER), BF16),
        scratch_shapes=scratch,
        compiler_params=_params(("parallel", "arbitrary")),
        name="ssd_forward",
    )(xs, bc, dts, dtst, alog, alogt)
    row, in_specs = specs(lambda c: nc - 1 - c)
    return pl.pallas_call(
        _ssd_bwd_kernel,
        grid=(B, nc),
        in_specs=in_specs + [row(SSM_INNER), row(SSM_INNER), _const_spec((1, SSM_INNER)),
                             _const_spec((1, SSM_INNER))],
        out_specs=row(SSM_INNER),
        out_shape=jax.ShapeDtypeStruct((B, L, SSM_INNER), BF16),
        scratch_shapes=scratch,
        compiler_params=_params(("parallel", "arbitrary")),
        name="ssd_reverse_gate_norm",
    )(xs, bc, dts, dtst, alog, alogt, yf, z, dskip, g_norm)


def _merge_kernel(att_ref, ssm_ref, gates_ref, x_ref, wau_ref, wsu_ref, wout_ref, gffn_ref, wr_ref, wrt_ref,
                  place_ref, h_ref, u2_ref, afft_ref):
    D = D_MODEL
    E = N_EXPERTS
    ya = jnp.dot(att_ref[...], wau_ref[...], preferred_element_type=F32)
    ysm = jnp.dot(ssm_ref[...], wsu_ref[...], preferred_element_type=F32)
    merged = gates_ref[:, :D] * ya + gates_ref[:, D:] * ysm
    h = x_ref[...] + jnp.dot(merged.astype(BF16), wout_ref[...], preferred_element_type=F32)
    h_ref[...] = h
    u2 = (h * lax.rsqrt(jnp.mean(h * h, axis=-1, keepdims=True) + EPS) * gffn_ref[...]).astype(BF16)
    u2_ref[:, :D] = u2
    logits = jnp.dot(u2, wr_ref[...], preferred_element_type=F32)
    e = jnp.exp(logits - jnp.max(logits, axis=-1, keepdims=True))
    aff = e / jnp.sum(e, axis=-1, keepdims=True)
    hi = aff.astype(BF16)
    r1 = aff - hi.astype(F32)
    mid = r1.astype(BF16)
    lo = (r1 - mid.astype(F32)).astype(BF16)
    ext = (jnp.dot(hi, place_ref[0:E, :], preferred_element_type=F32)
           + jnp.dot(mid, place_ref[E:2 * E, :], preferred_element_type=F32)
           + jnp.dot(lo, place_ref[2 * E:3 * E, :], preferred_element_type=F32))
    u2_ref[:, D:] = ext.astype(BF16)
    lt = _nt_dot(wrt_ref[...], u2)
    et = jnp.exp(lt - jnp.max(lt, axis=0, keepdims=True))
    afft_ref[...] = et / jnp.sum(et, axis=0, keepdims=True)


def _merge(att, ssm, gates, x, wau, wsu, wout, g_ffn, wr, *, tm=256):
    B, L, D = x.shape
    E = N_EXPERTS
    nl = L // tm
    row = lambda w: pl.BlockSpec((None, tm, w), lambda b, i: (b, i, 0))
    place = jnp.eye(3 * E, GATE_EXT, dtype=BF16)
    return pl.pallas_call(
        _merge_kernel,
        grid=(B, nl),
        in_specs=[row(ATT_W), row(SSM_INNER), row(2 * D), row(D), _const_spec(wau.shape),
                  _const_spec(wsu.shape), _const_spec(wout.shape), _const_spec((1, D)), _const_spec(wr.shape),
                  _const_spec((E, D)), _const_spec((3 * E, GATE_EXT))],
        out_specs=[row(D), row(D + GATE_EXT), pl.BlockSpec((E, tm), lambda b, i: (0, b * nl + i))],
        out_shape=[jax.ShapeDtypeStruct((B, L, D), F32), jax.ShapeDtypeStruct((B, L, D + GATE_EXT), BF16),
                   jax.ShapeDtypeStruct((E, B * L), F32)],
        compiler_params=_params(("parallel", "parallel")),
        name="merge_router",
    )(att, ssm, gates, x, wau, wsu, wout, g_ffn, wr, wr.T, place)


def _route_kernel(afft_ref, posm_ref, bstart_ref, *, cap, nb):
    E = N_EXPERTS
    W = ROUTE_BLK

    def bit_body(k, thr):
        cand = thr | lax.shift_left(jnp.int32(1), 30 - k)
        x = lax.bitcast_convert_type(afft_ref[...], jnp.int32)
        cnt = jnp.sum(jnp.where(x >= cand, 1.0, 0.0), axis=1, keepdims=True)
        return jnp.where(cnt >= cap, cand, thr)

    thr = lax.fori_loop(0, 31, bit_body, jnp.zeros((E, 1), jnp.int32))
    x = lax.bitcast_convert_type(afft_ref[...], jnp.int32)
    n_gt = jnp.sum(jnp.where(x > thr, 1.0, 0.0), axis=1, keepdims=True)
    need = cap - n_gt
    ri = lax.broadcasted_iota(jnp.int32, (W, W), 0)
    ci = lax.broadcasted_iota(jnp.int32, (W, W), 1)
    upper = jnp.where(ri <= ci, 1.0, 0.0).astype(BF16)
    bstart_ref[...] = jnp.zeros(bstart_ref.shape, jnp.int32)
    c_eq = jnp.zeros((E, 1), F32)
    c_sel = jnp.zeros((E, 1), F32)
    for b in range(nb):
        xb = lax.bitcast_convert_type(afft_ref[:, b * W:(b + 1) * W], jnp.int32)
        eq = xb == thr
        eqc = jnp.dot(jnp.where(eq, 1.0, 0.0).astype(BF16), upper, preferred_element_type=F32) + c_eq
        sel = (xb > thr) | (eq & (eqc <= need))
        selc = jnp.dot(jnp.where(sel, 1.0, 0.0).astype(BF16), upper, preferred_element_type=F32) + c_sel
        posm_ref[:, b * W:(b + 1) * W] = jnp.where(sel, selc, 0.0).astype(jnp.int32)
        bstart_ref[:, b:b + 1] = c_sel.astype(jnp.int32)
        c_eq = eqc[:, W - 1:W]
        c_sel = selc[:, W - 1:W]
    bstart_ref[:, nb:nb + 1] = c_sel.astype(jnp.int32)


def _route(afft, cap):
    E, T = afft.shape
    nb = T // ROUTE_BLK
    assert nb < BSTART_W and cap % SLOT_TILE == 0
    return pl.pallas_call(
        functools.partial(_route_kernel, cap=float(cap), nb=nb),
        out_shape=[jax.ShapeDtypeStruct((E, T), jnp.int32), jax.ShapeDtypeStruct((E, BSTART_W), jnp.int32)],
        compiler_params=_params(None),
        name="route_select",
    )(afft)


def _pair_schedule(bstart, cap, ngb):
    step = GATHER_BLK // ROUTE_BLK
    st = bstart[:, 0:ngb * step + 1:step]
    lo, hi = st[:, :-1], st[:, 1:]
    ntile = cap // SLOT_TILE
    t_lo = jnp.arange(ntile, dtype=jnp.int32) * SLOT_TILE
    t_hi = t_lo + SLOT_TILE
    b_first = jnp.sum(hi[:, None, :] <= t_lo[None, :, None], axis=2).astype(jnp.int32)
    b_last = jnp.sum(lo[:, None, :] < t_hi[None, :, None], axis=2).astype(jnp.int32) - 1
    return b_first, b_last - b_first + 1


def _expert_kernel(first_ref, cnt_ref, u_hbm, posm_ref, wg_ref, wu_ref, wd_ref, o_ref, x_sc, ubuf, sem):
    D = D_MODEL
    e = pl.program_id(0)
    j = pl.program_id(1)
    g = e * pl.num_programs(1) + j
    b0 = first_ref[g]
    n = cnt_ref[g]

    half = GATHER_BLK // 2

    def half_copies(b, slot):
        return [pltpu.make_async_copy(
            u_hbm.at[pl.ds(pl.multiple_of(b * GATHER_BLK + r * half, half), half)],
            ubuf.at[slot, pl.ds(r * half, half)], sem.at[slot, r]) for r in range(2)]

    def start(b, slot):
        for cp in half_copies(b, slot):
            cp.start()

    ahead = GATHER_BUFS - 1

    def start_tile(t):
        start(first_ref[t], 0)
        for r in range(1, ahead):
            @pl.when(cnt_ref[t] > r)
            def _(r=r):
                start(first_ref[t] + r, r)

    @pl.when(g == 0)
    def _():
        start_tile(0)

    x_sc[...] = jnp.zeros(x_sc.shape, F32)
    slot_ids = lax.broadcasted_iota(jnp.int32, (SLOT_TILE, GATHER_BLK), 0) + (j * SLOT_TILE + 1)

    def body(c, carry):
        slot = c % GATHER_BUFS
        for cp in half_copies(b0 + c, slot):
            cp.wait()

        @pl.when(c + ahead < n)
        def _():
            start(b0 + c + ahead, (c + ahead) % GATHER_BUFS)

        posrow = posm_ref[:, pl.ds(pl.multiple_of((b0 + c) * GATHER_BLK, GATHER_BLK), GATHER_BLK)]
        onehot = jnp.where(posrow == slot_ids, 1.0, 0.0).astype(BF16)
        x_sc[...] += jnp.dot(onehot, ubuf[slot], preferred_element_type=F32)
        return carry

    lax.fori_loop(0, n, body, 0)

    @pl.when(g + 1 < pl.num_programs(0) * pl.num_programs(1))
    def _():
        start_tile(g + 1)

    xs = x_sc[:, :D].astype(BF16)
    ext = x_sc[:, D:]
    lane = lax.broadcasted_iota(jnp.int32, ext.shape, 1)
    mine = (lane == e) | (lane == e + N_EXPERTS) | (lane == e + 2 * N_EXPERTS)
    gate = jnp.sum(jnp.where(mine, ext, 0.0), axis=1, keepdims=True)
    hg = jnp.dot(xs, wg_ref[...], preferred_element_type=F32)
    hu = jnp.dot(xs, wu_ref[...], preferred_element_type=F32)
    hmid = (hg * jax.nn.sigmoid(hg) * hu).astype(BF16)
    o_ref[...] = jnp.dot(hmid, wd_ref[...], preferred_element_type=F32) * gate


def _experts(u2ext, posm, b_first, b_cnt, wg, wu, wd, cap):
    T, DX = u2ext.shape
    E, D = N_EXPERTS, D_MODEL
    wspec = pl.BlockSpec((None, D, D), lambda e, j, bf, bc: (e, 0, 0))
    return pl.pallas_call(
        _expert_kernel,
        grid_spec=pltpu.PrefetchScalarGridSpec(
            num_scalar_prefetch=2,
            grid=(E, cap // SLOT_TILE),
            in_specs=[pl.BlockSpec(memory_space=pl.ANY),
                      pl.BlockSpec((None, 1, T), lambda e, j, bf, bc: (e, 0, 0)),
                      wspec, wspec, wspec],
            out_specs=pl.BlockSpec((None, SLOT_TILE, D), lambda e, j, bf, bc: (e, j, 0)),
            scratch_shapes=[pltpu.VMEM((SLOT_TILE, DX), F32), pltpu.VMEM((GATHER_BUFS, GATHER_BLK, DX), BF16),
                            pltpu.SemaphoreType.DMA((GATHER_BUFS, 2))]),
        out_shape=jax.ShapeDtypeStruct((E, cap, D), F32),
        compiler_params=_params(("arbitrary", "arbitrary")),
        name="experts_gather_mlp",
    )(b_first.reshape(-1), b_cnt.reshape(-1), u2ext, posm.reshape(E, 1, T), wg, wu, wd)


def _combine_kernel(bstart_ref, posm_ref, ye_ref, o_ref, buf, xbuf, sem, xsem, *, cap, nb):
    E = N_EXPERTS
    WN = COMBINE_WIN
    b = pl.program_id(0)
    top = cap - WN
    cur = b % 2

    def window(blk, slot, e):
        base = jnp.minimum((bstart_ref[e, blk] // 8) * 8, top)
        return pltpu.make_async_copy(ye_ref.at[e, pl.ds(pl.multiple_of(base, 8), WN)],
                                     buf.at[slot, pl.ds(e * WN, WN)], sem.at[slot, e])

    @pl.when(b == 0)
    def _():
        for e in range(E):
            window(0, 0, e).start()

    @pl.when(b + 1 < nb)
    def _():
        for e in range(E):
            window(b + 1, 1 - cur, e).start()

    lows = [(bstart_ref[e, b] // 8) * 8 for e in range(E)]
    w_iota = lax.broadcasted_iota(jnp.int32, (WN, ROUTE_BLK), 0)

    def onehot_rows(slot0, low, base):
        ok = (slot0 >= low) & (slot0 < low + WN)
        return jnp.where(ok & (slot0 - base == w_iota), 1.0, 0.0).astype(BF16)

    def scatter(onehot, rows):
        hi = rows.astype(BF16)
        lo = (rows - hi.astype(F32)).astype(BF16)
        tn = lambda a, c: lax.dot_general(a, c, (((0,), (0,)), ((), ())), preferred_element_type=F32)
        return tn(onehot, hi) + tn(onehot, lo)

    hots = []
    for e in range(E):
        slot0 = posm_ref[e:e + 1, :] - 1
        hots.append(onehot_rows(slot0, lows[e], jnp.minimum(lows[e], top)))
    for e in range(E):
        window(b, cur, e).wait()
    o_ref[...] = scatter(jnp.concatenate(hots, axis=0), buf[cur])

    def per_expert(e, carry):
        low0 = (bstart_ref[e, b] // 8) * 8
        nwin = (bstart_ref[e, b + 1] - low0 + WN - 1) // WN

        def per_window(c, carry):
            low = low0 + c * WN
            base = jnp.minimum(low, top)
            cp = pltpu.make_async_copy(ye_ref.at[e, pl.ds(pl.multiple_of(base, 8), WN)], xbuf, xsem.at[0])
            cp.start()
            cp.wait()
            mine = lax.broadcasted_iota(jnp.int32, (E, ROUTE_BLK), 0) == e
            slot0 = jnp.sum(jnp.where(mine, posm_ref[...], 0), axis=0, keepdims=True) - 1
            o_ref[...] += scatter(onehot_rows(slot0, low, base), xbuf[...])
            return carry

        return lax.fori_loop(1, nwin, per_window, carry)

    lax.fori_loop(0, E, per_expert, 0)


def _combine(ye, posm, bstart, cap):
    E, T = posm.shape
    D = D_MODEL
    nb = T // ROUTE_BLK
    assert cap >= COMBINE_WIN
    return pl.pallas_call(
        functools.partial(_combine_kernel, cap=cap, nb=nb),
        grid_spec=pltpu.PrefetchScalarGridSpec(
            num_scalar_prefetch=1,
            grid=(nb,),
            in_specs=[pl.BlockSpec((E, ROUTE_BLK), lambda b, bs: (0, b)),
                      pl.BlockSpec(memory_space=pl.ANY)],
            out_specs=pl.BlockSpec((ROUTE_BLK, D), lambda b, bs: (b, 0)),
            scratch_shapes=[pltpu.VMEM((2, E * COMBINE_WIN, D), F32), pltpu.VMEM((COMBINE_WIN, D), F32),
                            pltpu.SemaphoreType.DMA((2, E)), pltpu.SemaphoreType.DMA((1,))]),
        out_shape=jax.ShapeDtypeStruct((T, D), F32),
        compiler_params=_params(("arbitrary",)),
        name="experts_combine",
    )(bstart, posm, ye)


def _ple_kernel(h_ref, moe_ref, p_ref, wpg_ref, bpg_ref, wpp_ref, gfin_ref, o_ref):
    h = h_ref[...] + moe_ref[...]
    gate = jax.nn.sigmoid(jnp.dot(h.astype(BF16), wpg_ref[...], preferred_element_type=F32) + bpg_ref[...])
    h = h + gate * jnp.dot(p_ref[...].astype(BF16), wpp_ref[...], preferred_element_type=F32)
    o_ref[...] = h * lax.rsqrt(jnp.mean(h * h, axis=-1, keepdims=True) + EPS) * gfin_ref[...]


def _ple(h, moe, p, wpg, bpg, wpp, g_final, *, tm=256):
    B, L, D = h.shape
    row = lambda w: pl.BlockSpec((None, tm, w), lambda b, i: (b, i, 0))
    return pl.pallas_call(
        _ple_kernel,
        grid=(B, L // tm),
        in_specs=[row(D), row(D), row(PLE_DIM), _const_spec(wpg.shape), _const_spec((1, D)),
                  _const_spec(wpp.shape), _const_spec((1, D))],
        out_specs=row(D),
        out_shape=jax.ShapeDtypeStruct((B, L, D), F32),
        compiler_params=_params(("parallel", "parallel")),
        name="ple_final_norm",
    )(h, moe, p, wpg, bpg, wpp, g_final)


def _forward(x, p, w, btiles):
    B, L, D = x.shape
    T = B * L
    qt, k, vt, z, xbc, dt, dtt, gates = _inproj(x, w['g_mix'], w['wqt'], w['wk'], w['wvt'], w['wz'], w['wxbc'],
                                                w['wdt'], w['wdtt'], w['wgate'], w['bgate'])
    att = _attention(qt, k, vt, btiles, w['lam'], w['g_subln'])
    xs, bc, dts, dtst = _conv(xbc, dt, dtt, w['conv_w'], w['conv_b'], w['dtb'], w['dtbt'])
    ssm = _ssd(xs, bc, dts, dtst, w['alog'], w['alogt'], z, w['dskip'], w['g_ssm_norm'])
    h, u2ext, afft = _merge(att, ssm, gates, x, w['wau'], w['wsu'], w['wout'], w['g_ffn'], w['wr'])
    cap = max(1, CAPACITY_FACTOR * T // N_EXPERTS)
    posm, bstart = _route(afft, cap)
    b_first, b_cnt = _pair_schedule(bstart, cap, T // GATHER_BLK)
    ye = _experts(u2ext.reshape(T, D + GATE_EXT), posm, b_first, b_cnt, w['weg'], w['weu'], w['wed'], cap)
    moe = _combine(ye, posm, bstart, cap).reshape(B, L, D)
    return _ple(h, moe, p, w['wpg'], w['bpg'], w['wpp'], w['g_final'])


def kernel(x_prompt, x_sample, p_prompt, p_sample, rel_bias, g_mix, w_in, conv_w, conv_b, dt_bias, a_log,
           d_skip, g_ssm_norm, lam_params, g_subln, w_att_up, w_ssm_up, w_gate, b_gate, w_out, g_ffn,
           w_router, w_exp_gate, w_exp_up, w_exp_down, w_ple_gate, b_ple_gate, w_ple_proj, g_final):
    i = 0
    D = D_MODEL
    win = w_in[i]
    o1 = ATT_W
    o3 = 3 * ATT_W
    o4 = o3 + SSM_INNER
    o5 = o4 + CONV_CH
    wqt = (win[:, :o1] * (QK_DIM ** -0.5 * LOG2E)).T.astype(BF16)
    wk = win[:, o1:2 * o1].astype(BF16)
    wvt = win[:, 2 * o1:o3].T.astype(BF16)
    wdt = win[:, o5:].astype(BF16)
    lp = lam_params[i].astype(F32)
    lam = jnp.exp(jnp.sum(lp[0] * lp[1])) - jnp.exp(jnp.sum(lp[2] * lp[3])) + LAM_INIT
    ndt = 2 * SSM_HEADS
    w = {
        'g_mix': g_mix[i].reshape(1, D), 'wqt': wqt, 'wk': wk, 'wvt': wvt, 'wz': win[:, o3:o4].astype(BF16),
        'wxbc': win[:, o4:o5].astype(BF16), 'wdt': wdt, 'wdtt': wdt.T,
        'wgate': w_gate[i].astype(BF16), 'bgate': b_gate[i].reshape(1, 2 * D),
        'lam': lam.reshape(1), 'g_subln': g_subln[i].reshape(HEAD_W, 1),
        'conv_w': conv_w[i], 'conv_b': conv_b[i].reshape(1, CONV_CH),
        'dtb': dt_bias[i].reshape(1, ndt), 'dtbt': dt_bias[i].reshape(ndt, 1),
        'alog': a_log[i].reshape(1, ndt), 'alogt': a_log[i].reshape(ndt, 1),
        'dskip': jnp.repeat(d_skip[i], SSM_P).reshape(1, SSM_INNER),
        'g_ssm_norm': g_ssm_norm[i].reshape(1, SSM_INNER),
        'wau': w_att_up[i].astype(BF16), 'wsu': w_ssm_up[i].astype(BF16), 'wout': w_out[i].astype(BF16),
        'g_ffn': g_ffn[i].reshape(1, D), 'wr': w_router[i].astype(BF16),
        'weg': w_exp_gate[i].astype(BF16), 'weu': w_exp_up[i].astype(BF16), 'wed': w_exp_down[i].astype(BF16),
        'wpg': w_ple_gate[i].astype(BF16), 'bpg': b_ple_gate[i].reshape(1, D),
        'wpp': w_ple_proj[i].astype(BF16), 'g_final': g_final.reshape(1, D),
    }
    btiles = _bias_tiles(rel_bias, ATT_QK_RATIO * ATT_BLOCK, ATT_BLOCK)
    return (_forward(x_prompt, p_prompt[i], w, btiles), _forward(x_sample, p_sample[i], w, btiles))
```

```python
import functools
import math

import jax
import jax.numpy as jnp
from jax import lax
from jax.experimental import pallas as pl
from jax.experimental.pallas import tpu as pltpu

F32 = jnp.float32
BF16 = jnp.bfloat16

D_MODEL = 1024
N_HEADS = 4
QK_DIM = 64
HEAD_W = 2 * QK_DIM
ATT_W = N_HEADS * HEAD_W
REL_BUCKETS = 32
REL_MAX_DIST = 128
SSM_INNER = 1024
SSM_P = 64
SSM_HEADS = 16
SSM_GROUPS = 2
SSM_N = 64
SSM_CHUNK = 128
CONV_CH = SSM_INNER + 2 * SSM_GROUPS * SSM_N
N_EXPERTS = 16
CAPACITY_FACTOR = 2
PLE_DIM = 256
EPS = 1e-6
LAM_INIT = 0.8 - 0.6 * math.exp(-0.3 * 0)
LOG2E = math.log2(math.e)

GATE_EXT = 128
ROUTE_BLK = 256
GATHER_BLK = 512
GATHER_BUFS = 5
SLOT_TILE = 256
COMBINE_WIN = 72
BSTART_W = 256

ATT_BLOCK = 256
ATT_QK_RATIO = 4
ATT_BIAS_TILES = ATT_QK_RATIO + 4
ONES_ROWS = 16
BOUND_SLACK = 1.0 + 2.0 ** -6
EXP2_SPAN = 100.0
VMEM_LIMIT = 56 * 1024 * 1024


def _params(sem, vmem=VMEM_LIMIT):
    return pltpu.CompilerParams(dimension_semantics=sem, vmem_limit_bytes=vmem)


def _const_spec(shape):
    nd = len(shape)
    return pl.BlockSpec(shape, lambda *_: (0,) * nd)


def _nt_dot(a, b, **kw):
    return lax.dot_general(a, b, (((1,), (1,)), ((), ())), preferred_element_type=F32, **kw)


def _inproj_kernel(x_ref, g_ref, wqt_ref, wk_ref, wvt_ref, wz_ref, wxbc_ref, wdt_ref, wdtt_ref, wgate_ref,
                   bgate_ref, ind_ref, qt_ref, k_ref, vt_ref, z_ref, xbc_ref, dt_ref, dtt_ref, gates_ref, kn_ref):
    x = x_ref[...]
    u = x * lax.rsqrt(jnp.mean(x * x, axis=-1, keepdims=True) + EPS) * g_ref[...]
    ub = u.astype(BF16)
    qt_ref[...] = _nt_dot(wqt_ref[...], ub).astype(BF16)
    kb = jnp.dot(ub, wk_ref[...], preferred_element_type=F32).astype(BF16)
    k_ref[...] = kb
    ksq = (kb.astype(F32) ** 2).astype(BF16)
    kn_ref[...] = jnp.broadcast_to(
        jnp.max(jnp.dot(ksq, ind_ref[...], preferred_element_type=F32), axis=0, keepdims=True), kn_ref.shape)
    vt_ref[...] = _nt_dot(wvt_ref[...], ub).astype(BF16)
    z_ref[...] = jnp.dot(ub, wz_ref[...], preferred_element_type=F32).astype(z_ref.dtype)
    xbc_ref[...] = jnp.dot(ub, wxbc_ref[...], preferred_element_type=F32)
    dt_ref[...] = jnp.dot(ub, wdt_ref[...], preferred_element_type=F32)
    dtt_ref[...] = _nt_dot(wdtt_ref[...], ub)
    gates = jax.nn.sigmoid(jnp.dot(ub, wgate_ref[...], preferred_element_type=F32) + bgate_ref[...])
    gates_ref[...] = gates.astype(gates_ref.dtype)


def _inproj(x, g_mix, wqt, wk, wvt, wz, wxbc, wdt, wdtt, wgate, bgate, *, tm=256):
    B, L, D = x.shape
    grid = (B, L // tm)
    row = lambda w: pl.BlockSpec((None, tm, w), lambda b, i: (b, i, 0))
    col = lambda w: pl.BlockSpec((None, w, tm), lambda b, i: (b, 0, i))
    ndt = 2 * SSM_HEADS
    ind = (jnp.arange(ATT_W)[:, None] // QK_DIM == jnp.arange(HEAD_W)[None, :]).astype(BF16)
    return pl.pallas_call(
        _inproj_kernel,
        grid=grid,
        in_specs=[row(D), _const_spec((1, D)), _const_spec(wqt.shape), _const_spec(wk.shape),
                  _const_spec(wvt.shape), _const_spec(wz.shape),
                  _const_spec(wxbc.shape), _const_spec(wdt.shape), _const_spec(wdtt.shape),
                  _const_spec(wgate.shape), _const_spec((1, 2 * D)), _const_spec((ATT_W, HEAD_W))],
        out_specs=[col(ATT_W), row(ATT_W), col(ATT_W), row(SSM_INNER), row(CONV_CH), row(ndt),
                   col(ndt), row(2 * D), pl.BlockSpec((None, 8, HEAD_W), lambda b, i: (b, i, 0))],
        out_shape=[jax.ShapeDtypeStruct((B, ATT_W, L), BF16),
                   jax.ShapeDtypeStruct((B, L, ATT_W), BF16),
                   jax.ShapeDtypeStruct((B, ATT_W, L), BF16),
                   jax.ShapeDtypeStruct((B, L, SSM_INNER), BF16),
                   jax.ShapeDtypeStruct((B, L, CONV_CH), F32),
                   jax.ShapeDtypeStruct((B, L, ndt), F32),
                   jax.ShapeDtypeStruct((B, ndt, L), F32),
                   jax.ShapeDtypeStruct((B, L, 2 * D), BF16),
                   jax.ShapeDtypeStruct((B, (L // tm) * 8, HEAD_W), F32)],
        compiler_params=_params(("parallel", "parallel")),
        name="inproj",
    )(x, g_mix, wqt, wk, wvt, wz, wxbc, wdt, wdtt, wgate, bgate, ind)


def _rel_bucket(rel):
    half = REL_BUCKETS // 2
    max_exact = half // 2
    ret = jnp.where(rel > 0, half, 0)
    n = jnp.abs(rel)
    nf = jnp.maximum(n, 1).astype(jnp.float32)
    large = max_exact + (jnp.log(nf / max_exact) / math.log(REL_MAX_DIST / max_exact)
                         * (half - max_exact)).astype(jnp.int32)
    large = jnp.minimum(large, half - 1)
    return ret + jnp.where(n < max_exact, n, large)


def _bias_kernel(table_ref, bucket_ref, out_ref, *, TQ):
    h = pl.program_id(0)
    out_ref[...] = jnp.zeros(out_ref.shape, F32)
    for d in range(ATT_BIAS_TILES):
        for m in range(2):
            cols = pl.ds(m * TQ, TQ)

            def body(b, carry, d=d, m=m, cols=cols):
                val = table_ref[b, 2 * h + m] * LOG2E
                out_ref[d, :, cols] = jnp.where(bucket_ref[d] == b, val, out_ref[d, :, cols])
                return carry

            lax.fori_loop(0, REL_BUCKETS, body, 0)


def _bias_tiles(rel_bias, TQ, TK):
    assert TK > REL_MAX_DIST
    base = jnp.arange(TK, dtype=jnp.int32)[:, None] - jnp.arange(TQ, dtype=jnp.int32)[None, :]
    bucket = jnp.stack([_rel_bucket(base + (d - 2) * TK) for d in range(ATT_BIAS_TILES)])
    return pl.pallas_call(
        functools.partial(_bias_kernel, TQ=TQ),
        grid=(N_HEADS,),
        in_specs=[pl.BlockSpec(memory_space=pltpu.SMEM), _const_spec((ATT_BIAS_TILES, TK, TQ))],
        out_specs=pl.BlockSpec((None, ATT_BIAS_TILES, TK, 2 * TQ), lambda h: (h, 0, 0, 0)),
        out_shape=jax.ShapeDtypeStruct((N_HEADS, ATT_BIAS_TILES, TK, 2 * TQ), F32),
        compiler_params=_params(("parallel",)),
        name="rel_bias_tiles",
    )(rel_bias.astype(F32), bucket)


def _attn_kernel(lam_ref, brange_ref, qt_ref, k_ref, vt_ref, bt_ref, kn_ref, g_ref, o_ref, acc_sc, p_sc,
                 *, TQ, T, nk):
    h = pl.program_id(1)
    i = pl.program_id(2)
    qt = qt_ref[...]
    sub = lax.broadcasted_iota(jnp.int32, qt.shape, 0)
    zero = jnp.zeros_like(qt)
    q2t = jnp.concatenate([jnp.where(sub < QK_DIM, qt, zero), jnp.where(sub >= QK_DIM, qt, zero)], axis=1)
    acc_sc[...] = jnp.zeros(acc_sc.shape, F32)
    ones = jnp.ones((ONES_ROWS, T), BF16)

    def bias_tile(j):
        return bt_ref[jnp.clip(j - ATT_QK_RATIO * i, -2, ATT_QK_RATIO + 1) + 2]

    qsq = qt.astype(F32) ** 2
    qn = jnp.sqrt(jnp.concatenate([jnp.sum(qsq[:QK_DIM], axis=0, keepdims=True),
                                   jnp.sum(qsq[QK_DIM:], axis=0, keepdims=True)], axis=1))
    kn2 = jnp.max(kn_ref[...], axis=0, keepdims=True)
    klane = lax.broadcasted_iota(jnp.int32, kn2.shape, 1)
    kmax = [jnp.sqrt(jnp.max(jnp.where(klane == 2 * h + mp, kn2, 0.0), axis=1, keepdims=True)) for mp in range(2)]
    first = lax.broadcasted_iota(jnp.int32, qn.shape, 1) < TQ
    reach = qn * jnp.where(first, kmax[0], kmax[1]) * BOUND_SLACK
    bmax = jnp.where(first, brange_ref[h, 0], brange_ref[h, 1])
    bmin = jnp.where(first, brange_ref[h, 2], brange_ref[h, 3])
    shift = (reach + bmax).astype(BF16)
    bounded = jnp.max(2.0 * reach + (bmax - bmin)) < EXP2_SPAN

    @pl.when(bounded)
    def _():
        aug = jnp.where(lax.broadcasted_iota(jnp.int32, (HEAD_W, 2 * TQ), 0) == 0,
                        jnp.broadcast_to(-shift.astype(F32), (HEAD_W, 2 * TQ)), 0.0)
        qaug = jnp.concatenate([q2t, aug.astype(BF16)], axis=0)
        lane = lax.broadcasted_iota(jnp.int32, (T, HEAD_W), 1)
        onescol = jnp.where(lane == 0, 1.0, 0.0).astype(BF16)
        ones4 = jnp.ones((ONES_ROWS, 4 * T), BF16)

        def quad(u, carry):
            for r in range(4):
                j = 4 * u + r
                kb = jnp.concatenate([k_ref[pl.ds(pl.multiple_of(j * T, T), T), :], onescol], axis=1)
                s = jnp.dot(kb, qaug, preferred_element_type=F32) + bias_tile(j)
                p_sc[r * T:(r + 1) * T, :] = jnp.exp2(s).astype(BF16)
            vb = jnp.concatenate([vt_ref[:, pl.ds(pl.multiple_of(4 * u * T, 4 * T), 4 * T)], ones4], axis=0)
            acc_sc[...] += jnp.dot(vb, p_sc[...], preferred_element_type=F32)
            return carry

        lax.fori_loop(0, nk // 4, quad, 0)

    @pl.when(jnp.logical_not(bounded))
    def _():
        def scores_softmax(j, slot, m):
            kb = k_ref[pl.ds(pl.multiple_of(j * T, T), T), :]
            s = jnp.dot(kb, q2t, preferred_element_type=F32) + bias_tile(j)
            m_new = jnp.maximum(m, jnp.max(s, axis=0, keepdims=True))
            p_sc[slot * T:(slot + 1) * T, :] = jnp.exp2(s - m_new).astype(BF16)
            return m_new, jnp.exp2(m - m_new)

        def accumulate(j, slot, alpha):
            vb = jnp.concatenate([vt_ref[:, pl.ds(pl.multiple_of(j * T, T), T)], ones], axis=0)
            acc_sc[...] = alpha * acc_sc[...] + jnp.dot(vb, p_sc[slot * T:(slot + 1) * T, :],
                                                        preferred_element_type=F32)

        m = jnp.full((1, 2 * TQ), -jnp.inf, F32)
        m, alpha = scores_softmax(0, 0, m)

        def pair(u, carry):
            m, alpha = carry
            t = 2 * u
            m, alpha1 = scores_softmax(t - 1, 1, m)
            accumulate(t - 2, 0, alpha)
            m, alpha0 = scores_softmax(t, 0, m)
            accumulate(t - 1, 1, alpha1)
            return m, alpha0

        m, alpha = lax.fori_loop(1, nk // 2, pair, (m, alpha))
        m, alpha1 = scores_softmax(nk - 1, 1, m)
        accumulate(nk - 2, 0, alpha)
        accumulate(nk - 1, 1, alpha1)

    o = acc_sc[0:HEAD_W, :] / acc_sc[HEAD_W:HEAD_W + 1, :]
    od = o[:, :TQ] - lam_ref[0] * o[:, TQ:]
    y = od * lax.rsqrt(jnp.mean(od * od, axis=0, keepdims=True) + EPS) * g_ref[...]
    o_ref[...] = (y * (1.0 - LAM_INIT)).T.astype(o_ref.dtype)


def _attention(qt, k, vt, kn, btiles, brange, lam, g_col, *, T=ATT_BLOCK):
    B, L, _ = k.shape
    TQ = ATT_QK_RATIO * T
    nk = L // T
    assert nk % 4 == 0 and L % TQ == 0
    return pl.pallas_call(
        functools.partial(_attn_kernel, TQ=TQ, T=T, nk=nk),
        grid=(B, N_HEADS, L // TQ),
        in_specs=[pl.BlockSpec(memory_space=pltpu.SMEM),
                  pl.BlockSpec(memory_space=pltpu.SMEM),
                  pl.BlockSpec((None, HEAD_W, TQ), lambda b, h, i: (b, h, i)),
                  pl.BlockSpec((None, L, HEAD_W), lambda b, h, i: (b, 0, h)),
                  pl.BlockSpec((None, HEAD_W, L), lambda b, h, i: (b, h, 0)),
                  pl.BlockSpec((None, ATT_BIAS_TILES, T, 2 * TQ), lambda b, h, i: (h, 0, 0, 0)),
                  pl.BlockSpec((None,) + kn.shape[1:], lambda b, h, i: (b, 0, 0)),
                  _const_spec((HEAD_W, 1))],
        out_specs=pl.BlockSpec((None, TQ, HEAD_W), lambda b, h, i: (b, i, h)),
        out_shape=jax.ShapeDtypeStruct((B, L, ATT_W), BF16),
        scratch_shapes=[pltpu.VMEM((HEAD_W + ONES_ROWS, 2 * TQ), F32), pltpu.VMEM((4 * T, 2 * TQ), BF16)],
        compiler_params=_params(("parallel", "parallel", "parallel")),
        name="diff_attention",
    )(lam, brange, qt, k, vt, btiles, kn, g_col)


def _conv_kernel(x_ref, prev_ref, next_ref, dt_ref, dtt_ref, w_ref, b_ref, dtb_ref, dtbt_ref,
                 xs_ref, bc_ref, dts_ref, dtst_ref, *, tl, nt):
    i = pl.program_id(1)
    x = x_ref[...]
    row = lax.broadcasted_iota(jnp.int32, x.shape, 0)
    prev_row = jnp.where(i > 0, prev_ref[7:8, :], 0.0)
    next_row = jnp.where(i < nt - 1, next_ref[0:1, :], 0.0)
    xm1 = jnp.where(row == 0, prev_row, pltpu.roll(x, 1, 0))
    xp1 = jnp.where(row == tl - 1, next_row, pltpu.roll(x, tl - 1, 0))
    y = w_ref[0:1, :] * xm1 + w_ref[1:2, :] * x + w_ref[2:3, :] * xp1 + b_ref[...]
    y = y * jax.nn.sigmoid(y)
    xs_ref[...] = y[:, :SSM_INNER].astype(BF16)
    bc_ref[...] = y[:, SSM_INNER:].astype(BF16)

    def softplus(v):
        return jnp.maximum(v, 0.0) + jnp.log1p(jnp.exp(-jnp.abs(v)))

    dts_ref[...] = softplus(dt_ref[...] + dtb_ref[...])
    dtst_ref[...] = softplus(dtt_ref[...] + dtbt_ref[...])


def _conv(xbc, dt, dtt, conv_w, conv_b, dtb, dtbt, *, tl=512):
    B, L, _ = xbc.shape
    nt = L // tl
    r8 = tl // 8
    ndt = 2 * SSM_HEADS
    row = lambda w: pl.BlockSpec((None, tl, w), lambda b, i: (b, i, 0))
    colspec = pl.BlockSpec((None, ndt, tl), lambda b, i: (b, 0, i))
    return pl.pallas_call(
        functools.partial(_conv_kernel, tl=tl, nt=nt),
        grid=(B, nt),
        in_specs=[row(CONV_CH),
                  pl.BlockSpec((None, 8, CONV_CH), lambda b, i: (b, jnp.maximum(i * r8 - 1, 0), 0)),
                  pl.BlockSpec((None, 8, CONV_CH), lambda b, i: (b, jnp.minimum((i + 1) * r8, L // 8 - 1), 0)),
                  row(ndt), colspec,
                  _const_spec((3, CONV_CH)), _const_spec((1, CONV_CH)), _const_spec((1, ndt)),
                  _const_spec((ndt, 1))],
        out_specs=[row(SSM_INNER), row(2 * SSM_GROUPS * SSM_N), row(ndt), colspec],
        out_shape=[jax.ShapeDtypeStruct((B, L, SSM_INNER), BF16),
                   jax.ShapeDtypeStruct((B, L, 2 * SSM_GROUPS * SSM_N), BF16),
                   jax.ShapeDtypeStruct((B, L, ndt), F32),
                   jax.ShapeDtypeStruct((B, ndt, L), F32)],
        compiler_params=_params(("parallel", "parallel")),
        name="ssm_conv",
    )(xbc, xbc, xbc, dt, dtt, conv_w, conv_b, dtb, dtbt)


def _ssd_chunk(xs_ref, bc_ref, dts_ref, dtst_ref, alog_ref, alogt_ref, st_ref, *, rev):
    Q = SSM_CHUNK
    HI = lax.Precision.HIGHEST
    off = SSM_HEADS if rev else 0
    ri = lax.broadcasted_iota(jnp.int32, (Q, Q), 0)
    ci = lax.broadcasted_iota(jnp.int32, (Q, Q), 1)
    incl = (ci >= ri) if rev else (ci <= ri)
    tri = incl.astype(F32)
    dt = dts_ref[:, off:off + SSM_HEADS]
    dtt = dtst_ref[off:off + SSM_HEADS, :]
    a = dt * (-jnp.exp(alog_ref[:, off:off + SSM_HEADS]))
    at = dtt * (-jnp.exp(alogt_ref[off:off + SSM_HEADS, :]))
    acum = jnp.dot(tri, a, preferred_element_type=F32, precision=HI)
    acumt = _nt_dot(at, tri, precision=HI)
    edge = 0 if rev else Q - 1
    bc = bc_ref[...]
    bpair = bc[:, :Q]
    cpair = bc[:, Q:]
    low = ci < SSM_N
    rlow = ri < SSM_N
    blockmask = low == rlow
    total = acumt[:, edge:edge + 1]
    decay = jnp.exp(total)
    w_rows = dtt * jnp.exp(total - acumt)
    ys = []
    for g in range(SSM_GROUPS):
        gmask = low if g == 0 else jnp.logical_not(low)
        cg = jnp.where(gmask, cpair, jnp.zeros_like(cpair))
        cb = _nt_dot(cg, bpair)
        dupm = ((ri - g * SSM_N) == (ci % SSM_N)).astype(BF16)
        cdup = jnp.dot(cpair, dupm, preferred_element_type=F32)
        sel2 = ((ci - g * SSM_N) == (ri % SSM_N)).astype(BF16)
        bt2 = _nt_dot(sel2, bpair)
        for hp in range(SSM_HEADS // SSM_GROUPS // 2):
            h0 = g * (SSM_HEADS // SSM_GROUPS) + 2 * hp
            pair = h0 // 2
            xpair = xs_ref[:, h0 * SSM_P:(h0 + 2) * SSM_P]
            ms = []
            e_cols = []
            for h in (h0, h0 + 1):
                e_col = jnp.broadcast_to(acum[:, h:h + 1], (Q, Q))
                r_row = jnp.broadcast_to(acumt[h:h + 1, :], (Q, Q))
                seg = jnp.where(incl, e_col - r_row, -jnp.inf)
                ms.append((cb * jnp.exp(seg) * jnp.broadcast_to(dtt[h:h + 1, :], (Q, Q))).astype(BF16))
                e_cols.append(e_col)
            st = st_ref[pair]
            xblk = jnp.concatenate([jnp.where(low, xpair, jnp.zeros_like(xpair)),
                                    jnp.where(low, jnp.zeros_like(xpair), xpair)], axis=0)
            y = jnp.dot(jnp.concatenate(ms, axis=1), xblk, preferred_element_type=F32)
            ce = (cdup * jnp.exp(jnp.where(low, e_cols[0], e_cols[1]))).astype(BF16)
            y = y + jnp.dot(ce, st.astype(BF16), preferred_element_type=F32)
            ys.append(y)
            w_pair = jnp.where(rlow, jnp.broadcast_to(w_rows[h0:h0 + 1, :], (Q, Q)),
                               jnp.broadcast_to(w_rows[h0 + 1:h0 + 2, :], (Q, Q)))
            snew = jnp.dot((bt2 * w_pair).astype(BF16), xpair, preferred_element_type=F32)
            dec = jnp.where(rlow, decay[h0:h0 + 1, :], decay[h0 + 1:h0 + 2, :])
            st_ref[pair] = dec * st + jnp.where(blockmask, snew, 0.0)
    return jnp.concatenate(ys, axis=1)


def _ssd_fwd_kernel(xs_ref, bc_ref, dts_ref, dtst_ref, alog_ref, alogt_ref, y_ref, st_ref):
    @pl.when(pl.program_id(1) == 0)
    def _():
        st_ref[...] = jnp.zeros(st_ref.shape, F32)

    y = _ssd_chunk(xs_ref, bc_ref, dts_ref, dtst_ref, alog_ref, alogt_ref, st_ref, rev=False)
    y_ref[...] = y.astype(y_ref.dtype)


def _ssd_bwd_kernel(xs_ref, bc_ref, dts_ref, dtst_ref, alog_ref, alogt_ref, yf_ref, z_ref, dskip_ref, g_ref,
                    o_ref, st_ref):
    @pl.when(pl.program_id(1) == 0)
    def _():
        st_ref[...] = jnp.zeros(st_ref.shape, F32)

    yb = _ssd_chunk(xs_ref, bc_ref, dts_ref, dtst_ref, alog_ref, alogt_ref, st_ref, rev=True)
    z = z_ref[...].astype(F32)
    y = (yf_ref[...].astype(F32) + yb + xs_ref[...].astype(F32) * dskip_ref[...]) * (z * jax.nn.sigmoid(z))
    y = y * lax.rsqrt(jnp.mean(y * y, axis=-1, keepdims=True) + EPS) * g_ref[...]
    o_ref[...] = y.astype(o_ref.dtype)


def _ssd(xs, bc, dts, dtst, alog, alogt, z, dskip, g_norm):
    B, L, _ = xs.shape
    Q = SSM_CHUNK
    nc = L // Q
    ndt = 2 * SSM_HEADS
    npair = SSM_HEADS // 2

    def specs(cidx):
        row = lambda w: pl.BlockSpec((None, Q, w), lambda b, c: (b, cidx(c), 0))
        return row, [row(SSM_INNER), row(2 * SSM_GROUPS * SSM_N), row(ndt),
                     pl.BlockSpec((None, ndt, Q), lambda b, c: (b, 0, cidx(c))),
                     _const_spec((1, ndt)), _const_spec((ndt, 1))]

    scratch = [pltpu.VMEM((npair, 2 * SSM_N, 2 * SSM_P), F32)]
    row, in_specs = specs(lambda c: c)
    yf = pl.pallas_call(
        _ssd_fwd_kernel,
        grid=(B, nc),
        in_specs=in_specs,
        out_specs=row(SSM_INNER),
        out_shape=jax.ShapeDtypeStruct((B, L, SSM_INNER), BF16),
        scratch_shapes=scratch,
        compiler_params=_params(("parallel", "arbitrary")),
        name="ssd_forward",
    )(xs, bc, dts, dtst, alog, alogt)
    row, in_specs = specs(lambda c: nc - 1 - c)
    return pl.pallas_call(
        _ssd_bwd_kernel,
        grid=(B, nc),
        in_specs=in_specs + [row(SSM_INNER), row(SSM_INNER), _const_spec((1, SSM_INNER)),
                             _const_spec((1, SSM_INNER))],
        out_specs=row(SSM_INNER),
        out_shape=jax.ShapeDtypeStruct((B, L, SSM_INNER), BF16),
        scratch_shapes=scratch,
        compiler_params=_params(("parallel", "arbitrary")),
        name="ssd_reverse_gate_norm",
    )(xs, bc, dts, dtst, alog, alogt, yf, z, dskip, g_norm)


def _merge_kernel(att_ref, ssm_ref, gates_ref, x_ref, wau_ref, wsu_ref, wout_ref, gffn_ref, wr_ref, wrt_ref,
                  place_ref, h_ref, u2_ref, afft_ref):
    D = D_MODEL
    E = N_EXPERTS
    ya = jnp.dot(att_ref[...], wau_ref[...], preferred_element_type=F32)
    ysm = jnp.dot(ssm_ref[...], wsu_ref[...], preferred_element_type=F32)
    merged = gates_ref[:, :D] * ya + gates_ref[:, D:] * ysm
    h = x_ref[...] + jnp.dot(merged.astype(BF16), wout_ref[...], preferred_element_type=F32)
    h_ref[...] = h
    u2 = (h * lax.rsqrt(jnp.mean(h * h, axis=-1, keepdims=True) + EPS) * gffn_ref[...]).astype(BF16)
    u2_ref[:, :D] = u2
    logits = jnp.dot(u2, wr_ref[...], preferred_element_type=F32)
    e = jnp.exp(logits - jnp.max(logits, axis=-1, keepdims=True))
    aff = e / jnp.sum(e, axis=-1, keepdims=True)
    hi = aff.astype(BF16)
    r1 = aff - hi.astype(F32)
    mid = r1.astype(BF16)
    lo = (r1 - mid.astype(F32)).astype(BF16)
    ext = (jnp.dot(hi, place_ref[0:E, :], preferred_element_type=F32)
           + jnp.dot(mid, place_ref[E:2 * E, :], preferred_element_type=F32)
           + jnp.dot(lo, place_ref[2 * E:3 * E, :], preferred_element_type=F32))
    u2_ref[:, D:] = ext.astype(BF16)
    lt = _nt_dot(wrt_ref[...], u2)
    et = jnp.exp(lt - jnp.max(lt, axis=0, keepdims=True))
    afft_ref[...] = et / jnp.sum(et, axis=0, keepdims=True)


def _merge(att, ssm, gates, x, wau, wsu, wout, g_ffn, wr, *, tm=256):
    B, L, D = x.shape
    E = N_EXPERTS
    nl = L // tm
    row = lambda w: pl.BlockSpec((None, tm, w), lambda b, i: (b, i, 0))
    place = jnp.eye(3 * E, GATE_EXT, dtype=BF16)
    return pl.pallas_call(
        _merge_kernel,
        grid=(B, nl),
        in_specs=[row(ATT_W), row(SSM_INNER), row(2 * D), row(D), _const_spec(wau.shape),
                  _const_spec(wsu.shape), _const_spec(wout.shape), _const_spec((1, D)), _const_spec(wr.shape),
                  _const_spec((E, D)), _const_spec((3 * E, GATE_EXT))],
        out_specs=[row(D), row(D + GATE_EXT), pl.BlockSpec((E, tm), lambda b, i: (0, b * nl + i))],
        out_shape=[jax.ShapeDtypeStruct((B, L, D), F32), jax.ShapeDtypeStruct((B, L, D + GATE_EXT), BF16),
                   jax.ShapeDtypeStruct((E, B * L), F32)],
        compiler_params=_params(("parallel", "parallel")),
        name="merge_router",
    )(att, ssm, gates, x, wau, wsu, wout, g_ffn, wr, wr.T, place)


def _route_kernel(afft_ref, posm_ref, bstart_ref, *, cap, nb):
    E = N_EXPERTS
    W = ROUTE_BLK

    def bit_body(k, thr):
        cand = thr | lax.shift_left(jnp.int32(1), 30 - k)
        x = lax.bitcast_convert_type(afft_ref[...], jnp.int32)
        cnt = jnp.sum(jnp.where(x >= cand, 1.0, 0.0), axis=1, keepdims=True)
        return jnp.where(cnt >= cap, cand, thr)

    thr = lax.fori_loop(0, 31, bit_body, jnp.zeros((E, 1), jnp.int32))
    x = lax.bitcast_convert_type(afft_ref[...], jnp.int32)
    n_gt = jnp.sum(jnp.where(x > thr, 1.0, 0.0), axis=1, keepdims=True)
    need = cap - n_gt
    ri = lax.broadcasted_iota(jnp.int32, (W, W), 0)
    ci = lax.broadcasted_iota(jnp.int32, (W, W), 1)
    upper = jnp.where(ri <= ci, 1.0, 0.0).astype(BF16)
    bstart_ref[...] = jnp.zeros(bstart_ref.shape, jnp.int32)
    c_eq = jnp.zeros((E, 1), F32)
    c_sel = jnp.zeros((E, 1), F32)
    for b in range(nb):
        xb = lax.bitcast_convert_type(afft_ref[:, b * W:(b + 1) * W], jnp.int32)
        eq = xb == thr
        eqc = jnp.dot(jnp.where(eq, 1.0, 0.0).astype(BF16), upper, preferred_element_type=F32) + c_eq
        sel = (xb > thr) | (eq & (eqc <= need))
        selc = jnp.dot(jnp.where(sel, 1.0, 0.0).astype(BF16), upper, preferred_element_type=F32) + c_sel
        posm_ref[:, b * W:(b + 1) * W] = jnp.where(sel, selc, 0.0).astype(jnp.int32)
        bstart_ref[:, b:b + 1] = c_sel.astype(jnp.int32)
        c_eq = eqc[:, W - 1:W]
        c_sel = selc[:, W - 1:W]
    bstart_ref[:, nb:nb + 1] = c_sel.astype(jnp.int32)


def _route(afft, cap):
    E, T = afft.shape
    nb = T // ROUTE_BLK
    assert nb < BSTART_W and cap % SLOT_TILE == 0
    return pl.pallas_call(
        functools.partial(_route_kernel, cap=float(cap), nb=nb),
        out_shape=[jax.ShapeDtypeStruct((E, T), jnp.int32), jax.ShapeDtypeStruct((E, BSTART_W), jnp.int32)],
        compiler_params=_params(None),
        name="route_select",
    )(afft)


def _pair_schedule(bstart, cap, ngb):
    step = GATHER_BLK // ROUTE_BLK
    st = bstart[:, 0:ngb * step + 1:step]
    lo, hi = st[:, :-1], st[:, 1:]
    ntile = cap // SLOT_TILE
    t_lo = jnp.arange(ntile, dtype=jnp.int32) * SLOT_TILE
    t_hi = t_lo + SLOT_TILE
    b_first = jnp.sum(hi[:, None, :] <= t_lo[None, :, None], axis=2).astype(jnp.int32)
    b_last = jnp.sum(lo[:, None, :] < t_hi[None, :, None], axis=2).astype(jnp.int32) - 1
    return b_first, b_last - b_first + 1


def _expert_kernel(first_ref, cnt_ref, u_hbm, posm_ref, wg_ref, wu_ref, wd_ref, o_ref, x_sc, ubuf, sem):
    D = D_MODEL
    e = pl.program_id(0)
    j = pl.program_id(1)
    g = e * pl.num_programs(1) + j
    b0 = first_ref[g]
    n = cnt_ref[g]

    half = GATHER_BLK // 2

    def half_copies(b, slot):
        return [pltpu.make_async_copy(
            u_hbm.at[pl.ds(pl.multiple_of(b * GATHER_BLK + r * half, half), half)],
            ubuf.at[slot, pl.ds(r * half, half)], sem.at[slot, r]) for r in range(2)]

    def start(b, slot):
        for cp in half_copies(b, slot):
            cp.start()

    ahead = GATHER_BUFS - 1

    def start_tile(t):
        start(first_ref[t], 0)
        for r in range(1, ahead):
            @pl.when(cnt_ref[t] > r)
            def _(r=r):
                start(first_ref[t] + r, r)

    @pl.when(g == 0)
    def _():
        start_tile(0)

    x_sc[...] = jnp.zeros(x_sc.shape, F32)
    slot_ids = lax.broadcasted_iota(jnp.int32, (SLOT_TILE, GATHER_BLK), 0) + (j * SLOT_TILE + 1)

    def body(c, carry):
        slot = c % GATHER_BUFS
        for cp in half_copies(b0 + c, slot):
            cp.wait()

        @pl.when(c + ahead < n)
        def _():
            start(b0 + c + ahead, (c + ahead) % GATHER_BUFS)

        posrow = posm_ref[:, pl.ds(pl.multiple_of((b0 + c) * GATHER_BLK, GATHER_BLK), GATHER_BLK)]
        onehot = jnp.where(posrow == slot_ids, 1.0, 0.0).astype(BF16)
        x_sc[...] += jnp.dot(onehot, ubuf[slot], preferred_element_type=F32)
        return carry

    lax.fori_loop(0, n, body, 0)

    @pl.when(g + 1 < pl.num_programs(0) * pl.num_programs(1))
    def _():
        start_tile(g + 1)

    xs = x_sc[:, :D].astype(BF16)
    ext = x_sc[:, D:]
    lane = lax.broadcasted_iota(jnp.int32, ext.shape, 1)
    mine = (lane == e) | (lane == e + N_EXPERTS) | (lane == e + 2 * N_EXPERTS)
    gate = jnp.sum(jnp.where(mine, ext, 0.0), axis=1, keepdims=True)
    hg = jnp.dot(xs, wg_ref[...], preferred_element_type=F32)
    hu = jnp.dot(xs, wu_ref[...], preferred_element_type=F32)
    hmid = (hg * jax.nn.sigmoid(hg) * hu).astype(BF16)
    o_ref[...] = jnp.dot(hmid, wd_ref[...], preferred_element_type=F32) * gate


def _experts(u2ext, posm, b_first, b_cnt, wg, wu, wd, cap):
    T, DX = u2ext.shape
    E, D = N_EXPERTS, D_MODEL
    wspec = pl.BlockSpec((None, D, D), lambda e, j, bf, bc: (e, 0, 0))
    return pl.pallas_call(
        _expert_kernel,
        grid_spec=pltpu.PrefetchScalarGridSpec(
            num_scalar_prefetch=2,
            grid=(E, cap // SLOT_TILE),
            in_specs=[pl.BlockSpec(memory_space=pl.ANY),
                      pl.BlockSpec((None, 1, T), lambda e, j, bf, bc: (e, 0, 0)),
                      wspec, wspec, wspec],
            out_specs=pl.BlockSpec((None, SLOT_TILE, D), lambda e, j, bf, bc: (e, j, 0)),
            scratch_shapes=[pltpu.VMEM((SLOT_TILE, DX), F32), pltpu.VMEM((GATHER_BUFS, GATHER_BLK, DX), BF16),
                            pltpu.SemaphoreType.DMA((GATHER_BUFS, 2))]),
        out_shape=jax.ShapeDtypeStruct((E, cap, D), F32),
        compiler_params=_params(("arbitrary", "arbitrary")),
        name="experts_gather_mlp",
    )(b_first.reshape(-1), b_cnt.reshape(-1), u2ext, posm.reshape(E, 1, T), wg, wu, wd)


def _combine_kernel(bstart_ref, posm_ref, ye_ref, o_ref, buf, xbuf, sem, xsem, *, cap, nb):
    E = N_EXPERTS
    WN = COMBINE_WIN
    b = pl.program_id(0)
    top = cap - WN
    cur = b % 2

    def window(blk, slot, e):
        base = jnp.minimum((bstart_ref[e, blk] // 8) * 8, top)
        return pltpu.make_async_copy(ye_ref.at[e, pl.ds(pl.multiple_of(base, 8), WN)],
                                     buf.at[slot, pl.ds(e * WN, WN)], sem.at[slot, e])

    @pl.when(b == 0)
    def _():
        for e in range(E):
            window(0, 0, e).start()

    @pl.when(b + 1 < nb)
    def _():
        for e in range(E):
            window(b + 1, 1 - cur, e).start()

    lows = [(bstart_ref[e, b] // 8) * 8 for e in range(E)]
    w_iota = lax.broadcasted_iota(jnp.int32, (WN, ROUTE_BLK), 0)

    def onehot_rows(slot0, low, base):
        ok = (slot0 >= low) & (slot0 < low + WN)
        return jnp.where(ok & (slot0 - base == w_iota), 1.0, 0.0).astype(BF16)

    def scatter(onehot, rows):
        hi = rows.astype(BF16)
        lo = (rows - hi.astype(F32)).astype(BF16)
        tn = lambda a, c: lax.dot_general(a, c, (((0,), (0,)), ((), ())), preferred_element_type=F32)
        return tn(onehot, hi) + tn(onehot, lo)

    hots = []
    for e in range(E):
        slot0 = posm_ref[e:e + 1, :] - 1
        hots.append(onehot_rows(slot0, lows[e], jnp.minimum(lows[e], top)))
    for e in range(E):
        window(b, cur, e).wait()
    o_ref[...] = scatter(jnp.concatenate(hots, axis=0), buf[cur])

    def per_expert(e, carry):
        low0 = (bstart_ref[e, b] // 8) * 8
        nwin = (bstart_ref[e, b + 1] - low0 + WN - 1) // WN

        def per_window(c, carry):
            low = low0 + c * WN
            base = jnp.minimum(low, top)
            cp = pltpu.make_async_copy(ye_ref.at[e, pl.ds(pl.multiple_of(base, 8), WN)], xbuf, xsem.at[0])
            cp.start()
            cp.wait()
            mine = lax.broadcasted_iota(jnp.int32, (E, ROUTE_BLK), 0) == e
            slot0 = jnp.sum(jnp.where(mine, posm_ref[...], 0), axis=0, keepdims=True) - 1
            o_ref[...] += scatter(onehot_rows(slot0, low, base), xbuf[...])
            return carry

        return lax.fori_loop(1, nwin, per_window, carry)

    lax.fori_loop(0, E, per_expert, 0)


def _combine(ye, posm, bstart, cap):
    E, T = posm.shape
    D = D_MODEL
    nb = T // ROUTE_BLK
    assert cap >= COMBINE_WIN
    return pl.pallas_call(
        functools.partial(_combine_kernel, cap=cap, nb=nb),
        grid_spec=pltpu.PrefetchScalarGridSpec(
            num_scalar_prefetch=1,
            grid=(nb,),
            in_specs=[pl.BlockSpec((E, ROUTE_BLK), lambda b, bs: (0, b)),
                      pl.BlockSpec(memory_space=pl.ANY)],
            out_specs=pl.BlockSpec((ROUTE_BLK, D), lambda b, bs: (b, 0)),
            scratch_shapes=[pltpu.VMEM((2, E * COMBINE_WIN, D), F32), pltpu.VMEM((COMBINE_WIN, D), F32),
                            pltpu.SemaphoreType.DMA((2, E)), pltpu.SemaphoreType.DMA((1,))]),
        out_shape=jax.ShapeDtypeStruct((T, D), F32),
        compiler_params=_params(("arbitrary",)),
        name="experts_combine",
    )(bstart, posm, ye)


def _ple_kernel(h_ref, moe_ref, p_ref, wpg_ref, bpg_ref, wpp_ref, gfin_ref, o_ref):
    h = h_ref[...] + moe_ref[...]
    gate = jax.nn.sigmoid(jnp.dot(h.astype(BF16), wpg_ref[...], preferred_element_type=F32) + bpg_ref[...])
    h = h + gate * jnp.dot(p_ref[...].astype(BF16), wpp_ref[...], preferred_element_type=F32)
    o_ref[...] = h * lax.rsqrt(jnp.mean(h * h, axis=-1, keepdims=True) + EPS) * gfin_ref[...]


def _ple(h, moe, p, wpg, bpg, wpp, g_final, *, tm=256):
    B, L, D = h.shape
    row = lambda w: pl.BlockSpec((None, tm, w), lambda b, i: (b, i, 0))
    return pl.pallas_call(
        _ple_kernel,
        grid=(B, L // tm),
        in_specs=[row(D), row(D), row(PLE_DIM), _const_spec(wpg.shape), _const_spec((1, D)),
                  _const_spec(wpp.shape), _const_spec((1, D))],
        out_specs=row(D),
        out_shape=jax.ShapeDtypeStruct((B, L, D), F32),
        compiler_params=_params(("parallel", "parallel")),
        name="ple_final_norm",
    )(h, moe, p, wpg, bpg, wpp, g_final)


def _forward(x, p, w, btiles):
    B, L, D = x.shape
    T = B * L
    qt, k, vt, z, xbc, dt, dtt, gates, kn = _inproj(x, w['g_mix'], w['wqt'], w['wk'], w['wvt'], w['wz'],
                                                    w['wxbc'], w['wdt'], w['wdtt'], w['wgate'], w['bgate'])
    att = _attention(qt, k, vt, kn, btiles, w['brange'], w['lam'], w['g_subln'])
    xs, bc, dts, dtst = _conv(xbc, dt, dtt, w['conv_w'], w['conv_b'], w['dtb'], w['dtbt'])
    ssm = _ssd(xs, bc, dts, dtst, w['alog'], w['alogt'], z, w['dskip'], w['g_ssm_norm'])
    h, u2ext, afft = _merge(att, ssm, gates, x, w['wau'], w['wsu'], w['wout'], w['g_ffn'], w['wr'])
    cap = max(1, CAPACITY_FACTOR * T // N_EXPERTS)
    posm, bstart = _route(afft, cap)
    b_first, b_cnt = _pair_schedule(bstart, cap, T // GATHER_BLK)
    ye = _experts(u2ext.reshape(T, D + GATE_EXT), posm, b_first, b_cnt, w['weg'], w['weu'], w['wed'], cap)
    moe = _combine(ye, posm, bstart, cap).reshape(B, L, D)
    return _ple(h, moe, p, w['wpg'], w['bpg'], w['wpp'], w['g_final'])


def kernel(x_prompt, x_sample, p_prompt, p_sample, rel_bias, g_mix, w_in, conv_w, conv_b, dt_bias, a_log,
           d_skip, g_ssm_norm, lam_params, g_subln, w_att_up, w_ssm_up, w_gate, b_gate, w_out, g_ffn,
           w_router, w_exp_gate, w_exp_up, w_exp_down, w_ple_gate, b_ple_gate, w_ple_proj, g_final):
    i = 0
    D = D_MODEL
    win = w_in[i]
    o1 = ATT_W
    o3 = 3 * ATT_W
    o4 = o3 + SSM_INNER
    o5 = o4 + CONV_CH
    wqt = (win[:, :o1] * (QK_DIM ** -0.5 * LOG2E)).T.astype(BF16)
    wk = win[:, o1:2 * o1].astype(BF16)
    wvt = win[:, 2 * o1:o3].T.astype(BF16)
    wdt = win[:, o5:].astype(BF16)
    lp = lam_params[i].astype(F32)
    lam = jnp.exp(jnp.sum(lp[0] * lp[1])) - jnp.exp(jnp.sum(lp[2] * lp[3])) + LAM_INIT
    ndt = 2 * SSM_HEADS
    tb = rel_bias.astype(F32).reshape(REL_BUCKETS, N_HEADS, 2) * LOG2E
    brange = jnp.concatenate([jnp.max(tb, axis=0), jnp.min(tb, axis=0)], axis=1)
    w = {
        'g_mix': g_mix[i].reshape(1, D), 'wqt': wqt, 'wk': wk, 'wvt': wvt, 'wz': win[:, o3:o4].astype(BF16),
        'wxbc': win[:, o4:o5].astype(BF16), 'wdt': wdt, 'wdtt': wdt.T,
        'wgate': w_gate[i].astype(BF16), 'bgate': b_gate[i].reshape(1, 2 * D),
        'lam': lam.reshape(1), 'g_subln': g_subln[i].reshape(HEAD_W, 1), 'brange': brange,
        'conv_w': conv_w[i], 'conv_b': conv_b[i].reshape(1, CONV_CH),
        'dtb': dt_bias[i].reshape(1, ndt), 'dtbt': dt_bias[i].reshape(ndt, 1),
        'alog': a_log[i].reshape(1, ndt), 'alogt': a_log[i].reshape(ndt, 1),
        'dskip': jnp.repeat(d_skip[i], SSM_P).reshape(1, SSM_INNER),
        'g_ssm_norm': g_ssm_norm[i].reshape(1, SSM_INNER),
        'wau': w_att_up[i].astype(BF16), 'wsu': w_ssm_up[i].astype(BF16), 'wout': w_out[i].astype(BF16),
        'g_ffn': g_ffn[i].reshape(1, D), 'wr': w_router[i].astype(BF16),
        'weg': w_exp_gate[i].astype(BF16), 'weu': w_exp_up[i].astype(BF16), 'wed': w_exp_down[i].astype(BF16),
        'wpg': w_ple_gate[i].astype(BF16), 'bpg': b_ple_gate[i].reshape(1, D),
        'wpp': w_ple_proj[i].astype(BF16), 'g_final': g_final.reshape(1, D),
    }
    btiles = _bias_tiles(rel_bias, ATT_QK_RATIO * ATT_BLOCK, ATT_BLOCK)
    return (_forward(x_prompt, p_prompt[i], w, btiles), _forward(x_sample, p_sample[i], w, btiles))
```

```python
import functools
import math

import jax
import jax.numpy as jnp
from jax import lax
from jax.experimental import pallas as pl
from jax.experimental.pallas import tpu as pltpu

F32 = jnp.float32
BF16 = jnp.bfloat16

D_MODEL = 1024
N_HEADS = 4
QK_DIM = 64
HEAD_W = 2 * QK_DIM
ATT_W = N_HEADS * HEAD_W
REL_BUCKETS = 32
REL_MAX_DIST = 128
SSM_INNER = 1024
SSM_P = 64
SSM_HEADS = 16
SSM_GROUPS = 2
SSM_N = 64
SSM_CHUNK = 128
CONV_CH = SSM_INNER + 2 * SSM_GROUPS * SSM_N
N_EXPERTS = 16
CAPACITY_FACTOR = 2
PLE_DIM = 256
EPS = 1e-6
LAM_INIT = 0.8 - 0.6 * math.exp(-0.3 * 0)
LOG2E = math.log2(math.e)

GATE_EXT = 128
ROUTE_BLK = 256
GATHER_BLK = 512
GATHER_BUFS = 5
SLOT_TILE = 256
COMBINE_WIN = 72
BSTART_W = 256

CONV_ROWS = 128

ATT_BLOCK = 256
ATT_QK_RATIO = 4
ATT_BIAS_TILES = ATT_QK_RATIO + 4
ONES_ROWS = 16
BOUND_SLACK = 1.0 + 2.0 ** -6
EXP2_SPAN = 100.0
VMEM_LIMIT = 56 * 1024 * 1024


def _params(sem, vmem=VMEM_LIMIT):
    return pltpu.CompilerParams(dimension_semantics=sem, vmem_limit_bytes=vmem)


def _const_spec(shape):
    nd = len(shape)
    return pl.BlockSpec(shape, lambda *_: (0,) * nd)


def _nt_dot(a, b, **kw):
    return lax.dot_general(a, b, (((1,), (1,)), ((), ())), preferred_element_type=F32, **kw)


def _inproj_kernel(x_ref, g_ref, wqt_ref, wk_ref, wvt_ref, wz_ref, wxbc_ref, wdt_ref, wdtt_ref, wgate_ref,
                   bgate_ref, ind_ref, qt_ref, k_ref, vt_ref, z_ref, xbc_ref, dt_ref, dtt_ref, gates_ref, kn_ref):
    x = x_ref[...]
    u = x * lax.rsqrt(jnp.mean(x * x, axis=-1, keepdims=True) + EPS) * g_ref[...]
    ub = u.astype(BF16)
    qt_ref[...] = _nt_dot(wqt_ref[...], ub).astype(BF16)
    kb = jnp.dot(ub, wk_ref[...], preferred_element_type=F32).astype(BF16)
    k_ref[...] = kb
    ksq = (kb.astype(F32) ** 2).astype(BF16)
    kn_ref[...] = jnp.broadcast_to(
        jnp.max(jnp.dot(ksq, ind_ref[...], preferred_element_type=F32), axis=0, keepdims=True), kn_ref.shape)
    vt_ref[...] = _nt_dot(wvt_ref[...], ub).astype(BF16)
    z_ref[...] = jnp.dot(ub, wz_ref[...], preferred_element_type=F32).astype(z_ref.dtype)
    xbc_ref[...] = jnp.dot(ub, wxbc_ref[...], preferred_element_type=F32)
    dt_ref[...] = jnp.dot(ub, wdt_ref[...], preferred_element_type=F32)
    dtt_ref[...] = _nt_dot(wdtt_ref[...], ub)
    gates = jax.nn.sigmoid(jnp.dot(ub, wgate_ref[...], preferred_element_type=F32) + bgate_ref[...])
    gates_ref[...] = gates.astype(gates_ref.dtype)


def _inproj(x, g_mix, wqt, wk, wvt, wz, wxbc, wdt, wdtt, wgate, bgate, *, tm=256):
    B, L, D = x.shape
    grid = (B, L // tm)
    row = lambda w: pl.BlockSpec((None, tm, w), lambda b, i: (b, i, 0))
    col = lambda w: pl.BlockSpec((None, w, tm), lambda b, i: (b, 0, i))
    ndt = 2 * SSM_HEADS
    ind = (jnp.arange(ATT_W)[:, None] // QK_DIM == jnp.arange(HEAD_W)[None, :]).astype(BF16)
    return pl.pallas_call(
        _inproj_kernel,
        grid=grid,
        in_specs=[row(D), _const_spec((1, D)), _const_spec(wqt.shape), _const_spec(wk.shape),
                  _const_spec(wvt.shape), _const_spec(wz.shape),
                  _const_spec(wxbc.shape), _const_spec(wdt.shape), _const_spec(wdtt.shape),
                  _const_spec(wgate.shape), _const_spec((1, 2 * D)), _const_spec((ATT_W, HEAD_W))],
        out_specs=[col(ATT_W), row(ATT_W), col(ATT_W), row(SSM_INNER), row(CONV_CH), row(ndt),
                   col(ndt), row(2 * D), pl.BlockSpec((None, 8, HEAD_W), lambda b, i: (b, i, 0))],
        out_shape=[jax.ShapeDtypeStruct((B, ATT_W, L), BF16),
                   jax.ShapeDtypeStruct((B, L, ATT_W), BF16),
                   jax.ShapeDtypeStruct((B, ATT_W, L), BF16),
                   jax.ShapeDtypeStruct((B, L, SSM_INNER), BF16),
                   jax.ShapeDtypeStruct((B, L, CONV_CH), F32),
                   jax.ShapeDtypeStruct((B, L, ndt), F32),
                   jax.ShapeDtypeStruct((B, ndt, L), F32),
                   jax.ShapeDtypeStruct((B, L, 2 * D), BF16),
                   jax.ShapeDtypeStruct((B, (L // tm) * 8, HEAD_W), F32)],
        compiler_params=_params(("parallel", "parallel")),
        name="inproj",
    )(x, g_mix, wqt, wk, wvt, wz, wxbc, wdt, wdtt, wgate, bgate, ind)


def _rel_bucket(rel):
    half = REL_BUCKETS // 2
    max_exact = half // 2
    ret = jnp.where(rel > 0, half, 0)
    n = jnp.abs(rel)
    nf = jnp.maximum(n, 1).astype(jnp.float32)
    large = max_exact + (jnp.log(nf / max_exact) / math.log(REL_MAX_DIST / max_exact)
                         * (half - max_exact)).astype(jnp.int32)
    large = jnp.minimum(large, half - 1)
    return ret + jnp.where(n < max_exact, n, large)


def _bias_kernel(table_ref, bucket_ref, out_ref, *, TQ):
    h = pl.program_id(0)
    out_ref[...] = jnp.zeros(out_ref.shape, F32)
    for d in range(ATT_BIAS_TILES):
        for m in range(2):
            cols = pl.ds(m * TQ, TQ)

            def body(b, carry, d=d, m=m, cols=cols):
                val = table_ref[b, 2 * h + m] * LOG2E
                out_ref[d, :, cols] = jnp.where(bucket_ref[d] == b, val, out_ref[d, :, cols])
                return carry

            lax.fori_loop(0, REL_BUCKETS, body, 0)


def _bias_tiles(rel_bias, TQ, TK):
    assert TK > REL_MAX_DIST
    base = jnp.arange(TK, dtype=jnp.int32)[:, None] - jnp.arange(TQ, dtype=jnp.int32)[None, :]
    bucket = jnp.stack([_rel_bucket(base + (d - 2) * TK) for d in range(ATT_BIAS_TILES)])
    return pl.pallas_call(
        functools.partial(_bias_kernel, TQ=TQ),
        grid=(N_HEADS,),
        in_specs=[pl.BlockSpec(memory_space=pltpu.SMEM), _const_spec((ATT_BIAS_TILES, TK, TQ))],
        out_specs=pl.BlockSpec((None, ATT_BIAS_TILES, TK, 2 * TQ), lambda h: (h, 0, 0, 0)),
        out_shape=jax.ShapeDtypeStruct((N_HEADS, ATT_BIAS_TILES, TK, 2 * TQ), F32),
        compiler_params=_params(("parallel",)),
        name="rel_bias_tiles",
    )(rel_bias.astype(F32), bucket)


def _attn_kernel(lam_ref, brange_ref, qt_ref, k_ref, vt_ref, bt_ref, kn_ref, g_ref, o_ref, acc_sc, p_sc,
                 *, TQ, T, nk):
    h = pl.program_id(1)
    i = pl.program_id(2)
    qt = qt_ref[...]
    sub = lax.broadcasted_iota(jnp.int32, qt.shape, 0)
    zero = jnp.zeros_like(qt)
    q2t = jnp.concatenate([jnp.where(sub < QK_DIM, qt, zero), jnp.where(sub >= QK_DIM, qt, zero)], axis=1)
    acc_sc[...] = jnp.zeros(acc_sc.shape, F32)
    ones = jnp.ones((ONES_ROWS, T), BF16)

    def bias_tile(j):
        return bt_ref[jnp.clip(j - ATT_QK_RATIO * i, -2, ATT_QK_RATIO + 1) + 2]

    qsq = qt.astype(F32) ** 2
    qn = jnp.sqrt(jnp.concatenate([jnp.sum(qsq[:QK_DIM], axis=0, keepdims=True),
                                   jnp.sum(qsq[QK_DIM:], axis=0, keepdims=True)], axis=1))
    kn2 = jnp.max(kn_ref[...], axis=0, keepdims=True)
    klane = lax.broadcasted_iota(jnp.int32, kn2.shape, 1)
    kmax = [jnp.sqrt(jnp.max(jnp.where(klane == 2 * h + mp, kn2, 0.0), axis=1, keepdims=True)) for mp in range(2)]
    first = lax.broadcasted_iota(jnp.int32, qn.shape, 1) < TQ
    reach = qn * jnp.where(first, kmax[0], kmax[1]) * BOUND_SLACK
    bmax = jnp.where(first, brange_ref[h, 0], brange_ref[h, 1])
    bmin = jnp.where(first, brange_ref[h, 2], brange_ref[h, 3])
    shift = (reach + bmax).astype(BF16)
    bounded = jnp.max(2.0 * reach + (bmax - bmin)) < EXP2_SPAN

    @pl.when(bounded)
    def _():
        aug = jnp.where(lax.broadcasted_iota(jnp.int32, (HEAD_W, 2 * TQ), 0) == 0,
                        jnp.broadcast_to(-shift.astype(F32), (HEAD_W, 2 * TQ)), 0.0)
        qaug = jnp.concatenate([q2t, aug.astype(BF16)], axis=0)
        lane = lax.broadcasted_iota(jnp.int32, (T, HEAD_W), 1)
        onescol = jnp.where(lane == 0, 1.0, 0.0).astype(BF16)
        ones4 = jnp.ones((ONES_ROWS, 4 * T), BF16)

        def quad(u, carry):
            for r in range(4):
                j = 4 * u + r
                kb = jnp.concatenate([k_ref[pl.ds(pl.multiple_of(j * T, T), T), :], onescol], axis=1)
                s = jnp.dot(kb, qaug, preferred_element_type=F32) + bias_tile(j)
                p_sc[r * T:(r + 1) * T, :] = jnp.exp2(s).astype(BF16)
            vb = jnp.concatenate([vt_ref[:, pl.ds(pl.multiple_of(4 * u * T, 4 * T), 4 * T)], ones4], axis=0)
            acc_sc[...] += jnp.dot(vb, p_sc[...], preferred_element_type=F32)
            return carry

        lax.fori_loop(0, nk // 4, quad, 0)

    @pl.when(jnp.logical_not(bounded))
    def _():
        def scores_softmax(j, slot, m):
            kb = k_ref[pl.ds(pl.multiple_of(j * T, T), T), :]
            s = jnp.dot(kb, q2t, preferred_element_type=F32) + bias_tile(j)
            m_new = jnp.maximum(m, jnp.max(s, axis=0, keepdims=True))
            p_sc[slot * T:(slot + 1) * T, :] = jnp.exp2(s - m_new).astype(BF16)
            return m_new, jnp.exp2(m - m_new)

        def accumulate(j, slot, alpha):
            vb = jnp.concatenate([vt_ref[:, pl.ds(pl.multiple_of(j * T, T), T)], ones], axis=0)
            acc_sc[...] = alpha * acc_sc[...] + jnp.dot(vb, p_sc[slot * T:(slot + 1) * T, :],
                                                        preferred_element_type=F32)

        m = jnp.full((1, 2 * TQ), -jnp.inf, F32)
        m, alpha = scores_softmax(0, 0, m)

        def pair(u, carry):
            m, alpha = carry
            t = 2 * u
            m, alpha1 = scores_softmax(t - 1, 1, m)
            accumulate(t - 2, 0, alpha)
            m, alpha0 = scores_softmax(t, 0, m)
            accumulate(t - 1, 1, alpha1)
            return m, alpha0

        m, alpha = lax.fori_loop(1, nk // 2, pair, (m, alpha))
        m, alpha1 = scores_softmax(nk - 1, 1, m)
        accumulate(nk - 2, 0, alpha)
        accumulate(nk - 1, 1, alpha1)

    o = acc_sc[0:HEAD_W, :] / acc_sc[HEAD_W:HEAD_W + 1, :]
    od = o[:, :TQ] - lam_ref[0] * o[:, TQ:]
    y = od * lax.rsqrt(jnp.mean(od * od, axis=0, keepdims=True) + EPS) * g_ref[...]
    o_ref[...] = (y * (1.0 - LAM_INIT)).T.astype(o_ref.dtype)


def _attention(qt, k, vt, kn, btiles, brange, lam, g_col, *, T=ATT_BLOCK):
    B, L, _ = k.shape
    TQ = ATT_QK_RATIO * T
    nk = L // T
    assert nk % 4 == 0 and L % TQ == 0
    return pl.pallas_call(
        functools.partial(_attn_kernel, TQ=TQ, T=T, nk=nk),
        grid=(B, N_HEADS, L // TQ),
        in_specs=[pl.BlockSpec(memory_space=pltpu.SMEM),
                  pl.BlockSpec(memory_space=pltpu.SMEM),
                  pl.BlockSpec((None, HEAD_W, TQ), lambda b, h, i: (b, h, i)),
                  pl.BlockSpec((None, L, HEAD_W), lambda b, h, i: (b, 0, h)),
                  pl.BlockSpec((None, HEAD_W, L), lambda b, h, i: (b, h, 0)),
                  pl.BlockSpec((None, ATT_BIAS_TILES, T, 2 * TQ), lambda b, h, i: (h, 0, 0, 0)),
                  pl.BlockSpec((None,) + kn.shape[1:], lambda b, h, i: (b, 0, 0)),
                  _const_spec((HEAD_W, 1))],
        out_specs=pl.BlockSpec((None, TQ, HEAD_W), lambda b, h, i: (b, i, h)),
        out_shape=jax.ShapeDtypeStruct((B, L, ATT_W), BF16),
        scratch_shapes=[pltpu.VMEM((HEAD_W + ONES_ROWS, 2 * TQ), F32), pltpu.VMEM((4 * T, 2 * TQ), BF16)],
        compiler_params=_params(("parallel", "parallel", "parallel")),
        name="diff_attention",
    )(lam, brange, qt, k, vt, btiles, kn, g_col)


def _conv_kernel(x_ref, prev_ref, next_ref, dt_ref, dtt_ref, w_ref, b_ref, dtb_ref, dtbt_ref,
                 xs_ref, bc_ref, dts_ref, dtst_ref, *, tl, nt):
    i = pl.program_id(1)
    R = CONV_ROWS
    row = lax.broadcasted_iota(jnp.int32, (R, 128), 0)
    for r0 in range(0, tl, R):
        for c0 in range(0, CONV_CH, 128):
            cols = slice(c0, c0 + 128)
            x = x_ref[r0:r0 + R, cols]
            above = x_ref[r0 - 1:r0, cols] if r0 > 0 else jnp.where(i > 0, prev_ref[7:8, cols], 0.0)
            below = x_ref[r0 + R:r0 + R + 1, cols] if r0 + R < tl else jnp.where(i < nt - 1, next_ref[0:1, cols], 0.0)
            xm1 = jnp.where(row == 0, above, pltpu.roll(x, 1, 0))
            xp1 = jnp.where(row == R - 1, below, pltpu.roll(x, R - 1, 0))
            y = w_ref[0:1, cols] * xm1 + w_ref[1:2, cols] * x + w_ref[2:3, cols] * xp1 + b_ref[:, cols]
            y = (y * jax.nn.sigmoid(y)).astype(BF16)
            if c0 < SSM_INNER:
                xs_ref[r0:r0 + R, cols] = y
            else:
                bc_ref[r0:r0 + R, c0 - SSM_INNER:c0 - SSM_INNER + 128] = y

    def softplus(v):
        return jnp.maximum(v, 0.0) + jnp.log1p(jnp.exp(-jnp.abs(v)))

    dts_ref[...] = softplus(dt_ref[...] + dtb_ref[...])
    dtst_ref[...] = softplus(dtt_ref[...] + dtbt_ref[...])


def _conv(xbc, dt, dtt, conv_w, conv_b, dtb, dtbt, *, tl=512):
    B, L, _ = xbc.shape
    nt = L // tl
    r8 = tl // 8
    ndt = 2 * SSM_HEADS
    row = lambda w: pl.BlockSpec((None, tl, w), lambda b, i: (b, i, 0))
    colspec = pl.BlockSpec((None, ndt, tl), lambda b, i: (b, 0, i))
    return pl.pallas_call(
        functools.partial(_conv_kernel, tl=tl, nt=nt),
        grid=(B, nt),
        in_specs=[row(CONV_CH),
                  pl.BlockSpec((None, 8, CONV_CH), lambda b, i: (b, jnp.maximum(i * r8 - 1, 0), 0)),
                  pl.BlockSpec((None, 8, CONV_CH), lambda b, i: (b, jnp.minimum((i + 1) * r8, L // 8 - 1), 0)),
                  row(ndt), colspec,
                  _const_spec((3, CONV_CH)), _const_spec((1, CONV_CH)), _const_spec((1, ndt)),
                  _const_spec((ndt, 1))],
        out_specs=[row(SSM_INNER), row(2 * SSM_GROUPS * SSM_N), row(ndt), colspec],
        out_shape=[jax.ShapeDtypeStruct((B, L, SSM_INNER), BF16),
                   jax.ShapeDtypeStruct((B, L, 2 * SSM_GROUPS * SSM_N), BF16),
                   jax.ShapeDtypeStruct((B, L, ndt), F32),
                   jax.ShapeDtypeStruct((B, ndt, L), F32)],
        compiler_params=_params(("parallel", "parallel")),
        name="ssm_conv",
    )(xbc, xbc, xbc, dt, dtt, conv_w, conv_b, dtb, dtbt)


def _ssd_chunk(xs_ref, bc_ref, dts_ref, dtst_ref, alog_ref, alogt_ref, st_ref, *, rev):
    Q = SSM_CHUNK
    HI = lax.Precision.HIGHEST
    off = SSM_HEADS if rev else 0
    ri = lax.broadcasted_iota(jnp.int32, (Q, Q), 0)
    ci = lax.broadcasted_iota(jnp.int32, (Q, Q), 1)
    incl = (ci >= ri) if rev else (ci <= ri)
    tri = incl.astype(F32)
    dt = dts_ref[:, off:off + SSM_HEADS]
    dtt = dtst_ref[off:off + SSM_HEADS, :]
    a = dt * (-jnp.exp(alog_ref[:, off:off + SSM_HEADS]))
    at = dtt * (-jnp.exp(alogt_ref[off:off + SSM_HEADS, :]))
    acum = jnp.dot(tri, a, preferred_element_type=F32, precision=HI)
    acumt = _nt_dot(at, tri, precision=HI)
    edge = 0 if rev else Q - 1
    bc = bc_ref[...]
    bpair = bc[:, :Q]
    cpair = bc[:, Q:]
    low = ci < SSM_N
    rlow = ri < SSM_N
    blockmask = low == rlow
    total = acumt[:, edge:edge + 1]
    decay = jnp.exp(total)
    w_rows = dtt * jnp.exp(total - acumt)
    ys = []
    for g in range(SSM_GROUPS):
        gmask = low if g == 0 else jnp.logical_not(low)
        cg = jnp.where(gmask, cpair, jnp.zeros_like(cpair))
        cb = _nt_dot(cg, bpair)
        dupm = ((ri - g * SSM_N) == (ci % SSM_N)).astype(BF16)
        cdup = jnp.dot(cpair, dupm, preferred_element_type=F32)
        sel2 = ((ci - g * SSM_N) == (ri % SSM_N)).astype(BF16)
        bt2 = _nt_dot(sel2, bpair)
        for hp in range(SSM_HEADS // SSM_GROUPS // 2):
            h0 = g * (SSM_HEADS // SSM_GROUPS) + 2 * hp
            pair = h0 // 2
            xpair = xs_ref[:, h0 * SSM_P:(h0 + 2) * SSM_P]
            ms = []
            e_cols = []
            for h in (h0, h0 + 1):
                e_col = jnp.broadcast_to(acum[:, h:h + 1], (Q, Q))
                r_row = jnp.broadcast_to(acumt[h:h + 1, :], (Q, Q))
                seg = jnp.where(incl, e_col - r_row, -jnp.inf)
                ms.append((cb * jnp.exp(seg) * jnp.broadcast_to(dtt[h:h + 1, :], (Q, Q))).astype(BF16))
                e_cols.append(e_col)
            st = st_ref[pair]
            xblk = jnp.concatenate([jnp.where(low, xpair, jnp.zeros_like(xpair)),
                                    jnp.where(low, jnp.zeros_like(xpair), xpair)], axis=0)
            y = jnp.dot(jnp.concatenate(ms, axis=1), xblk, preferred_element_type=F32)
            ce = (cdup * jnp.exp(jnp.where(low, e_cols[0], e_cols[1]))).astype(BF16)
            y = y + jnp.dot(ce, st.astype(BF16), preferred_element_type=F32)
            ys.append(y)
            w_pair = jnp.where(rlow, jnp.broadcast_to(w_rows[h0:h0 + 1, :], (Q, Q)),
                               jnp.broadcast_to(w_rows[h0 + 1:h0 + 2, :], (Q, Q)))
            snew = jnp.dot((bt2 * w_pair).astype(BF16), xpair, preferred_element_type=F32)
            dec = jnp.where(rlow, decay[h0:h0 + 1, :], decay[h0 + 1:h0 + 2, :])
            st_ref[pair] = dec * st + jnp.where(blockmask, snew, 0.0)
    return jnp.concatenate(ys, axis=1)


def _ssd_fwd_kernel(xs_ref, bc_ref, dts_ref, dtst_ref, alog_ref, alogt_ref, y_ref, st_ref):
    @pl.when(pl.program_id(1) == 0)
    def _():
        st_ref[...] = jnp.zeros(st_ref.shape, F32)

    y = _ssd_chunk(xs_ref, bc_ref, dts_ref, dtst_ref, alog_ref, alogt_ref, st_ref, rev=False)
    y_ref[...] = y.astype(y_ref.dtype)


def _ssd_bwd_kernel(xs_ref, bc_ref, dts_ref, dtst_ref, alog_ref, alogt_ref, yf_ref, z_ref, dskip_ref, g_ref,
                    o_ref, st_ref):
    @pl.when(pl.program_id(1) == 0)
    def _():
        st_ref[...] = jnp.zeros(st_ref.shape, F32)

    yb = _ssd_chunk(xs_ref, bc_ref, dts_ref, dtst_ref, alog_ref, alogt_ref, st_ref, rev=True)
    z = z_ref[...].astype(F32)
    y = (yf_ref[...].astype(F32) + yb + xs_ref[...].astype(F32) * dskip_ref[...]) * (z * jax.nn.sigmoid(z))
    y = y * lax.rsqrt(jnp.mean(y * y, axis=-1, keepdims=True) + EPS) * g_ref[...]
    o_ref[...] = y.astype(o_ref.dtype)


def _ssd(xs, bc, dts, dtst, alog, alogt, z, dskip, g_norm):
    B, L, _ = xs.shape
    Q = SSM_CHUNK
    nc = L // Q
    ndt = 2 * SSM_HEADS
    npair = SSM_HEADS // 2

    def specs(cidx):
        row = lambda w: pl.BlockSpec((None, Q, w), lambda b, c: (b, cidx(c), 0))
        return row, [row(SSM_INNER), row(2 * SSM_GROUPS * SSM_N), row(ndt),
                     pl.BlockSpec((None, ndt, Q), lambda b, c: (b, 0, cidx(c))),
                     _const_spec((1, ndt)), _const_spec((ndt, 1))]

    scratch = [pltpu.VMEM((npair, 2 * SSM_N, 2 * SSM_P), F32)]
    row, in_specs = specs(lambda c: c)
    yf = pl.pallas_call(
        _ssd_fwd_kernel,
        grid=(B, nc),
        in_specs=in_specs,
        out_specs=row(SSM_INNER),
        out_shape=jax.ShapeDtypeStruct((B, L, SSM_INNER), BF16),
        scratch_shapes=scratch,
        compiler_params=_params(("parallel", "arbitrary")),
        name="ssd_forward",
    )(xs, bc, dts, dtst, alog, alogt)
    row, in_specs = specs(lambda c: nc - 1 - c)
    return pl.pallas_call(
        _ssd_bwd_kernel,
        grid=(B, nc),
        in_specs=in_specs + [row(SSM_INNER), row(SSM_INNER), _const_spec((1, SSM_INNER)),
                             _const_spec((1, SSM_INNER))],
        out_specs=row(SSM_INNER),
        out_shape=jax.ShapeDtypeStruct((B, L, SSM_INNER), BF16),
        scratch_shapes=scratch,
        compiler_params=_params(("parallel", "arbitrary")),
        name="ssd_reverse_gate_norm",
    )(xs, bc, dts, dtst, alog, alogt, yf, z, dskip, g_norm)


def _merge_kernel(att_ref, ssm_ref, gates_ref, x_ref, wau_ref, wsu_ref, wout_ref, gffn_ref, wr_ref, wrt_ref,
                  place_ref, h_ref, u2_ref, afft_ref):
    D = D_MODEL
    E = N_EXPERTS
    ya = jnp.dot(att_ref[...], wau_ref[...], preferred_element_type=F32)
    ysm = jnp.dot(ssm_ref[...], wsu_ref[...], preferred_element_type=F32)
    merged = gates_ref[:, :D] * ya + gates_ref[:, D:] * ysm
    h = x_ref[...] + jnp.dot(merged.astype(BF16), wout_ref[...], preferred_element_type=F32)
    h_ref[...] = h
    u2 = (h * lax.rsqrt(jnp.mean(h * h, axis=-1, keepdims=True) + EPS) * gffn_ref[...]).astype(BF16)
    u2_ref[:, :D] = u2
    logits = jnp.dot(u2, wr_ref[...], preferred_element_type=F32)
    e = jnp.exp(logits - jnp.max(logits, axis=-1, keepdims=True))
    aff = e / jnp.sum(e, axis=-1, keepdims=True)
    hi = aff.astype(BF16)
    r1 = aff - hi.astype(F32)
    mid = r1.astype(BF16)
    lo = (r1 - mid.astype(F32)).astype(BF16)
    ext = (jnp.dot(hi, place_ref[0:E, :], preferred_element_type=F32)
           + jnp.dot(mid, place_ref[E:2 * E, :], preferred_element_type=F32)
           + jnp.dot(lo, place_ref[2 * E:3 * E, :], preferred_element_type=F32))
    u2_ref[:, D:] = ext.astype(BF16)
    lt = _nt_dot(wrt_ref[...], u2)
    et = jnp.exp(lt - jnp.max(lt, axis=0, keepdims=True))
    afft_ref[...] = et / jnp.sum(et, axis=0, keepdims=True)


def _merge(att, ssm, gates, x, wau, wsu, wout, g_ffn, wr, *, tm=256):
    B, L, D = x.shape
    E = N_EXPERTS
    nl = L // tm
    row = lambda w: pl.BlockSpec((None, tm, w), lambda b, i: (b, i, 0))
    place = jnp.eye(3 * E, GATE_EXT, dtype=BF16)
    return pl.pallas_call(
        _merge_kernel,
        grid=(B, nl),
        in_specs=[row(ATT_W), row(SSM_INNER), row(2 * D), row(D), _const_spec(wau.shape),
                  _const_spec(wsu.shape), _const_spec(wout.shape), _const_spec((1, D)), _const_spec(wr.shape),
                  _const_spec((E, D)), _const_spec((3 * E, GATE_EXT))],
        out_specs=[row(D), row(D + GATE_EXT), pl.BlockSpec((E, tm), lambda b, i: (0, b * nl + i))],
        out_shape=[jax.ShapeDtypeStruct((B, L, D), F32), jax.ShapeDtypeStruct((B, L, D + GATE_EXT), BF16),
                   jax.ShapeDtypeStruct((E, B * L), F32)],
        compiler_params=_params(("parallel", "parallel")),
        name="merge_router",
    )(att, ssm, gates, x, wau, wsu, wout, g_ffn, wr, wr.T, place)


def _route_kernel(afft_ref, posm_ref, bstart_ref, *, cap, nb):
    E = N_EXPERTS
    W = ROUTE_BLK

    def bit_body(k, thr):
        cand = thr | lax.shift_left(jnp.int32(1), 30 - k)
        x = lax.bitcast_convert_type(afft_ref[...], jnp.int32)
        cnt = jnp.sum(jnp.where(x >= cand, 1.0, 0.0), axis=1, keepdims=True)
        return jnp.where(cnt >= cap, cand, thr)

    thr = lax.fori_loop(0, 31, bit_body, jnp.zeros((E, 1), jnp.int32))
    x = lax.bitcast_convert_type(afft_ref[...], jnp.int32)
    n_gt = jnp.sum(jnp.where(x > thr, 1.0, 0.0), axis=1, keepdims=True)
    need = cap - n_gt
    ri = lax.broadcasted_iota(jnp.int32, (W, W), 0)
    ci = lax.broadcasted_iota(jnp.int32, (W, W), 1)
    upper = jnp.where(ri <= ci, 1.0, 0.0).astype(BF16)
    bstart_ref[...] = jnp.zeros(bstart_ref.shape, jnp.int32)
    c_eq = jnp.zeros((E, 1), F32)
    c_sel = jnp.zeros((E, 1), F32)
    for b in range(nb):
        xb = lax.bitcast_convert_type(afft_ref[:, b * W:(b + 1) * W], jnp.int32)
        eq = xb == thr
        eqc = jnp.dot(jnp.where(eq, 1.0, 0.0).astype(BF16), upper, preferred_element_type=F32) + c_eq
        sel = (xb > thr) | (eq & (eqc <= need))
        selc = jnp.dot(jnp.where(sel, 1.0, 0.0).astype(BF16), upper, preferred_element_type=F32) + c_sel
        posm_ref[:, b * W:(b + 1) * W] = jnp.where(sel, selc, 0.0).astype(jnp.int32)
        bstart_ref[:, b:b + 1] = c_sel.astype(jnp.int32)
        c_eq = eqc[:, W - 1:W]
        c_sel = selc[:, W - 1:W]
    bstart_ref[:, nb:nb + 1] = c_sel.astype(jnp.int32)


def _route(afft, cap):
    E, T = afft.shape
    nb = T // ROUTE_BLK
    assert nb < BSTART_W and cap % SLOT_TILE == 0
    return pl.pallas_call(
        functools.partial(_route_kernel, cap=float(cap), nb=nb),
        out_shape=[jax.ShapeDtypeStruct((E, T), jnp.int32), jax.ShapeDtypeStruct((E, BSTART_W), jnp.int32)],
        compiler_params=_params(None),
        name="route_select",
    )(afft)


def _pair_schedule(bstart, cap, ngb):
    step = GATHER_BLK // ROUTE_BLK
    st = bstart[:, 0:ngb * step + 1:step]
    lo, hi = st[:, :-1], st[:, 1:]
    ntile = cap // SLOT_TILE
    t_lo = jnp.arange(ntile, dtype=jnp.int32) * SLOT_TILE
    t_hi = t_lo + SLOT_TILE
    b_first = jnp.sum(hi[:, None, :] <= t_lo[None, :, None], axis=2).astype(jnp.int32)
    b_last = jnp.sum(lo[:, None, :] < t_hi[None, :, None], axis=2).astype(jnp.int32) - 1
    return b_first, b_last - b_first + 1


def _expert_kernel(first_ref, cnt_ref, u_hbm, posm_ref, wg_ref, wu_ref, wd_ref, o_ref, x_sc, ubuf, sem):
    D = D_MODEL
    e = pl.program_id(0)
    j = pl.program_id(1)
    g = e * pl.num_programs(1) + j
    b0 = first_ref[g]
    n = cnt_ref[g]

    half = GATHER_BLK // 2

    def half_copies(b, slot):
        return [pltpu.make_async_copy(
            u_hbm.at[pl.ds(pl.multiple_of(b * GATHER_BLK + r * half, half), half)],
            ubuf.at[slot, pl.ds(r * half, half)], sem.at[slot, r]) for r in range(2)]

    def start(b, slot):
        for cp in half_copies(b, slot):
            cp.start()

    ahead = GATHER_BUFS - 1

    def start_tile(t):
        start(first_ref[t], 0)
        for r in range(1, ahead):
            @pl.when(cnt_ref[t] > r)
            def _(r=r):
                start(first_ref[t] + r, r)

    @pl.when(g == 0)
    def _():
        start_tile(0)

    x_sc[...] = jnp.zeros(x_sc.shape, F32)
    slot_ids = lax.broadcasted_iota(jnp.int32, (SLOT_TILE, GATHER_BLK), 0) + (j * SLOT_TILE + 1)

    def body(c, carry):
        slot = c % GATHER_BUFS
        for cp in half_copies(b0 + c, slot):
            cp.wait()

        @pl.when(c + ahead < n)
        def _():
            start(b0 + c + ahead, (c + ahead) % GATHER_BUFS)

        posrow = posm_ref[:, pl.ds(pl.multiple_of((b0 + c) * GATHER_BLK, GATHER_BLK), GATHER_BLK)]
        onehot = jnp.where(posrow == slot_ids, 1.0, 0.0).astype(BF16)
        x_sc[...] += jnp.dot(onehot, ubuf[slot], preferred_element_type=F32)
        return carry

    lax.fori_loop(0, n, body, 0)

    @pl.when(g + 1 < pl.num_programs(0) * pl.num_programs(1))
    def _():
        start_tile(g + 1)

    xs = x_sc[:, :D].astype(BF16)
    ext = x_sc[:, D:]
    lane = lax.broadcasted_iota(jnp.int32, ext.shape, 1)
    mine = (lane == e) | (lane == e + N_EXPERTS) | (lane == e + 2 * N_EXPERTS)
    gate = jnp.sum(jnp.where(mine, ext, 0.0), axis=1, keepdims=True)
    hg = jnp.dot(xs, wg_ref[...], preferred_element_type=F32)
    hu = jnp.dot(xs, wu_ref[...], preferred_element_type=F32)
    hmid = (hg * jax.nn.sigmoid(hg) * hu).astype(BF16)
    o_ref[...] = jnp.dot(hmid, wd_ref[...], preferred_element_type=F32) * gate


def _experts(u2ext, posm, b_first, b_cnt, wg, wu, wd, cap):
    T, DX = u2ext.shape
    E, D = N_EXPERTS, D_MODEL
    wspec = pl.BlockSpec((None, D, D), lambda e, j, bf, bc: (e, 0, 0))
    return pl.pallas_call(
        _expert_kernel,
        grid_spec=pltpu.PrefetchScalarGridSpec(
            num_scalar_prefetch=2,
            grid=(E, cap // SLOT_TILE),
            in_specs=[pl.BlockSpec(memory_space=pl.ANY),
                      pl.BlockSpec((None, 1, T), lambda e, j, bf, bc: (e, 0, 0)),
                      wspec, wspec, wspec],
            out_specs=pl.BlockSpec((None, SLOT_TILE, D), lambda e, j, bf, bc: (e, j, 0)),
            scratch_shapes=[pltpu.VMEM((SLOT_TILE, DX), F32), pltpu.VMEM((GATHER_BUFS, GATHER_BLK, DX), BF16),
                            pltpu.SemaphoreType.DMA((GATHER_BUFS, 2))]),
        out_shape=jax.ShapeDtypeStruct((E, cap, D), F32),
        compiler_params=_params(("arbitrary", "arbitrary")),
        name="experts_gather_mlp",
    )(b_first.reshape(-1), b_cnt.reshape(-1), u2ext, posm.reshape(E, 1, T), wg, wu, wd)


def _combine_kernel(bstart_ref, posm_ref, ye_ref, h_ref, p_ref, wpg_ref, bpg_ref, wpp_ref, gfin_ref, o_ref,
                    moe, buf, xbuf, sem, xsem, *, cap, nb):
    E = N_EXPERTS
    WN = COMBINE_WIN
    b = pl.program_id(0)
    top = cap - WN
    cur = b % 2

    def window(blk, slot, e):
        base = jnp.minimum((bstart_ref[e, blk] // 8) * 8, top)
        return pltpu.make_async_copy(ye_ref.at[e, pl.ds(pl.multiple_of(base, 8), WN)],
                                     buf.at[slot, pl.ds(e * WN, WN)], sem.at[slot, e])

    @pl.when(b == 0)
    def _():
        for e in range(E):
            window(0, 0, e).start()

    @pl.when(b + 1 < nb)
    def _():
        for e in range(E):
            window(b + 1, 1 - cur, e).start()

    lows = [(bstart_ref[e, b] // 8) * 8 for e in range(E)]
    w_iota = lax.broadcasted_iota(jnp.int32, (WN, ROUTE_BLK), 0)

    def onehot_rows(slot0, low, base):
        ok = (slot0 >= low) & (slot0 < low + WN)
        return jnp.where(ok & (slot0 - base == w_iota), 1.0, 0.0).astype(BF16)

    def scatter(onehot, rows):
        hi = rows.astype(BF16)
        lo = (rows - hi.astype(F32)).astype(BF16)
        tn = lambda a, c: lax.dot_general(a, c, (((0,), (0,)), ((), ())), preferred_element_type=F32)
        return tn(onehot, hi) + tn(onehot, lo)

    hots = []
    for e in range(E):
        slot0 = posm_ref[e:e + 1, :] - 1
        hots.append(onehot_rows(slot0, lows[e], jnp.minimum(lows[e], top)))
    for e in range(E):
        window(b, cur, e).wait()
    moe[...] = scatter(jnp.concatenate(hots, axis=0), buf[cur])

    def per_expert(e, carry):
        low0 = (bstart_ref[e, b] // 8) * 8
        nwin = (bstart_ref[e, b + 1] - low0 + WN - 1) // WN

        def per_window(c, carry):
            low = low0 + c * WN
            base = jnp.minimum(low, top)
            cp = pltpu.make_async_copy(ye_ref.at[e, pl.ds(pl.multiple_of(base, 8), WN)], xbuf, xsem.at[0])
            cp.start()
            cp.wait()
            mine = lax.broadcasted_iota(jnp.int32, (E, ROUTE_BLK), 0) == e
            slot0 = jnp.sum(jnp.where(mine, posm_ref[...], 0), axis=0, keepdims=True) - 1
            moe[...] += scatter(onehot_rows(slot0, low, base), xbuf[...])
            return carry

        return lax.fori_loop(1, nwin, per_window, carry)

    lax.fori_loop(0, E, per_expert, 0)

    h = h_ref[...] + moe[...]
    gate = jax.nn.sigmoid(jnp.dot(h.astype(BF16), wpg_ref[...], preferred_element_type=F32) + bpg_ref[...])
    h = h + gate * jnp.dot(p_ref[...].astype(BF16), wpp_ref[...], preferred_element_type=F32)
    o_ref[...] = h * lax.rsqrt(jnp.mean(h * h, axis=-1, keepdims=True) + EPS) * gfin_ref[...]


def _combine_ple(ye, posm, bstart, cap, h, p, wpg, bpg, wpp, g_final):
    E, T = posm.shape
    D = D_MODEL
    nb = T // ROUTE_BLK
    assert cap >= COMBINE_WIN
    row = lambda w: pl.BlockSpec((ROUTE_BLK, w), lambda b, bs: (b, 0))
    const = lambda shape: pl.BlockSpec(shape, lambda b, bs: (0,) * len(shape))
    return pl.pallas_call(
        functools.partial(_combine_kernel, cap=cap, nb=nb),
        grid_spec=pltpu.PrefetchScalarGridSpec(
            num_scalar_prefetch=1,
            grid=(nb,),
            in_specs=[pl.BlockSpec((E, ROUTE_BLK), lambda b, bs: (0, b)),
                      pl.BlockSpec(memory_space=pl.ANY), row(D), row(PLE_DIM), const(wpg.shape), const((1, D)),
                      const(wpp.shape), const((1, D))],
            out_specs=row(D),
            scratch_shapes=[pltpu.VMEM((ROUTE_BLK, D), F32), pltpu.VMEM((2, E * COMBINE_WIN, D), F32),
                            pltpu.VMEM((COMBINE_WIN, D), F32),
                            pltpu.SemaphoreType.DMA((2, E)), pltpu.SemaphoreType.DMA((1,))]),
        out_shape=jax.ShapeDtypeStruct((T, D), F32),
        compiler_params=_params(("arbitrary",)),
        name="experts_combine_ple",
    )(bstart, posm, ye, h.reshape(T, D), p.reshape(T, PLE_DIM), wpg, bpg, wpp, g_final)


def _forward(x, p, w, btiles):
    B, L, D = x.shape
    T = B * L
    qt, k, vt, z, xbc, dt, dtt, gates, kn = _inproj(x, w['g_mix'], w['wqt'], w['wk'], w['wvt'], w['wz'],
                                                    w['wxbc'], w['wdt'], w['wdtt'], w['wgate'], w['bgate'])
    att = _attention(qt, k, vt, kn, btiles, w['brange'], w['lam'], w['g_subln'])
    xs, bc, dts, dtst = _conv(xbc, dt, dtt, w['conv_w'], w['conv_b'], w['dtb'], w['dtbt'])
    ssm = _ssd(xs, bc, dts, dtst, w['alog'], w['alogt'], z, w['dskip'], w['g_ssm_norm'])
    h, u2ext, afft = _merge(att, ssm, gates, x, w['wau'], w['wsu'], w['wout'], w['g_ffn'], w['wr'])
    cap = max(1, CAPACITY_FACTOR * T // N_EXPERTS)
    posm, bstart = _route(afft, cap)
    b_first, b_cnt = _pair_schedule(bstart, cap, T // GATHER_BLK)
    ye = _experts(u2ext.reshape(T, D + GATE_EXT), posm, b_first, b_cnt, w['weg'], w['weu'], w['wed'], cap)
    out = _combine_ple(ye, posm, bstart, cap, h, p, w['wpg'], w['bpg'], w['wpp'], w['g_final'])
    return out.reshape(B, L, D)


def kernel(x_prompt, x_sample, p_prompt, p_sample, rel_bias, g_mix, w_in, conv_w, conv_b, dt_bias, a_log,
           d_skip, g_ssm_norm, lam_params, g_subln, w_att_up, w_ssm_up, w_gate, b_gate, w_out, g_ffn,
           w_router, w_exp_gate, w_exp_up, w_exp_down, w_ple_gate, b_ple_gate, w_ple_proj, g_final):
    i = 0
    D = D_MODEL
    win = w_in[i]
    o1 = ATT_W
    o3 = 3 * ATT_W
    o4 = o3 + SSM_INNER
    o5 = o4 + CONV_CH
    wqt = (win[:, :o1] * (QK_DIM ** -0.5 * LOG2E)).T.astype(BF16)
    wk = win[:, o1:2 * o1].astype(BF16)
    wvt = win[:, 2 * o1:o3].T.astype(BF16)
    wdt = win[:, o5:].astype(BF16)
    lp = lam_params[i].astype(F32)
    lam = jnp.exp(jnp.sum(lp[0] * lp[1])) - jnp.exp(jnp.sum(lp[2] * lp[3])) + LAM_INIT
    ndt = 2 * SSM_HEADS
    tb = rel_bias.astype(F32).reshape(REL_BUCKETS, N_HEADS, 2) * LOG2E
    brange = jnp.concatenate([jnp.max(tb, axis=0), jnp.min(tb, axis=0)], axis=1)
    w = {
        'g_mix': g_mix[i].reshape(1, D), 'wqt': wqt, 'wk': wk, 'wvt': wvt, 'wz': win[:, o3:o4].astype(BF16),
        'wxbc': win[:, o4:o5].astype(BF16), 'wdt': wdt, 'wdtt': wdt.T,
        'wgate': w_gate[i].astype(BF16), 'bgate': b_gate[i].reshape(1, 2 * D),
        'lam': lam.reshape(1), 'g_subln': g_subln[i].reshape(HEAD_W, 1), 'brange': brange,
        'conv_w': conv_w[i], 'conv_b': conv_b[i].reshape(1, CONV_CH),
        'dtb': dt_bias[i].reshape(1, ndt), 'dtbt': dt_bias[i].reshape(ndt, 1),
        'alog': a_log[i].reshape(1, ndt), 'alogt': a_log[i].reshape(ndt, 1),
        'dskip': jnp.repeat(d_skip[i], SSM_P).reshape(1, SSM_INNER),
        'g_ssm_norm': g_ssm_norm[i].reshape(1, SSM_INNER),
        'wau': w_att_up[i].astype(BF16), 'wsu': w_ssm_up[i].astype(BF16), 'wout': w_out[i].astype(BF16),
        'g_ffn': g_ffn[i].reshape(1, D), 'wr': w_router[i].astype(BF16),
        'weg': w_exp_gate[i].astype(BF16), 'weu': w_exp_up[i].astype(BF16), 'wed': w_exp_down[i].astype(BF16),
        'wpg': w_ple_gate[i].astype(BF16), 'bpg': b_ple_gate[i].reshape(1, D),
        'wpp': w_ple_proj[i].astype(BF16), 'g_final': g_final.reshape(1, D),
    }
    btiles = _bias_tiles(rel_bias, ATT_QK_RATIO * ATT_BLOCK, ATT_BLOCK)
    return (_forward(x_prompt, p_prompt[i], w, btiles), _forward(x_sample, p_sample[i], w, btiles))
```

```python
import functools
import math

import jax
import jax.numpy as jnp
from jax import lax
from jax.experimental import pallas as pl
from jax.experimental.pallas import tpu as pltpu

F32 = jnp.float32
BF16 = jnp.bfloat16

D_MODEL = 1024
N_HEADS = 4
QK_DIM = 64
HEAD_W = 2 * QK_DIM
ATT_W = N_HEADS * HEAD_W
REL_BUCKETS = 32
REL_MAX_DIST = 128
SSM_INNER = 1024
SSM_P = 64
SSM_HEADS = 16
SSM_GROUPS = 2
SSM_N = 64
SSM_CHUNK = 128
CONV_CH = SSM_INNER + 2 * SSM_GROUPS * SSM_N
N_EXPERTS = 16
CAPACITY_FACTOR = 2
PLE_DIM = 256
EPS = 1e-6
LAM_INIT = 0.8 - 0.6 * math.exp(-0.3 * 0)
LOG2E = math.log2(math.e)

GATE_EXT = 128
ROUTE_BLK = 256
GATHER_BLK = 512
GATHER_BUFS = 5
SLOT_TILE = 256
GATHER_ROWS = 128
COMBINE_WIN = 72
BSTART_W = 256

CONV_ROWS = 128

ATT_BLOCK = 256
ATT_QK_RATIO = 4
ATT_BIAS_TILES = ATT_QK_RATIO + 4
ONES_ROWS = 16
BOUND_SLACK = 1.0 + 2.0 ** -6
EXP2_SPAN = 100.0
VMEM_LIMIT = 56 * 1024 * 1024


def _params(sem, vmem=VMEM_LIMIT):
    return pltpu.CompilerParams(dimension_semantics=sem, vmem_limit_bytes=vmem)


def _const_spec(shape):
    nd = len(shape)
    return pl.BlockSpec(shape, lambda *_: (0,) * nd)


def _nt_dot(a, b, **kw):
    return lax.dot_general(a, b, (((1,), (1,)), ((), ())), preferred_element_type=F32, **kw)


def _inproj_kernel(x_ref, g_ref, wqt_ref, wk_ref, wvt_ref, wz_ref, wxbc_ref, wdt_ref, wdtt_ref, wgate_ref,
                   bgate_ref, ind_ref, qt_ref, k_ref, vt_ref, z_ref, xbc_ref, dt_ref, dtt_ref, gates_ref, kn_ref):
    x = x_ref[...]
    u = x * lax.rsqrt(jnp.mean(x * x, axis=-1, keepdims=True) + EPS) * g_ref[...]
    ub = u.astype(BF16)
    qt_ref[...] = _nt_dot(wqt_ref[...], ub).astype(BF16)
    kb = jnp.dot(ub, wk_ref[...], preferred_element_type=F32).astype(BF16)
    k_ref[...] = kb
    ksq = (kb.astype(F32) ** 2).astype(BF16)
    kn_ref[...] = jnp.broadcast_to(
        jnp.max(jnp.dot(ksq, ind_ref[...], preferred_element_type=F32), axis=0, keepdims=True), kn_ref.shape)
    vt_ref[...] = _nt_dot(wvt_ref[...], ub).astype(BF16)
    z_ref[...] = jnp.dot(ub, wz_ref[...], preferred_element_type=F32).astype(z_ref.dtype)
    xbc_ref[...] = jnp.dot(ub, wxbc_ref[...], preferred_element_type=F32)
    dt_ref[...] = jnp.dot(ub, wdt_ref[...], preferred_element_type=F32)
    dtt_ref[...] = _nt_dot(wdtt_ref[...], ub)
    gates = jax.nn.sigmoid(jnp.dot(ub, wgate_ref[...], preferred_element_type=F32) + bgate_ref[...])
    gates_ref[...] = gates.astype(gates_ref.dtype)


def _inproj(x, g_mix, wqt, wk, wvt, wz, wxbc, wdt, wdtt, wgate, bgate, *, tm=256):
    B, L, D = x.shape
    grid = (B, L // tm)
    row = lambda w: pl.BlockSpec((None, tm, w), lambda b, i: (b, i, 0))
    col = lambda w: pl.BlockSpec((None, w, tm), lambda b, i: (b, 0, i))
    ndt = 2 * SSM_HEADS
    ind = (jnp.arange(ATT_W)[:, None] // QK_DIM == jnp.arange(HEAD_W)[None, :]).astype(BF16)
    return pl.pallas_call(
        _inproj_kernel,
        grid=grid,
        in_specs=[row(D), _const_spec((1, D)), _const_spec(wqt.shape), _const_spec(wk.shape),
                  _const_spec(wvt.shape), _const_spec(wz.shape),
                  _const_spec(wxbc.shape), _const_spec(wdt.shape), _const_spec(wdtt.shape),
                  _const_spec(wgate.shape), _const_spec((1, 2 * D)), _const_spec((ATT_W, HEAD_W))],
        out_specs=[col(ATT_W), row(ATT_W), col(ATT_W), row(SSM_INNER), row(CONV_CH), row(ndt),
                   col(ndt), row(2 * D), pl.BlockSpec((None, 8, HEAD_W), lambda b, i: (b, i, 0))],
        out_shape=[jax.ShapeDtypeStruct((B, ATT_W, L), BF16),
                   jax.ShapeDtypeStruct((B, L, ATT_W), BF16),
                   jax.ShapeDtypeStruct((B, ATT_W, L), BF16),
                   jax.ShapeDtypeStruct((B, L, SSM_INNER), BF16),
                   jax.ShapeDtypeStruct((B, L, CONV_CH), F32),
                   jax.ShapeDtypeStruct((B, L, ndt), F32),
                   jax.ShapeDtypeStruct((B, ndt, L), F32),
                   jax.ShapeDtypeStruct((B, L, 2 * D), BF16),
                   jax.ShapeDtypeStruct((B, (L // tm) * 8, HEAD_W), F32)],
        compiler_params=_params(("parallel", "parallel")),
        name="inproj",
    )(x, g_mix, wqt, wk, wvt, wz, wxbc, wdt, wdtt, wgate, bgate, ind)


def _rel_bucket(rel):
    half = REL_BUCKETS // 2
    max_exact = half // 2
    ret = jnp.where(rel > 0, half, 0)
    n = jnp.abs(rel)
    nf = jnp.maximum(n, 1).astype(jnp.float32)
    large = max_exact + (jnp.log(nf / max_exact) / math.log(REL_MAX_DIST / max_exact)
                         * (half - max_exact)).astype(jnp.int32)
    large = jnp.minimum(large, half - 1)
    return ret + jnp.where(n < max_exact, n, large)


def _bias_kernel(table_ref, bucket_ref, out_ref, *, TQ, TK):
    h = pl.program_id(0)
    half = REL_BUCKETS // 2
    far = math.ceil((half // 2) * (REL_MAX_DIST / (half // 2)) ** ((half // 2 - 1) / (half // 2)))
    for d in range(ATT_BIAS_TILES):
        for m in range(2):
            for c0 in range(0, TQ, 128):
                cols = pl.ds(m * TQ + c0, 128)
                rel_min = (d - 2) * TK - (c0 + 127)
                rel_max = (d - 2) * TK + TK - 1 - c0
                if rel_max <= -far or rel_min >= far:
                    last = half - 1 if rel_max < 0 else REL_BUCKETS - 1
                    out_ref[d, :, cols] = jnp.full((TK, 128), table_ref[last, 2 * h + m] * LOG2E, F32)
                    continue
                out_ref[d, :, cols] = jnp.zeros((TK, 128), F32)

                def body(b, carry, d=d, m=m, cols=cols, c0=c0):
                    val = table_ref[b, 2 * h + m] * LOG2E
                    out_ref[d, :, cols] = jnp.where(bucket_ref[d, :, c0:c0 + 128] == b, val, out_ref[d, :, cols])
                    return carry

                lax.fori_loop(0, REL_BUCKETS, body, 0)


def _bias_tiles(rel_bias, TQ, TK):
    assert TK > REL_MAX_DIST
    base = jnp.arange(TK, dtype=jnp.int32)[:, None] - jnp.arange(TQ, dtype=jnp.int32)[None, :]
    bucket = jnp.stack([_rel_bucket(base + (d - 2) * TK) for d in range(ATT_BIAS_TILES)])
    return pl.pallas_call(
        functools.partial(_bias_kernel, TQ=TQ, TK=TK),
        grid=(N_HEADS,),
        in_specs=[pl.BlockSpec(memory_space=pltpu.SMEM), _const_spec((ATT_BIAS_TILES, TK, TQ))],
        out_specs=pl.BlockSpec((None, ATT_BIAS_TILES, TK, 2 * TQ), lambda h: (h, 0, 0, 0)),
        out_shape=jax.ShapeDtypeStruct((N_HEADS, ATT_BIAS_TILES, TK, 2 * TQ), F32),
        compiler_params=_params(("parallel",)),
        name="rel_bias_tiles",
    )(rel_bias.astype(F32), bucket)


def _attn_kernel(lam_ref, brange_ref, qt_ref, k_ref, vt_ref, bt_ref, kn_ref, g_ref, o_ref, acc_sc, p_sc,
                 *, TQ, T, nk):
    h = pl.program_id(1)
    i = pl.program_id(2)
    qt = qt_ref[...]
    sub = lax.broadcasted_iota(jnp.int32, qt.shape, 0)
    zero = jnp.zeros_like(qt)
    q2t = jnp.concatenate([jnp.where(sub < QK_DIM, qt, zero), jnp.where(sub >= QK_DIM, qt, zero)], axis=1)
    acc_sc[...] = jnp.zeros(acc_sc.shape, F32)
    ones = jnp.ones((ONES_ROWS, T), BF16)

    def bias_tile(j):
        return bt_ref[jnp.clip(j - ATT_QK_RATIO * i, -2, ATT_QK_RATIO + 1) + 2]

    qsq = qt.astype(F32) ** 2
    qn = jnp.sqrt(jnp.concatenate([jnp.sum(qsq[:QK_DIM], axis=0, keepdims=True),
                                   jnp.sum(qsq[QK_DIM:], axis=0, keepdims=True)], axis=1))
    kn2 = jnp.max(kn_ref[...], axis=0, keepdims=True)
    klane = lax.broadcasted_iota(jnp.int32, kn2.shape, 1)
    kmax = [jnp.sqrt(jnp.max(jnp.where(klane == 2 * h + mp, kn2, 0.0), axis=1, keepdims=True)) for mp in range(2)]
    first = lax.broadcasted_iota(jnp.int32, qn.shape, 1) < TQ
    reach = qn * jnp.where(first, kmax[0], kmax[1]) * BOUND_SLACK
    bmax = jnp.where(first, brange_ref[h, 0], brange_ref[h, 1])
    bmin = jnp.where(first, brange_ref[h, 2], brange_ref[h, 3])
    shift = (reach + bmax).astype(BF16)
    bounded = jnp.max(2.0 * reach + (bmax - bmin)) < EXP2_SPAN

    @pl.when(bounded)
    def _():
        aug = jnp.where(lax.broadcasted_iota(jnp.int32, (HEAD_W, 2 * TQ), 0) == 0,
                        jnp.broadcast_to(-shift.astype(F32), (HEAD_W, 2 * TQ)), 0.0)
        qaug = jnp.concatenate([q2t, aug.astype(BF16)], axis=0)
        lane = lax.broadcasted_iota(jnp.int32, (T, HEAD_W), 1)
        onescol = jnp.where(lane == 0, 1.0, 0.0).astype(BF16)
        ones4 = jnp.ones((ONES_ROWS, 4 * T), BF16)

        def quad(u, carry):
            for r in range(4):
                j = 4 * u + r
                kb = jnp.concatenate([k_ref[pl.ds(pl.multiple_of(j * T, T), T), :], onescol], axis=1)
                s = jnp.dot(kb, qaug, preferred_element_type=F32) + bias_tile(j)
                p_sc[r * T:(r + 1) * T, :] = jnp.exp2(s).astype(BF16)
            vb = jnp.concatenate([vt_ref[:, pl.ds(pl.multiple_of(4 * u * T, 4 * T), 4 * T)], ones4], axis=0)
            acc_sc[...] += jnp.dot(vb, p_sc[...], preferred_element_type=F32)
            return carry

        lax.fori_loop(0, nk // 4, quad, 0)

    @pl.when(jnp.logical_not(bounded))
    def _():
        def scores_softmax(j, slot, m):
            kb = k_ref[pl.ds(pl.multiple_of(j * T, T), T), :]
            s = jnp.dot(kb, q2t, preferred_element_type=F32) + bias_tile(j)
            m_new = jnp.maximum(m, jnp.max(s, axis=0, keepdims=True))
            p_sc[slot * T:(slot + 1) * T, :] = jnp.exp2(s - m_new).astype(BF16)
            return m_new, jnp.exp2(m - m_new)

        def accumulate(j, slot, alpha):
            vb = jnp.concatenate([vt_ref[:, pl.ds(pl.multiple_of(j * T, T), T)], ones], axis=0)
            acc_sc[...] = alpha * acc_sc[...] + jnp.dot(vb, p_sc[slot * T:(slot + 1) * T, :],
                                                        preferred_element_type=F32)

        m = jnp.full((1, 2 * TQ), -jnp.inf, F32)
        m, alpha = scores_softmax(0, 0, m)

        def pair(u, carry):
            m, alpha = carry
            t = 2 * u
            m, alpha1 = scores_softmax(t - 1, 1, m)
            accumulate(t - 2, 0, alpha)
            m, alpha0 = scores_softmax(t, 0, m)
            accumulate(t - 1, 1, alpha1)
            return m, alpha0

        m, alpha = lax.fori_loop(1, nk // 2, pair, (m, alpha))
        m, alpha1 = scores_softmax(nk - 1, 1, m)
        accumulate(nk - 2, 0, alpha)
        accumulate(nk - 1, 1, alpha1)

    o = acc_sc[0:HEAD_W, :] / acc_sc[HEAD_W:HEAD_W + 1, :]
    od = o[:, :TQ] - lam_ref[0] * o[:, TQ:]
    y = od * lax.rsqrt(jnp.mean(od * od, axis=0, keepdims=True) + EPS) * g_ref[...]
    o_ref[...] = (y * (1.0 - LAM_INIT)).T.astype(o_ref.dtype)


def _attention(qt, k, vt, kn, btiles, brange, lam, g_col, *, T=ATT_BLOCK):
    B, L, _ = k.shape
    TQ = ATT_QK_RATIO * T
    nk = L // T
    assert nk % 4 == 0 and L % TQ == 0
    return pl.pallas_call(
        functools.partial(_attn_kernel, TQ=TQ, T=T, nk=nk),
        grid=(B, N_HEADS, L // TQ),
        in_specs=[pl.BlockSpec(memory_space=pltpu.SMEM),
                  pl.BlockSpec(memory_space=pltpu.SMEM),
                  pl.BlockSpec((None, HEAD_W, TQ), lambda b, h, i: (b, h, i)),
                  pl.BlockSpec((None, L, HEAD_W), lambda b, h, i: (b, 0, h)),
                  pl.BlockSpec((None, HEAD_W, L), lambda b, h, i: (b, h, 0)),
                  pl.BlockSpec((None, ATT_BIAS_TILES, T, 2 * TQ), lambda b, h, i: (h, 0, 0, 0)),
                  pl.BlockSpec((None,) + kn.shape[1:], lambda b, h, i: (b, 0, 0)),
                  _const_spec((HEAD_W, 1))],
        out_specs=pl.BlockSpec((None, TQ, HEAD_W), lambda b, h, i: (b, i, h)),
        out_shape=jax.ShapeDtypeStruct((B, L, ATT_W), BF16),
        scratch_shapes=[pltpu.VMEM((HEAD_W + ONES_ROWS, 2 * TQ), F32), pltpu.VMEM((4 * T, 2 * TQ), BF16)],
        compiler_params=_params(("parallel", "parallel", "parallel")),
        name="diff_attention",
    )(lam, brange, qt, k, vt, btiles, kn, g_col)


def _conv_kernel(x_ref, prev_ref, next_ref, dt_ref, dtt_ref, w_ref, b_ref, dtb_ref, dtbt_ref,
                 xs_ref, bc_ref, dts_ref, dtst_ref, *, tl, nt):
    i = pl.program_id(1)
    R = CONV_ROWS
    row = lax.broadcasted_iota(jnp.int32, (R, 128), 0)
    for r0 in range(0, tl, R):
        for c0 in range(0, CONV_CH, 128):
            cols = slice(c0, c0 + 128)
            x = x_ref[r0:r0 + R, cols]
            above = x_ref[r0 - 1:r0, cols] if r0 > 0 else jnp.where(i > 0, prev_ref[7:8, cols], 0.0)
            below = x_ref[r0 + R:r0 + R + 1, cols] if r0 + R < tl else jnp.where(i < nt - 1, next_ref[0:1, cols], 0.0)
            xm1 = jnp.where(row == 0, above, pltpu.roll(x, 1, 0))
            xp1 = jnp.where(row == R - 1, below, pltpu.roll(x, R - 1, 0))
            y = w_ref[0:1, cols] * xm1 + w_ref[1:2, cols] * x + w_ref[2:3, cols] * xp1 + b_ref[:, cols]
            y = (y * jax.nn.sigmoid(y)).astype(BF16)
            if c0 < SSM_INNER:
                xs_ref[r0:r0 + R, cols] = y
            else:
                bc_ref[r0:r0 + R, c0 - SSM_INNER:c0 - SSM_INNER + 128] = y

    def softplus(v):
        return jnp.maximum(v, 0.0) + jnp.log1p(jnp.exp(-jnp.abs(v)))

    dts_ref[...] = softplus(dt_ref[...] + dtb_ref[...])
    dtst_ref[...] = softplus(dtt_ref[...] + dtbt_ref[...])


def _conv(xbc, dt, dtt, conv_w, conv_b, dtb, dtbt, *, tl=512):
    B, L, _ = xbc.shape
    nt = L // tl
    r8 = tl // 8
    ndt = 2 * SSM_HEADS
    row = lambda w: pl.BlockSpec((None, tl, w), lambda b, i: (b, i, 0))
    colspec = pl.BlockSpec((None, ndt, tl), lambda b, i: (b, 0, i))
    return pl.pallas_call(
        functools.partial(_conv_kernel, tl=tl, nt=nt),
        grid=(B, nt),
        in_specs=[row(CONV_CH),
                  pl.BlockSpec((None, 8, CONV_CH), lambda b, i: (b, jnp.maximum(i * r8 - 1, 0), 0)),
                  pl.BlockSpec((None, 8, CONV_CH), lambda b, i: (b, jnp.minimum((i + 1) * r8, L // 8 - 1), 0)),
                  row(ndt), colspec,
                  _const_spec((3, CONV_CH)), _const_spec((1, CONV_CH)), _const_spec((1, ndt)),
                  _const_spec((ndt, 1))],
        out_specs=[row(SSM_INNER), row(2 * SSM_GROUPS * SSM_N), row(ndt), colspec],
        out_shape=[jax.ShapeDtypeStruct((B, L, SSM_INNER), BF16),
                   jax.ShapeDtypeStruct((B, L, 2 * SSM_GROUPS * SSM_N), BF16),
                   jax.ShapeDtypeStruct((B, L, ndt), F32),
                   jax.ShapeDtypeStruct((B, ndt, L), F32)],
        compiler_params=_params(("parallel", "parallel")),
        name="ssm_conv",
    )(xbc, xbc, xbc, dt, dtt, conv_w, conv_b, dtb, dtbt)


def _ssd_chunk(xs_ref, bc_ref, dts_ref, dtst_ref, alog_ref, alogt_ref, st_ref, *, rev):
    Q = SSM_CHUNK
    HI = lax.Precision.HIGHEST
    off = SSM_HEADS if rev else 0
    ri = lax.broadcasted_iota(jnp.int32, (Q, Q), 0)
    ci = lax.broadcasted_iota(jnp.int32, (Q, Q), 1)
    incl = (ci >= ri) if rev else (ci <= ri)
    tri = incl.astype(F32)
    dt = dts_ref[:, off:off + SSM_HEADS]
    dtt = dtst_ref[off:off + SSM_HEADS, :]
    a = dt * (-jnp.exp(alog_ref[:, off:off + SSM_HEADS]))
    at = dtt * (-jnp.exp(alogt_ref[off:off + SSM_HEADS, :]))
    acum = jnp.dot(tri, a, preferred_element_type=F32, precision=HI)
    acumt = _nt_dot(at, tri, precision=HI)
    edge = 0 if rev else Q - 1
    bc = bc_ref[...]
    bpair = bc[:, :Q]
    cpair = bc[:, Q:]
    low = ci < SSM_N
    rlow = ri < SSM_N
    blockmask = low == rlow
    total = acumt[:, edge:edge + 1]
    decay = jnp.exp(total)
    w_rows = dtt * jnp.exp(total - acumt)
    ys = []
    for g in range(SSM_GROUPS):
        gmask = low if g == 0 else jnp.logical_not(low)
        cg = jnp.where(gmask, cpair, jnp.zeros_like(cpair))
        cb = _nt_dot(cg, bpair)
        dupm = ((ri - g * SSM_N) == (ci % SSM_N)).astype(BF16)
        cdup = jnp.dot(cpair, dupm, preferred_element_type=F32)
        sel2 = ((ci - g * SSM_N) == (ri % SSM_N)).astype(BF16)
        bt2 = _nt_dot(sel2, bpair)
        for hp in range(SSM_HEADS // SSM_GROUPS // 2):
            h0 = g * (SSM_HEADS // SSM_GROUPS) + 2 * hp
            pair = h0 // 2
            xpair = xs_ref[:, h0 * SSM_P:(h0 + 2) * SSM_P]
            ms = []
            e_cols = []
            for h in (h0, h0 + 1):
                e_col = jnp.broadcast_to(acum[:, h:h + 1], (Q, Q))
                r_row = jnp.broadcast_to(acumt[h:h + 1, :], (Q, Q))
                seg = jnp.where(incl, e_col - r_row, -jnp.inf)
                ms.append((cb * jnp.exp(seg) * jnp.broadcast_to(dtt[h:h + 1, :], (Q, Q))).astype(BF16))
                e_cols.append(e_col)
            st = st_ref[pair]
            xblk = jnp.concatenate([jnp.where(low, xpair, jnp.zeros_like(xpair)),
                                    jnp.where(low, jnp.zeros_like(xpair), xpair)], axis=0)
            y = jnp.dot(jnp.concatenate(ms, axis=1), xblk, preferred_element_type=F32)
            ce = (cdup * jnp.exp(jnp.where(low, e_cols[0], e_cols[1]))).astype(BF16)
            y = y + jnp.dot(ce, st.astype(BF16), preferred_element_type=F32)
            ys.append(y)
            w_pair = jnp.where(rlow, jnp.broadcast_to(w_rows[h0:h0 + 1, :], (Q, Q)),
                               jnp.broadcast_to(w_rows[h0 + 1:h0 + 2, :], (Q, Q)))
            snew = jnp.dot((bt2 * w_pair).astype(BF16), xpair, preferred_element_type=F32)
            dec = jnp.where(rlow, decay[h0:h0 + 1, :], decay[h0 + 1:h0 + 2, :])
            st_ref[pair] = dec * st + jnp.where(blockmask, snew, 0.0)
    return jnp.concatenate(ys, axis=1)


def _ssd_fwd_kernel(xs_ref, bc_ref, dts_ref, dtst_ref, alog_ref, alogt_ref, y_ref, st_ref):
    @pl.when(pl.program_id(1) == 0)
    def _():
        st_ref[...] = jnp.zeros(st_ref.shape, F32)

    y = _ssd_chunk(xs_ref, bc_ref, dts_ref, dtst_ref, alog_ref, alogt_ref, st_ref, rev=False)
    y_ref[...] = y.astype(y_ref.dtype)


def _ssd_bwd_kernel(xs_ref, bc_ref, dts_ref, dtst_ref, alog_ref, alogt_ref, yf_ref, z_ref, dskip_ref, g_ref,
                    o_ref, st_ref):
    @pl.when(pl.program_id(1) == 0)
    def _():
        st_ref[...] = jnp.zeros(st_ref.shape, F32)

    yb = _ssd_chunk(xs_ref, bc_ref, dts_ref, dtst_ref, alog_ref, alogt_ref, st_ref, rev=True)
    z = z_ref[...].astype(F32)
    y = (yf_ref[...].astype(F32) + yb + xs_ref[...].astype(F32) * dskip_ref[...]) * (z * jax.nn.sigmoid(z))
    y = y * lax.rsqrt(jnp.mean(y * y, axis=-1, keepdims=True) + EPS) * g_ref[...]
    o_ref[...] = y.astype(o_ref.dtype)


def _ssd(xs, bc, dts, dtst, alog, alogt, z, dskip, g_norm):
    B, L, _ = xs.shape
    Q = SSM_CHUNK
    nc = L // Q
    ndt = 2 * SSM_HEADS
    npair = SSM_HEADS // 2

    def specs(cidx):
        row = lambda w: pl.BlockSpec((None, Q, w), lambda b, c: (b, cidx(c), 0))
        return row, [row(SSM_INNER), row(2 * SSM_GROUPS * SSM_N), row(ndt),
                     pl.BlockSpec((None, ndt, Q), lambda b, c: (b, 0, cidx(c))),
                     _const_spec((1, ndt)), _const_spec((ndt, 1))]

    scratch = [pltpu.VMEM((npair, 2 * SSM_N, 2 * SSM_P), F32)]
    row, in_specs = specs(lambda c: c)
    yf = pl.pallas_call(
        _ssd_fwd_kernel,
        grid=(B, nc),
        in_specs=in_specs,
        out_specs=row(SSM_INNER),
        out_shape=jax.ShapeDtypeStruct((B, L, SSM_INNER), BF16),
        scratch_shapes=scratch,
        compiler_params=_params(("parallel", "arbitrary")),
        name="ssd_forward",
    )(xs, bc, dts, dtst, alog, alogt)
    row, in_specs = specs(lambda c: nc - 1 - c)
    return pl.pallas_call(
        _ssd_bwd_kernel,
        grid=(B, nc),
        in_specs=in_specs + [row(SSM_INNER), row(SSM_INNER), _const_spec((1, SSM_INNER)),
                             _const_spec((1, SSM_INNER))],
        out_specs=row(SSM_INNER),
        out_shape=jax.ShapeDtypeStruct((B, L, SSM_INNER), BF16),
        scratch_shapes=scratch,
        compiler_params=_params(("parallel", "arbitrary")),
        name="ssd_reverse_gate_norm",
    )(xs, bc, dts, dtst, alog, alogt, yf, z, dskip, g_norm)


def _merge_kernel(att_ref, ssm_ref, gates_ref, x_ref, wau_ref, wsu_ref, wout_ref, gffn_ref, wr_ref, wrt_ref,
                  place_ref, h_ref, u2_ref, afft_ref):
    D = D_MODEL
    E = N_EXPERTS
    ya = jnp.dot(att_ref[...], wau_ref[...], preferred_element_type=F32)
    ysm = jnp.dot(ssm_ref[...], wsu_ref[...], preferred_element_type=F32)
    merged = gates_ref[:, :D] * ya + gates_ref[:, D:] * ysm
    h = x_ref[...] + jnp.dot(merged.astype(BF16), wout_ref[...], preferred_element_type=F32)
    h_ref[...] = h
    u2 = (h * lax.rsqrt(jnp.mean(h * h, axis=-1, keepdims=True) + EPS) * gffn_ref[...]).astype(BF16)
    u2_ref[:, :D] = u2
    logits = jnp.dot(u2, wr_ref[...], preferred_element_type=F32)
    e = jnp.exp(logits - jnp.max(logits, axis=-1, keepdims=True))
    aff = e / jnp.sum(e, axis=-1, keepdims=True)
    hi = aff.astype(BF16)
    r1 = aff - hi.astype(F32)
    mid = r1.astype(BF16)
    lo = (r1 - mid.astype(F32)).astype(BF16)
    ext = (jnp.dot(hi, place_ref[0:E, :], preferred_element_type=F32)
           + jnp.dot(mid, place_ref[E:2 * E, :], preferred_element_type=F32)
           + jnp.dot(lo, place_ref[2 * E:3 * E, :], preferred_element_type=F32))
    u2_ref[:, D:] = ext.astype(BF16)
    lt = _nt_dot(wrt_ref[...], u2)
    et = jnp.exp(lt - jnp.max(lt, axis=0, keepdims=True))
    afft_ref[...] = et / jnp.sum(et, axis=0, keepdims=True)


def _merge(att, ssm, gates, x, wau, wsu, wout, g_ffn, wr, *, tm=256):
    B, L, D = x.shape
    E = N_EXPERTS
    nl = L // tm
    row = lambda w: pl.BlockSpec((None, tm, w), lambda b, i: (b, i, 0))
    place = jnp.eye(3 * E, GATE_EXT, dtype=BF16)
    return pl.pallas_call(
        _merge_kernel,
        grid=(B, nl),
        in_specs=[row(ATT_W), row(SSM_INNER), row(2 * D), row(D), _const_spec(wau.shape),
                  _const_spec(wsu.shape), _const_spec(wout.shape), _const_spec((1, D)), _const_spec(wr.shape),
                  _const_spec((E, D)), _const_spec((3 * E, GATE_EXT))],
        out_specs=[row(D), row(D + GATE_EXT), pl.BlockSpec((E, tm), lambda b, i: (0, b * nl + i))],
        out_shape=[jax.ShapeDtypeStruct((B, L, D), F32), jax.ShapeDtypeStruct((B, L, D + GATE_EXT), BF16),
                   jax.ShapeDtypeStruct((E, B * L), F32)],
        compiler_params=_params(("parallel", "parallel")),
        name="merge_router",
    )(att, ssm, gates, x, wau, wsu, wout, g_ffn, wr, wr.T, place)


def _route_kernel(afft_ref, posm_ref, bstart_ref, *, cap, nb):
    E = N_EXPERTS
    W = ROUTE_BLK

    def bit_body(k, thr):
        cand = thr | lax.shift_left(jnp.int32(1), 30 - k)
        x = lax.bitcast_convert_type(afft_ref[...], jnp.int32)
        cnt = jnp.sum(jnp.where(x >= cand, 1.0, 0.0), axis=1, keepdims=True)
        return jnp.where(cnt >= cap, cand, thr)

    thr = lax.fori_loop(0, 31, bit_body, jnp.zeros((E, 1), jnp.int32))
    x = lax.bitcast_convert_type(afft_ref[...], jnp.int32)
    n_gt = jnp.sum(jnp.where(x > thr, 1.0, 0.0), axis=1, keepdims=True)
    need = cap - n_gt
    ri = lax.broadcasted_iota(jnp.int32, (W, W), 0)
    ci = lax.broadcasted_iota(jnp.int32, (W, W), 1)
    upper = jnp.where(ri <= ci, 1.0, 0.0).astype(BF16)
    bstart_ref[...] = jnp.zeros(bstart_ref.shape, jnp.int32)
    c_eq = jnp.zeros((E, 1), F32)
    c_sel = jnp.zeros((E, 1), F32)
    for b in range(nb):
        xb = lax.bitcast_convert_type(afft_ref[:, b * W:(b + 1) * W], jnp.int32)
        eq = xb == thr
        eqc = jnp.dot(jnp.where(eq, 1.0, 0.0).astype(BF16), upper, preferred_element_type=F32) + c_eq
        sel = (xb > thr) | (eq & (eqc <= need))
        selc = jnp.dot(jnp.where(sel, 1.0, 0.0).astype(BF16), upper, preferred_element_type=F32) + c_sel
        posm_ref[:, b * W:(b + 1) * W] = jnp.where(sel, selc, 0.0).astype(jnp.int32)
        bstart_ref[:, b:b + 1] = c_sel.astype(jnp.int32)
        c_eq = eqc[:, W - 1:W]
        c_sel = selc[:, W - 1:W]
    bstart_ref[:, nb:nb + 1] = c_sel.astype(jnp.int32)


def _route(afft, cap):
    E, T = afft.shape
    nb = T // ROUTE_BLK
    assert nb < BSTART_W and cap % SLOT_TILE == 0
    return pl.pallas_call(
        functools.partial(_route_kernel, cap=float(cap), nb=nb),
        out_shape=[jax.ShapeDtypeStruct((E, T), jnp.int32), jax.ShapeDtypeStruct((E, BSTART_W), jnp.int32)],
        compiler_params=_params(None),
        name="route_select",
    )(afft)


def _pair_schedule(bstart, cap, ngb):
    step = GATHER_BLK // ROUTE_BLK
    st = bstart[:, 0:ngb * step + 1:step]
    lo, hi = st[:, :-1], st[:, 1:]
    ntile = cap // SLOT_TILE
    t_lo = jnp.arange(ntile, dtype=jnp.int32) * SLOT_TILE
    t_hi = t_lo + SLOT_TILE
    b_first = jnp.sum(hi[:, None, :] <= t_lo[None, :, None], axis=2).astype(jnp.int32)
    b_last = jnp.sum(lo[:, None, :] < t_hi[None, :, None], axis=2).astype(jnp.int32) - 1
    return b_first, b_last - b_first + 1


def _expert_kernel(first_ref, cnt_ref, bs_ref, u_hbm, posm_ref, wg_ref, wu_ref, wd_ref, o_ref, x_sc, ubuf, sem):
    D = D_MODEL
    e = pl.program_id(0)
    j = pl.program_id(1)
    g = e * pl.num_programs(1) + j
    b0 = first_ref[g]
    n = cnt_ref[g]

    half = GATHER_BLK // 2

    def half_copies(b, slot):
        return [pltpu.make_async_copy(
            u_hbm.at[pl.ds(pl.multiple_of(b * GATHER_BLK + r * half, half), half)],
            ubuf.at[slot, pl.ds(r * half, half)], sem.at[slot, r]) for r in range(2)]

    def start(b, slot):
        for cp in half_copies(b, slot):
            cp.start()

    ahead = GATHER_BUFS - 1

    def start_tile(t):
        start(first_ref[t], 0)
        for r in range(1, ahead):
            @pl.when(cnt_ref[t] > r)
            def _(r=r):
                start(first_ref[t] + r, r)

    @pl.when(g == 0)
    def _():
        start_tile(0)

    x_sc[...] = jnp.zeros(x_sc.shape, F32)
    WM = GATHER_ROWS
    row_ids = lax.broadcasted_iota(jnp.int32, (WM, GATHER_BLK), 0)
    per_blk = GATHER_BLK // ROUTE_BLK

    def body(c, carry):
        slot = c % GATHER_BUFS
        for cp in half_copies(b0 + c, slot):
            cp.wait()

        @pl.when(c + ahead < n)
        def _():
            start(b0 + c + ahead, (c + ahead) % GATHER_BUFS)

        blk = b0 + c
        posrow = posm_ref[:, pl.ds(pl.multiple_of(blk * GATHER_BLK, GATHER_BLK), GATHER_BLK)] - (j * SLOT_TILE + 1)
        lo = jnp.maximum(bs_ref[e * BSTART_W + blk * per_blk] - j * SLOT_TILE, 0)
        hi = jnp.minimum(bs_ref[e * BSTART_W + (blk + 1) * per_blk] - j * SLOT_TILE, SLOT_TILE)
        w0 = (lo // 16) * 16

        def window(k, carry):
            wlo = w0 + k * WM
            top = pl.multiple_of(jnp.minimum(wlo, SLOT_TILE - WM), 16)
            hit = (posrow >= wlo) & (posrow < wlo + WM) & (posrow - top == row_ids)
            onehot = jnp.where(hit, 1.0, 0.0).astype(BF16)
            x_sc[pl.ds(top, WM), :] += jnp.dot(onehot, ubuf[slot], preferred_element_type=F32)
            return carry

        lax.fori_loop(0, jnp.maximum((hi - w0 + WM - 1) // WM, 0), window, 0)
        return carry

    lax.fori_loop(0, n, body, 0)

    @pl.when(g + 1 < pl.num_programs(0) * pl.num_programs(1))
    def _():
        start_tile(g + 1)

    xs = x_sc[:, :D].astype(BF16)
    ext = x_sc[:, D:]
    lane = lax.broadcasted_iota(jnp.int32, ext.shape, 1)
    mine = (lane == e) | (lane == e + N_EXPERTS) | (lane == e + 2 * N_EXPERTS)
    gate = jnp.sum(jnp.where(mine, ext, 0.0), axis=1, keepdims=True)
    hg = jnp.dot(xs, wg_ref[...], preferred_element_type=F32)
    hu = jnp.dot(xs, wu_ref[...], preferred_element_type=F32)
    hmid = (hg * jax.nn.sigmoid(hg) * hu).astype(BF16)
    o_ref[...] = jnp.dot(hmid, wd_ref[...], preferred_element_type=F32) * gate


def _experts(u2ext, posm, bstart, b_first, b_cnt, wg, wu, wd, cap):
    T, DX = u2ext.shape
    E, D = N_EXPERTS, D_MODEL
    wspec = pl.BlockSpec((None, D, D), lambda e, j, bf, bc, bs: (e, 0, 0))
    return pl.pallas_call(
        _expert_kernel,
        grid_spec=pltpu.PrefetchScalarGridSpec(
            num_scalar_prefetch=3,
            grid=(E, cap // SLOT_TILE),
            in_specs=[pl.BlockSpec(memory_space=pl.ANY),
                      pl.BlockSpec((None, 1, T), lambda e, j, bf, bc, bs: (e, 0, 0)),
                      wspec, wspec, wspec],
            out_specs=pl.BlockSpec((None, SLOT_TILE, D), lambda e, j, bf, bc, bs: (e, j, 0)),
            scratch_shapes=[pltpu.VMEM((SLOT_TILE, DX), F32), pltpu.VMEM((GATHER_BUFS, GATHER_BLK, DX), BF16),
                            pltpu.SemaphoreType.DMA((GATHER_BUFS, 2))]),
        out_shape=jax.ShapeDtypeStruct((E, cap, D), F32),
        compiler_params=_params(("arbitrary", "arbitrary")),
        name="experts_gather_mlp",
    )(b_first.reshape(-1), b_cnt.reshape(-1), bstart.reshape(-1), u2ext, posm.reshape(E, 1, T), wg, wu, wd)


def _combine_kernel(bstart_ref, posm_ref, ye_ref, h_ref, p_ref, wpg_ref, bpg_ref, wpp_ref, gfin_ref, o_ref,
                    moe, buf, xbuf, sem, xsem, *, cap, nb):
    E = N_EXPERTS
    WN = COMBINE_WIN
    b = pl.program_id(0)
    top = cap - WN
    cur = b % 2

    def window(blk, slot, e):
        base = jnp.minimum((bstart_ref[e, blk] // 8) * 8, top)
        return pltpu.make_async_copy(ye_ref.at[e, pl.ds(pl.multiple_of(base, 8), WN)],
                                     buf.at[slot, pl.ds(e * WN, WN)], sem.at[slot, e])

    @pl.when(b == 0)
    def _():
        for e in range(E):
            window(0, 0, e).start()

    @pl.when(b + 1 < nb)
    def _():
        for e in range(E):
            window(b + 1, 1 - cur, e).start()

    lows = [(bstart_ref[e, b] // 8) * 8 for e in range(E)]
    w_iota = lax.broadcasted_iota(jnp.int32, (WN, ROUTE_BLK), 0)

    def onehot_rows(slot0, low, base):
        ok = (slot0 >= low) & (slot0 < low + WN)
        return jnp.where(ok & (slot0 - base == w_iota), 1.0, 0.0).astype(BF16)

    def scatter(onehot, rows):
        hi = rows.astype(BF16)
        lo = (rows - hi.astype(F32)).astype(BF16)
        tn = lambda a, c: lax.dot_general(a, c, (((0,), (0,)), ((), ())), preferred_element_type=F32)
        return tn(onehot, hi) + tn(onehot, lo)

    hots = []
    for e in range(E):
        slot0 = posm_ref[e:e + 1, :] - 1
        hots.append(onehot_rows(slot0, lows[e], jnp.minimum(lows[e], top)))
    for e in range(E):
        window(b, cur, e).wait()
    moe[...] = scatter(jnp.concatenate(hots, axis=0), buf[cur])

    def per_expert(e, carry):
        low0 = (bstart_ref[e, b] // 8) * 8
        nwin = (bstart_ref[e, b + 1] - low0 + WN - 1) // WN

        def per_window(c, carry):
            low = low0 + c * WN
            base = jnp.minimum(low, top)
            cp = pltpu.make_async_copy(ye_ref.at[e, pl.ds(pl.multiple_of(base, 8), WN)], xbuf, xsem.at[0])
            cp.start()
            cp.wait()
            mine = lax.broadcasted_iota(jnp.int32, (E, ROUTE_BLK), 0) == e
            slot0 = jnp.sum(jnp.where(mine, posm_ref[...], 0), axis=0, keepdims=True) - 1
            moe[...] += scatter(onehot_rows(slot0, low, base), xbuf[...])
            return carry

        return lax.fori_loop(1, nwin, per_window, carry)

    lax.fori_loop(0, E, per_expert, 0)

    h = h_ref[...] + moe[...]
    gate = jax.nn.sigmoid(jnp.dot(h.astype(BF16), wpg_ref[...], preferred_element_type=F32) + bpg_ref[...])
    h = h + gate * jnp.dot(p_ref[...].astype(BF16), wpp_ref[...], preferred_element_type=F32)
    o_ref[...] = h * lax.rsqrt(jnp.mean(h * h, axis=-1, keepdims=True) + EPS) * gfin_ref[...]


def _combine_ple(ye, posm, bstart, cap, h, p, wpg, bpg, wpp, g_final):
    E, T = posm.shape
    D = D_MODEL
    nb = T // ROUTE_BLK
    assert cap >= COMBINE_WIN
    row = lambda w: pl.BlockSpec((ROUTE_BLK, w), lambda b, bs: (b, 0))
    const = lambda shape: pl.BlockSpec(shape, lambda b, bs: (0,) * len(shape))
    return pl.pallas_call(
        functools.partial(_combine_kernel, cap=cap, nb=nb),
        grid_spec=pltpu.PrefetchScalarGridSpec(
            num_scalar_prefetch=1,
            grid=(nb,),
            in_specs=[pl.BlockSpec((E, ROUTE_BLK), lambda b, bs: (0, b)),
                      pl.BlockSpec(memory_space=pl.ANY), row(D), row(PLE_DIM), const(wpg.shape), const((1, D)),
                      const(wpp.shape), const((1, D))],
            out_specs=row(D),
            scratch_shapes=[pltpu.VMEM((ROUTE_BLK, D), F32), pltpu.VMEM((2, E * COMBINE_WIN, D), F32),
                            pltpu.VMEM((COMBINE_WIN, D), F32),
                            pltpu.SemaphoreType.DMA((2, E)), pltpu.SemaphoreType.DMA((1,))]),
        out_shape=jax.ShapeDtypeStruct((T, D), F32),
        compiler_params=_params(("arbitrary",)),
        name="experts_combine_ple",
    )(bstart, posm, ye, h.reshape(T, D), p.reshape(T, PLE_DIM), wpg, bpg, wpp, g_final)


def _forward(x, p, w, btiles):
    B, L, D = x.shape
    T = B * L
    qt, k, vt, z, xbc, dt, dtt, gates, kn = _inproj(x, w['g_mix'], w['wqt'], w['wk'], w['wvt'], w['wz'],
                                                    w['wxbc'], w['wdt'], w['wdtt'], w['wgate'], w['bgate'])
    att = _attention(qt, k, vt, kn, btiles, w['brange'], w['lam'], w['g_subln'])
    xs, bc, dts, dtst = _conv(xbc, dt, dtt, w['conv_w'], w['conv_b'], w['dtb'], w['dtbt'])
    ssm = _ssd(xs, bc, dts, dtst, w['alog'], w['alogt'], z, w['dskip'], w['g_ssm_norm'])
    h, u2ext, afft = _merge(att, ssm, gates, x, w['wau'], w['wsu'], w['wout'], w['g_ffn'], w['wr'])
    cap = max(1, CAPACITY_FACTOR * T // N_EXPERTS)
    posm, bstart = _route(afft, cap)
    b_first, b_cnt = _pair_schedule(bstart, cap, T // GATHER_BLK)
    ye = _experts(u2ext.reshape(T, D + GATE_EXT), posm, bstart, b_first, b_cnt, w['weg'], w['weu'], w['wed'], cap)
    out = _combine_ple(ye, posm, bstart, cap, h, p, w['wpg'], w['bpg'], w['wpp'], w['g_final'])
    return out.reshape(B, L, D)


def kernel(x_prompt, x_sample, p_prompt, p_sample, rel_bias, g_mix, w_in, conv_w, conv_b, dt_bias, a_log,
           d_skip, g_ssm_norm, lam_params, g_subln, w_att_up, w_ssm_up, w_gate, b_gate, w_out, g_ffn,
           w_router, w_exp_gate, w_exp_up, w_exp_down, w_ple_gate, b_ple_gate, w_ple_proj, g_final):
    i = 0
    D = D_MODEL
    win = w_in[i]
    o1 = ATT_W
    o3 = 3 * ATT_W
    o4 = o3 + SSM_INNER
    o5 = o4 + CONV_CH
    wqt = (win[:, :o1] * (QK_DIM ** -0.5 * LOG2E)).T.astype(BF16)
    wk = win[:, o1:2 * o1].astype(BF16)
    wvt = win[:, 2 * o1:o3].T.astype(BF16)
    wdt = win[:, o5:].astype(BF16)
    lp = lam_params[i].astype(F32)
    lam = jnp.exp(jnp.sum(lp[0] * lp[1])) - jnp.exp(jnp.sum(lp[2] * lp[3])) + LAM_INIT
    ndt = 2 * SSM_HEADS
    tb = rel_bias.astype(F32).reshape(REL_BUCKETS, N_HEADS, 2) * LOG2E
    brange = jnp.concatenate([jnp.max(tb, axis=0), jnp.min(tb, axis=0)], axis=1)
    w = {
        'g_mix': g_mix[i].reshape(1, D), 'wqt': wqt, 'wk': wk, 'wvt': wvt, 'wz': win[:, o3:o4].astype(BF16),
        'wxbc': win[:, o4:o5].astype(BF16), 'wdt': wdt, 'wdtt': wdt.T,
        'wgate': w_gate[i].astype(BF16), 'bgate': b_gate[i].reshape(1, 2 * D),
        'lam': lam.reshape(1), 'g_subln': g_subln[i].reshape(HEAD_W, 1), 'brange': brange,
        'conv_w': conv_w[i], 'conv_b': conv_b[i].reshape(1, CONV_CH),
        'dtb': dt_bias[i].reshape(1, ndt), 'dtbt': dt_bias[i].reshape(ndt, 1),
        'alog': a_log[i].reshape(1, ndt), 'alogt': a_log[i].reshape(ndt, 1),
        'dskip': jnp.repeat(d_skip[i], SSM_P).reshape(1, SSM_INNER),
        'g_ssm_norm': g_ssm_norm[i].reshape(1, SSM_INNER),
        'wau': w_att_up[i].astype(BF16), 'wsu': w_ssm_up[i].astype(BF16), 'wout': w_out[i].astype(BF16),
        'g_ffn': g_ffn[i].reshape(1, D), 'wr': w_router[i].astype(BF16),
        'weg': w_exp_gate[i].astype(BF16), 'weu': w_exp_up[i].astype(BF16), 'wed': w_exp_down[i].astype(BF16),
        'wpg': w_ple_gate[i].astype(BF16), 'bpg': b_ple_gate[i].reshape(1, D),
        'wpp': w_ple_proj[i].astype(BF16), 'g_final': g_final.reshape(1, D),
    }
    btiles = _bias_tiles(rel_bias, ATT_QK_RATIO * ATT_BLOCK, ATT_BLOCK)
    return (_forward(x_prompt, p_prompt[i], w, btiles), _forward(x_sample, p_sample[i], w, btiles))
```

```python
import functools
import math

import jax
import jax.numpy as jnp
from jax import lax
from jax.experimental import pallas as pl
from jax.experimental.pallas import tpu as pltpu

F32 = jnp.float32
BF16 = jnp.bfloat16

D_MODEL = 1024
N_HEADS = 4
QK_DIM = 64
HEAD_W = 2 * QK_DIM
ATT_W = N_HEADS * HEAD_W
REL_BUCKETS = 32
REL_MAX_DIST = 128
SSM_INNER = 1024
SSM_P = 64
SSM_HEADS = 16
SSM_GROUPS = 2
SSM_N = 64
SSM_CHUNK = 128
CONV_CH = SSM_INNER + 2 * SSM_GROUPS * SSM_N
N_EXPERTS = 16
CAPACITY_FACTOR = 2
PLE_DIM = 256
EPS = 1e-6
LAM_INIT = 0.8 - 0.6 * math.exp(-0.3 * 0)
LOG2E = math.log2(math.e)

GATE_EXT = 128
ROUTE_BLK = 256
GATHER_BLK = 512
GATHER_BUFS = 5
SLOT_TILE = 256
GATHER_ROWS = 128
COMBINE_WIN = 72
BSTART_W = 256

CONV_ROWS = 128

ATT_BLOCK = 256
ATT_QK_RATIO = 4
ATT_BIAS_TILES = ATT_QK_RATIO + 4
ONES_ROWS = 16
BOUND_SLACK = 1.0 + 2.0 ** -6
EXP2_SPAN = 100.0
VMEM_LIMIT = 56 * 1024 * 1024


def _params(sem, vmem=VMEM_LIMIT):
    return pltpu.CompilerParams(dimension_semantics=sem, vmem_limit_bytes=vmem)


def _const_spec(shape):
    nd = len(shape)
    return pl.BlockSpec(shape, lambda *_: (0,) * nd)


def _nt_dot(a, b, **kw):
    return lax.dot_general(a, b, (((1,), (1,)), ((), ())), preferred_element_type=F32, **kw)


def _inproj_kernel(x_ref, g_ref, wqt_ref, wk_ref, wvt_ref, wz_ref, wxbc_ref, wdt_ref, wdtt_ref, wgate_ref,
                   bgate_ref, ind_ref, qt_ref, k_ref, vt_ref, z_ref, xbc_ref, dt_ref, dtt_ref, gates_ref, kn_ref):
    x = x_ref[...]
    u = x * lax.rsqrt(jnp.mean(x * x, axis=-1, keepdims=True) + EPS) * g_ref[...]
    ub = u.astype(BF16)
    qt_ref[...] = _nt_dot(wqt_ref[...], ub).astype(BF16)
    kb = jnp.dot(ub, wk_ref[...], preferred_element_type=F32).astype(BF16)
    k_ref[...] = kb
    ksq = (kb.astype(F32) ** 2).astype(BF16)
    kn_ref[...] = jnp.broadcast_to(
        jnp.max(jnp.dot(ksq, ind_ref[...], preferred_element_type=F32), axis=0, keepdims=True), kn_ref.shape)
    vt_ref[...] = _nt_dot(wvt_ref[...], ub).astype(BF16)
    z_ref[...] = jnp.dot(ub, wz_ref[...], preferred_element_type=F32).astype(z_ref.dtype)
    xbc_ref[...] = jnp.dot(ub, wxbc_ref[...], preferred_element_type=F32)
    dt_ref[...] = jnp.dot(ub, wdt_ref[...], preferred_element_type=F32)
    dtt_ref[...] = _nt_dot(wdtt_ref[...], ub)
    gates = jax.nn.sigmoid(jnp.dot(ub, wgate_ref[...], preferred_element_type=F32) + bgate_ref[...])
    gates_ref[...] = gates.astype(gates_ref.dtype)


def _inproj(x, g_mix, wqt, wk, wvt, wz, wxbc, wdt, wdtt, wgate, bgate, *, tm=256):
    B, L, D = x.shape
    grid = (B, L // tm)
    row = lambda w: pl.BlockSpec((None, tm, w), lambda b, i: (b, i, 0))
    col = lambda w: pl.BlockSpec((None, w, tm), lambda b, i: (b, 0, i))
    ndt = 2 * SSM_HEADS
    ind = (jnp.arange(ATT_W)[:, None] // QK_DIM == jnp.arange(HEAD_W)[None, :]).astype(BF16)
    return pl.pallas_call(
        _inproj_kernel,
        grid=grid,
        in_specs=[row(D), _const_spec((1, D)), _const_spec(wqt.shape), _const_spec(wk.shape),
                  _const_spec(wvt.shape), _const_spec(wz.shape),
                  _const_spec(wxbc.shape), _const_spec(wdt.shape), _const_spec(wdtt.shape),
                  _const_spec(wgate.shape), _const_spec((1, 2 * D)), _const_spec((ATT_W, HEAD_W))],
        out_specs=[col(ATT_W), row(ATT_W), col(ATT_W), row(SSM_INNER), row(CONV_CH), row(ndt),
                   col(ndt), row(2 * D), pl.BlockSpec((None, 8, HEAD_W), lambda b, i: (b, i, 0))],
        out_shape=[jax.ShapeDtypeStruct((B, ATT_W, L), BF16),
                   jax.ShapeDtypeStruct((B, L, ATT_W), BF16),
                   jax.ShapeDtypeStruct((B, ATT_W, L), BF16),
                   jax.ShapeDtypeStruct((B, L, SSM_INNER), BF16),
                   jax.ShapeDtypeStruct((B, L, CONV_CH), F32),
                   jax.ShapeDtypeStruct((B, L, ndt), F32),
                   jax.ShapeDtypeStruct((B, ndt, L), F32),
                   jax.ShapeDtypeStruct((B, L, 2 * D), BF16),
                   jax.ShapeDtypeStruct((B, (L // tm) * 8, HEAD_W), F32)],
        compiler_params=_params(("parallel", "parallel")),
        name="inproj",
    )(x, g_mix, wqt, wk, wvt, wz, wxbc, wdt, wdtt, wgate, bgate, ind)


def _rel_bucket(rel):
    half = REL_BUCKETS // 2
    max_exact = half // 2
    ret = jnp.where(rel > 0, half, 0)
    n = jnp.abs(rel)
    nf = jnp.maximum(n, 1).astype(jnp.float32)
    large = max_exact + (jnp.log(nf / max_exact) / math.log(REL_MAX_DIST / max_exact)
                         * (half - max_exact)).astype(jnp.int32)
    large = jnp.minimum(large, half - 1)
    return ret + jnp.where(n < max_exact, n, large)


def _bias_kernel(table_ref, bucket_ref, out_ref, *, TQ, TK):
    h = pl.program_id(0)
    half = REL_BUCKETS // 2
    far = math.ceil((half // 2) * (REL_MAX_DIST / (half // 2)) ** ((half // 2 - 1) / (half // 2)))
    for d in range(ATT_BIAS_TILES):
        for m in range(2):
            for c0 in range(0, TQ, 128):
                cols = pl.ds(m * TQ + c0, 128)
                rel_min = (d - 2) * TK - (c0 + 127)
                rel_max = (d - 2) * TK + TK - 1 - c0
                if rel_max <= -far or rel_min >= far:
                    last = half - 1 if rel_max < 0 else REL_BUCKETS - 1
                    out_ref[d, :, cols] = jnp.full((TK, 128), table_ref[last, 2 * h + m] * LOG2E, F32)
                    continue
                for r0 in range(0, TK, 128):
                    bk = bucket_ref[d, r0:r0 + 128, c0:c0 + 128]

                    def body(b, acc, bk=bk, m=m):
                        return jnp.where(bk == b, table_ref[b, 2 * h + m] * LOG2E, acc)

                    out_ref[d, r0:r0 + 128, cols] = lax.fori_loop(0, REL_BUCKETS, body, jnp.zeros((128, 128), F32))


def _bias_tiles(rel_bias, TQ, TK):
    assert TK > REL_MAX_DIST
    base = jnp.arange(TK, dtype=jnp.int32)[:, None] - jnp.arange(TQ, dtype=jnp.int32)[None, :]
    bucket = jnp.stack([_rel_bucket(base + (d - 2) * TK) for d in range(ATT_BIAS_TILES)])
    return pl.pallas_call(
        functools.partial(_bias_kernel, TQ=TQ, TK=TK),
        grid=(N_HEADS,),
        in_specs=[pl.BlockSpec(memory_space=pltpu.SMEM), _const_spec((ATT_BIAS_TILES, TK, TQ))],
        out_specs=pl.BlockSpec((None, ATT_BIAS_TILES, TK, 2 * TQ), lambda h: (h, 0, 0, 0)),
        out_shape=jax.ShapeDtypeStruct((N_HEADS, ATT_BIAS_TILES, TK, 2 * TQ), F32),
        compiler_params=_params(("parallel",)),
        name="rel_bias_tiles",
    )(rel_bias.astype(F32), bucket)


def _attn_kernel(lam_ref, brange_ref, qt_ref, k_ref, vt_ref, bt_ref, kn_ref, g_ref, o_ref, acc_sc, p_sc,
                 *, TQ, T, nk):
    h = pl.program_id(1)
    i = pl.program_id(2)
    qt = qt_ref[...]
    sub = lax.broadcasted_iota(jnp.int32, qt.shape, 0)
    zero = jnp.zeros_like(qt)
    q2t = jnp.concatenate([jnp.where(sub < QK_DIM, qt, zero), jnp.where(sub >= QK_DIM, qt, zero)], axis=1)
    acc_sc[...] = jnp.zeros(acc_sc.shape, F32)
    ones = jnp.ones((ONES_ROWS, T), BF16)

    def bias_tile(j):
        return bt_ref[jnp.clip(j - ATT_QK_RATIO * i, -2, ATT_QK_RATIO + 1) + 2]

    qsq = qt.astype(F32) ** 2
    qn = jnp.sqrt(jnp.concatenate([jnp.sum(qsq[:QK_DIM], axis=0, keepdims=True),
                                   jnp.sum(qsq[QK_DIM:], axis=0, keepdims=True)], axis=1))
    kn2 = jnp.max(kn_ref[...], axis=0, keepdims=True)
    klane = lax.broadcasted_iota(jnp.int32, kn2.shape, 1)
    kmax = [jnp.sqrt(jnp.max(jnp.where(klane == 2 * h + mp, kn2, 0.0), axis=1, keepdims=True)) for mp in range(2)]
    first = lax.broadcasted_iota(jnp.int32, qn.shape, 1) < TQ
    reach = qn * jnp.where(first, kmax[0], kmax[1]) * BOUND_SLACK
    bmax = jnp.where(first, brange_ref[h, 0], brange_ref[h, 1])
    bmin = jnp.where(first, brange_ref[h, 2], brange_ref[h, 3])
    shift = (reach + bmax).astype(BF16)
    bounded = jnp.max(2.0 * reach + (bmax - bmin)) < EXP2_SPAN

    @pl.when(bounded)
    def _():
        aug = jnp.where(lax.broadcasted_iota(jnp.int32, (HEAD_W, 2 * TQ), 0) == 0,
                        jnp.broadcast_to(-shift.astype(F32), (HEAD_W, 2 * TQ)), 0.0)
        qaug = jnp.concatenate([q2t, aug.astype(BF16)], axis=0)
        lane = lax.broadcasted_iota(jnp.int32, (T, HEAD_W), 1)
        onescol = jnp.where(lane == 0, 1.0, 0.0).astype(BF16)
        ones4 = jnp.ones((ONES_ROWS, 4 * T), BF16)

        def quad(u, carry):
            for r in range(4):
                j = 4 * u + r
                kb = jnp.concatenate([k_ref[pl.ds(pl.multiple_of(j * T, T), T), :], onescol], axis=1)
                s = jnp.dot(kb, qaug, preferred_element_type=F32) + bias_tile(j)
                p_sc[r * T:(r + 1) * T, :] = jnp.exp2(s).astype(BF16)
            vb = jnp.concatenate([vt_ref[:, pl.ds(pl.multiple_of(4 * u * T, 4 * T), 4 * T)], ones4], axis=0)
            acc_sc[...] += jnp.dot(vb, p_sc[...], preferred_element_type=F32)
            return carry

        lax.fori_loop(0, nk // 4, quad, 0)

    @pl.when(jnp.logical_not(bounded))
    def _():
        def scores_softmax(j, slot, m):
            kb = k_ref[pl.ds(pl.multiple_of(j * T, T), T), :]
            s = jnp.dot(kb, q2t, preferred_element_type=F32) + bias_tile(j)
            m_new = jnp.maximum(m, jnp.max(s, axis=0, keepdims=True))
            p_sc[slot * T:(slot + 1) * T, :] = jnp.exp2(s - m_new).astype(BF16)
            return m_new, jnp.exp2(m - m_new)

        def accumulate(j, slot, alpha):
            vb = jnp.concatenate([vt_ref[:, pl.ds(pl.multiple_of(j * T, T), T)], ones], axis=0)
            acc_sc[...] = alpha * acc_sc[...] + jnp.dot(vb, p_sc[slot * T:(slot + 1) * T, :],
                                                        preferred_element_type=F32)

        m = jnp.full((1, 2 * TQ), -jnp.inf, F32)
        m, alpha = scores_softmax(0, 0, m)

        def pair(u, carry):
            m, alpha = carry
            t = 2 * u
            m, alpha1 = scores_softmax(t - 1, 1, m)
            accumulate(t - 2, 0, alpha)
            m, alpha0 = scores_softmax(t, 0, m)
            accumulate(t - 1, 1, alpha1)
            return m, alpha0

        m, alpha = lax.fori_loop(1, nk // 2, pair, (m, alpha))
        m, alpha1 = scores_softmax(nk - 1, 1, m)
        accumulate(nk - 2, 0, alpha)
        accumulate(nk - 1, 1, alpha1)

    o = acc_sc[0:HEAD_W, :] / acc_sc[HEAD_W:HEAD_W + 1, :]
    od = o[:, :TQ] - lam_ref[0] * o[:, TQ:]
    y = od * lax.rsqrt(jnp.mean(od * od, axis=0, keepdims=True) + EPS) * g_ref[...]
    o_ref[...] = (y * (1.0 - LAM_INIT)).T.astype(o_ref.dtype)


def _attention(qt, k, vt, kn, btiles, brange, lam, g_col, *, T=ATT_BLOCK):
    B, L, _ = k.shape
    TQ = ATT_QK_RATIO * T
    nk = L // T
    assert nk % 4 == 0 and L % TQ == 0
    return pl.pallas_call(
        functools.partial(_attn_kernel, TQ=TQ, T=T, nk=nk),
        grid=(B, N_HEADS, L // TQ),
        in_specs=[pl.BlockSpec(memory_space=pltpu.SMEM),
                  pl.BlockSpec(memory_space=pltpu.SMEM),
                  pl.BlockSpec((None, HEAD_W, TQ), lambda b, h, i: (b, h, i)),
                  pl.BlockSpec((None, L, HEAD_W), lambda b, h, i: (b, 0, h)),
                  pl.BlockSpec((None, HEAD_W, L), lambda b, h, i: (b, h, 0)),
                  pl.BlockSpec((None, ATT_BIAS_TILES, T, 2 * TQ), lambda b, h, i: (h, 0, 0, 0)),
                  pl.BlockSpec((None,) + kn.shape[1:], lambda b, h, i: (b, 0, 0)),
                  _const_spec((HEAD_W, 1))],
        out_specs=pl.BlockSpec((None, TQ, HEAD_W), lambda b, h, i: (b, i, h)),
        out_shape=jax.ShapeDtypeStruct((B, L, ATT_W), BF16),
        scratch_shapes=[pltpu.VMEM((HEAD_W + ONES_ROWS, 2 * TQ), F32), pltpu.VMEM((4 * T, 2 * TQ), BF16)],
        compiler_params=_params(("parallel", "parallel", "parallel")),
        name="diff_attention",
    )(lam, brange, qt, k, vt, btiles, kn, g_col)


def _conv_kernel(x_ref, prev_ref, next_ref, dt_ref, dtt_ref, w_ref, b_ref, dtb_ref, dtbt_ref,
                 xs_ref, bc_ref, dts_ref, dtst_ref, *, tl, nt):
    i = pl.program_id(1)
    R = CONV_ROWS
    row = lax.broadcasted_iota(jnp.int32, (R, 128), 0)
    for r0 in range(0, tl, R):
        for c0 in range(0, CONV_CH, 128):
            cols = slice(c0, c0 + 128)
            x = x_ref[r0:r0 + R, cols]
            above = x_ref[r0 - 1:r0, cols] if r0 > 0 else jnp.where(i > 0, prev_ref[7:8, cols], 0.0)
            below = x_ref[r0 + R:r0 + R + 1, cols] if r0 + R < tl else jnp.where(i < nt - 1, next_ref[0:1, cols], 0.0)
            xm1 = jnp.where(row == 0, above, pltpu.roll(x, 1, 0))
            xp1 = jnp.where(row == R - 1, below, pltpu.roll(x, R - 1, 0))
            y = w_ref[0:1, cols] * xm1 + w_ref[1:2, cols] * x + w_ref[2:3, cols] * xp1 + b_ref[:, cols]
            y = (y * jax.nn.sigmoid(y)).astype(BF16)
            if c0 < SSM_INNER:
                xs_ref[r0:r0 + R, cols] = y
            else:
                bc_ref[r0:r0 + R, c0 - SSM_INNER:c0 - SSM_INNER + 128] = y

    def softplus(v):
        return jnp.maximum(v, 0.0) + jnp.log1p(jnp.exp(-jnp.abs(v)))

    dts_ref[...] = softplus(dt_ref[...] + dtb_ref[...])
    dtst_ref[...] = softplus(dtt_ref[...] + dtbt_ref[...])


def _conv(xbc, dt, dtt, conv_w, conv_b, dtb, dtbt, *, tl=512):
    B, L, _ = xbc.shape
    nt = L // tl
    r8 = tl // 8
    ndt = 2 * SSM_HEADS
    row = lambda w: pl.BlockSpec((None, tl, w), lambda b, i: (b, i, 0))
    colspec = pl.BlockSpec((None, ndt, tl), lambda b, i: (b, 0, i))
    return pl.pallas_call(
        functools.partial(_conv_kernel, tl=tl, nt=nt),
        grid=(B, nt),
        in_specs=[row(CONV_CH),
                  pl.BlockSpec((None, 8, CONV_CH), lambda b, i: (b, jnp.maximum(i * r8 - 1, 0), 0)),
                  pl.BlockSpec((None, 8, CONV_CH), lambda b, i: (b, jnp.minimum((i + 1) * r8, L // 8 - 1), 0)),
                  row(ndt), colspec,
                  _const_spec((3, CONV_CH)), _const_spec((1, CONV_CH)), _const_spec((1, ndt)),
                  _const_spec((ndt, 1))],
        out_specs=[row(SSM_INNER), row(2 * SSM_GROUPS * SSM_N), row(ndt), colspec],
        out_shape=[jax.ShapeDtypeStruct((B, L, SSM_INNER), BF16),
                   jax.ShapeDtypeStruct((B, L, 2 * SSM_GROUPS * SSM_N), BF16),
                   jax.ShapeDtypeStruct((B, L, ndt), F32),
                   jax.ShapeDtypeStruct((B, ndt, L), F32)],
        compiler_params=_params(("parallel", "parallel")),
        name="ssm_conv",
    )(xbc, xbc, xbc, dt, dtt, conv_w, conv_b, dtb, dtbt)


def _ssd_chunk(xs_ref, bc_ref, dts_ref, dtst_ref, alog_ref, alogt_ref, st_ref, *, rev):
    Q = SSM_CHUNK
    HI = lax.Precision.HIGHEST
    off = SSM_HEADS if rev else 0
    ri = lax.broadcasted_iota(jnp.int32, (Q, Q), 0)
    ci = lax.broadcasted_iota(jnp.int32, (Q, Q), 1)
    incl = (ci >= ri) if rev else (ci <= ri)
    tri = incl.astype(F32)
    dt = dts_ref[:, off:off + SSM_HEADS]
    dtt = dtst_ref[off:off + SSM_HEADS, :]
    a = dt * (-jnp.exp(alog_ref[:, off:off + SSM_HEADS]))
    at = dtt * (-jnp.exp(alogt_ref[off:off + SSM_HEADS, :]))
    acum = jnp.dot(tri, a, preferred_element_type=F32, precision=HI)
    acumt = _nt_dot(at, tri, precision=HI)
    edge = 0 if rev else Q - 1
    bc = bc_ref[...]
    bpair = bc[:, :Q]
    cpair = bc[:, Q:]
    low = ci < SSM_N
    rlow = ri < SSM_N
    blockmask = low == rlow
    total = acumt[:, edge:edge + 1]
    decay = jnp.exp(total)
    w_rows = dtt * jnp.exp(total - acumt)
    ys = []
    for g in range(SSM_GROUPS):
        gmask = low if g == 0 else jnp.logical_not(low)
        cg = jnp.where(gmask, cpair, jnp.zeros_like(cpair))
        cb = _nt_dot(cg, bpair)
        dupm = ((ri - g * SSM_N) == (ci % SSM_N)).astype(BF16)
        cdup = jnp.dot(cpair, dupm, preferred_element_type=F32)
        sel2 = ((ci - g * SSM_N) == (ri % SSM_N)).astype(BF16)
        bt2 = _nt_dot(sel2, bpair)
        for hp in range(SSM_HEADS // SSM_GROUPS // 2):
            h0 = g * (SSM_HEADS // SSM_GROUPS) + 2 * hp
            pair = h0 // 2
            xpair = xs_ref[:, h0 * SSM_P:(h0 + 2) * SSM_P]
            ms = []
            e_cols = []
            for h in (h0, h0 + 1):
                e_col = jnp.broadcast_to(acum[:, h:h + 1], (Q, Q))
                r_row = jnp.broadcast_to(acumt[h:h + 1, :], (Q, Q))
                seg = jnp.where(incl, e_col - r_row, -jnp.inf)
                ms.append((cb * jnp.exp(seg) * jnp.broadcast_to(dtt[h:h + 1, :], (Q, Q))).astype(BF16))
                e_cols.append(e_col)
            st = st_ref[pair]
            xblk = jnp.concatenate([jnp.where(low, xpair, jnp.zeros_like(xpair)),
                                    jnp.where(low, jnp.zeros_like(xpair), xpair)], axis=0)
            y = jnp.dot(jnp.concatenate(ms, axis=1), xblk, preferred_element_type=F32)
            ce = (cdup * jnp.exp(jnp.where(low, e_cols[0], e_cols[1]))).astype(BF16)
            y = y + jnp.dot(ce, st.astype(BF16), preferred_element_type=F32)
            ys.append(y)
            w_pair = jnp.where(rlow, jnp.broadcast_to(w_rows[h0:h0 + 1, :], (Q, Q)),
                               jnp.broadcast_to(w_rows[h0 + 1:h0 + 2, :], (Q, Q)))
            snew = jnp.dot((bt2 * w_pair).astype(BF16), xpair, preferred_element_type=F32)
            dec = jnp.where(rlow, decay[h0:h0 + 1, :], decay[h0 + 1:h0 + 2, :])
            st_ref[pair] = dec * st + jnp.where(blockmask, snew, 0.0)
    return jnp.concatenate(ys, axis=1)


def _ssd_fwd_kernel(xs_ref, bc_ref, dts_ref, dtst_ref, alog_ref, alogt_ref, y_ref, st_ref):
    @pl.when(pl.program_id(1) == 0)
    def _():
        st_ref[...] = jnp.zeros(st_ref.shape, F32)

    y = _ssd_chunk(xs_ref, bc_ref, dts_ref, dtst_ref, alog_ref, alogt_ref, st_ref, rev=False)
    y_ref[...] = y.astype(y_ref.dtype)


def _ssd_bwd_kernel(xs_ref, bc_ref, dts_ref, dtst_ref, alog_ref, alogt_ref, yf_ref, z_ref, dskip_ref, g_ref,
                    o_ref, st_ref):
    @pl.when(pl.program_id(1) == 0)
    def _():
        st_ref[...] = jnp.zeros(st_ref.shape, F32)

    yb = _ssd_chunk(xs_ref, bc_ref, dts_ref, dtst_ref, alog_ref, alogt_ref, st_ref, rev=True)
    z = z_ref[...].astype(F32)
    y = (yf_ref[...].astype(F32) + yb + xs_ref[...].astype(F32) * dskip_ref[...]) * (z * jax.nn.sigmoid(z))
    y = y * lax.rsqrt(jnp.mean(y * y, axis=-1, keepdims=True) + EPS) * g_ref[...]
    o_ref[...] = y.astype(o_ref.dtype)


def _ssd(xs, bc, dts, dtst, alog, alogt, z, dskip, g_norm):
    B, L, _ = xs.shape
    Q = SSM_CHUNK
    nc = L // Q
    ndt = 2 * SSM_HEADS
    npair = SSM_HEADS // 2

    def specs(cidx):
        row = lambda w: pl.BlockSpec((None, Q, w), lambda b, c: (b, cidx(c), 0))
        return row, [row(SSM_INNER), row(2 * SSM_GROUPS * SSM_N), row(ndt),
                     pl.BlockSpec((None, ndt, Q), lambda b, c: (b, 0, cidx(c))),
                     _const_spec((1, ndt)), _const_spec((ndt, 1))]

    scratch = [pltpu.VMEM((npair, 2 * SSM_N, 2 * SSM_P), F32)]
    row, in_specs = specs(lambda c: c)
    yf = pl.pallas_call(
        _ssd_fwd_kernel,
        grid=(B, nc),
        in_specs=in_specs,
        out_specs=row(SSM_INNER),
        out_shape=jax.ShapeDtypeStruct((B, L, SSM_INNER), BF16),
        scratch_shapes=scratch,
        compiler_params=_params(("parallel", "arbitrary")),
        name="ssd_forward",
    )(xs, bc, dts, dtst, alog, alogt)
    row, in_specs = specs(lambda c: nc - 1 - c)
    return pl.pallas_call(
        _ssd_bwd_kernel,
        grid=(B, nc),
        in_specs=in_specs + [row(SSM_INNER), row(SSM_INNER), _const_spec((1, SSM_INNER)),
                             _const_spec((1, SSM_INNER))],
        out_specs=row(SSM_INNER),
        out_shape=jax.ShapeDtypeStruct((B, L, SSM_INNER), BF16),
        scratch_shapes=scratch,
        compiler_params=_params(("parallel", "arbitrary")),
        name="ssd_reverse_gate_norm",
    )(xs, bc, dts, dtst, alog, alogt, yf, z, dskip, g_norm)


def _merge_kernel(att_ref, ssm_ref, gates_ref, x_ref, wau_ref, wsu_ref, wout_ref, gffn_ref, wr_ref, wrt_ref,
                  place_ref, h_ref, u2_ref, afft_ref):
    D = D_MODEL
    E = N_EXPERTS
    ya = jnp.dot(att_ref[...], wau_ref[...], preferred_element_type=F32)
    ysm = jnp.dot(ssm_ref[...], wsu_ref[...], preferred_element_type=F32)
    merged = gates_ref[:, :D] * ya + gates_ref[:, D:] * ysm
    h = x_ref[...] + jnp.dot(merged.astype(BF16), wout_ref[...], preferred_element_type=F32)
    h_ref[...] = h
    u2 = (h * lax.rsqrt(jnp.mean(h * h, axis=-1, keepdims=True) + EPS) * gffn_ref[...]).astype(BF16)
    u2_ref[:, :D] = u2
    logits = jnp.dot(u2, wr_ref[...], preferred_element_type=F32)
    e = jnp.exp(logits - jnp.max(logits, axis=-1, keepdims=True))
    aff = e / jnp.sum(e, axis=-1, keepdims=True)
    hi = aff.astype(BF16)
    r1 = aff - hi.astype(F32)
    mid = r1.astype(BF16)
    lo = (r1 - mid.astype(F32)).astype(BF16)
    ext = (jnp.dot(hi, place_ref[0:E, :], preferred_element_type=F32)
           + jnp.dot(mid, place_ref[E:2 * E, :], preferred_element_type=F32)
           + jnp.dot(lo, place_ref[2 * E:3 * E, :], preferred_element_type=F32))
    u2_ref[:, D:] = ext.astype(BF16)
    lt = _nt_dot(wrt_ref[...], u2)
    et = jnp.exp(lt - jnp.max(lt, axis=0, keepdims=True))
    afft_ref[...] = et / jnp.sum(et, axis=0, keepdims=True)


def _merge(att, ssm, gates, x, wau, wsu, wout, g_ffn, wr, *, tm=256):
    B, L, D = x.shape
    E = N_EXPERTS
    nl = L // tm
    row = lambda w: pl.BlockSpec((None, tm, w), lambda b, i: (b, i, 0))
    place = jnp.eye(3 * E, GATE_EXT, dtype=BF16)
    return pl.pallas_call(
        _merge_kernel,
        grid=(B, nl),
        in_specs=[row(ATT_W), row(SSM_INNER), row(2 * D), row(D), _const_spec(wau.shape),
                  _const_spec(wsu.shape), _const_spec(wout.shape), _const_spec((1, D)), _const_spec(wr.shape),
                  _const_spec((E, D)), _const_spec((3 * E, GATE_EXT))],
        out_specs=[row(D), row(D + GATE_EXT), pl.BlockSpec((E, tm), lambda b, i: (0, b * nl + i))],
        out_shape=[jax.ShapeDtypeStruct((B, L, D), F32), jax.ShapeDtypeStruct((B, L, D + GATE_EXT), BF16),
                   jax.ShapeDtypeStruct((E, B * L), F32)],
        compiler_params=_params(("parallel", "parallel")),
        name="merge_router",
    )(att, ssm, gates, x, wau, wsu, wout, g_ffn, wr, wr.T, place)


def _route_kernel(afft_ref, posm_ref, bstart_ref, *, cap, nb):
    E = N_EXPERTS
    W = ROUTE_BLK

    def bit_body(k, thr):
        cand = thr | lax.shift_left(jnp.int32(1), 30 - k)
        x = lax.bitcast_convert_type(afft_ref[...], jnp.int32)
        cnt = jnp.sum(jnp.where(x >= cand, 1.0, 0.0), axis=1, keepdims=True)
        return jnp.where(cnt >= cap, cand, thr)

    thr = lax.fori_loop(0, 31, bit_body, jnp.zeros((E, 1), jnp.int32))
    x = lax.bitcast_convert_type(afft_ref[...], jnp.int32)
    n_gt = jnp.sum(jnp.where(x > thr, 1.0, 0.0), axis=1, keepdims=True)
    need = cap - n_gt
    ri = lax.broadcasted_iota(jnp.int32, (W, W), 0)
    ci = lax.broadcasted_iota(jnp.int32, (W, W), 1)
    upper = jnp.where(ri <= ci, 1.0, 0.0).astype(BF16)
    bstart_ref[...] = jnp.zeros(bstart_ref.shape, jnp.int32)
    c_eq = jnp.zeros((E, 1), F32)
    c_sel = jnp.zeros((E, 1), F32)
    for b in range(nb):
        xb = lax.bitcast_convert_type(afft_ref[:, b * W:(b + 1) * W], jnp.int32)
        eq = xb == thr
        eqc = jnp.dot(jnp.where(eq, 1.0, 0.0).astype(BF16), upper, preferred_element_type=F32) + c_eq
        sel = (xb > thr) | (eq & (eqc <= need))
        selc = jnp.dot(jnp.where(sel, 1.0, 0.0).astype(BF16), upper, preferred_element_type=F32) + c_sel
        posm_ref[:, b * W:(b + 1) * W] = jnp.where(sel, selc, 0.0).astype(jnp.int32)
        bstart_ref[:, b:b + 1] = c_sel.astype(jnp.int32)
        c_eq = eqc[:, W - 1:W]
        c_sel = selc[:, W - 1:W]
    bstart_ref[:, nb:nb + 1] = c_sel.astype(jnp.int32)


def _route(afft, cap):
    E, T = afft.shape
    nb = T // ROUTE_BLK
    assert nb < BSTART_W and cap % SLOT_TILE == 0
    return pl.pallas_call(
        functools.partial(_route_kernel, cap=float(cap), nb=nb),
        out_shape=[jax.ShapeDtypeStruct((E, T), jnp.int32), jax.ShapeDtypeStruct((E, BSTART_W), jnp.int32)],
        compiler_params=_params(None),
        name="route_select",
    )(afft)


def _pair_schedule(bstart, cap, ngb):
    step = GATHER_BLK // ROUTE_BLK
    st = bstart[:, 0:ngb * step + 1:step]
    lo, hi = st[:, :-1], st[:, 1:]
    ntile = cap // SLOT_TILE
    t_lo = jnp.arange(ntile, dtype=jnp.int32) * SLOT_TILE
    t_hi = t_lo + SLOT_TILE
    b_first = jnp.sum(hi[:, None, :] <= t_lo[None, :, None], axis=2).astype(jnp.int32)
    b_last = jnp.sum(lo[:, None, :] < t_hi[None, :, None], axis=2).astype(jnp.int32) - 1
    return b_first, b_last - b_first + 1


def _expert_kernel(first_ref, cnt_ref, bs_ref, u_hbm, posm_ref, wg_ref, wu_ref, wd_ref, o_ref, x_sc, ubuf, sem):
    D = D_MODEL
    e = pl.program_id(0)
    j = pl.program_id(1)
    g = e * pl.num_programs(1) + j
    b0 = first_ref[g]
    n = cnt_ref[g]

    half = GATHER_BLK // 2

    def half_copies(b, slot):
        return [pltpu.make_async_copy(
            u_hbm.at[pl.ds(pl.multiple_of(b * GATHER_BLK + r * half, half), half)],
            ubuf.at[slot, pl.ds(r * half, half)], sem.at[slot, r]) for r in range(2)]

    def start(b, slot):
        for cp in half_copies(b, slot):
            cp.start()

    ahead = GATHER_BUFS - 1

    def start_tile(t):
        start(first_ref[t], 0)
        for r in range(1, ahead):
            @pl.when(cnt_ref[t] > r)
            def _(r=r):
                start(first_ref[t] + r, r)

    @pl.when(g == 0)
    def _():
        start_tile(0)

    x_sc[...] = jnp.zeros(x_sc.shape, F32)
    WM = GATHER_ROWS
    row_ids = lax.broadcasted_iota(jnp.int32, (WM, GATHER_BLK), 0)
    per_blk = GATHER_BLK // ROUTE_BLK

    def body(c, carry):
        slot = c % GATHER_BUFS
        for cp in half_copies(b0 + c, slot):
            cp.wait()

        @pl.when(c + ahead < n)
        def _():
            start(b0 + c + ahead, (c + ahead) % GATHER_BUFS)

        blk = b0 + c
        posrow = posm_ref[:, pl.ds(pl.multiple_of(blk * GATHER_BLK, GATHER_BLK), GATHER_BLK)] - (j * SLOT_TILE + 1)
        lo = jnp.maximum(bs_ref[e * BSTART_W + blk * per_blk] - j * SLOT_TILE, 0)
        hi = jnp.minimum(bs_ref[e * BSTART_W + (blk + 1) * per_blk] - j * SLOT_TILE, SLOT_TILE)
        w0 = (lo // 16) * 16

        def window(k, carry):
            wlo = w0 + k * WM
            top = pl.multiple_of(jnp.minimum(wlo, SLOT_TILE - WM), 16)
            hit = (posrow >= wlo) & (posrow < wlo + WM) & (posrow - top == row_ids)
            onehot = jnp.where(hit, 1.0, 0.0).astype(BF16)
            x_sc[pl.ds(top, WM), :] += jnp.dot(onehot, ubuf[slot], preferred_element_type=F32)
            return carry

        lax.fori_loop(0, jnp.maximum((hi - w0 + WM - 1) // WM, 0), window, 0)
        return carry

    lax.fori_loop(0, n, body, 0)

    @pl.when(g + 1 < pl.num_programs(0) * pl.num_programs(1))
    def _():
        start_tile(g + 1)

    xs = x_sc[:, :D].astype(BF16)
    ext = x_sc[:, D:]
    lane = lax.broadcasted_iota(jnp.int32, ext.shape, 1)
    mine = (lane == e) | (lane == e + N_EXPERTS) | (lane == e + 2 * N_EXPERTS)
    gate = jnp.sum(jnp.where(mine, ext, 0.0), axis=1, keepdims=True)
    hg = jnp.dot(xs, wg_ref[...], preferred_element_type=F32)
    hu = jnp.dot(xs, wu_ref[...], preferred_element_type=F32)
    hmid = (hg * jax.nn.sigmoid(hg) * hu).astype(BF16)
    o_ref[...] = jnp.dot(hmid, wd_ref[...], preferred_element_type=F32) * gate


def _experts(u2ext, posm, bstart, b_first, b_cnt, wg, wu, wd, cap):
    T, DX = u2ext.shape
    E, D = N_EXPERTS, D_MODEL
    wspec = pl.BlockSpec((None, D, D), lambda e, j, bf, bc, bs: (e, 0, 0))
    return pl.pallas_call(
        _expert_kernel,
        grid_spec=pltpu.PrefetchScalarGridSpec(
            num_scalar_prefetch=3,
            grid=(E, cap // SLOT_TILE),
            in_specs=[pl.BlockSpec(memory_space=pl.ANY),
                      pl.BlockSpec((None, 1, T), lambda e, j, bf, bc, bs: (e, 0, 0)),
                      wspec, wspec, wspec],
            out_specs=pl.BlockSpec((None, SLOT_TILE, D), lambda e, j, bf, bc, bs: (e, j, 0)),
            scratch_shapes=[pltpu.VMEM((SLOT_TILE, DX), F32), pltpu.VMEM((GATHER_BUFS, GATHER_BLK, DX), BF16),
                            pltpu.SemaphoreType.DMA((GATHER_BUFS, 2))]),
        out_shape=jax.ShapeDtypeStruct((E, cap, D), F32),
        compiler_params=_params(("arbitrary", "arbitrary")),
        name="experts_gather_mlp",
    )(b_first.reshape(-1), b_cnt.reshape(-1), bstart.reshape(-1), u2ext, posm.reshape(E, 1, T), wg, wu, wd)


def _combine_kernel(bstart_ref, posm_ref, ye_ref, h_ref, p_ref, wpg_ref, bpg_ref, wpp_ref, gfin_ref, o_ref,
                    moe, buf, xbuf, sem, xsem, *, cap, nb):
    E = N_EXPERTS
    WN = COMBINE_WIN
    b = pl.program_id(0)
    top = cap - WN
    cur = b % 2

    def window(blk, slot, e):
        base = jnp.minimum((bstart_ref[e, blk] // 8) * 8, top)
        return pltpu.make_async_copy(ye_ref.at[e, pl.ds(pl.multiple_of(base, 8), WN)],
                                     buf.at[slot, pl.ds(e * WN, WN)], sem.at[slot, e])

    @pl.when(b == 0)
    def _():
        for e in range(E):
            window(0, 0, e).start()

    @pl.when(b + 1 < nb)
    def _():
        for e in range(E):
            window(b + 1, 1 - cur, e).start()

    lows = [(bstart_ref[e, b] // 8) * 8 for e in range(E)]
    w_iota = lax.broadcasted_iota(jnp.int32, (WN, ROUTE_BLK), 0)

    def onehot_rows(slot0, low, base):
        ok = (slot0 >= low) & (slot0 < low + WN)
        return jnp.where(ok & (slot0 - base == w_iota), 1.0, 0.0).astype(BF16)

    def scatter(onehot, rows):
        hi = rows.astype(BF16)
        lo = (rows - hi.astype(F32)).astype(BF16)
        tn = lambda a, c: lax.dot_general(a, c, (((0,), (0,)), ((), ())), preferred_element_type=F32)
        return tn(onehot, hi) + tn(onehot, lo)

    hots = []
    for e in range(E):
        slot0 = posm_ref[e:e + 1, :] - 1
        hots.append(onehot_rows(slot0, lows[e], jnp.minimum(lows[e], top)))
    for e in range(E):
        window(b, cur, e).wait()
    moe[...] = scatter(jnp.concatenate(hots, axis=0), buf[cur])

    def per_expert(e, carry):
        low0 = (bstart_ref[e, b] // 8) * 8
        nwin = (bstart_ref[e, b + 1] - low0 + WN - 1) // WN

        def per_window(c, carry):
            low = low0 + c * WN
            base = jnp.minimum(low, top)
            cp = pltpu.make_async_copy(ye_ref.at[e, pl.ds(pl.multiple_of(base, 8), WN)], xbuf, xsem.at[0])
            cp.start()
            cp.wait()
            mine = lax.broadcasted_iota(jnp.int32, (E, ROUTE_BLK), 0) == e
            slot0 = jnp.sum(jnp.where(mine, posm_ref[...], 0), axis=0, keepdims=True) - 1
            moe[...] += scatter(onehot_rows(slot0, low, base), xbuf[...])
            return carry

        return lax.fori_loop(1, nwin, per_window, carry)

    lax.fori_loop(0, E, per_expert, 0)

    h = h_ref[...] + moe[...]
    gate = jax.nn.sigmoid(jnp.dot(h.astype(BF16), wpg_ref[...], preferred_element_type=F32) + bpg_ref[...])
    h = h + gate * jnp.dot(p_ref[...].astype(BF16), wpp_ref[...], preferred_element_type=F32)
    o_ref[...] = h * lax.rsqrt(jnp.mean(h * h, axis=-1, keepdims=True) + EPS) * gfin_ref[...]


def _combine_ple(ye, posm, bstart, cap, h, p, wpg, bpg, wpp, g_final):
    E, T = posm.shape
    D = D_MODEL
    nb = T // ROUTE_BLK
    assert cap >= COMBINE_WIN
    row = lambda w: pl.BlockSpec((ROUTE_BLK, w), lambda b, bs: (b, 0))
    const = lambda shape: pl.BlockSpec(shape, lambda b, bs: (0,) * len(shape))
    return pl.pallas_call(
        functools.partial(_combine_kernel, cap=cap, nb=nb),
        grid_spec=pltpu.PrefetchScalarGridSpec(
            num_scalar_prefetch=1,
            grid=(nb,),
            in_specs=[pl.BlockSpec((E, ROUTE_BLK), lambda b, bs: (0, b)),
                      pl.BlockSpec(memory_space=pl.ANY), row(D), row(PLE_DIM), const(wpg.shape), const((1, D)),
                      const(wpp.shape), const((1, D))],
            out_specs=row(D),
            scratch_shapes=[pltpu.VMEM((ROUTE_BLK, D), F32), pltpu.VMEM((2, E * COMBINE_WIN, D), F32),
                            pltpu.VMEM((COMBINE_WIN, D), F32),
                            pltpu.SemaphoreType.DMA((2, E)), pltpu.SemaphoreType.DMA((1,))]),
        out_shape=jax.ShapeDtypeStruct((T, D), F32),
        compiler_params=_params(("arbitrary",)),
        name="experts_combine_ple",
    )(bstart, posm, ye, h.reshape(T, D), p.reshape(T, PLE_DIM), wpg, bpg, wpp, g_final)


def _forward(x, p, w, btiles):
    B, L, D = x.shape
    T = B * L
    qt, k, vt, z, xbc, dt, dtt, gates, kn = _inproj(x, w['g_mix'], w['wqt'], w['wk'], w['wvt'], w['wz'],
                                                    w['wxbc'], w['wdt'], w['wdtt'], w['wgate'], w['bgate'])
    att = _attention(qt, k, vt, kn, btiles, w['brange'], w['lam'], w['g_subln'])
    xs, bc, dts, dtst = _conv(xbc, dt, dtt, w['conv_w'], w['conv_b'], w['dtb'], w['dtbt'])
    ssm = _ssd(xs, bc, dts, dtst, w['alog'], w['alogt'], z, w['dskip'], w['g_ssm_norm'])
    h, u2ext, afft = _merge(att, ssm, gates, x, w['wau'], w['wsu'], w['wout'], w['g_ffn'], w['wr'])
    cap = max(1, CAPACITY_FACTOR * T // N_EXPERTS)
    posm, bstart = _route(afft, cap)
    b_first, b_cnt = _pair_schedule(bstart, cap, T // GATHER_BLK)
    ye = _experts(u2ext.reshape(T, D + GATE_EXT), posm, bstart, b_first, b_cnt, w['weg'], w['weu'], w['wed'], cap)
    out = _combine_ple(ye, posm, bstart, cap, h, p, w['wpg'], w['bpg'], w['wpp'], w['g_final'])
    return out.reshape(B, L, D)


def kernel(x_prompt, x_sample, p_prompt, p_sample, rel_bias, g_mix, w_in, conv_w, conv_b, dt_bias, a_log,
           d_skip, g_ssm_norm, lam_params, g_subln, w_att_up, w_ssm_up, w_gate, b_gate, w_out, g_ffn,
           w_router, w_exp_gate, w_exp_up, w_exp_down, w_ple_gate, b_ple_gate, w_ple_proj, g_final):
    i = 0
    D = D_MODEL
    win = w_in[i]
    o1 = ATT_W
    o3 = 3 * ATT_W
    o4 = o3 + SSM_INNER
    o5 = o4 + CONV_CH
    wqt = (win[:, :o1] * (QK_DIM ** -0.5 * LOG2E)).T.astype(BF16)
    wk = win[:, o1:2 * o1].astype(BF16)
    wvt = win[:, 2 * o1:o3].T.astype(BF16)
    wdt = win[:, o5:].astype(BF16)
    lp = lam_params[i].astype(F32)
    lam = jnp.exp(jnp.sum(lp[0] * lp[1])) - jnp.exp(jnp.sum(lp[2] * lp[3])) + LAM_INIT
    ndt = 2 * SSM_HEADS
    tb = rel_bias.astype(F32).reshape(REL_BUCKETS, N_HEADS, 2) * LOG2E
    brange = jnp.concatenate([jnp.max(tb, axis=0), jnp.min(tb, axis=0)], axis=1)
    w = {
        'g_mix': g_mix[i].reshape(1, D), 'wqt': wqt, 'wk': wk, 'wvt': wvt, 'wz': win[:, o3:o4].astype(BF16),
        'wxbc': win[:, o4:o5].astype(BF16), 'wdt': wdt, 'wdtt': wdt.T,
        'wgate': w_gate[i].astype(BF16), 'bgate': b_gate[i].reshape(1, 2 * D),
        'lam': lam.reshape(1), 'g_subln': g_subln[i].reshape(HEAD_W, 1), 'brange': brange,
        'conv_w': conv_w[i], 'conv_b': conv_b[i].reshape(1, CONV_CH),
        'dtb': dt_bias[i].reshape(1, ndt), 'dtbt': dt_bias[i].reshape(ndt, 1),
        'alog': a_log[i].reshape(1, ndt), 'alogt': a_log[i].reshape(ndt, 1),
        'dskip': jnp.repeat(d_skip[i], SSM_P).reshape(1, SSM_INNER),
        'g_ssm_norm': g_ssm_norm[i].reshape(1, SSM_INNER),
        'wau': w_att_up[i].astype(BF16), 'wsu': w_ssm_up[i].astype(BF16), 'wout': w_out[i].astype(BF16),
        'g_ffn': g_ffn[i].reshape(1, D), 'wr': w_router[i].astype(BF16),
        'weg': w_exp_gate[i].astype(BF16), 'weu': w_exp_up[i].astype(BF16), 'wed': w_exp_down[i].astype(BF16),
        'wpg': w_ple_gate[i].astype(BF16), 'bpg': b_ple_gate[i].reshape(1, D),
        'wpp': w_ple_proj[i].astype(BF16), 'g_final': g_final.reshape(1, D),
    }
    btiles = _bias_tiles(rel_bias, ATT_QK_RATIO * ATT_BLOCK, ATT_BLOCK)
    return (_forward(x_prompt, p_prompt[i], w, btiles), _forward(x_sample, p_sample[i], w, btiles))
```

```python
import functools
import math

import jax
import jax.numpy as jnp
from jax import lax
from jax.experimental import pallas as pl
from jax.experimental.pallas import tpu as pltpu

F32 = jnp.float32
BF16 = jnp.bfloat16

D_MODEL = 1024
N_HEADS = 4
QK_DIM = 64
HEAD_W = 2 * QK_DIM
ATT_W = N_HEADS * HEAD_W
REL_BUCKETS = 32
REL_MAX_DIST = 128
SSM_INNER = 1024
SSM_P = 64
SSM_HEADS = 16
SSM_GROUPS = 2
SSM_N = 64
SSM_CHUNK = 128
CONV_CH = SSM_INNER + 2 * SSM_GROUPS * SSM_N
N_EXPERTS = 16
CAPACITY_FACTOR = 2
PLE_DIM = 256
EPS = 1e-6
LAM_INIT = 0.8 - 0.6 * math.exp(-0.3 * 0)
LOG2E = math.log2(math.e)

GATE_EXT = 128
ROUTE_BLK = 256
GATHER_BLK = 512
GATHER_BUFS = 8
SLOT_TILE = 256
GATHER_ROWS = 128
COMBINE_WIN = 72
BSTART_W = 256

CONV_ROWS = 128

ATT_BLOCK = 256
ATT_QK_RATIO = 4
ATT_BIAS_TILES = ATT_QK_RATIO + 4
ONES_ROWS = 16
BOUND_SLACK = 1.0 + 2.0 ** -6
EXP2_SPAN = 100.0
VMEM_LIMIT = 56 * 1024 * 1024


def _params(sem, vmem=VMEM_LIMIT):
    return pltpu.CompilerParams(dimension_semantics=sem, vmem_limit_bytes=vmem)


def _const_spec(shape):
    nd = len(shape)
    return pl.BlockSpec(shape, lambda *_: (0,) * nd)


def _nt_dot(a, b, **kw):
    return lax.dot_general(a, b, (((1,), (1,)), ((), ())), preferred_element_type=F32, **kw)


def _inproj_kernel(x_ref, g_ref, wqt_ref, wk_ref, wvt_ref, wz_ref, wxbc_ref, wdt_ref, wdtt_ref, wgate_ref,
                   bgate_ref, ind_ref, qt_ref, k_ref, vt_ref, z_ref, xbc_ref, dt_ref, dtt_ref, gates_ref, kn_ref):
    x = x_ref[...]
    u = x * lax.rsqrt(jnp.mean(x * x, axis=-1, keepdims=True) + EPS) * g_ref[...]
    ub = u.astype(BF16)
    qt_ref[...] = _nt_dot(wqt_ref[...], ub).astype(BF16)
    kb = jnp.dot(ub, wk_ref[...], preferred_element_type=F32).astype(BF16)
    k_ref[...] = kb
    ksq = (kb.astype(F32) ** 2).astype(BF16)
    kn_ref[...] = jnp.broadcast_to(
        jnp.max(jnp.dot(ksq, ind_ref[...], preferred_element_type=F32), axis=0, keepdims=True), kn_ref.shape)
    vt_ref[...] = _nt_dot(wvt_ref[...], ub).astype(BF16)
    z_ref[...] = jnp.dot(ub, wz_ref[...], preferred_element_type=F32).astype(z_ref.dtype)
    xbc_ref[...] = jnp.dot(ub, wxbc_ref[...], preferred_element_type=F32)
    dt_ref[...] = jnp.dot(ub, wdt_ref[...], preferred_element_type=F32)
    dtt_ref[...] = _nt_dot(wdtt_ref[...], ub)
    gates = jax.nn.sigmoid(jnp.dot(ub, wgate_ref[...], preferred_element_type=F32) + bgate_ref[...])
    gates_ref[...] = gates.astype(gates_ref.dtype)


def _inproj(x, g_mix, wqt, wk, wvt, wz, wxbc, wdt, wdtt, wgate, bgate, *, tm=256):
    B, L, D = x.shape
    grid = (B, L // tm)
    row = lambda w: pl.BlockSpec((None, tm, w), lambda b, i: (b, i, 0))
    col = lambda w: pl.BlockSpec((None, w, tm), lambda b, i: (b, 0, i))
    ndt = 2 * SSM_HEADS
    ind = (jnp.arange(ATT_W)[:, None] // QK_DIM == jnp.arange(HEAD_W)[None, :]).astype(BF16)
    return pl.pallas_call(
        _inproj_kernel,
        grid=grid,
        in_specs=[row(D), _const_spec((1, D)), _const_spec(wqt.shape), _const_spec(wk.shape),
                  _const_spec(wvt.shape), _const_spec(wz.shape),
                  _const_spec(wxbc.shape), _const_spec(wdt.shape), _const_spec(wdtt.shape),
                  _const_spec(wgate.shape), _const_spec((1, 2 * D)), _const_spec((ATT_W, HEAD_W))],
        out_specs=[col(ATT_W), row(ATT_W), col(ATT_W), row(SSM_INNER), row(CONV_CH), row(ndt),
                   col(ndt), row(2 * D), pl.BlockSpec((None, 8, HEAD_W), lambda b, i: (b, i, 0))],
        out_shape=[jax.ShapeDtypeStruct((B, ATT_W, L), BF16),
                   jax.ShapeDtypeStruct((B, L, ATT_W), BF16),
                   jax.ShapeDtypeStruct((B, ATT_W, L), BF16),
                   jax.ShapeDtypeStruct((B, L, SSM_INNER), BF16),
                   jax.ShapeDtypeStruct((B, L, CONV_CH), F32),
                   jax.ShapeDtypeStruct((B, L, ndt), F32),
                   jax.ShapeDtypeStruct((B, ndt, L), F32),
                   jax.ShapeDtypeStruct((B, L, 2 * D), BF16),
                   jax.ShapeDtypeStruct((B, (L // tm) * 8, HEAD_W), F32)],
        compiler_params=_params(("parallel", "parallel")),
        name="inproj",
    )(x, g_mix, wqt, wk, wvt, wz, wxbc, wdt, wdtt, wgate, bgate, ind)


def _rel_bucket(rel):
    half = REL_BUCKETS // 2
    max_exact = half // 2
    ret = jnp.where(rel > 0, half, 0)
    n = jnp.abs(rel)
    nf = jnp.maximum(n, 1).astype(jnp.float32)
    large = max_exact + (jnp.log(nf / max_exact) / math.log(REL_MAX_DIST / max_exact)
                         * (half - max_exact)).astype(jnp.int32)
    large = jnp.minimum(large, half - 1)
    return ret + jnp.where(n < max_exact, n, large)


def _bias_kernel(table_ref, bucket_ref, out_ref, *, TQ, TK):
    h = pl.program_id(0)
    half = REL_BUCKETS // 2
    far = math.ceil((half // 2) * (REL_MAX_DIST / (half // 2)) ** ((half // 2 - 1) / (half // 2)))
    for d in range(ATT_BIAS_TILES):
        for m in range(2):
            for c0 in range(0, TQ, 128):
                cols = pl.ds(m * TQ + c0, 128)
                rel_min = (d - 2) * TK - (c0 + 127)
                rel_max = (d - 2) * TK + TK - 1 - c0
                if rel_max <= -far or rel_min >= far:
                    last = half - 1 if rel_max < 0 else REL_BUCKETS - 1
                    out_ref[d, :, cols] = jnp.full((TK, 128), table_ref[last, 2 * h + m] * LOG2E, F32)
                    continue
                for r0 in range(0, TK, 128):
                    bk = bucket_ref[d, r0:r0 + 128, c0:c0 + 128]

                    def body(b, acc, bk=bk, m=m):
                        return jnp.where(bk == b, table_ref[b, 2 * h + m] * LOG2E, acc)

                    out_ref[d, r0:r0 + 128, cols] = lax.fori_loop(0, REL_BUCKETS, body, jnp.zeros((128, 128), F32))


def _bias_tiles(rel_bias, TQ, TK):
    assert TK > REL_MAX_DIST
    base = jnp.arange(TK, dtype=jnp.int32)[:, None] - jnp.arange(TQ, dtype=jnp.int32)[None, :]
    bucket = jnp.stack([_rel_bucket(base + (d - 2) * TK) for d in range(ATT_BIAS_TILES)])
    return pl.pallas_call(
        functools.partial(_bias_kernel, TQ=TQ, TK=TK),
        grid=(N_HEADS,),
        in_specs=[pl.BlockSpec(memory_space=pltpu.SMEM), _const_spec((ATT_BIAS_TILES, TK, TQ))],
        out_specs=pl.BlockSpec((None, ATT_BIAS_TILES, TK, 2 * TQ), lambda h: (h, 0, 0, 0)),
        out_shape=jax.ShapeDtypeStruct((N_HEADS, ATT_BIAS_TILES, TK, 2 * TQ), F32),
        compiler_params=_params(("parallel",)),
        name="rel_bias_tiles",
    )(rel_bias.astype(F32), bucket)


def _attn_kernel(lam_ref, brange_ref, qt_ref, k_ref, vt_ref, bt_ref, kn_ref, g_ref, o_ref, acc_sc, p_sc,
                 *, TQ, T, nk):
    h = pl.program_id(1)
    i = pl.program_id(2)
    qt = qt_ref[...]
    sub = lax.broadcasted_iota(jnp.int32, qt.shape, 0)
    zero = jnp.zeros_like(qt)
    q2t = jnp.concatenate([jnp.where(sub < QK_DIM, qt, zero), jnp.where(sub >= QK_DIM, qt, zero)], axis=1)
    acc_sc[...] = jnp.zeros(acc_sc.shape, F32)
    ones = jnp.ones((ONES_ROWS, T), BF16)

    def bias_tile(j):
        return bt_ref[jnp.clip(j - ATT_QK_RATIO * i, -2, ATT_QK_RATIO + 1) + 2]

    qsq = qt.astype(F32) ** 2
    qn = jnp.sqrt(jnp.concatenate([jnp.sum(qsq[:QK_DIM], axis=0, keepdims=True),
                                   jnp.sum(qsq[QK_DIM:], axis=0, keepdims=True)], axis=1))
    kn2 = jnp.max(kn_ref[...], axis=0, keepdims=True)
    klane = lax.broadcasted_iota(jnp.int32, kn2.shape, 1)
    kmax = [jnp.sqrt(jnp.max(jnp.where(klane == 2 * h + mp, kn2, 0.0), axis=1, keepdims=True)) for mp in range(2)]
    first = lax.broadcasted_iota(jnp.int32, qn.shape, 1) < TQ
    reach = qn * jnp.where(first, kmax[0], kmax[1]) * BOUND_SLACK
    bmax = jnp.where(first, brange_ref[h, 0], brange_ref[h, 1])
    bmin = jnp.where(first, brange_ref[h, 2], brange_ref[h, 3])
    shift = (reach + bmax).astype(BF16)
    bounded = jnp.max(2.0 * reach + (bmax - bmin)) < EXP2_SPAN

    @pl.when(bounded)
    def _():
        aug = jnp.where(lax.broadcasted_iota(jnp.int32, (HEAD_W, 2 * TQ), 0) == 0,
                        jnp.broadcast_to(-shift.astype(F32), (HEAD_W, 2 * TQ)), 0.0)
        qaug = jnp.concatenate([q2t, aug.astype(BF16)], axis=0)
        lane = lax.broadcasted_iota(jnp.int32, (T, HEAD_W), 1)
        onescol = jnp.where(lane == 0, 1.0, 0.0).astype(BF16)
        ones4 = jnp.ones((ONES_ROWS, 4 * T), BF16)

        def quad(u, carry):
            for r in range(4):
                j = 4 * u + r
                kb = jnp.concatenate([k_ref[pl.ds(pl.multiple_of(j * T, T), T), :], onescol], axis=1)
                s = jnp.dot(kb, qaug, preferred_element_type=F32) + bias_tile(j)
                p_sc[r * T:(r + 1) * T, :] = jnp.exp2(s).astype(BF16)
            vb = jnp.concatenate([vt_ref[:, pl.ds(pl.multiple_of(4 * u * T, 4 * T), 4 * T)], ones4], axis=0)
            acc_sc[...] += jnp.dot(vb, p_sc[...], preferred_element_type=F32)
            return carry

        lax.fori_loop(0, nk // 4, quad, 0)

    @pl.when(jnp.logical_not(bounded))
    def _():
        def scores_softmax(j, slot, m):
            kb = k_ref[pl.ds(pl.multiple_of(j * T, T), T), :]
            s = jnp.dot(kb, q2t, preferred_element_type=F32) + bias_tile(j)
            m_new = jnp.maximum(m, jnp.max(s, axis=0, keepdims=True))
            p_sc[slot * T:(slot + 1) * T, :] = jnp.exp2(s - m_new).astype(BF16)
            return m_new, jnp.exp2(m - m_new)

        def accumulate(j, slot, alpha):
            vb = jnp.concatenate([vt_ref[:, pl.ds(pl.multiple_of(j * T, T), T)], ones], axis=0)
            acc_sc[...] = alpha * acc_sc[...] + jnp.dot(vb, p_sc[slot * T:(slot + 1) * T, :],
                                                        preferred_element_type=F32)

        m = jnp.full((1, 2 * TQ), -jnp.inf, F32)
        m, alpha = scores_softmax(0, 0, m)

        def pair(u, carry):
            m, alpha = carry
            t = 2 * u
            m, alpha1 = scores_softmax(t - 1, 1, m)
            accumulate(t - 2, 0, alpha)
            m, alpha0 = scores_softmax(t, 0, m)
            accumulate(t - 1, 1, alpha1)
            return m, alpha0

        m, alpha = lax.fori_loop(1, nk // 2, pair, (m, alpha))
        m, alpha1 = scores_softmax(nk - 1, 1, m)
        accumulate(nk - 2, 0, alpha)
        accumulate(nk - 1, 1, alpha1)

    o = acc_sc[0:HEAD_W, :] / acc_sc[HEAD_W:HEAD_W + 1, :]
    od = o[:, :TQ] - lam_ref[0] * o[:, TQ:]
    y = od * lax.rsqrt(jnp.mean(od * od, axis=0, keepdims=True) + EPS) * g_ref[...]
    o_ref[...] = (y * (1.0 - LAM_INIT)).T.astype(o_ref.dtype)


def _attention(qt, k, vt, kn, btiles, brange, lam, g_col, *, T=ATT_BLOCK):
    B, L, _ = k.shape
    TQ = ATT_QK_RATIO * T
    nk = L // T
    assert nk % 4 == 0 and L % TQ == 0
    return pl.pallas_call(
        functools.partial(_attn_kernel, TQ=TQ, T=T, nk=nk),
        grid=(B, N_HEADS, L // TQ),
        in_specs=[pl.BlockSpec(memory_space=pltpu.SMEM),
                  pl.BlockSpec(memory_space=pltpu.SMEM),
                  pl.BlockSpec((None, HEAD_W, TQ), lambda b, h, i: (b, h, i)),
                  pl.BlockSpec((None, L, HEAD_W), lambda b, h, i: (b, 0, h)),
                  pl.BlockSpec((None, HEAD_W, L), lambda b, h, i: (b, h, 0)),
                  pl.BlockSpec((None, ATT_BIAS_TILES, T, 2 * TQ), lambda b, h, i: (h, 0, 0, 0)),
                  pl.BlockSpec((None,) + kn.shape[1:], lambda b, h, i: (b, 0, 0)),
                  _const_spec((HEAD_W, 1))],
        out_specs=pl.BlockSpec((None, TQ, HEAD_W), lambda b, h, i: (b, i, h)),
        out_shape=jax.ShapeDtypeStruct((B, L, ATT_W), BF16),
        scratch_shapes=[pltpu.VMEM((HEAD_W + ONES_ROWS, 2 * TQ), F32), pltpu.VMEM((4 * T, 2 * TQ), BF16)],
        compiler_params=_params(("parallel", "parallel", "parallel")),
        name="diff_attention",
    )(lam, brange, qt, k, vt, btiles, kn, g_col)


def _conv_kernel(x_ref, prev_ref, next_ref, dt_ref, dtt_ref, w_ref, b_ref, dtb_ref, dtbt_ref,
                 xs_ref, bc_ref, dts_ref, dtst_ref, *, tl, nt):
    i = pl.program_id(1)
    R = CONV_ROWS
    row = lax.broadcasted_iota(jnp.int32, (R, 128), 0)
    for r0 in range(0, tl, R):
        for c0 in range(0, CONV_CH, 128):
            cols = slice(c0, c0 + 128)
            x = x_ref[r0:r0 + R, cols]
            above = x_ref[r0 - 1:r0, cols] if r0 > 0 else jnp.where(i > 0, prev_ref[7:8, cols], 0.0)
            below = x_ref[r0 + R:r0 + R + 1, cols] if r0 + R < tl else jnp.where(i < nt - 1, next_ref[0:1, cols], 0.0)
            xm1 = jnp.where(row == 0, above, pltpu.roll(x, 1, 0))
            xp1 = jnp.where(row == R - 1, below, pltpu.roll(x, R - 1, 0))
            y = w_ref[0:1, cols] * xm1 + w_ref[1:2, cols] * x + w_ref[2:3, cols] * xp1 + b_ref[:, cols]
            y = (y * jax.nn.sigmoid(y)).astype(BF16)
            if c0 < SSM_INNER:
                xs_ref[r0:r0 + R, cols] = y
            else:
                bc_ref[r0:r0 + R, c0 - SSM_INNER:c0 - SSM_INNER + 128] = y

    def softplus(v):
        return jnp.maximum(v, 0.0) + jnp.log1p(jnp.exp(-jnp.abs(v)))

    dts_ref[...] = softplus(dt_ref[...] + dtb_ref[...])
    dtst_ref[...] = softplus(dtt_ref[...] + dtbt_ref[...])


def _conv(xbc, dt, dtt, conv_w, conv_b, dtb, dtbt, *, tl=512):
    B, L, _ = xbc.shape
    nt = L // tl
    r8 = tl // 8
    ndt = 2 * SSM_HEADS
    row = lambda w: pl.BlockSpec((None, tl, w), lambda b, i: (b, i, 0))
    colspec = pl.BlockSpec((None, ndt, tl), lambda b, i: (b, 0, i))
    return pl.pallas_call(
        functools.partial(_conv_kernel, tl=tl, nt=nt),
        grid=(B, nt),
        in_specs=[row(CONV_CH),
                  pl.BlockSpec((None, 8, CONV_CH), lambda b, i: (b, jnp.maximum(i * r8 - 1, 0), 0)),
                  pl.BlockSpec((None, 8, CONV_CH), lambda b, i: (b, jnp.minimum((i + 1) * r8, L // 8 - 1), 0)),
                  row(ndt), colspec,
                  _const_spec((3, CONV_CH)), _const_spec((1, CONV_CH)), _const_spec((1, ndt)),
                  _const_spec((ndt, 1))],
        out_specs=[row(SSM_INNER), row(2 * SSM_GROUPS * SSM_N), row(ndt), colspec],
        out_shape=[jax.ShapeDtypeStruct((B, L, SSM_INNER), BF16),
                   jax.ShapeDtypeStruct((B, L, 2 * SSM_GROUPS * SSM_N), BF16),
                   jax.ShapeDtypeStruct((B, L, ndt), F32),
                   jax.ShapeDtypeStruct((B, ndt, L), F32)],
        compiler_params=_params(("parallel", "parallel")),
        name="ssm_conv",
    )(xbc, xbc, xbc, dt, dtt, conv_w, conv_b, dtb, dtbt)


def _ssd_chunk(xs_ref, bc_ref, dts_ref, dtst_ref, alog_ref, alogt_ref, st_ref, *, rev):
    Q = SSM_CHUNK
    HI = lax.Precision.HIGHEST
    off = SSM_HEADS if rev else 0
    ri = lax.broadcasted_iota(jnp.int32, (Q, Q), 0)
    ci = lax.broadcasted_iota(jnp.int32, (Q, Q), 1)
    incl = (ci >= ri) if rev else (ci <= ri)
    tri = incl.astype(F32)
    dt = dts_ref[:, off:off + SSM_HEADS]
    dtt = dtst_ref[off:off + SSM_HEADS, :]
    a = dt * (-jnp.exp(alog_ref[:, off:off + SSM_HEADS]))
    at = dtt * (-jnp.exp(alogt_ref[off:off + SSM_HEADS, :]))
    acum = jnp.dot(tri, a, preferred_element_type=F32, precision=HI)
    acumt = _nt_dot(at, tri, precision=HI)
    edge = 0 if rev else Q - 1
    bc = bc_ref[...]
    bpair = bc[:, :Q]
    cpair = bc[:, Q:]
    low = ci < SSM_N
    rlow = ri < SSM_N
    blockmask = low == rlow
    total = acumt[:, edge:edge + 1]
    decay = jnp.exp(total)
    w_rows = dtt * jnp.exp(total - acumt)
    ys = []
    for g in range(SSM_GROUPS):
        gmask = low if g == 0 else jnp.logical_not(low)
        cg = jnp.where(gmask, cpair, jnp.zeros_like(cpair))
        cb = _nt_dot(cg, bpair)
        dupm = ((ri - g * SSM_N) == (ci % SSM_N)).astype(BF16)
        cdup = jnp.dot(cpair, dupm, preferred_element_type=F32)
        sel2 = ((ci - g * SSM_N) == (ri % SSM_N)).astype(BF16)
        bt2 = _nt_dot(sel2, bpair)
        for hp in range(SSM_HEADS // SSM_GROUPS // 2):
            h0 = g * (SSM_HEADS // SSM_GROUPS) + 2 * hp
            pair = h0 // 2
            xpair = xs_ref[:, h0 * SSM_P:(h0 + 2) * SSM_P]
            ms = []
            e_cols = []
            for h in (h0, h0 + 1):
                e_col = jnp.broadcast_to(acum[:, h:h + 1], (Q, Q))
                r_row = jnp.broadcast_to(acumt[h:h + 1, :], (Q, Q))
                seg = jnp.where(incl, e_col - r_row, -jnp.inf)
                ms.append((cb * jnp.exp(seg) * jnp.broadcast_to(dtt[h:h + 1, :], (Q, Q))).astype(BF16))
                e_cols.append(e_col)
            st = st_ref[pair]
            xblk = jnp.concatenate([jnp.where(low, xpair, jnp.zeros_like(xpair)),
                                    jnp.where(low, jnp.zeros_like(xpair), xpair)], axis=0)
            y = jnp.dot(jnp.concatenate(ms, axis=1), xblk, preferred_element_type=F32)
            ce = (cdup * jnp.exp(jnp.where(low, e_cols[0], e_cols[1]))).astype(BF16)
            y = y + jnp.dot(ce, st.astype(BF16), preferred_element_type=F32)
            ys.append(y)
            w_pair = jnp.where(rlow, jnp.broadcast_to(w_rows[h0:h0 + 1, :], (Q, Q)),
                               jnp.broadcast_to(w_rows[h0 + 1:h0 + 2, :], (Q, Q)))
            snew = jnp.dot((bt2 * w_pair).astype(BF16), xpair, preferred_element_type=F32)
            dec = jnp.where(rlow, decay[h0:h0 + 1, :], decay[h0 + 1:h0 + 2, :])
            st_ref[pair] = dec * st + jnp.where(blockmask, snew, 0.0)
    return jnp.concatenate(ys, axis=1)


def _ssd_fwd_kernel(xs_ref, bc_ref, dts_ref, dtst_ref, alog_ref, alogt_ref, y_ref, st_ref):
    @pl.when(pl.program_id(1) == 0)
    def _():
        st_ref[...] = jnp.zeros(st_ref.shape, F32)

    y = _ssd_chunk(xs_ref, bc_ref, dts_ref, dtst_ref, alog_ref, alogt_ref, st_ref, rev=False)
    y_ref[...] = y.astype(y_ref.dtype)


def _ssd_bwd_kernel(xs_ref, bc_ref, dts_ref, dtst_ref, alog_ref, alogt_ref, yf_ref, z_ref, dskip_ref, g_ref,
                    o_ref, st_ref):
    @pl.when(pl.program_id(1) == 0)
    def _():
        st_ref[...] = jnp.zeros(st_ref.shape, F32)

    yb = _ssd_chunk(xs_ref, bc_ref, dts_ref, dtst_ref, alog_ref, alogt_ref, st_ref, rev=True)
    z = z_ref[...].astype(F32)
    y = (yf_ref[...].astype(F32) + yb + xs_ref[...].astype(F32) * dskip_ref[...]) * (z * jax.nn.sigmoid(z))
    y = y * lax.rsqrt(jnp.mean(y * y, axis=-1, keepdims=True) + EPS) * g_ref[...]
    o_ref[...] = y.astype(o_ref.dtype)


def _ssd(xs, bc, dts, dtst, alog, alogt, z, dskip, g_norm):
    B, L, _ = xs.shape
    Q = SSM_CHUNK
    nc = L // Q
    ndt = 2 * SSM_HEADS
    npair = SSM_HEADS // 2

    def specs(cidx):
        row = lambda w: pl.BlockSpec((None, Q, w), lambda b, c: (b, cidx(c), 0))
        return row, [row(SSM_INNER), row(2 * SSM_GROUPS * SSM_N), row(ndt),
                     pl.BlockSpec((None, ndt, Q), lambda b, c: (b, 0, cidx(c))),
                     _const_spec((1, ndt)), _const_spec((ndt, 1))]

    scratch = [pltpu.VMEM((npair, 2 * SSM_N, 2 * SSM_P), F32)]
    row, in_specs = specs(lambda c: c)
    yf = pl.pallas_call(
        _ssd_fwd_kernel,
        grid=(B, nc),
        in_specs=in_specs,
        out_specs=row(SSM_INNER),
        out_shape=jax.ShapeDtypeStruct((B, L, SSM_INNER), BF16),
        scratch_shapes=scratch,
        compiler_params=_params(("parallel", "arbitrary")),
        name="ssd_forward",
    )(xs, bc, dts, dtst, alog, alogt)
    row, in_specs = specs(lambda c: nc - 1 - c)
    return pl.pallas_call(
        _ssd_bwd_kernel,
        grid=(B, nc),
        in_specs=in_specs + [row(SSM_INNER), row(SSM_INNER), _const_spec((1, SSM_INNER)),
                             _const_spec((1, SSM_INNER))],
        out_specs=row(SSM_INNER),
        out_shape=jax.ShapeDtypeStruct((B, L, SSM_INNER), BF16),
        scratch_shapes=scratch,
        compiler_params=_params(("parallel", "arbitrary")),
        name="ssd_reverse_gate_norm",
    )(xs, bc, dts, dtst, alog, alogt, yf, z, dskip, g_norm)


def _merge_kernel(att_ref, ssm_ref, gates_ref, x_ref, wau_ref, wsu_ref, wout_ref, gffn_ref, wr_ref, wrt_ref,
                  place_ref, h_ref, u2_ref, afft_ref):
    D = D_MODEL
    E = N_EXPERTS
    ya = jnp.dot(att_ref[...], wau_ref[...], preferred_element_type=F32)
    ysm = jnp.dot(ssm_ref[...], wsu_ref[...], preferred_element_type=F32)
    merged = gates_ref[:, :D] * ya + gates_ref[:, D:] * ysm
    h = x_ref[...] + jnp.dot(merged.astype(BF16), wout_ref[...], preferred_element_type=F32)
    h_ref[...] = h
    u2 = (h * lax.rsqrt(jnp.mean(h * h, axis=-1, keepdims=True) + EPS) * gffn_ref[...]).astype(BF16)
    u2_ref[:, :D] = u2
    logits = jnp.dot(u2, wr_ref[...], preferred_element_type=F32)
    e = jnp.exp(logits - jnp.max(logits, axis=-1, keepdims=True))
    aff = e / jnp.sum(e, axis=-1, keepdims=True)
    hi = aff.astype(BF16)
    r1 = aff - hi.astype(F32)
    mid = r1.astype(BF16)
    lo = (r1 - mid.astype(F32)).astype(BF16)
    ext = (jnp.dot(hi, place_ref[0:E, :], preferred_element_type=F32)
           + jnp.dot(mid, place_ref[E:2 * E, :], preferred_element_type=F32)
           + jnp.dot(lo, place_ref[2 * E:3 * E, :], preferred_element_type=F32))
    u2_ref[:, D:] = ext.astype(BF16)
    lt = _nt_dot(wrt_ref[...], u2)
    et = jnp.exp(lt - jnp.max(lt, axis=0, keepdims=True))
    afft_ref[...] = et / jnp.sum(et, axis=0, keepdims=True)


def _merge(att, ssm, gates, x, wau, wsu, wout, g_ffn, wr, *, tm=256):
    B, L, D = x.shape
    E = N_EXPERTS
    nl = L // tm
    row = lambda w: pl.BlockSpec((None, tm, w), lambda b, i: (b, i, 0))
    place = jnp.eye(3 * E, GATE_EXT, dtype=BF16)
    return pl.pallas_call(
        _merge_kernel,
        grid=(B, nl),
        in_specs=[row(ATT_W), row(SSM_INNER), row(2 * D), row(D), _const_spec(wau.shape),
                  _const_spec(wsu.shape), _const_spec(wout.shape), _const_spec((1, D)), _const_spec(wr.shape),
                  _const_spec((E, D)), _const_spec((3 * E, GATE_EXT))],
        out_specs=[row(D), row(D + GATE_EXT), pl.BlockSpec((E, tm), lambda b, i: (0, b * nl + i))],
        out_shape=[jax.ShapeDtypeStruct((B, L, D), F32), jax.ShapeDtypeStruct((B, L, D + GATE_EXT), BF16),
                   jax.ShapeDtypeStruct((E, B * L), F32)],
        compiler_params=_params(("parallel", "parallel")),
        name="merge_router",
    )(att, ssm, gates, x, wau, wsu, wout, g_ffn, wr, wr.T, place)


def _route_kernel(afft_ref, posm_ref, bstart_ref, *, cap, nb):
    E = N_EXPERTS
    W = ROUTE_BLK

    def bit_body(k, thr):
        cand = thr | lax.shift_left(jnp.int32(1), 30 - k)
        x = lax.bitcast_convert_type(afft_ref[...], jnp.int32)
        cnt = jnp.sum(jnp.where(x >= cand, 1.0, 0.0), axis=1, keepdims=True)
        return jnp.where(cnt >= cap, cand, thr)

    thr = lax.fori_loop(0, 31, bit_body, jnp.zeros((E, 1), jnp.int32))
    x = lax.bitcast_convert_type(afft_ref[...], jnp.int32)
    n_gt = jnp.sum(jnp.where(x > thr, 1.0, 0.0), axis=1, keepdims=True)
    need = cap - n_gt
    ri = lax.broadcasted_iota(jnp.int32, (W, W), 0)
    ci = lax.broadcasted_iota(jnp.int32, (W, W), 1)
    upper = jnp.where(ri <= ci, 1.0, 0.0).astype(BF16)
    bstart_ref[...] = jnp.zeros(bstart_ref.shape, jnp.int32)
    c_eq = jnp.zeros((E, 1), F32)
    c_sel = jnp.zeros((E, 1), F32)
    for b in range(nb):
        xb = lax.bitcast_convert_type(afft_ref[:, b * W:(b + 1) * W], jnp.int32)
        eq = xb == thr
        eqc = jnp.dot(jnp.where(eq, 1.0, 0.0).astype(BF16), upper, preferred_element_type=F32) + c_eq
        sel = (xb > thr) | (eq & (eqc <= need))
        selc = jnp.dot(jnp.where(sel, 1.0, 0.0).astype(BF16), upper, preferred_element_type=F32) + c_sel
        posm_ref[:, b * W:(b + 1) * W] = jnp.where(sel, selc, 0.0).astype(jnp.int32)
        bstart_ref[:, b:b + 1] = c_sel.astype(jnp.int32)
        c_eq = eqc[:, W - 1:W]
        c_sel = selc[:, W - 1:W]
    bstart_ref[:, nb:nb + 1] = c_sel.astype(jnp.int32)


def _route(afft, cap):
    E, T = afft.shape
    nb = T // ROUTE_BLK
    assert nb < BSTART_W and cap % SLOT_TILE == 0
    return pl.pallas_call(
        functools.partial(_route_kernel, cap=float(cap), nb=nb),
        out_shape=[jax.ShapeDtypeStruct((E, T), jnp.int32), jax.ShapeDtypeStruct((E, BSTART_W), jnp.int32)],
        compiler_params=_params(None),
        name="route_select",
    )(afft)


def _pair_schedule(bstart, cap, ngb):
    step = GATHER_BLK // ROUTE_BLK
    st = bstart[:, 0:ngb * step + 1:step]
    lo, hi = st[:, :-1], st[:, 1:]
    ntile = cap // SLOT_TILE
    t_lo = jnp.arange(ntile, dtype=jnp.int32) * SLOT_TILE
    t_hi = t_lo + SLOT_TILE
    b_first = jnp.sum(hi[:, None, :] <= t_lo[None, :, None], axis=2).astype(jnp.int32)
    b_last = jnp.sum(lo[:, None, :] < t_hi[None, :, None], axis=2).astype(jnp.int32) - 1
    return b_first, b_last - b_first + 1


def _expert_kernel(first_ref, cnt_ref, bs_ref, u_hbm, posm_ref, wg_ref, wu_ref, wd_ref, o_ref, x_sc, ubuf, sem):
    D = D_MODEL
    e = pl.program_id(0)
    j = pl.program_id(1)
    g = e * pl.num_programs(1) + j
    b0 = first_ref[g]
    n = cnt_ref[g]

    half = GATHER_BLK // 2

    def half_copies(b, slot):
        return [pltpu.make_async_copy(
            u_hbm.at[pl.ds(pl.multiple_of(b * GATHER_BLK + r * half, half), half)],
            ubuf.at[slot, pl.ds(r * half, half)], sem.at[slot, r]) for r in range(2)]

    def start(b, slot):
        for cp in half_copies(b, slot):
            cp.start()

    ahead = GATHER_BUFS - 1

    def start_tile(t):
        start(first_ref[t], 0)
        for r in range(1, ahead):
            @pl.when(cnt_ref[t] > r)
            def _(r=r):
                start(first_ref[t] + r, r)

    @pl.when(g == 0)
    def _():
        start_tile(0)

    x_sc[...] = jnp.zeros(x_sc.shape, F32)
    WM = GATHER_ROWS
    row_ids = lax.broadcasted_iota(jnp.int32, (WM, GATHER_BLK), 0)
    per_blk = GATHER_BLK // ROUTE_BLK

    def body(c, carry):
        slot = c % GATHER_BUFS
        for cp in half_copies(b0 + c, slot):
            cp.wait()

        @pl.when(c + ahead < n)
        def _():
            start(b0 + c + ahead, (c + ahead) % GATHER_BUFS)

        blk = b0 + c
        posrow = posm_ref[:, pl.ds(pl.multiple_of(blk * GATHER_BLK, GATHER_BLK), GATHER_BLK)] - (j * SLOT_TILE + 1)
        lo = jnp.maximum(bs_ref[e * BSTART_W + blk * per_blk] - j * SLOT_TILE, 0)
        hi = jnp.minimum(bs_ref[e * BSTART_W + (blk + 1) * per_blk] - j * SLOT_TILE, SLOT_TILE)
        w0 = (lo // 16) * 16

        def window(k, carry):
            wlo = w0 + k * WM
            top = pl.multiple_of(jnp.minimum(wlo, SLOT_TILE - WM), 16)
            hit = (posrow >= wlo) & (posrow < wlo + WM) & (posrow - top == row_ids)
            onehot = jnp.where(hit, 1.0, 0.0).astype(BF16)
            x_sc[pl.ds(top, WM), :] += jnp.dot(onehot, ubuf[slot], preferred_element_type=F32)
            return carry

        lax.fori_loop(0, jnp.maximum((hi - w0 + WM - 1) // WM, 0), window, 0)
        return carry

    lax.fori_loop(0, n, body, 0)

    @pl.when(g + 1 < pl.num_programs(0) * pl.num_programs(1))
    def _():
        start_tile(g + 1)

    xs = x_sc[:, :D].astype(BF16)
    ext = x_sc[:, D:]
    lane = lax.broadcasted_iota(jnp.int32, ext.shape, 1)
    mine = (lane == e) | (lane == e + N_EXPERTS) | (lane == e + 2 * N_EXPERTS)
    gate = jnp.sum(jnp.where(mine, ext, 0.0), axis=1, keepdims=True)
    hg = jnp.dot(xs, wg_ref[...], preferred_element_type=F32)
    hu = jnp.dot(xs, wu_ref[...], preferred_element_type=F32)
    hmid = (hg * jax.nn.sigmoid(hg) * hu).astype(BF16)
    o_ref[...] = jnp.dot(hmid, wd_ref[...], preferred_element_type=F32) * gate


def _experts(u2ext, posm, bstart, b_first, b_cnt, wg, wu, wd, cap):
    T, DX = u2ext.shape
    E, D = N_EXPERTS, D_MODEL
    wspec = pl.BlockSpec((None, D, D), lambda e, j, bf, bc, bs: (e, 0, 0))
    return pl.pallas_call(
        _expert_kernel,
        grid_spec=pltpu.PrefetchScalarGridSpec(
            num_scalar_prefetch=3,
            grid=(E, cap // SLOT_TILE),
            in_specs=[pl.BlockSpec(memory_space=pl.ANY),
                      pl.BlockSpec((None, 1, T), lambda e, j, bf, bc, bs: (e, 0, 0)),
                      wspec, wspec, wspec],
            out_specs=pl.BlockSpec((None, SLOT_TILE, D), lambda e, j, bf, bc, bs: (e, j, 0)),
            scratch_shapes=[pltpu.VMEM((SLOT_TILE, DX), F32), pltpu.VMEM((GATHER_BUFS, GATHER_BLK, DX), BF16),
                            pltpu.SemaphoreType.DMA((GATHER_BUFS, 2))]),
        out_shape=jax.ShapeDtypeStruct((E, cap, D), F32),
        compiler_params=_params(("arbitrary", "arbitrary")),
        name="experts_gather_mlp",
    )(b_first.reshape(-1), b_cnt.reshape(-1), bstart.reshape(-1), u2ext, posm.reshape(E, 1, T), wg, wu, wd)


def _combine_kernel(bstart_ref, posm_ref, ye_ref, h_ref, p_ref, wpg_ref, bpg_ref, wpp_ref, gfin_ref, o_ref,
                    moe, buf, xbuf, sem, xsem, *, cap, nb):
    E = N_EXPERTS
    WN = COMBINE_WIN
    b = pl.program_id(0)
    top = cap - WN
    cur = b % 2

    def window(blk, slot, e):
        base = jnp.minimum((bstart_ref[e, blk] // 8) * 8, top)
        return pltpu.make_async_copy(ye_ref.at[e, pl.ds(pl.multiple_of(base, 8), WN)],
                                     buf.at[slot, pl.ds(e * WN, WN)], sem.at[slot, e])

    @pl.when(b == 0)
    def _():
        for e in range(E):
            window(0, 0, e).start()

    @pl.when(b + 1 < nb)
    def _():
        for e in range(E):
            window(b + 1, 1 - cur, e).start()

    lows = [(bstart_ref[e, b] // 8) * 8 for e in range(E)]
    w_iota = lax.broadcasted_iota(jnp.int32, (WN, ROUTE_BLK), 0)

    def onehot_rows(slot0, low, base):
        ok = (slot0 >= low) & (slot0 < low + WN)
        return jnp.where(ok & (slot0 - base == w_iota), 1.0, 0.0).astype(BF16)

    def scatter(onehot, rows):
        hi = rows.astype(BF16)
        lo = (rows - hi.astype(F32)).astype(BF16)
        tn = lambda a, c: lax.dot_general(a, c, (((0,), (0,)), ((), ())), preferred_element_type=F32)
        return tn(onehot, hi) + tn(onehot, lo)

    hots = []
    for e in range(E):
        slot0 = posm_ref[e:e + 1, :] - 1
        hots.append(onehot_rows(slot0, lows[e], jnp.minimum(lows[e], top)))
    for e in range(E):
        window(b, cur, e).wait()
    moe[...] = scatter(jnp.concatenate(hots, axis=0), buf[cur])

    def per_expert(e, carry):
        low0 = (bstart_ref[e, b] // 8) * 8
        nwin = (bstart_ref[e, b + 1] - low0 + WN - 1) // WN

        def per_window(c, carry):
            low = low0 + c * WN
            base = jnp.minimum(low, top)
            cp = pltpu.make_async_copy(ye_ref.at[e, pl.ds(pl.multiple_of(base, 8), WN)], xbuf, xsem.at[0])
            cp.start()
            cp.wait()
            mine = lax.broadcasted_iota(jnp.int32, (E, ROUTE_BLK), 0) == e
            slot0 = jnp.sum(jnp.where(mine, posm_ref[...], 0), axis=0, keepdims=True) - 1
            moe[...] += scatter(onehot_rows(slot0, low, base), xbuf[...])
            return carry

        return lax.fori_loop(1, nwin, per_window, carry)

    lax.fori_loop(0, E, per_expert, 0)

    h = h_ref[...] + moe[...]
    gate = jax.nn.sigmoid(jnp.dot(h.astype(BF16), wpg_ref[...], preferred_element_type=F32) + bpg_ref[...])
    h = h + gate * jnp.dot(p_ref[...].astype(BF16), wpp_ref[...], preferred_element_type=F32)
    o_ref[...] = h * lax.rsqrt(jnp.mean(h * h, axis=-1, keepdims=True) + EPS) * gfin_ref[...]


def _combine_ple(ye, posm, bstart, cap, h, p, wpg, bpg, wpp, g_final):
    E, T = posm.shape
    D = D_MODEL
    nb = T // ROUTE_BLK
    assert cap >= COMBINE_WIN
    row = lambda w: pl.BlockSpec((ROUTE_BLK, w), lambda b, bs: (b, 0))
    const = lambda shape: pl.BlockSpec(shape, lambda b, bs: (0,) * len(shape))
    return pl.pallas_call(
        functools.partial(_combine_kernel, cap=cap, nb=nb),
        grid_spec=pltpu.PrefetchScalarGridSpec(
            num_scalar_prefetch=1,
            grid=(nb,),
            in_specs=[pl.BlockSpec((E, ROUTE_BLK), lambda b, bs: (0, b)),
                      pl.BlockSpec(memory_space=pl.ANY), row(D), row(PLE_DIM), const(wpg.shape), const((1, D)),
                      const(wpp.shape), const((1, D))],
            out_specs=row(D),
            scratch_shapes=[pltpu.VMEM((ROUTE_BLK, D), F32), pltpu.VMEM((2, E * COMBINE_WIN, D), F32),
                            pltpu.VMEM((COMBINE_WIN, D), F32),
                            pltpu.SemaphoreType.DMA((2, E)), pltpu.SemaphoreType.DMA((1,))]),
        out_shape=jax.ShapeDtypeStruct((T, D), F32),
        compiler_params=_params(("arbitrary",)),
        name="experts_combine_ple",
    )(bstart, posm, ye, h.reshape(T, D), p.reshape(T, PLE_DIM), wpg, bpg, wpp, g_final)


def _forward(x, p, w, btiles):
    B, L, D = x.shape
    T = B * L
    qt, k, vt, z, xbc, dt, dtt, gates, kn = _inproj(x, w['g_mix'], w['wqt'], w['wk'], w['wvt'], w['wz'],
                                                    w['wxbc'], w['wdt'], w['wdtt'], w['wgate'], w['bgate'])
    att = _attention(qt, k, vt, kn, btiles, w['brange'], w['lam'], w['g_subln'])
    xs, bc, dts, dtst = _conv(xbc, dt, dtt, w['conv_w'], w['conv_b'], w['dtb'], w['dtbt'])
    ssm = _ssd(xs, bc, dts, dtst, w['alog'], w['alogt'], z, w['dskip'], w['g_ssm_norm'])
    h, u2ext, afft = _merge(att, ssm, gates, x, w['wau'], w['wsu'], w['wout'], w['g_ffn'], w['wr'])
    cap = max(1, CAPACITY_FACTOR * T // N_EXPERTS)
    posm, bstart = _route(afft, cap)
    b_first, b_cnt = _pair_schedule(bstart, cap, T // GATHER_BLK)
    ye = _experts(u2ext.reshape(T, D + GATE_EXT), posm, bstart, b_first, b_cnt, w['weg'], w['weu'], w['wed'], cap)
    out = _combine_ple(ye, posm, bstart, cap, h, p, w['wpg'], w['bpg'], w['wpp'], w['g_final'])
    return out.reshape(B, L, D)


def kernel(x_prompt, x_sample, p_prompt, p_sample, rel_bias, g_mix, w_in, conv_w, conv_b, dt_bias, a_log,
           d_skip, g_ssm_norm, lam_params, g_subln, w_att_up, w_ssm_up, w_gate, b_gate, w_out, g_ffn,
           w_router, w_exp_gate, w_exp_up, w_exp_down, w_ple_gate, b_ple_gate, w_ple_proj, g_final):
    i = 0
    D = D_MODEL
    win = w_in[i]
    o1 = ATT_W
    o3 = 3 * ATT_W
    o4 = o3 + SSM_INNER
    o5 = o4 + CONV_CH
    wqt = (win[:, :o1] * (QK_DIM ** -0.5 * LOG2E)).T.astype(BF16)
    wk = win[:, o1:2 * o1].astype(BF16)
    wvt = win[:, 2 * o1:o3].T.astype(BF16)
    wdt = win[:, o5:].astype(BF16)
    lp = lam_params[i].astype(F32)
    lam = jnp.exp(jnp.sum(lp[0] * lp[1])) - jnp.exp(jnp.sum(lp[2] * lp[3])) + LAM_INIT
    ndt = 2 * SSM_HEADS
    tb = rel_bias.astype(F32).reshape(REL_BUCKETS, N_HEADS, 2) * LOG2E
    brange = jnp.concatenate([jnp.max(tb, axis=0), jnp.min(tb, axis=0)], axis=1)
    w = {
        'g_mix': g_mix[i].reshape(1, D), 'wqt': wqt, 'wk': wk, 'wvt': wvt, 'wz': win[:, o3:o4].astype(BF16),
        'wxbc': win[:, o4:o5].astype(BF16), 'wdt': wdt, 'wdtt': wdt.T,
        'wgate': w_gate[i].astype(BF16), 'bgate': b_gate[i].reshape(1, 2 * D),
        'lam': lam.reshape(1), 'g_subln': g_subln[i].reshape(HEAD_W, 1), 'brange': brange,
        'conv_w': conv_w[i], 'conv_b': conv_b[i].reshape(1, CONV_CH),
        'dtb': dt_bias[i].reshape(1, ndt), 'dtbt': dt_bias[i].reshape(ndt, 1),
        'alog': a_log[i].reshape(1, ndt), 'alogt': a_log[i].reshape(ndt, 1),
        'dskip': jnp.repeat(d_skip[i], SSM_P).reshape(1, SSM_INNER),
        'g_ssm_norm': g_ssm_norm[i].reshape(1, SSM_INNER),
        'wau': w_att_up[i].astype(BF16), 'wsu': w_ssm_up[i].astype(BF16), 'wout': w_out[i].astype(BF16),
        'g_ffn': g_ffn[i].reshape(1, D), 'wr': w_router[i].astype(BF16),
        'weg': w_exp_gate[i].astype(BF16), 'weu': w_exp_up[i].astype(BF16), 'wed': w_exp_down[i].astype(BF16),
        'wpg': w_ple_gate[i].astype(BF16), 'bpg': b_ple_gate[i].reshape(1, D),
        'wpp': w_ple_proj[i].astype(BF16), 'g_final': g_final.reshape(1, D),
    }
    btiles = _bias_tiles(rel_bias, ATT_QK_RATIO * ATT_BLOCK, ATT_BLOCK)
    return (_forward(x_prompt, p_prompt[i], w, btiles), _forward(x_sample, p_sample[i], w, btiles))
```

```python
import functools
import math

import jax
import jax.numpy as jnp
from jax import lax
from jax.experimental import pallas as pl
from jax.experimental.pallas import tpu as pltpu

F32 = jnp.float32
BF16 = jnp.bfloat16

D_MODEL = 1024
N_HEADS = 4
QK_DIM = 64
HEAD_W = 2 * QK_DIM
ATT_W = N_HEADS * HEAD_W
REL_BUCKETS = 32
REL_MAX_DIST = 128
SSM_INNER = 1024
SSM_P = 64
SSM_HEADS = 16
SSM_GROUPS = 2
SSM_N = 64
SSM_CHUNK = 128
CONV_CH = SSM_INNER + 2 * SSM_GROUPS * SSM_N
N_EXPERTS = 16
CAPACITY_FACTOR = 2
PLE_DIM = 256
EPS = 1e-6
LAM_INIT = 0.8 - 0.6 * math.exp(-0.3 * 0)
LOG2E = math.log2(math.e)

GATE_EXT = 128
ROUTE_BLK = 256
GATHER_BLK = 512
GATHER_BUFS = 8
SLOT_TILE = 256
GATHER_ROWS = 128
COMBINE_WIN = 72
BSTART_W = 256

CONV_ROWS = 128

ATT_BLOCK = 256
ATT_QK_RATIO = 4
ATT_BIAS_TILES = ATT_QK_RATIO + 4
ONES_ROWS = 16
BOUND_SLACK = 1.0 + 2.0 ** -6
EXP2_SPAN = 100.0
VMEM_LIMIT = 56 * 1024 * 1024


def _params(sem, vmem=VMEM_LIMIT):
    return pltpu.CompilerParams(dimension_semantics=sem, vmem_limit_bytes=vmem)


def _const_spec(shape):
    nd = len(shape)
    return pl.BlockSpec(shape, lambda *_: (0,) * nd)


def _nt_dot(a, b, **kw):
    return lax.dot_general(a, b, (((1,), (1,)), ((), ())), preferred_element_type=F32, **kw)


def _inproj_kernel(x_ref, g_ref, wqt_ref, wk_ref, wvt_ref, wz_ref, wxbc_ref, wdt_ref, wdtt_ref, wgate_ref,
                   bgate_ref, ind_ref, qt_ref, k_ref, vt_ref, z_ref, xbc_ref, dt_ref, dtt_ref, gates_ref, kn_ref):
    x = x_ref[...]
    u = x * lax.rsqrt(jnp.mean(x * x, axis=-1, keepdims=True) + EPS) * g_ref[...]
    ub = u.astype(BF16)
    qt_ref[...] = _nt_dot(wqt_ref[...], ub).astype(BF16)
    kb = jnp.dot(ub, wk_ref[...], preferred_element_type=F32).astype(BF16)
    k_ref[...] = kb
    ksq = (kb.astype(F32) ** 2).astype(BF16)
    kn_ref[...] = jnp.broadcast_to(
        jnp.max(jnp.dot(ksq, ind_ref[...], preferred_element_type=F32), axis=0, keepdims=True), kn_ref.shape)
    vt_ref[...] = _nt_dot(wvt_ref[...], ub).astype(BF16)
    z_ref[...] = jnp.dot(ub, wz_ref[...], preferred_element_type=F32).astype(z_ref.dtype)
    xbc_ref[...] = jnp.dot(ub, wxbc_ref[...], preferred_element_type=F32)
    dt_ref[...] = jnp.dot(ub, wdt_ref[...], preferred_element_type=F32)
    dtt_ref[...] = _nt_dot(wdtt_ref[...], ub)
    gates = jax.nn.sigmoid(jnp.dot(ub, wgate_ref[...], preferred_element_type=F32) + bgate_ref[...])
    gates_ref[...] = gates.astype(gates_ref.dtype)


def _inproj(x, g_mix, wqt, wk, wvt, wz, wxbc, wdt, wdtt, wgate, bgate, *, tm=256):
    B, L, D = x.shape
    grid = (B, L // tm)
    row = lambda w: pl.BlockSpec((None, tm, w), lambda b, i: (b, i, 0))
    col = lambda w: pl.BlockSpec((None, w, tm), lambda b, i: (b, 0, i))
    ndt = 2 * SSM_HEADS
    ind = (jnp.arange(ATT_W)[:, None] // QK_DIM == jnp.arange(HEAD_W)[None, :]).astype(BF16)
    return pl.pallas_call(
        _inproj_kernel,
        grid=grid,
        in_specs=[row(D), _const_spec((1, D)), _const_spec(wqt.shape), _const_spec(wk.shape),
                  _const_spec(wvt.shape), _const_spec(wz.shape),
                  _const_spec(wxbc.shape), _const_spec(wdt.shape), _const_spec(wdtt.shape),
                  _const_spec(wgate.shape), _const_spec((1, 2 * D)), _const_spec((ATT_W, HEAD_W))],
        out_specs=[col(ATT_W), row(ATT_W), col(ATT_W), row(SSM_INNER), row(CONV_CH), row(ndt),
                   col(ndt), row(2 * D), pl.BlockSpec((None, 8, HEAD_W), lambda b, i: (b, i, 0))],
        out_shape=[jax.ShapeDtypeStruct((B, ATT_W, L), BF16),
                   jax.ShapeDtypeStruct((B, L, ATT_W), BF16),
                   jax.ShapeDtypeStruct((B, ATT_W, L), BF16),
                   jax.ShapeDtypeStruct((B, L, SSM_INNER), BF16),
                   jax.ShapeDtypeStruct((B, L, CONV_CH), F32),
                   jax.ShapeDtypeStruct((B, L, ndt), F32),
                   jax.ShapeDtypeStruct((B, ndt, L), F32),
                   jax.ShapeDtypeStruct((B, L, 2 * D), BF16),
                   jax.ShapeDtypeStruct((B, (L // tm) * 8, HEAD_W), F32)],
        compiler_params=_params(("parallel", "parallel")),
        name="inproj",
    )(x, g_mix, wqt, wk, wvt, wz, wxbc, wdt, wdtt, wgate, bgate, ind)


def _rel_bucket(rel):
    half = REL_BUCKETS // 2
    max_exact = half // 2
    ret = jnp.where(rel > 0, half, 0)
    n = jnp.abs(rel)
    nf = jnp.maximum(n, 1).astype(jnp.float32)
    large = max_exact + (jnp.log(nf / max_exact) / math.log(REL_MAX_DIST / max_exact)
                         * (half - max_exact)).astype(jnp.int32)
    large = jnp.minimum(large, half - 1)
    return ret + jnp.where(n < max_exact, n, large)


def _bias_kernel(table_ref, bucket_ref, out_ref, *, TQ, TK):
    h = pl.program_id(0)
    half = REL_BUCKETS // 2
    far = math.ceil((half // 2) * (REL_MAX_DIST / (half // 2)) ** ((half // 2 - 1) / (half // 2)))
    for d in range(ATT_BIAS_TILES):
        for m in range(2):
            for c0 in range(0, TQ, 128):
                cols = pl.ds(m * TQ + c0, 128)
                rel_min = (d - 2) * TK - (c0 + 127)
                rel_max = (d - 2) * TK + TK - 1 - c0
                if rel_max <= -far or rel_min >= far:
                    last = half - 1 if rel_max < 0 else REL_BUCKETS - 1
                    out_ref[d, :, cols] = jnp.full((TK, 128), table_ref[last, 2 * h + m] * LOG2E, F32)
                    continue
                for r0 in range(0, TK, 128):
                    bk = bucket_ref[d, r0:r0 + 128, c0:c0 + 128]

                    def body(b, acc, bk=bk, m=m):
                        return jnp.where(bk == b, table_ref[b, 2 * h + m] * LOG2E, acc)

                    out_ref[d, r0:r0 + 128, cols] = lax.fori_loop(0, REL_BUCKETS, body, jnp.zeros((128, 128), F32))


def _bias_tiles(rel_bias, TQ, TK):
    assert TK > REL_MAX_DIST
    base = jnp.arange(TK, dtype=jnp.int32)[:, None] - jnp.arange(TQ, dtype=jnp.int32)[None, :]
    bucket = jnp.stack([_rel_bucket(base + (d - 2) * TK) for d in range(ATT_BIAS_TILES)])
    return pl.pallas_call(
        functools.partial(_bias_kernel, TQ=TQ, TK=TK),
        grid=(N_HEADS,),
        in_specs=[pl.BlockSpec(memory_space=pltpu.SMEM), _const_spec((ATT_BIAS_TILES, TK, TQ))],
        out_specs=pl.BlockSpec((None, ATT_BIAS_TILES, TK, 2 * TQ), lambda h: (h, 0, 0, 0)),
        out_shape=jax.ShapeDtypeStruct((N_HEADS, ATT_BIAS_TILES, TK, 2 * TQ), F32),
        compiler_params=_params(("parallel",)),
        name="rel_bias_tiles",
    )(rel_bias.astype(F32), bucket)


def _attn_kernel(lam_ref, brange_ref, qt_ref, k_ref, vt_ref, bt_ref, kn_ref, g_ref, o_ref, acc_sc, p_sc,
                 *, TQ, T, nk):
    h = pl.program_id(1)
    i = pl.program_id(2)
    qt = qt_ref[...]
    sub = lax.broadcasted_iota(jnp.int32, qt.shape, 0)
    zero = jnp.zeros_like(qt)
    q2t = jnp.concatenate([jnp.where(sub < QK_DIM, qt, zero), jnp.where(sub >= QK_DIM, qt, zero)], axis=1)
    acc_sc[...] = jnp.zeros(acc_sc.shape, F32)
    ones = jnp.ones((ONES_ROWS, T), BF16)

    def bias_tile(j):
        return bt_ref[jnp.clip(j - ATT_QK_RATIO * i, -2, ATT_QK_RATIO + 1) + 2]

    qsq = qt.astype(F32) ** 2
    qn = jnp.sqrt(jnp.concatenate([jnp.sum(qsq[:QK_DIM], axis=0, keepdims=True),
                                   jnp.sum(qsq[QK_DIM:], axis=0, keepdims=True)], axis=1))
    kn2 = jnp.max(kn_ref[...], axis=0, keepdims=True)
    klane = lax.broadcasted_iota(jnp.int32, kn2.shape, 1)
    kmax = [jnp.sqrt(jnp.max(jnp.where(klane == 2 * h + mp, kn2, 0.0), axis=1, keepdims=True)) for mp in range(2)]
    first = lax.broadcasted_iota(jnp.int32, qn.shape, 1) < TQ
    reach = qn * jnp.where(first, kmax[0], kmax[1]) * BOUND_SLACK
    bmax = jnp.where(first, brange_ref[h, 0], brange_ref[h, 1])
    bmin = jnp.where(first, brange_ref[h, 2], brange_ref[h, 3])
    shift = (reach + bmax).astype(BF16)
    bounded = jnp.max(2.0 * reach + (bmax - bmin)) < EXP2_SPAN

    @pl.when(bounded)
    def _():
        aug = jnp.where(lax.broadcasted_iota(jnp.int32, (HEAD_W, 2 * TQ), 0) == 0,
                        jnp.broadcast_to(-shift.astype(F32), (HEAD_W, 2 * TQ)), 0.0)
        qaug = jnp.concatenate([q2t, aug.astype(BF16)], axis=0)
        lane = lax.broadcasted_iota(jnp.int32, (T, HEAD_W), 1)
        onescol = jnp.where(lane == 0, 1.0, 0.0).astype(BF16)
        ones4 = jnp.ones((ONES_ROWS, 4 * T), BF16)

        def quad(u, carry):
            for r in range(4):
                j = 4 * u + r
                kb = jnp.concatenate([k_ref[pl.ds(pl.multiple_of(j * T, T), T), :], onescol], axis=1)
                s = jnp.dot(kb, qaug, preferred_element_type=F32) + bias_tile(j)
                p_sc[r * T:(r + 1) * T, :] = jnp.exp2(s).astype(BF16)
            vb = jnp.concatenate([vt_ref[:, pl.ds(pl.multiple_of(4 * u * T, 4 * T), 4 * T)], ones4], axis=0)
            acc_sc[...] += jnp.dot(vb, p_sc[...], preferred_element_type=F32)
            return carry

        lax.fori_loop(0, nk // 4, quad, 0)

    @pl.when(jnp.logical_not(bounded))
    def _():
        def scores_softmax(j, slot, m):
            kb = k_ref[pl.ds(pl.multiple_of(j * T, T), T), :]
            s = jnp.dot(kb, q2t, preferred_element_type=F32) + bias_tile(j)
            m_new = jnp.maximum(m, jnp.max(s, axis=0, keepdims=True))
            p_sc[slot * T:(slot + 1) * T, :] = jnp.exp2(s - m_new).astype(BF16)
            return m_new, jnp.exp2(m - m_new)

        def accumulate(j, slot, alpha):
            vb = jnp.concatenate([vt_ref[:, pl.ds(pl.multiple_of(j * T, T), T)], ones], axis=0)
            acc_sc[...] = alpha * acc_sc[...] + jnp.dot(vb, p_sc[slot * T:(slot + 1) * T, :],
                                                        preferred_element_type=F32)

        m = jnp.full((1, 2 * TQ), -jnp.inf, F32)
        m, alpha = scores_softmax(0, 0, m)

        def pair(u, carry):
            m, alpha = carry
            t = 2 * u
            m, alpha1 = scores_softmax(t - 1, 1, m)
            accumulate(t - 2, 0, alpha)
            m, alpha0 = scores_softmax(t, 0, m)
            accumulate(t - 1, 1, alpha1)
            return m, alpha0

        m, alpha = lax.fori_loop(1, nk // 2, pair, (m, alpha))
        m, alpha1 = scores_softmax(nk - 1, 1, m)
        accumulate(nk - 2, 0, alpha)
        accumulate(nk - 1, 1, alpha1)

    o = acc_sc[0:HEAD_W, :] / acc_sc[HEAD_W:HEAD_W + 1, :]
    od = o[:, :TQ] - lam_ref[0] * o[:, TQ:]
    y = od * lax.rsqrt(jnp.mean(od * od, axis=0, keepdims=True) + EPS) * g_ref[...]
    o_ref[...] = (y * (1.0 - LAM_INIT)).T.astype(o_ref.dtype)


def _attention(qt, k, vt, kn, btiles, brange, lam, g_col, *, T=ATT_BLOCK):
    B, L, _ = k.shape
    TQ = ATT_QK_RATIO * T
    nk = L // T
    assert nk % 4 == 0 and L % TQ == 0
    return pl.pallas_call(
        functools.partial(_attn_kernel, TQ=TQ, T=T, nk=nk),
        grid=(B, N_HEADS, L // TQ),
        in_specs=[pl.BlockSpec(memory_space=pltpu.SMEM),
                  pl.BlockSpec(memory_space=pltpu.SMEM),
                  pl.BlockSpec((None, HEAD_W, TQ), lambda b, h, i: (b, h, i)),
                  pl.BlockSpec((None, L, HEAD_W), lambda b, h, i: (b, 0, h)),
                  pl.BlockSpec((None, HEAD_W, L), lambda b, h, i: (b, h, 0)),
                  pl.BlockSpec((None, ATT_BIAS_TILES, T, 2 * TQ), lambda b, h, i: (h, 0, 0, 0)),
                  pl.BlockSpec((None,) + kn.shape[1:], lambda b, h, i: (b, 0, 0)),
                  _const_spec((HEAD_W, 1))],
        out_specs=pl.BlockSpec((None, TQ, HEAD_W), lambda b, h, i: (b, i, h)),
        out_shape=jax.ShapeDtypeStruct((B, L, ATT_W), BF16),
        scratch_shapes=[pltpu.VMEM((HEAD_W + ONES_ROWS, 2 * TQ), F32), pltpu.VMEM((4 * T, 2 * TQ), BF16)],
        compiler_params=_params(("parallel", "parallel", "parallel")),
        name="diff_attention",
    )(lam, brange, qt, k, vt, btiles, kn, g_col)


def _conv_kernel(x_ref, prev_ref, next_ref, dt_ref, dtt_ref, w_ref, b_ref, dtb_ref, dtbt_ref,
                 xs_ref, bc_ref, dts_ref, dtst_ref, *, tl, nt):
    i = pl.program_id(1)
    R = CONV_ROWS
    row = lax.broadcasted_iota(jnp.int32, (R, 128), 0)
    for r0 in range(0, tl, R):
        for c0 in range(0, CONV_CH, 128):
            cols = slice(c0, c0 + 128)
            x = x_ref[r0:r0 + R, cols]
            above = x_ref[r0 - 1:r0, cols] if r0 > 0 else jnp.where(i > 0, prev_ref[7:8, cols], 0.0)
            below = x_ref[r0 + R:r0 + R + 1, cols] if r0 + R < tl else jnp.where(i < nt - 1, next_ref[0:1, cols], 0.0)
            xm1 = jnp.where(row == 0, above, pltpu.roll(x, 1, 0))
            xp1 = jnp.where(row == R - 1, below, pltpu.roll(x, R - 1, 0))
            y = w_ref[0:1, cols] * xm1 + w_ref[1:2, cols] * x + w_ref[2:3, cols] * xp1 + b_ref[:, cols]
            y = (y * jax.nn.sigmoid(y)).astype(BF16)
            if c0 < SSM_INNER:
                xs_ref[r0:r0 + R, cols] = y
            else:
                bc_ref[r0:r0 + R, c0 - SSM_INNER:c0 - SSM_INNER + 128] = y

    def softplus(v):
        return jnp.maximum(v, 0.0) + jnp.log1p(jnp.exp(-jnp.abs(v)))

    dts_ref[...] = softplus(dt_ref[...] + dtb_ref[...])
    dtst_ref[...] = softplus(dtt_ref[...] + dtbt_ref[...])


def _conv(xbc, dt, dtt, conv_w, conv_b, dtb, dtbt, *, tl=512):
    B, L, _ = xbc.shape
    nt = L // tl
    r8 = tl // 8
    ndt = 2 * SSM_HEADS
    row = lambda w: pl.BlockSpec((None, tl, w), lambda b, i: (b, i, 0))
    colspec = pl.BlockSpec((None, ndt, tl), lambda b, i: (b, 0, i))
    return pl.pallas_call(
        functools.partial(_conv_kernel, tl=tl, nt=nt),
        grid=(B, nt),
        in_specs=[row(CONV_CH),
                  pl.BlockSpec((None, 8, CONV_CH), lambda b, i: (b, jnp.maximum(i * r8 - 1, 0), 0)),
                  pl.BlockSpec((None, 8, CONV_CH), lambda b, i: (b, jnp.minimum((i + 1) * r8, L // 8 - 1), 0)),
                  row(ndt), colspec,
                  _const_spec((3, CONV_CH)), _const_spec((1, CONV_CH)), _const_spec((1, ndt)),
                  _const_spec((ndt, 1))],
        out_specs=[row(SSM_INNER), row(2 * SSM_GROUPS * SSM_N), row(ndt), colspec],
        out_shape=[jax.ShapeDtypeStruct((B, L, SSM_INNER), BF16),
                   jax.ShapeDtypeStruct((B, L, 2 * SSM_GROUPS * SSM_N), BF16),
                   jax.ShapeDtypeStruct((B, L, ndt), F32),
                   jax.ShapeDtypeStruct((B, ndt, L), F32)],
        compiler_params=_params(("parallel", "parallel")),
        name="ssm_conv",
    )(xbc, xbc, xbc, dt, dtt, conv_w, conv_b, dtb, dtbt)


def _ssd_chunk(xs_ref, bc_ref, dts_ref, dtst_ref, alog_ref, alogt_ref, st_ref, *, rev):
    Q = SSM_CHUNK
    HI = lax.Precision.HIGHEST
    off = SSM_HEADS if rev else 0
    ri = lax.broadcasted_iota(jnp.int32, (Q, Q), 0)
    ci = lax.broadcasted_iota(jnp.int32, (Q, Q), 1)
    incl = (ci >= ri) if rev else (ci <= ri)
    tri = incl.astype(F32)
    dt = dts_ref[:, off:off + SSM_HEADS]
    dtt = dtst_ref[off:off + SSM_HEADS, :]
    a = dt * (-jnp.exp(alog_ref[:, off:off + SSM_HEADS]))
    at = dtt * (-jnp.exp(alogt_ref[off:off + SSM_HEADS, :]))
    acum = jnp.dot(tri, a, preferred_element_type=F32, precision=HI)
    acumt = _nt_dot(at, tri, precision=HI)
    edge = 0 if rev else Q - 1
    bc = bc_ref[...]
    bpair = bc[:, :Q]
    cpair = bc[:, Q:]
    low = ci < SSM_N
    rlow = ri < SSM_N
    blockmask = low == rlow
    total = acumt[:, edge:edge + 1]
    decay = jnp.exp(total)
    w_rows = dtt * jnp.exp(total - acumt)
    ys = []
    for g in range(SSM_GROUPS):
        gmask = low if g == 0 else jnp.logical_not(low)
        cg = jnp.where(gmask, cpair, jnp.zeros_like(cpair))
        cb = _nt_dot(cg, bpair)
        dupm = ((ri - g * SSM_N) == (ci % SSM_N)).astype(BF16)
        cdup = jnp.dot(cpair, dupm, preferred_element_type=F32)
        sel2 = ((ci - g * SSM_N) == (ri % SSM_N)).astype(BF16)
        bt2 = _nt_dot(sel2, bpair)
        for hp in range(SSM_HEADS // SSM_GROUPS // 2):
            h0 = g * (SSM_HEADS // SSM_GROUPS) + 2 * hp
            pair = h0 // 2
            xpair = xs_ref[:, h0 * SSM_P:(h0 + 2) * SSM_P]
            ms = []
            e_cols = []
            for h in (h0, h0 + 1):
                e_col = jnp.broadcast_to(acum[:, h:h + 1], (Q, Q))
                r_row = jnp.broadcast_to(acumt[h:h + 1, :], (Q, Q))
                seg = jnp.where(incl, e_col - r_row, -jnp.inf)
                ms.append((cb * jnp.exp(seg) * jnp.broadcast_to(dtt[h:h + 1, :], (Q, Q))).astype(BF16))
                e_cols.append(e_col)
            st = st_ref[pair]
            xblk = jnp.concatenate([jnp.where(low, xpair, jnp.zeros_like(xpair)),
                                    jnp.where(low, jnp.zeros_like(xpair), xpair)], axis=0)
            y = jnp.dot(jnp.concatenate(ms, axis=1), xblk, preferred_element_type=F32)
            ce = (cdup * jnp.exp(jnp.where(low, e_cols[0], e_cols[1]))).astype(BF16)
            y = y + jnp.dot(ce, st.astype(BF16), preferred_element_type=F32)
            ys.append(y)
            w_pair = jnp.where(rlow, jnp.broadcast_to(w_rows[h0:h0 + 1, :], (Q, Q)),
                               jnp.broadcast_to(w_rows[h0 + 1:h0 + 2, :], (Q, Q)))
            snew = jnp.dot((bt2 * w_pair).astype(BF16), xpair, preferred_element_type=F32)
            dec = jnp.where(rlow, decay[h0:h0 + 1, :], decay[h0 + 1:h0 + 2, :])
            st_ref[pair] = dec * st + jnp.where(blockmask, snew, 0.0)
    return jnp.concatenate(ys, axis=1)


def _ssd_fwd_kernel(xs_ref, bc_ref, dts_ref, dtst_ref, alog_ref, alogt_ref, y_ref, st_ref):
    @pl.when(pl.program_id(1) == 0)
    def _():
        st_ref[...] = jnp.zeros(st_ref.shape, F32)

    y = _ssd_chunk(xs_ref, bc_ref, dts_ref, dtst_ref, alog_ref, alogt_ref, st_ref, rev=False)
    y_ref[...] = y.astype(y_ref.dtype)


def _ssd_bwd_kernel(xs_ref, bc_ref, dts_ref, dtst_ref, alog_ref, alogt_ref, yf_ref, z_ref, dskip_ref, g_ref,
                    o_ref, st_ref):
    @pl.when(pl.program_id(1) == 0)
    def _():
        st_ref[...] = jnp.zeros(st_ref.shape, F32)

    yb = _ssd_chunk(xs_ref, bc_ref, dts_ref, dtst_ref, alog_ref, alogt_ref, st_ref, rev=True)
    z = z_ref[...].astype(F32)
    y = (yf_ref[...].astype(F32) + yb + xs_ref[...].astype(F32) * dskip_ref[...]) * (z * jax.nn.sigmoid(z))
    y = y * lax.rsqrt(jnp.mean(y * y, axis=-1, keepdims=True) + EPS) * g_ref[...]
    o_ref[...] = y.astype(o_ref.dtype)


def _ssd(xs, bc, dts, dtst, alog, alogt, z, dskip, g_norm):
    B, L, _ = xs.shape
    Q = SSM_CHUNK
    nc = L // Q
    ndt = 2 * SSM_HEADS
    npair = SSM_HEADS // 2

    def specs(cidx):
        row = lambda w: pl.BlockSpec((None, Q, w), lambda b, c: (b, cidx(c), 0))
        return row, [row(SSM_INNER), row(2 * SSM_GROUPS * SSM_N), row(ndt),
                     pl.BlockSpec((None, ndt, Q), lambda b, c: (b, 0, cidx(c))),
                     _const_spec((1, ndt)), _const_spec((ndt, 1))]

    scratch = [pltpu.VMEM((npair, 2 * SSM_N, 2 * SSM_P), F32)]
    row, in_specs = specs(lambda c: c)
    yf = pl.pallas_call(
        _ssd_fwd_kernel,
        grid=(B, nc),
        in_specs=in_specs,
        out_specs=row(SSM_INNER),
        out_shape=jax.ShapeDtypeStruct((B, L, SSM_INNER), BF16),
        scratch_shapes=scratch,
        compiler_params=_params(("parallel", "arbitrary")),
        name="ssd_forward",
    )(xs, bc, dts, dtst, alog, alogt)
    row, in_specs = specs(lambda c: nc - 1 - c)
    return pl.pallas_call(
        _ssd_bwd_kernel,
        grid=(B, nc),
        in_specs=in_specs + [row(SSM_INNER), row(SSM_INNER), _const_spec((1, SSM_INNER)),
                             _const_spec((1, SSM_INNER))],
        out_specs=row(SSM_INNER),
        out_shape=jax.ShapeDtypeStruct((B, L, SSM_INNER), BF16),
        scratch_shapes=scratch,
        compiler_params=_params(("parallel", "arbitrary")),
        name="ssd_reverse_gate_norm",
    )(xs, bc, dts, dtst, alog, alogt, yf, z, dskip, g_norm)


def _merge_kernel(att_ref, ssm_ref, gates_ref, x_ref, wau_ref, wsu_ref, wout_ref, gffn_ref, wr_ref, wrt_ref,
                  place_ref, h_ref, u2_ref, afft_ref):
    D = D_MODEL
    E = N_EXPERTS
    ya = jnp.dot(att_ref[...], wau_ref[...], preferred_element_type=F32)
    ysm = jnp.dot(ssm_ref[...], wsu_ref[...], preferred_element_type=F32)
    merged = gates_ref[:, :D] * ya + gates_ref[:, D:] * ysm
    h = x_ref[...] + jnp.dot(merged.astype(BF16), wout_ref[...], preferred_element_type=F32)
    h_ref[...] = h
    u2 = (h * lax.rsqrt(jnp.mean(h * h, axis=-1, keepdims=True) + EPS) * gffn_ref[...]).astype(BF16)
    u2_ref[:, :D] = u2
    logits = jnp.dot(u2, wr_ref[...], preferred_element_type=F32)
    e = jnp.exp(logits - jnp.max(logits, axis=-1, keepdims=True))
    aff = e / jnp.sum(e, axis=-1, keepdims=True)
    hi = aff.astype(BF16)
    r1 = aff - hi.astype(F32)
    mid = r1.astype(BF16)
    lo = (r1 - mid.astype(F32)).astype(BF16)
    ext = (jnp.dot(hi, place_ref[0:E, :], preferred_element_type=F32)
           + jnp.dot(mid, place_ref[E:2 * E, :], preferred_element_type=F32)
           + jnp.dot(lo, place_ref[2 * E:3 * E, :], preferred_element_type=F32))
    u2_ref[:, D:] = ext.astype(BF16)
    lt = _nt_dot(wrt_ref[...], u2)
    et = jnp.exp(lt - jnp.max(lt, axis=0, keepdims=True))
    afft_ref[...] = et / jnp.sum(et, axis=0, keepdims=True)


def _merge(att, ssm, gates, x, wau, wsu, wout, g_ffn, wr, *, tm=256):
    B, L, D = x.shape
    E = N_EXPERTS
    nl = L // tm
    row = lambda w: pl.BlockSpec((None, tm, w), lambda b, i: (b, i, 0))
    place = jnp.eye(3 * E, GATE_EXT, dtype=BF16)
    return pl.pallas_call(
        _merge_kernel,
        grid=(B, nl),
        in_specs=[row(ATT_W), row(SSM_INNER), row(2 * D), row(D), _const_spec(wau.shape),
                  _const_spec(wsu.shape), _const_spec(wout.shape), _const_spec((1, D)), _const_spec(wr.shape),
                  _const_spec((E, D)), _const_spec((3 * E, GATE_EXT))],
        out_specs=[row(D), row(D + GATE_EXT), pl.BlockSpec((E, tm), lambda b, i: (0, b * nl + i))],
        out_shape=[jax.ShapeDtypeStruct((B, L, D), F32), jax.ShapeDtypeStruct((B, L, D + GATE_EXT), BF16),
                   jax.ShapeDtypeStruct((E, B * L), F32)],
        compiler_params=_params(("parallel", "parallel")),
        name="merge_router",
    )(att, ssm, gates, x, wau, wsu, wout, g_ffn, wr, wr.T, place)


def _route_kernel(afft_ref, posm_ref, bstart_ref, *, cap, nb):
    E = N_EXPERTS
    W = ROUTE_BLK

    def bit_body(k, thr):
        cand = thr | lax.shift_left(jnp.int32(1), 30 - k)
        x = lax.bitcast_convert_type(afft_ref[...], jnp.int32)
        cnt = jnp.sum(jnp.where(x >= cand, 1.0, 0.0), axis=1, keepdims=True)
        return jnp.where(cnt >= cap, cand, thr)

    thr = lax.fori_loop(0, 31, bit_body, jnp.zeros((E, 1), jnp.int32))
    x = lax.bitcast_convert_type(afft_ref[...], jnp.int32)
    n_gt = jnp.sum(jnp.where(x > thr, 1.0, 0.0), axis=1, keepdims=True)
    need = cap - n_gt
    ri = lax.broadcasted_iota(jnp.int32, (W, W), 0)
    ci = lax.broadcasted_iota(jnp.int32, (W, W), 1)
    upper = jnp.where(ri <= ci, 1.0, 0.0).astype(BF16)
    bstart_ref[...] = jnp.zeros(bstart_ref.shape, jnp.int32)
    c_eq = jnp.zeros((E, 1), F32)
    c_sel = jnp.zeros((E, 1), F32)
    for b in range(nb):
        xb = lax.bitcast_convert_type(afft_ref[:, b * W:(b + 1) * W], jnp.int32)
        eq = xb == thr
        eqc = jnp.dot(jnp.where(eq, 1.0, 0.0).astype(BF16), upper, preferred_element_type=F32) + c_eq
        sel = (xb > thr) | (eq & (eqc <= need))
        selc = jnp.dot(jnp.where(sel, 1.0, 0.0).astype(BF16), upper, preferred_element_type=F32) + c_sel
        posm_ref[:, b * W:(b + 1) * W] = jnp.where(sel, selc, 0.0).astype(jnp.int32)
        bstart_ref[:, b:b + 1] = c_sel.astype(jnp.int32)
        c_eq = eqc[:, W - 1:W]
        c_sel = selc[:, W - 1:W]
    bstart_ref[:, nb:nb + 1] = c_sel.astype(jnp.int32)


def _route(afft, cap):
    E, T = afft.shape
    nb = T // ROUTE_BLK
    assert nb < BSTART_W and cap % SLOT_TILE == 0
    return pl.pallas_call(
        functools.partial(_route_kernel, cap=float(cap), nb=nb),
        out_shape=[jax.ShapeDtypeStruct((E, T), jnp.int32), jax.ShapeDtypeStruct((E, BSTART_W), jnp.int32)],
        compiler_params=_params(None),
        name="route_select",
    )(afft)


def _pair_schedule(bstart, cap, ngb):
    step = GATHER_BLK // ROUTE_BLK
    st = bstart[:, 0:ngb * step + 1:step]
    lo, hi = st[:, :-1], st[:, 1:]
    ntile = cap // SLOT_TILE
    t_lo = jnp.arange(ntile, dtype=jnp.int32) * SLOT_TILE
    t_hi = t_lo + SLOT_TILE
    b_first = jnp.sum(hi[:, None, :] <= t_lo[None, :, None], axis=2).astype(jnp.int32)
    b_last = jnp.sum(lo[:, None, :] < t_hi[None, :, None], axis=2).astype(jnp.int32) - 1
    return b_first, b_last - b_first + 1


def _expert_kernel(first_ref, cnt_ref, bs_ref, u_hbm, posm_ref, wg_ref, wu_ref, wd_ref, o_ref, x_sc, ubuf, sem):
    D = D_MODEL
    e = pl.program_id(0)
    j = pl.program_id(1)
    g = e * pl.num_programs(1) + j
    b0 = first_ref[g]
    n = cnt_ref[g]

    half = GATHER_BLK // 2

    def half_copies(b, ring, slot):
        return [pltpu.make_async_copy(
            u_hbm.at[pl.ds(pl.multiple_of(b * GATHER_BLK + r * half, half), half)],
            ubuf.at[ring, slot, pl.ds(r * half, half)], sem.at[ring, slot, r]) for r in range(2)]

    def start(b, ring, slot):
        for cp in half_copies(b, ring, slot):
            cp.start()

    ahead = GATHER_BUFS - 1
    ring = g % 2

    def start_tile(t):
        start(first_ref[t], t % 2, 0)
        for r in range(1, ahead):
            @pl.when(cnt_ref[t] > r)
            def _(r=r):
                start(first_ref[t] + r, t % 2, r)

    @pl.when(g == 0)
    def _():
        start_tile(0)

    @pl.when(g + 1 < pl.num_programs(0) * pl.num_programs(1))
    def _():
        start_tile(g + 1)

    x_sc[...] = jnp.zeros(x_sc.shape, F32)
    WM = GATHER_ROWS
    row_ids = lax.broadcasted_iota(jnp.int32, (WM, GATHER_BLK), 0)
    per_blk = GATHER_BLK // ROUTE_BLK

    def body(c, carry):
        slot = c % GATHER_BUFS
        for cp in half_copies(b0 + c, ring, slot):
            cp.wait()

        @pl.when(c + ahead < n)
        def _():
            start(b0 + c + ahead, ring, (c + ahead) % GATHER_BUFS)

        blk = b0 + c
        posrow = posm_ref[:, pl.ds(pl.multiple_of(blk * GATHER_BLK, GATHER_BLK), GATHER_BLK)] - (j * SLOT_TILE + 1)
        lo = jnp.maximum(bs_ref[e * BSTART_W + blk * per_blk] - j * SLOT_TILE, 0)
        hi = jnp.minimum(bs_ref[e * BSTART_W + (blk + 1) * per_blk] - j * SLOT_TILE, SLOT_TILE)
        w0 = (lo // 16) * 16

        def window(k, carry):
            wlo = w0 + k * WM
            top = pl.multiple_of(jnp.minimum(wlo, SLOT_TILE - WM), 16)
            hit = (posrow >= wlo) & (posrow < wlo + WM) & (posrow - top == row_ids)
            onehot = jnp.where(hit, 1.0, 0.0).astype(BF16)
            x_sc[pl.ds(top, WM), :] += jnp.dot(onehot, ubuf[ring, slot], preferred_element_type=F32)
            return carry

        lax.fori_loop(0, jnp.maximum((hi - w0 + WM - 1) // WM, 0), window, 0)
        return carry

    lax.fori_loop(0, n, body, 0)
    xs = x_sc[:, :D].astype(BF16)
    ext = x_sc[:, D:]
    lane = lax.broadcasted_iota(jnp.int32, ext.shape, 1)
    mine = (lane == e) | (lane == e + N_EXPERTS) | (lane == e + 2 * N_EXPERTS)
    gate = jnp.sum(jnp.where(mine, ext, 0.0), axis=1, keepdims=True)
    hg = jnp.dot(xs, wg_ref[...], preferred_element_type=F32)
    hu = jnp.dot(xs, wu_ref[...], preferred_element_type=F32)
    hmid = (hg * jax.nn.sigmoid(hg) * hu).astype(BF16)
    o_ref[...] = jnp.dot(hmid, wd_ref[...], preferred_element_type=F32) * gate


def _experts(u2ext, posm, bstart, b_first, b_cnt, wg, wu, wd, cap):
    T, DX = u2ext.shape
    E, D = N_EXPERTS, D_MODEL
    wspec = pl.BlockSpec((None, D, D), lambda e, j, bf, bc, bs: (e, 0, 0))
    return pl.pallas_call(
        _expert_kernel,
        grid_spec=pltpu.PrefetchScalarGridSpec(
            num_scalar_prefetch=3,
            grid=(E, cap // SLOT_TILE),
            in_specs=[pl.BlockSpec(memory_space=pl.ANY),
                      pl.BlockSpec((None, 1, T), lambda e, j, bf, bc, bs: (e, 0, 0)),
                      wspec, wspec, wspec],
            out_specs=pl.BlockSpec((None, SLOT_TILE, D), lambda e, j, bf, bc, bs: (e, j, 0)),
            scratch_shapes=[pltpu.VMEM((SLOT_TILE, DX), F32), pltpu.VMEM((2, GATHER_BUFS, GATHER_BLK, DX), BF16),
                            pltpu.SemaphoreType.DMA((2, GATHER_BUFS, 2))]),
        out_shape=jax.ShapeDtypeStruct((E, cap, D), F32),
        compiler_params=_params(("arbitrary", "arbitrary")),
        name="experts_gather_mlp",
    )(b_first.reshape(-1), b_cnt.reshape(-1), bstart.reshape(-1), u2ext, posm.reshape(E, 1, T), wg, wu, wd)


def _combine_kernel(bstart_ref, posm_ref, ye_ref, h_ref, p_ref, wpg_ref, bpg_ref, wpp_ref, gfin_ref, o_ref,
                    moe, buf, xbuf, sem, xsem, *, cap, nb):
    E = N_EXPERTS
    WN = COMBINE_WIN
    b = pl.program_id(0)
    top = cap - WN
    cur = b % 2

    def window(blk, slot, e):
        base = jnp.minimum((bstart_ref[e, blk] // 8) * 8, top)
        return pltpu.make_async_copy(ye_ref.at[e, pl.ds(pl.multiple_of(base, 8), WN)],
                                     buf.at[slot, pl.ds(e * WN, WN)], sem.at[slot, e])

    @pl.when(b == 0)
    def _():
        for e in range(E):
            window(0, 0, e).start()

    @pl.when(b + 1 < nb)
    def _():
        for e in range(E):
            window(b + 1, 1 - cur, e).start()

    lows = [(bstart_ref[e, b] // 8) * 8 for e in range(E)]
    w_iota = lax.broadcasted_iota(jnp.int32, (WN, ROUTE_BLK), 0)

    def onehot_rows(slot0, low, base):
        ok = (slot0 >= low) & (slot0 < low + WN)
        return jnp.where(ok & (slot0 - base == w_iota), 1.0, 0.0).astype(BF16)

    def scatter(onehot, rows):
        hi = rows.astype(BF16)
        lo = (rows - hi.astype(F32)).astype(BF16)
        tn = lambda a, c: lax.dot_general(a, c, (((0,), (0,)), ((), ())), preferred_element_type=F32)
        return tn(onehot, hi) + tn(onehot, lo)

    hots = []
    for e in range(E):
        slot0 = posm_ref[e:e + 1, :] - 1
        hots.append(onehot_rows(slot0, lows[e], jnp.minimum(lows[e], top)))
    for e in range(E):
        window(b, cur, e).wait()
    moe[...] = scatter(jnp.concatenate(hots, axis=0), buf[cur])

    def per_expert(e, carry):
        low0 = (bstart_ref[e, b] // 8) * 8
        nwin = (bstart_ref[e, b + 1] - low0 + WN - 1) // WN

        def per_window(c, carry):
            low = low0 + c * WN
            base = jnp.minimum(low, top)
            cp = pltpu.make_async_copy(ye_ref.at[e, pl.ds(pl.multiple_of(base, 8), WN)], xbuf, xsem.at[0])
            cp.start()
            cp.wait()
            mine = lax.broadcasted_iota(jnp.int32, (E, ROUTE_BLK), 0) == e
            slot0 = jnp.sum(jnp.where(mine, posm_ref[...], 0), axis=0, keepdims=True) - 1
            moe[...] += scatter(onehot_rows(slot0, low, base), xbuf[...])
            return carry

        return lax.fori_loop(1, nwin, per_window, carry)

    lax.fori_loop(0, E, per_expert, 0)

    h = h_ref[...] + moe[...]
    gate = jax.nn.sigmoid(jnp.dot(h.astype(BF16), wpg_ref[...], preferred_element_type=F32) + bpg_ref[...])
    h = h + gate * jnp.dot(p_ref[...].astype(BF16), wpp_ref[...], preferred_element_type=F32)
    o_ref[...] = h * lax.rsqrt(jnp.mean(h * h, axis=-1, keepdims=True) + EPS) * gfin_ref[...]


def _combine_ple(ye, posm, bstart, cap, h, p, wpg, bpg, wpp, g_final):
    E, T = posm.shape
    D = D_MODEL
    nb = T // ROUTE_BLK
    assert cap >= COMBINE_WIN
    row = lambda w: pl.BlockSpec((ROUTE_BLK, w), lambda b, bs: (b, 0))
    const = lambda shape: pl.BlockSpec(shape, lambda b, bs: (0,) * len(shape))
    return pl.pallas_call(
        functools.partial(_combine_kernel, cap=cap, nb=nb),
        grid_spec=pltpu.PrefetchScalarGridSpec(
            num_scalar_prefetch=1,
            grid=(nb,),
            in_specs=[pl.BlockSpec((E, ROUTE_BLK), lambda b, bs: (0, b)),
                      pl.BlockSpec(memory_space=pl.ANY), row(D), row(PLE_DIM), const(wpg.shape), const((1, D)),
                      const(wpp.shape), const((1, D))],
            out_specs=row(D),
            scratch_shapes=[pltpu.VMEM((ROUTE_BLK, D), F32), pltpu.VMEM((2, E * COMBINE_WIN, D), F32),
                            pltpu.VMEM((COMBINE_WIN, D), F32),
                            pltpu.SemaphoreType.DMA((2, E)), pltpu.SemaphoreType.DMA((1,))]),
        out_shape=jax.ShapeDtypeStruct((T, D), F32),
        compiler_params=_params(("arbitrary",)),
        name="experts_combine_ple",
    )(bstart, posm, ye, h.reshape(T, D), p.reshape(T, PLE_DIM), wpg, bpg, wpp, g_final)


def _forward(x, p, w, btiles):
    B, L, D = x.shape
    T = B * L
    qt, k, vt, z, xbc, dt, dtt, gates, kn = _inproj(x, w['g_mix'], w['wqt'], w['wk'], w['wvt'], w['wz'],
                                                    w['wxbc'], w['wdt'], w['wdtt'], w['wgate'], w['bgate'])
    att = _attention(qt, k, vt, kn, btiles, w['brange'], w['lam'], w['g_subln'])
    xs, bc, dts, dtst = _conv(xbc, dt, dtt, w['conv_w'], w['conv_b'], w['dtb'], w['dtbt'])
    ssm = _ssd(xs, bc, dts, dtst, w['alog'], w['alogt'], z, w['dskip'], w['g_ssm_norm'])
    h, u2ext, afft = _merge(att, ssm, gates, x, w['wau'], w['wsu'], w['wout'], w['g_ffn'], w['wr'])
    cap = max(1, CAPACITY_FACTOR * T // N_EXPERTS)
    posm, bstart = _route(afft, cap)
    b_first, b_cnt = _pair_schedule(bstart, cap, T // GATHER_BLK)
    ye = _experts(u2ext.reshape(T, D + GATE_EXT), posm, bstart, b_first, b_cnt, w['weg'], w['weu'], w['wed'], cap)
    out = _combine_ple(ye, posm, bstart, cap, h, p, w['wpg'], w['bpg'], w['wpp'], w['g_final'])
    return out.reshape(B, L, D)


def kernel(x_prompt, x_sample, p_prompt, p_sample, rel_bias, g_mix, w_in, conv_w, conv_b, dt_bias, a_log,
           d_skip, g_ssm_norm, lam_params, g_subln, w_att_up, w_ssm_up, w_gate, b_gate, w_out, g_ffn,
           w_router, w_exp_gate, w_exp_up, w_exp_down, w_ple_gate, b_ple_gate, w_ple_proj, g_final):
    i = 0
    D = D_MODEL
    win = w_in[i]
    o1 = ATT_W
    o3 = 3 * ATT_W
    o4 = o3 + SSM_INNER
    o5 = o4 + CONV_CH
    wqt = (win[:, :o1] * (QK_DIM ** -0.5 * LOG2E)).T.astype(BF16)
    wk = win[:, o1:2 * o1].astype(BF16)
    wvt = win[:, 2 * o1:o3].T.astype(BF16)
    wdt = win[:, o5:].astype(BF16)
    lp = lam_params[i].astype(F32)
    lam = jnp.exp(jnp.sum(lp[0] * lp[1])) - jnp.exp(jnp.sum(lp[2] * lp[3])) + LAM_INIT
    ndt = 2 * SSM_HEADS
    tb = rel_bias.astype(F32).reshape(REL_BUCKETS, N_HEADS, 2) * LOG2E
    brange = jnp.concatenate([jnp.max(tb, axis=0), jnp.min(tb, axis=0)], axis=1)
    w = {
        'g_mix': g_mix[i].reshape(1, D), 'wqt': wqt, 'wk': wk, 'wvt': wvt, 'wz': win[:, o3:o4].astype(BF16),
        'wxbc': win[:, o4:o5].astype(BF16), 'wdt': wdt, 'wdtt': wdt.T,
        'wgate': w_gate[i].astype(BF16), 'bgate': b_gate[i].reshape(1, 2 * D),
        'lam': lam.reshape(1), 'g_subln': g_subln[i].reshape(HEAD_W, 1), 'brange': brange,
        'conv_w': conv_w[i], 'conv_b': conv_b[i].reshape(1, CONV_CH),
        'dtb': dt_bias[i].reshape(1, ndt), 'dtbt': dt_bias[i].reshape(ndt, 1),
        'alog': a_log[i].reshape(1, ndt), 'alogt': a_log[i].reshape(ndt, 1),
        'dskip': jnp.repeat(d_skip[i], SSM_P).reshape(1, SSM_INNER),
        'g_ssm_norm': g_ssm_norm[i].reshape(1, SSM_INNER),
        'wau': w_att_up[i].astype(BF16), 'wsu': w_ssm_up[i].astype(BF16), 'wout': w_out[i].astype(BF16),
        'g_ffn': g_ffn[i].reshape(1, D), 'wr': w_router[i].astype(BF16),
        'weg': w_exp_gate[i].astype(BF16), 'weu': w_exp_up[i].astype(BF16), 'wed': w_exp_down[i].astype(BF16),
        'wpg': w_ple_gate[i].astype(BF16), 'bpg': b_ple_gate[i].reshape(1, D),
        'wpp': w_ple_proj[i].astype(BF16), 'g_final': g_final.reshape(1, D),
    }
    btiles = _bias_tiles(rel_bias, ATT_QK_RATIO * ATT_BLOCK, ATT_BLOCK)
    return (_forward(x_prompt, p_prompt[i], w, btiles), _forward(x_sample, p_sample[i], w, btiles))
```
